```python
import math
import jax, jax.numpy as jnp
from jax import lax
import numpy as np

D_MODEL = 1024
BATCH = 32
SEQ = 2048
DEPTH = 1
DEC_BATCH = 128
DEC_SEQ = 4
PAST_LEN = 8192
PAGE_SIZE = 128

SSD_HEADS = 16
SSD_HEAD_DIM = 64
SSD_INNER = SSD_HEADS * SSD_HEAD_DIM
SSD_GROUPS = 2
SSD_STATE = 128
SSD_CONV = 4
SSD_CHUNK = 128
CONV_DIM = SSD_INNER + 2 * SSD_GROUPS * SSD_STATE
ATT_HEADS = 8
ATT_HEAD_DIM = 64
ATT_WIDTH = ATT_HEADS * ATT_HEAD_DIM
DILATED_PATTERNS = ((128, 1), (512, 4), (2048, 16))
MAX_WINDOW = 2048
ATT_BLOCK = 128
MIX_WIDTH = SSD_INNER + ATT_WIDTH
IN_WIDTH = SSD_INNER + CONV_DIM + SSD_HEADS + 3 * ATT_WIDTH
N_EXPERTS = 32
TOP_K = 4
D_EXPERT = D_MODEL
SWIGLU_LIMIT = 7.0
SWIGLU_ALPHA = 1.702
MOE_BLOCK = 128
NORM_EPS = 1e-5

kernel_name = 'hymba_ssd_dilated_swa_moe_step'


def rmsnorm(x, g):
    xf = x.astype(jnp.float32)
    y = xf * lax.rsqrt(jnp.mean(xf * xf, axis=-1, keepdims=True) + NORM_EPS)
    return (y * g.astype(jnp.float32)).astype(x.dtype)


def alibi_slopes():
    return jnp.exp2(-8.0 * jnp.arange(1, ATT_HEADS + 1, dtype=jnp.float32) / ATT_HEADS)


def split_in(u):
    o = 0
    parts = []
    for w in (SSD_INNER, CONV_DIM, SSD_HEADS, ATT_WIDTH, ATT_WIDTH, ATT_WIDTH):
        parts.append(u[..., o:o + w])
        o += w
    return parts


def causal_dwconv(xp, w, b):
    L = xp.shape[1] - (SSD_CONV - 1)
    out = b
    for j in range(SSD_CONV):
        out = out + xp[:, j:j + L] * w[j]
    return out


def segsum(a):
    T = a.shape[-1]
    ab = jnp.broadcast_to(a[..., :, None], a.shape + (T,))
    strict = jnp.tril(jnp.ones((T, T), bool), -1)
    cs = jnp.cumsum(jnp.where(strict, ab, 0.0), axis=-2)
    return jnp.where(jnp.tril(jnp.ones((T, T), bool)), cs, -jnp.inf)


def ssd_chunked(xh, dt, A, Bg, Cg, init_state, chunk):
    b, L, H, P = xh.shape
    G, N = Bg.shape[2], Bg.shape[3]
    E = H // G
    c = L // chunk
    X = (xh * dt[..., None]).reshape(b, c, chunk, G, E, P)
    dA = (dt * A).reshape(b, c, chunk, G, E).transpose(0, 1, 3, 4, 2)
    Bc = Bg.reshape(b, c, chunk, G, N)
    Cc = Cg.reshape(b, c, chunk, G, N)
    cs = jnp.cumsum(dA, axis=-1)
    Lmat = jnp.exp(segsum(dA))
    CB = jnp.einsum('bclgn,bcsgn->bcgls', Cc, Bc)
    y_diag = jnp.einsum('bcgels,bcsgep->bclgep', CB[:, :, :, None] * Lmat, X)
    decay_states = jnp.exp(cs[..., -1:] - cs).transpose(0, 1, 4, 2, 3)
    states = jnp.einsum('bclgn,bclgep->bcgepn', Bc, X * decay_states[..., None])
    states = jnp.concatenate([init_state.reshape(b, G, E, P, N)[:, None], states], axis=1)
    tot = jnp.pad(cs[..., -1].transpose(0, 2, 3, 1), ((0, 0), (0, 0), (0, 0), (1, 0)))
    decay_chunk = jnp.exp(segsum(tot))
    new_states = jnp.einsum('bgezc,bcgepn->bzgepn', decay_chunk, states)
    prev_states, final = new_states[:, :-1], new_states[:, -1]
    y_off = jnp.einsum('bclgn,bcgepn->bclgep', Cc, prev_states) * jnp.exp(cs).transpose(0, 1, 4, 2, 3)[..., None]
    y = (y_diag + y_off).reshape(b, L, H, P)
    return y, final.reshape(b, H, P, N)


def ssd_branch(z, xbc_act, dt_raw, init_state, dt_bias, a_log, d_skip, ssd_norm):
    b, L, _ = xbc_act.shape
    f32 = jnp.float32
    gn = SSD_GROUPS * SSD_STATE
    xh = xbc_act[..., :SSD_INNER].reshape(b, L, SSD_HEADS, SSD_HEAD_DIM).astype(f32)
    Bg = xbc_act[..., SSD_INNER:SSD_INNER + gn].reshape(b, L, SSD_GROUPS, SSD_STATE).astype(f32)
    Cg = xbc_act[..., SSD_INNER + gn:].reshape(b, L, SSD_GROUPS, SSD_STATE).astype(f32)
    dt = jax.nn.softplus(dt_raw.astype(f32) + dt_bias.astype(f32))
    A = -jnp.exp(a_log.astype(f32))
    chunk = SSD_CHUNK if L % SSD_CHUNK == 0 else L
    y, final = ssd_chunked(xh, dt, A, Bg, Cg, init_state.astype(f32), chunk)
    y = y + d_skip.astype(f32)[:, None] * xh
    y = y.reshape(b, L, SSD_INNER) * jax.nn.silu(z.astype(f32))
    yg = y.reshape(b, L, SSD_GROUPS, SSD_INNER // SSD_GROUPS)
    yg = yg * lax.rsqrt(jnp.mean(yg * yg, axis=-1, keepdims=True) + NORM_EPS)
    return yg.reshape(b, L, SSD_INNER) * ssd_norm.astype(f32), final


def merge_by_denominator(outs, lses):
    w = jax.nn.softmax(jnp.stack(lses, 0), axis=0)
    return jnp.einsum('pblh,pblhe->blhe', w, jnp.stack(outs, 0))


def dilated_attn_prompt(q, k, v, slopes):
    b, S, H, e = q.shape
    scale = ATT_HEAD_DIM ** -0.5
    outs, lses = [], []
    for window, dil in DILATED_PATTERNS:
        span = window // dil
        sc = S // dil
        nb = -(-sc // ATT_BLOCK)
        pad = nb * ATT_BLOCK - sc

        def to_blocks(t):
            t = t.reshape(b, sc, dil, H, e).transpose(0, 2, 1, 3, 4)
            t = jnp.pad(t, ((0, 0), (0, 0), (0, pad), (0, 0), (0, 0)))
            return t.reshape(b, dil, nb, ATT_BLOCK, H, e)

        def with_prev(t):
            prev = jnp.pad(t[:, :, :-1], ((0, 0), (0, 0), (1, 0), (0, 0), (0, 0), (0, 0)))
            return jnp.concatenate([prev, t], axis=3)

        qb = to_blocks(q)
        kk = with_prev(to_blocks(k))
        vv = with_prev(to_blocks(v))
        s = jnp.einsum('brnqhe,brnkhe->brnhqk', qb, kk) * scale
        qi = jnp.arange(ATT_BLOCK)[:, None] + ATT_BLOCK
        kj = jnp.arange(2 * ATT_BLOCK)[None, :]
        delta = qi - kj
        blk = jnp.arange(nb)[:, None, None]
        valid = (delta >= 0) & (delta <= span) & (blk * ATT_BLOCK + kj - ATT_BLOCK >= 0)
        bias = -slopes[:, None, None] * (dil * delta).astype(jnp.float32)[None]
        s = jnp.where(valid[:, None], s + bias, -jnp.inf)
        lse = jax.nn.logsumexp(s, axis=-1)
        o = jnp.einsum('brnhqk,brnkhe->brnqhe', jnp.exp(s - lse[..., None]), vv)
        o = o.reshape(b, dil, nb * ATT_BLOCK, H, e)[:, :, :sc].transpose(0, 2, 1, 3, 4).reshape(b, S, H, e)
        lse = lse.transpose(0, 1, 2, 4, 3).reshape(b, dil, nb * ATT_BLOCK, H)[:, :, :sc]
        lse = lse.transpose(0, 2, 1, 3).reshape(b, S, H)
        outs.append(o)
        lses.append(lse)
    return merge_by_denominator(outs, lses)


def dilated_attn_decode(q, k_all, v_all, slopes):
    T = q.shape[1]
    M = k_all.shape[1] - T
    scale = ATT_HEAD_DIM ** -0.5
    outs, lses = [], []
    for window, dil in DILATED_PATTERNS:
        steps = jnp.arange(window // dil + 1)
        idx = M + jnp.arange(T)[:, None] - dil * steps[None, :]
        valid = idx >= 0
        idx_c = jnp.maximum(idx, 0)
        kg = k_all[:, idx_c]
        vg = v_all[:, idx_c]
        s = jnp.einsum('bthe,btkhe->bthk', q, kg) * scale
        bias = -slopes[:, None] * (dil * steps).astype(jnp.float32)[None, :]
        s = jnp.where(valid[:, None, :], s + bias, -jnp.inf)
        lse = jax.nn.logsumexp(s, axis=-1)
        outs.append(jnp.einsum('bthk,btkhe->bthe', jnp.exp(s - lse[..., None]), vg))
        lses.append(lse)
    return merge_by_denominator(outs, lses)


def token_mixer(hn, conv_prev, ssm_prev, win_k_prev, win_v_prev, slopes,
                w_in, conv_w, conv_b, dt_bias, a_log, d_skip, ssd_norm, att_norm, w_out):
    b, L, _ = hn.shape
    f32 = jnp.float32
    z, xbc, dt_raw, q, k, v = split_in(hn @ w_in)
    xbc_ext = jnp.concatenate([conv_prev.astype(xbc.dtype), xbc], axis=1)
    new_conv = xbc_ext[:, -(SSD_CONV - 1):]
    xbc_act = jax.nn.silu(causal_dwconv(xbc_ext, conv_w, conv_b))
    y_ssd, new_ssm = ssd_branch(z, xbc_act, dt_raw, ssm_prev, dt_bias, a_log, d_skip, ssd_norm)
    q = q.reshape(b, L, ATT_HEADS, ATT_HEAD_DIM).astype(f32)
    k = k.reshape(b, L, ATT_HEADS, ATT_HEAD_DIM).astype(f32)
    v = v.reshape(b, L, ATT_HEADS, ATT_HEAD_DIM).astype(f32)
    if win_k_prev is None:
        o = dilated_attn_prompt(q, k, v, slopes)
        keep = min(MAX_WINDOW, L)
        new_k, new_v = k[:, L - keep:], v[:, L - keep:]
    else:
        k_all = jnp.concatenate([win_k_prev.astype(f32), k], axis=1)
        v_all = jnp.concatenate([win_v_prev.astype(f32), v], axis=1)
        o = dilated_attn_decode(q, k_all, v_all, slopes)
        new_k, new_v = k, v
    o = rmsnorm(o.reshape(b, L, ATT_WIDTH), att_norm)
    y = jnp.concatenate([y_ssd, o], axis=-1) @ w_out
    return y, new_conv, new_ssm, new_k, new_v


def moe_ffn(xf, router_w, router_b, w1, b1, w2, b2):
    n_tok, d = xf.shape
    logits = (xf @ router_w).astype(jnp.float32) + router_b.astype(jnp.float32)
    top_val, top_idx = lax.top_k(logits, TOP_K)
    gates = jax.nn.softmax(top_val, axis=-1)
    n_asg = n_tok * TOP_K
    e_flat = top_idx.reshape(-1)
    tok_flat = jnp.arange(n_asg, dtype=jnp.int32) // TOP_K
    order = jnp.argsort(e_flat)
    e_sorted = e_flat[order]
    counts = jnp.bincount(e_flat, length=N_EXPERTS)
    padded = (counts + MOE_BLOCK - 1) // MOE_BLOCK * MOE_BLOCK
    pad_end = jnp.cumsum(padded)
    pad_start = pad_end - padded
    start = jnp.cumsum(counts) - counts
    dest = pad_start[e_sorted] + jnp.arange(n_asg, dtype=jnp.int32) - start[e_sorted]
    n_blocks = -(-n_asg // MOE_BLOCK) + N_EXPERTS
    n_slots = n_blocks * MOE_BLOCK
    slot_tok = jnp.full((n_slots,), n_tok, jnp.int32).at[dest].set(tok_flat[order])
    slot_gate = jnp.zeros((n_slots,), jnp.float32).at[dest].set(gates.reshape(-1)[order])
    block_expert = jnp.minimum(
        jnp.searchsorted(pad_end, jnp.arange(n_blocks, dtype=pad_end.dtype) * MOE_BLOCK, side='right'),
        N_EXPERTS - 1)
    x_pad = jnp.concatenate([xf, jnp.zeros((1, d), xf.dtype)], axis=0)
    xs = x_pad[slot_tok].reshape(n_blocks, MOE_BLOCK, d)

    def expert_block(args):
        xb, e = args
        hcat = xb @ w1[e] + b1[e]
        x_glu = jnp.minimum(hcat[:, ::2], SWIGLU_LIMIT)
        x_lin = jnp.clip(hcat[:, 1::2], -SWIGLU_LIMIT, SWIGLU_LIMIT)
        g = x_glu * jax.nn.sigmoid(SWIGLU_ALPHA * x_glu) * (x_lin + 1.0)
        return g @ w2[e] + b2[e]

    ys = lax.map(expert_block, (xs, block_expert))
    ys = ys.reshape(n_slots, d).astype(jnp.float32) * slot_gate[:, None]
    return jnp.zeros((n_tok + 1, d), jnp.float32).at[slot_tok].add(ys)[:n_tok]


def channel_mixer(hn, router_w, router_b, w1, b1, w2, b2):
    b, L, d = hn.shape
    return moe_ffn(hn.reshape(b * L, d), router_w, router_b, w1, b1, w2, b2).reshape(b, L, d)


def setup_inputs(seed: int = 0) -> dict:
    key = jax.random.key(seed)
    ks = jax.random.split(key, 24)
    f32 = jnp.float32
    buf = min(MAX_WINDOW, PAST_LEN)

    def nrm(k, shape, scale):
        return scale * jax.random.normal(k, shape, f32)

    dt0 = jnp.exp(jax.random.uniform(ks[7], (DEPTH, SSD_HEADS), f32, math.log(1e-3), math.log(1e-1)))
    return {
        'x_prompt': nrm(ks[0], (BATCH, SEQ, D_MODEL), 1.0),
        'x_sample': nrm(ks[1], (DEC_BATCH, DEC_SEQ, D_MODEL), 1.0),
        'state_conv': nrm(ks[2], (DEPTH, DEC_BATCH, SSD_CONV - 1, CONV_DIM), 1.0),
        'state_ssm': nrm(ks[3], (DEPTH, DEC_BATCH, SSD_HEADS, SSD_HEAD_DIM, SSD_STATE), 0.5),
        'cache_win_k': nrm(ks[4], (DEPTH, DEC_BATCH, buf, ATT_HEADS, ATT_HEAD_DIM), 1.0),
        'cache_win_v': nrm(ks[5], (DEPTH, DEC_BATCH, buf, ATT_HEADS, ATT_HEAD_DIM), 1.0),
        'norm_mix': 1.0 + nrm(ks[6], (DEPTH, D_MODEL), 0.02),
        'w_in': nrm(ks[8], (DEPTH, D_MODEL, IN_WIDTH), D_MODEL ** -0.5),
        'conv_w': nrm(ks[9], (DEPTH, SSD_CONV, CONV_DIM), SSD_CONV ** -0.5),
        'conv_b': nrm(ks[10], (DEPTH, CONV_DIM), 0.02),
        'dt_bias': dt0 + jnp.log(-jnp.expm1(-dt0)),
        'a_log': jnp.log(jax.random.uniform(ks[11], (DEPTH, SSD_HEADS), f32, 1.0, 16.0)),
        'd_skip': 1.0 + nrm(ks[12], (DEPTH, SSD_HEADS), 0.1),
        'ssd_norm': 1.0 + nrm(ks[13], (DEPTH, SSD_INNER), 0.02),
        'att_norm': 1.0 + nrm(ks[14], (DEPTH, ATT_WIDTH), 0.02),
        'w_out': nrm(ks[15], (DEPTH, MIX_WIDTH, D_MODEL), MIX_WIDTH ** -0.5),
        'norm_ffn': 1.0 + nrm(ks[16], (DEPTH, D_MODEL), 0.02),
        'router_w': nrm(ks[17], (DEPTH, D_MODEL, N_EXPERTS), D_MODEL ** -0.5),
        'router_b': nrm(ks[18], (DEPTH, N_EXPERTS), 0.01),
        'w1': nrm(ks[19], (DEPTH, N_EXPERTS, D_MODEL, 2 * D_EXPERT), D_MODEL ** -0.5),
        'b1': nrm(ks[20], (DEPTH, N_EXPERTS, 2 * D_EXPERT), 0.01),
        'w2': nrm(ks[21], (DEPTH, N_EXPERTS, D_EXPERT, D_MODEL), D_EXPERT ** -0.5),
        'b2': nrm(ks[22], (DEPTH, N_EXPERTS, D_MODEL), 0.01),
        'norm_final': 1.0 + nrm(ks[23], (D_MODEL,), 0.02),
    }


def reference(x_prompt, x_sample, state_conv, state_ssm, cache_win_k, cache_win_v,
              norm_mix, w_in, conv_w, conv_b, dt_bias, a_log, d_skip, ssd_norm, att_norm, w_out,
              norm_ffn, router_w, router_b, w1, b1, w2, b2, norm_final):
    slopes = alibi_slopes()
    hp, hs = x_prompt, x_sample
    bp = x_prompt.shape[0]
    conv_p, ssm_p, k_p, v_p = [], [], [], []
    conv_s, ssm_s, k_s, v_s = [], [], [], []
    for l in range(DEPTH):
        mix_w = (w_in[l], conv_w[l], conv_b[l], dt_bias[l], a_log[l], d_skip[l], ssd_norm[l], att_norm[l], w_out[l])
        ffn_w = (router_w[l], router_b[l], w1[l], b1[l], w2[l], b2[l])
        zero_conv = jnp.zeros((bp, SSD_CONV - 1, CONV_DIM), hp.dtype)
        zero_ssm = jnp.zeros((bp, SSD_HEADS, SSD_HEAD_DIM, SSD_STATE), jnp.float32)
        m, c, s, k, v = token_mixer(rmsnorm(hp, norm_mix[l]), zero_conv, zero_ssm, None, None, slopes, *mix_w)
        hp = hp + m
        hp = hp + channel_mixer(rmsnorm(hp, norm_ffn[l]), *ffn_w)
        conv_p.append(c); ssm_p.append(s); k_p.append(k); v_p.append(v)
        m, c, s, k, v = token_mixer(rmsnorm(hs, norm_mix[l]), state_conv[l], state_ssm[l],
                                    cache_win_k[l], cache_win_v[l], slopes, *mix_w)
        hs = hs + m
        hs = hs + channel_mixer(rmsnorm(hs, norm_ffn[l]), *ffn_w)
        conv_s.append(c); ssm_s.append(s); k_s.append(k); v_s.append(v)
    y_prompt = rmsnorm(hp, norm_final)
    y_sample = rmsnorm(hs, norm_final)
    return (y_prompt, y_sample,
            jnp.stack(conv_p), jnp.stack(ssm_p), jnp.stack(k_p), jnp.stack(v_p),
            jnp.stack(conv_s), jnp.stack(ssm_s), jnp.stack(k_s), jnp.stack(v_s))
```

```python
import functools

import jax
import jax.numpy as jnp
import numpy as np
from jax import lax
from jax.experimental import pallas as pl
from jax.experimental.pallas import tpu as pltpu

F32 = jnp.float32
BF16 = jnp.bfloat16

D_MODEL = 1024
SSD_HEADS = 16
SSD_HEAD_DIM = 64
SSD_INNER = SSD_HEADS * SSD_HEAD_DIM
SSD_GROUPS = 2
SSD_STATE = 128
SSD_CONV = 4
SSD_CHUNK = 128
CONV_DIM = SSD_INNER + 2 * SSD_GROUPS * SSD_STATE
ATT_HEADS = 8
ATT_HEAD_DIM = 64
ATT_WIDTH = ATT_HEADS * ATT_HEAD_DIM
DILATED_PATTERNS = ((128, 1), (512, 4), (2048, 16))
ATT_BLOCK = 128
N_EXPERTS = 32
TOP_K = 4
SWIGLU_LIMIT = 7.0
SWIGLU_ALPHA = 1.702
NORM_EPS = 1e-5

LANES = 128
SUBLANES = 8
ZX_WIDTH = SSD_INNER + CONV_DIM
DT_PAD = LANES
W_CAT = ZX_WIDTH + 3 * ATT_WIDTH + DT_PAD
CONV_HIST = SUBLANES
MOE_ROWS = 512
VMEM_LIMIT = 56 * 1024 * 1024

NEG_INF = float("-inf")


def _cparams(sem):
    return pltpu.CompilerParams(dimension_semantics=sem, vmem_limit_bytes=VMEM_LIMIT)


def _rms(x, g):
    return x * lax.rsqrt(jnp.mean(x * x, axis=-1, keepdims=True) + NORM_EPS) * g


def _dot(a, b):
    return jnp.dot(a, b, preferred_element_type=F32)


def _dot_nt(a, b):
    return lax.dot_general(a, b, (((1,), (1,)), ((), ())), preferred_element_type=F32)


def _split3(v):
    hi = v.astype(BF16)
    r1 = v - hi.astype(F32)
    mid = r1.astype(BF16)
    lo = (r1 - mid.astype(F32)).astype(BF16)
    return hi, mid, lo


def _dot3(v, m):
    hi, mid, lo = _split3(v)
    return _dot(hi, m) + _dot(mid, m) + _dot(lo, m)


def _dot3_lhs(m, v):
    hi, mid, lo = _split3(v)
    return _dot(m, hi) + _dot(m, mid) + _dot(m, lo)


def _silu(x):
    return x * jax.nn.sigmoid(x)


def _in_proj_body(x_ref, g_ref, w_ref, zx_ref, q_ref, k_ref, v_ref, dt_ref):
    hn = _rms(x_ref[...], g_ref[...]).astype(BF16)
    o = 0
    for ref, width in ((zx_ref, ZX_WIDTH), (q_ref, ATT_WIDTH), (k_ref, ATT_WIDTH), (v_ref, ATT_WIDTH),
                       (dt_ref, DT_PAD)):
        ref[...] = _dot(hn, w_ref[:, o:o + width])
        o += width


def _in_proj(x2d, g, w_cat, tm):
    n = x2d.shape[0]
    tm = min(tm, n)
    row = lambda w: pl.BlockSpec((tm, w), lambda i: (i, 0))
    full = lambda a: pl.BlockSpec(a.shape, lambda i: (0,) * a.ndim)
    widths = (ZX_WIDTH, ATT_WIDTH, ATT_WIDTH, ATT_WIDTH, DT_PAD)
    return pl.pallas_call(
        _in_proj_body,
        grid=(n // tm,),
        in_specs=[row(D_MODEL), full(g), full(w_cat)],
        out_specs=[row(w) for w in widths],
        out_shape=[jax.ShapeDtypeStruct((n, w), F32) for w in widths],
        compiler_params=_cparams(("parallel",)),
        name="in_proj",
    )(x2d, g, w_cat)


def _ssd_body(*refs, l_blk, has_init):
    T = SSD_CHUNK
    if has_init:
        (zx_ref, dt_ref, cw_ref, cb_ref, dtb_ref, alog_ref, dskip_ref, norm_ref, e64_ref, e128_ref,
         cinit_ref, sinit_ref, y_ref, conv_out_ref, ssm_out_ref,
         ext_ref, act_ref, state_ref, cst_ref, ybuf_ref, zpad_ref, dtpad_ref) = refs
    else:
        (zx_ref, dt_ref, cw_ref, cb_ref, dtb_ref, alog_ref, dskip_ref, norm_ref, e64_ref, e128_ref,
         y_ref, conv_out_ref, ssm_out_ref,
         ext_ref, act_ref, state_ref, cst_ref, ybuf_ref) = refs
    c = pl.program_id(1)
    n_tile = SSD_INNER // LANES

    @pl.when(c == 0)
    def _start():
        if has_init:
            ext_ref[0:CONV_HIST, :] = jnp.zeros((CONV_HIST, CONV_DIM), F32)
            ext_ref[CONV_HIST - (SSD_CONV - 1):CONV_HIST, :] = cinit_ref[0]
            for j in range(n_tile):
                state_ref[:, j * LANES:(j + 1) * LANES] = sinit_ref[0, j * LANES:(j + 1) * LANES, :].T
        else:
            ext_ref[0:CONV_HIST, :] = jnp.zeros((CONV_HIST, CONV_DIM), F32)
            state_ref[...] = jnp.zeros_like(state_ref)

    if l_blk == T:
        ext_ref[CONV_HIST:CONV_HIST + T, :] = zx_ref[0, :, SSD_INNER:ZX_WIDTH]
        z_of = lambda sl: zx_ref[0, :, sl]
        dt_raw = dt_ref[0]
    else:
        ext_ref[CONV_HIST:CONV_HIST + T, :] = jnp.zeros((T, CONV_DIM), F32)
        ext_ref[CONV_HIST:CONV_HIST + l_blk, :] = zx_ref[0, :, SSD_INNER:ZX_WIDTH]
        zpad_ref[...] = jnp.zeros_like(zpad_ref)
        zpad_ref[0:l_blk, :] = zx_ref[0, :, 0:SSD_INNER]
        dtpad_ref[...] = jnp.zeros_like(dtpad_ref)
        dtpad_ref[0:l_blk, :] = dt_ref[0]
        z_of = lambda sl: zpad_ref[:, sl]
        dt_raw = dtpad_ref[...]

    cw = CONV_DIM // 3
    for cc in range(3):
        sl = slice(cc * cw, (cc + 1) * cw)
        acc = cb_ref[:, sl]
        for j in range(SSD_CONV):
            o = CONV_HIST - (SSD_CONV - 1) + j
            acc = acc + ext_ref[o:o + T, sl] * cw_ref[j:j + 1, sl]
        act_ref[:, sl] = _silu(acc)

    row = lax.broadcasted_iota(jnp.int32, (T, LANES), 0)
    col = lax.broadcasted_iota(jnp.int32, (T, LANES), 1)
    tri = row >= col
    tri_bf = jnp.where(tri, 1.0, 0.0).astype(BF16)
    even = col < SSD_HEAD_DIM

    xdt = dt_raw + dtb_ref[...]
    dtv = jnp.maximum(xdt, 0.0) + jnp.log1p(jnp.exp(-jnp.abs(xdt)))
    if l_blk < T:
        dtv = jnp.where(row < l_blk, dtv, 0.0)
    d_a = dtv * (-jnp.exp(alog_ref[...]))
    cs = _dot3_lhs(tri_bf, d_a)
    cs_last = cs[T - 1:T, :]
    dec = jnp.exp(cs_last - cs)
    ecs = jnp.exp(cs)
    cst_ref[...] = cs.T
    per_head = jnp.concatenate([dtv, dtv * dec, ecs], axis=0)
    ex = _dot3(per_head, e64_ref[...])
    dt_ex, dd_ex, ecs_ex = ex[0:T], ex[T:2 * T], ex[2 * T:3 * T]
    colb = _dot3(cs, e128_ref[...])

    gw = SSD_INNER // SSD_GROUPS
    heads_per_group = SSD_HEADS // SSD_GROUPS
    for g in range(SSD_GROUPS):
        gsl = slice(g * gw, (g + 1) * gw)
        b_g = act_ref[:, SSD_INNER + g * SSD_STATE:SSD_INNER + (g + 1) * SSD_STATE]
        c_off = SSD_INNER + SSD_GROUPS * SSD_STATE
        c_g = act_ref[:, c_off + g * SSD_STATE:c_off + (g + 1) * SSD_STATE].astype(BF16)
        cb = _dot_nt(c_g, b_g.astype(BF16))
        b_gt = b_g.T.astype(BF16)
        x_g = act_ref[:, gsl]
        x_dt = (x_g * dt_ex[:, gsl]).astype(BF16)
        x_dd = (x_g * dd_ex[:, gsl]).astype(BF16)
        st_old = state_ref[:, gsl]
        y_off = _dot(c_g, st_old.astype(BF16)) * ecs_ex[:, gsl]
        state_ref[:, gsl] = st_old * ecs_ex[T - 1:T, gsl] + _dot(b_gt, x_dd)
        for jp in range(heads_per_group // 2):
            h0 = g * heads_per_group + 2 * jp
            psl = slice(jp * LANES, (jp + 1) * LANES)
            yd = []
            for h in (h0, h0 + 1):
                seg = colb[:, h * LANES:(h + 1) * LANES] - cst_ref[h:h + 1, :]
                lmat = jnp.exp(jnp.where(tri, seg, NEG_INF))
                yd.append(_dot((cb * lmat).astype(BF16), x_dt[:, psl]))
            y_pair = jnp.where(even, yd[0], yd[1]) + y_off[:, psl]
            osl = slice(g * gw + jp * LANES, g * gw + (jp + 1) * LANES)
            ybuf_ref[:, osl] = y_pair + dskip_ref[:, osl] * x_g[:, psl]

    for g in range(SSD_GROUPS):
        gsl = slice(g * gw, (g + 1) * gw)
        yg = ybuf_ref[:, gsl] * _silu(z_of(gsl))
        yn = yg * lax.rsqrt(jnp.mean(yg * yg, axis=-1, keepdims=True) + NORM_EPS) * norm_ref[:, gsl]
        y_ref[0, :, gsl] = yn[0:l_blk].astype(y_ref.dtype)

    @pl.when(c == pl.num_programs(1) - 1)
    def _finish():
        lo = CONV_HIST + l_blk - (SSD_CONV - 1)
        conv_out_ref[0] = ext_ref[lo:lo + SSD_CONV - 1, :]
        for j in range(n_tile):
            ssm_out_ref[0, j * LANES:(j + 1) * LANES, :] = state_ref[:, j * LANES:(j + 1) * LANES].T

    ext_ref[0:CONV_HIST, :] = ext_ref[T:T + CONV_HIST, :]


def _ssd(zx3, dt3, consts, init, y_dtype):
    b, L, _ = zx3.shape
    T = SSD_CHUNK
    l_blk = T if L % T == 0 else L
    n_chunks = L // l_blk
    has_init = init is not None
    full = lambda a: pl.BlockSpec(a.shape, lambda i, c: (0,) * a.ndim)
    in_specs = [pl.BlockSpec((1, l_blk, ZX_WIDTH), lambda i, c: (i, c, 0)),
                pl.BlockSpec((1, l_blk, DT_PAD), lambda i, c: (i, c, 0))] + [full(a) for a in consts]
    args = [zx3, dt3, *consts]
    scratch = [pltpu.VMEM((T + CONV_HIST, CONV_DIM), F32),
               pltpu.VMEM((T, CONV_DIM), F32),
               pltpu.VMEM((SSD_STATE, SSD_INNER), F32),
               pltpu.VMEM((LANES, T), F32),
               pltpu.VMEM((T, SSD_INNER), F32)]
    if has_init:
        in_specs += [pl.BlockSpec((1, SSD_CONV - 1, CONV_DIM), lambda i, c: (i, 0, 0)),
                     pl.BlockSpec((1, SSD_INNER, SSD_STATE), lambda i, c: (i, 0, 0))]
        args += list(init)
        scratch += [pltpu.VMEM((T, SSD_INNER), F32), pltpu.VMEM((T, DT_PAD), F32)]
    return pl.pallas_call(
        functools.partial(_ssd_body, l_blk=l_blk, has_init=has_init),
        grid=(b, n_chunks),
        in_specs=in_specs,
        out_specs=[pl.BlockSpec((1, l_blk, SSD_INNER), lambda i, c: (i, c, 0)),
                   pl.BlockSpec((1, SSD_CONV - 1, CONV_DIM), lambda i, c: (i, 0, 0)),
                   pl.BlockSpec((1, SSD_INNER, SSD_STATE), lambda i, c: (i, 0, 0))],
        out_shape=[jax.ShapeDtypeStruct((b, L, SSD_INNER), y_dtype),
                   jax.ShapeDtypeStruct((b, SSD_CONV - 1, CONV_DIM), F32),
                   jax.ShapeDtypeStruct((b, SSD_INNER, SSD_STATE), F32)],
        scratch_shapes=scratch,
        compiler_params=_cparams(("parallel", "arbitrary")),
        name="ssd_init" if has_init else "ssd",
    )(*args)


def _attn_body(slopes_ref, q_ref, k_ref, v_ref, o_ref, op_ref, lp_ref, b2_ref, b1_ref, *, seq):
    B = ATT_BLOCK
    hp = pl.program_id(1)
    lane = lax.broadcasted_iota(jnp.int32, (B, LANES), 1)
    even = lane < ATT_HEAD_DIM
    scale = ATT_HEAD_DIM ** -0.5

    d2 = (lax.broadcasted_iota(jnp.int32, (B, 2 * B), 0) + B - lax.broadcasted_iota(jnp.int32, (B, 2 * B), 1))
    d1 = lax.broadcasted_iota(jnp.int32, (B, B), 0) - lax.broadcasted_iota(jnp.int32, (B, B), 1)
    for e in range(2):
        slope = slopes_ref[2 * hp + e]
        for p, (window, dil) in enumerate(DILATED_PATTERNS):
            span = window // dil
            if seq // dil > B:
                b2_ref[e, p] = jnp.where((d2 >= 0) & (d2 <= span), -slope * (dil * d2).astype(F32), NEG_INF)
            b1_ref[e, p] = jnp.where((d1 >= 0) & (d1 <= span), -slope * (dil * d1).astype(F32), NEG_INF)

    def block(p, dil, qs, ks, nk, bias_ref):
        def rows(start, n):
            return pl.ds(start, n) if dil == 1 else pl.ds(start, n, stride=dil)
        qb = q_ref[rows(qs, B), :] * scale
        kb = k_ref[rows(ks, nk), :].astype(BF16)
        vb = v_ref[rows(ks, nk), :].astype(BF16)
        outs, lses = [], []
        for e in range(2):
            qe = jnp.where(even if e == 0 else jnp.logical_not(even), qb, 0.0).astype(BF16)
            s = _dot_nt(qe, kb) + bias_ref[e, p]
            m = jnp.max(s, axis=-1, keepdims=True)
            pr = jnp.exp(s - m)
            l = jnp.sum(pr, axis=-1, keepdims=True)
            outs.append(_dot(pr.astype(BF16), vb) / l)
            lses.append(m + jnp.log(l))
        op_ref[p, rows(qs, B), :] = jnp.where(even, outs[0], outs[1])
        lp_ref[p, rows(qs, B), :] = jnp.where(even, lses[0], lses[1])

    for p, (_, dil) in enumerate(DILATED_PATTERNS):
        sc = seq // dil
        nb = sc // B

        def first(r, carry, p=p, dil=dil):
            block(p, dil, r, r, B, b1_ref)
            return carry
        lax.fori_loop(0, dil, first, 0)
        if nb > 1:
            def later(i, carry, p=p, dil=dil, nb=nb):
                r = i // (nb - 1)
                n = i % (nb - 1) + 1
                block(p, dil, r + dil * B * n, r + dil * B * (n - 1), 2 * B, b2_ref)
                return carry
            lax.fori_loop(0, dil * (nb - 1), later, 0)

    def merge(i, carry):
        sl = pl.ds(pl.multiple_of(i * B, B), B)
        l0, l1, l2 = lp_ref[0, sl, :], lp_ref[1, sl, :], lp_ref[2, sl, :]
        m = jnp.maximum(jnp.maximum(l0, l1), l2)
        w0, w1, w2 = jnp.exp(l0 - m), jnp.exp(l1 - m), jnp.exp(l2 - m)
        num = w0 * op_ref[0, sl, :] + w1 * op_ref[1, sl, :] + w2 * op_ref[2, sl, :]
        o_ref[sl, :] = num / (w0 + w1 + w2)
        return carry
    lax.fori_loop(0, seq // B, merge, 0)


def _attn_prompt(q, k, v, slopes, batch, seq):
    n_hp = ATT_HEADS // 2
    blk = pl.BlockSpec((seq, LANES), lambda b, h: (b, h))
    n_pat = len(DILATED_PATTERNS)
    return pl.pallas_call(
        functools.partial(_attn_body, seq=seq),
        grid=(batch, n_hp),
        in_specs=[pl.BlockSpec(memory_space=pltpu.SMEM), blk, blk, blk],
        out_specs=blk,
        out_shape=jax.ShapeDtypeStruct((batch * seq, ATT_WIDTH), F32),
        scratch_shapes=[pltpu.VMEM((n_pat, seq, LANES), F32), pltpu.VMEM((n_pat, seq, LANES), F32),
                        pltpu.VMEM((2, n_pat, ATT_BLOCK, 2 * ATT_BLOCK), F32),
                        pltpu.VMEM((2, n_pat, ATT_BLOCK, ATT_BLOCK), F32)],
        compiler_params=_cparams(("parallel", "parallel")),
        name="attn_prompt",
    )(slopes, q, k, v)


DEC_KEYS = ATT_BLOCK + SUBLANES


def _decode_tables(t_new, m_cache):
    slopes = 2.0 ** (-8.0 * np.arange(1, ATT_HEADS + 1) / ATT_HEADS)
    bias_all = np.full((t_new * ATT_HEADS, DEC_KEYS), -np.inf, np.float32)
    for t in range(t_new):
        for a in range(ATT_BLOCK + t_new):
            dist = ATT_BLOCK + t - a
            if 0 <= dist <= DILATED_PATTERNS[0][0]:
                bias_all[t * ATT_HEADS:(t + 1) * ATT_HEADS, a] = -slopes * dist
    bias_dil = np.full((len(DILATED_PATTERNS) - 1, ATT_HEADS, DEC_KEYS), -np.inf, np.float32)
    for p, (window, dil) in enumerate(DILATED_PATTERNS[1:]):
        for a in range(ATT_BLOCK + 1):
            dist = window - dil * a if a < ATT_BLOCK else 0
            bias_dil[p, :, a] = -slopes * dist
    return jnp.asarray(bias_all), jnp.asarray(bias_dil)


def _attn_dec_body(q_ref, kn_ref, vn_ref, ck_ref, cv_ref, ball_ref, bdil_ref, o_ref, kbuf_ref, vbuf_ref,
                   *, t_new, m_cache):
    H = ATT_HEADS
    scale = ATT_HEAD_DIM ** -0.5
    lane_head = lax.broadcasted_iota(jnp.int32, (H, ATT_WIDTH), 1) // ATT_HEAD_DIM
    own = lane_head == lax.broadcasted_iota(jnp.int32, (H, ATT_WIDTH), 0)

    def q_rows(t):
        return jnp.where(own, jnp.broadcast_to(q_ref[0, t:t + 1, :] * scale, (H, ATT_WIDTH)), 0.0).astype(BF16)

    def attend(qe, bias):
        s = _dot_nt(qe, kbuf_ref[...].astype(BF16)) + bias
        m = jnp.max(s, axis=-1, keepdims=True)
        pr = jnp.exp(s - m)
        l = jnp.sum(pr, axis=-1, keepdims=True)
        return _dot(pr.astype(BF16), vbuf_ref[...].astype(BF16)) / l, m + jnp.log(l)

    zeros_tail = jnp.zeros((DEC_KEYS - ATT_BLOCK, ATT_WIDTH), F32)
    n_lt = ATT_WIDTH // LANES

    def load_cache(start, dil):
        for j in range(n_lt):
            rows = pl.ds(start * n_lt + j, ATT_BLOCK, stride=dil * n_lt)
            kbuf_ref[0:ATT_BLOCK, j * LANES:(j + 1) * LANES] = ck_ref[0, rows, :]
            vbuf_ref[0:ATT_BLOCK, j * LANES:(j + 1) * LANES] = cv_ref[0, rows, :]

    load_cache(m_cache - ATT_BLOCK, 1)
    kbuf_ref[ATT_BLOCK:DEC_KEYS, :] = zeros_tail
    vbuf_ref[ATT_BLOCK:DEC_KEYS, :] = zeros_tail
    kbuf_ref[ATT_BLOCK:ATT_BLOCK + t_new, :] = kn_ref[0]
    vbuf_ref[ATT_BLOCK:ATT_BLOCK + t_new, :] = vn_ref[0]
    outs = [[None] * len(DILATED_PATTERNS) for _ in range(t_new)]
    for t in range(t_new):
        outs[t][0] = attend(q_rows(t), ball_ref[t * H:(t + 1) * H, :])
    for p, (window, dil) in enumerate(DILATED_PATTERNS[1:]):
        for t in range(t_new):
            start = m_cache + t - window
            load_cache(start, dil)
            kbuf_ref[ATT_BLOCK:DEC_KEYS, :] = zeros_tail
            vbuf_ref[ATT_BLOCK:DEC_KEYS, :] = zeros_tail
            kbuf_ref[ATT_BLOCK:ATT_BLOCK + 1, :] = kn_ref[0, t:t + 1, :]
            vbuf_ref[ATT_BLOCK:ATT_BLOCK + 1, :] = vn_ref[0, t:t + 1, :]
            outs[t][p + 1] = attend(q_rows(t), bdil_ref[p])
    for t in range(t_new):
        lses = [o[1] for o in outs[t]]
        m = functools.reduce(jnp.maximum, lses)
        ws = [jnp.exp(l - m) for l in lses]
        num = sum(w * o[0] for w, o in zip(ws, outs[t]))
        merged = num / sum(ws)
        o_ref[0, t:t + 1, :] = jnp.sum(jnp.where(own, merged, 0.0), axis=0, keepdims=True)


def _attn_decode(q3, kn3, vn3, cache_k, cache_v):
    b, t_new, _ = q3.shape
    m_cache = cache_k.shape[1]
    assert m_cache >= DILATED_PATTERNS[-1][0] and t_new <= DILATED_PATTERNS[1][1] and t_new <= SUBLANES
    ball, bdil = _decode_tables(t_new, m_cache)
    new = pl.BlockSpec((1, t_new, ATT_WIDTH), lambda i: (i, 0, 0))
    n_lt = ATT_WIDTH // LANES
    cache_k = cache_k.reshape(b, m_cache * n_lt, LANES)
    cache_v = cache_v.reshape(b, m_cache * n_lt, LANES)
    cache = pl.BlockSpec((1, m_cache * n_lt, LANES), lambda i: (i, 0, 0))
    full = lambda a: pl.BlockSpec(a.shape, lambda i: (0,) * a.ndim)
    return pl.pallas_call(
        functools.partial(_attn_dec_body, t_new=t_new, m_cache=m_cache),
        grid=(b,),
        in_specs=[new, new, new, cache, cache, full(ball), full(bdil)],
        out_specs=new,
        out_shape=jax.ShapeDtypeStruct((b, t_new, ATT_WIDTH), F32),
        scratch_shapes=[pltpu.VMEM((DEC_KEYS, ATT_WIDTH), F32), pltpu.VMEM((DEC_KEYS, ATT_WIDTH), F32)],
        compiler_params=_cparams(("parallel",)),
        name="attn_decode",
    )(q3, kn3, vn3, cache_k, cache_v, ball, bdil)


def _out_proj_body(x_ref, y_ref, o_ref, an_ref, wy_ref, wo_ref, nf_ref, rw_ref, rb_ref, cnt0_ref,
                   h_ref, xn_ref, idx_ref, gate_ref, rank_ref, cnt_ref, carry_ref, *, tm):
    i = pl.program_id(0)

    @pl.when(i == 0)
    def _start():
        carry_ref[...] = cnt0_ref[...]

    on = _rms(o_ref[...], an_ref[...]).astype(BF16)
    mixed = _dot(y_ref[...].astype(BF16), wy_ref[...]) + _dot(on, wo_ref[...])
    h = x_ref[...] + mixed
    h_ref[...] = h
    xn = _rms(h, nf_ref[...])
    xn_ref[...] = xn
    logits = _dot_nt(rw_ref[...], xn.astype(BF16)) + rb_ref[...]

    e_iota = lax.broadcasted_iota(jnp.int32, (N_EXPERTS, tm), 0)
    vals, idxs, sels = [], [], []
    cur = logits
    for _ in range(TOP_K):
        mx = jnp.max(cur, axis=0, keepdims=True)
        ix = jnp.min(jnp.where(cur == mx, e_iota, N_EXPERTS), axis=0, keepdims=True)
        sel = e_iota == ix
        cur = jnp.where(sel, NEG_INF, cur)
        vals.append(mx)
        idxs.append(ix)
        sels.append(sel)
    ex = [jnp.exp(v - vals[0]) for v in vals]
    den = functools.reduce(lambda a, b: a + b, ex)

    sel_any = functools.reduce(jnp.logical_or, sels)
    sel_f = jnp.where(sel_any, 1.0, 0.0)
    r = lax.broadcasted_iota(jnp.int32, (tm, tm), 0)
    c = lax.broadcasted_iota(jnp.int32, (tm, tm), 1)
    upper = jnp.where(r <= c, 1.0, 0.0).astype(BF16)
    cum = _dot(sel_f.astype(BF16), upper)
    before = carry_ref[:, 0:1] + cum - sel_f
    for k in range(TOP_K):
        idx_ref[k:k + 1, :] = idxs[k]
        gate_ref[k:k + 1, :] = ex[k] / den
        rank_ref[k:k + 1, :] = jnp.sum(jnp.where(sels[k], before, 0.0), axis=0, keepdims=True).astype(jnp.int32)
    idx_ref[TOP_K:SUBLANES, :] = jnp.zeros((SUBLANES - TOP_K, tm), jnp.int32)
    gate_ref[TOP_K:SUBLANES, :] = jnp.zeros((SUBLANES - TOP_K, tm), F32)
    rank_ref[TOP_K:SUBLANES, :] = jnp.zeros((SUBLANES - TOP_K, tm), jnp.int32)
    carry_ref[...] = carry_ref[...] + jnp.max(cum, axis=1, keepdims=True)
    cnt_ref[...] = carry_ref[...]


def _out_proj(x2d, y2d, o2d, consts, cnt0, tm):
    n = x2d.shape[0]
    tm = min(tm, n)
    row = lambda w: pl.BlockSpec((tm, w), lambda i: (i, 0))
    colb = pl.BlockSpec((SUBLANES, tm), lambda i: (0, i))
    full = lambda a: pl.BlockSpec(a.shape, lambda i: (0,) * a.ndim)
    return pl.pallas_call(
        functools.partial(_out_proj_body, tm=tm),
        grid=(n // tm,),
        in_specs=[row(D_MODEL), row(SSD_INNER), row(ATT_WIDTH)] + [full(a) for a in consts] + [full(cnt0)],
        out_specs=[row(D_MODEL), row(D_MODEL), colb, colb, colb, full(cnt0)],
        out_shape=[jax.ShapeDtypeStruct((n, D_MODEL), F32), jax.ShapeDtypeStruct((n, D_MODEL), F32),
                   jax.ShapeDtypeStruct((SUBLANES, n), jnp.int32), jax.ShapeDtypeStruct((SUBLANES, n), F32),
                   jax.ShapeDtypeStruct((SUBLANES, n), jnp.int32), jax.ShapeDtypeStruct(cnt0.shape, F32)],
        scratch_shapes=[pltpu.VMEM(cnt0.shape, F32)],
        compiler_params=_cparams(("arbitrary",)),
        name="out_proj",
    )(x2d, y2d, o2d, *consts, cnt0)


def _dispatch_body(dest_ref, xn_ref, xs_in_ref, xs_ref, sem, *, tm):
    del xs_in_ref

    def issue(t, carry):
        for k in range(TOP_K):
            pltpu.make_async_copy(xn_ref.at[pl.ds(t, 1)], xs_ref.at[pl.ds(dest_ref[k, t], 1)], sem).start()
        return carry
    lax.fori_loop(0, tm, issue, 0)
    for _ in range(TOP_K):
        pltpu.make_async_copy(xn_ref, xs_ref.at[pl.ds(0, tm)], sem).wait()


def _dispatch(dest, xn, xs, tm):
    n = xn.shape[0]
    tm = min(tm, n)
    return pl.pallas_call(
        functools.partial(_dispatch_body, tm=tm),
        grid=(n // tm,),
        in_specs=[pl.BlockSpec((SUBLANES, tm), lambda i: (0, i), memory_space=pltpu.SMEM),
                  pl.BlockSpec((tm, D_MODEL), lambda i: (i, 0)),
                  pl.BlockSpec(memory_space=pl.ANY)],
        out_specs=pl.BlockSpec(memory_space=pl.ANY),
        out_shape=jax.ShapeDtypeStruct(xs.shape, xs.dtype),
        scratch_shapes=[pltpu.SemaphoreType.DMA(())],
        input_output_aliases={2: 0},
        compiler_params=_cparams(("arbitrary",)),
        name="dispatch",
    )(dest, xn, xs)


def _experts_body(be_ref, bv_ref, xs_ref, w1g_ref, w1l_ref, b1g_ref, b1l_ref, w2_ref, b2_ref, ys_ref):
    j = pl.program_id(0)
    valid = bv_ref[j]

    @pl.when(valid > 0)
    def _compute():
        rows = lax.broadcasted_iota(jnp.int32, (MOE_ROWS, 1), 0)
        x = jnp.where(rows < valid, xs_ref[...], 0.0).astype(BF16)
        glu = jnp.minimum(_dot(x, w1g_ref[0]) + b1g_ref[0], SWIGLU_LIMIT)
        lin = jnp.clip(_dot(x, w1l_ref[0]) + b1l_ref[0], -SWIGLU_LIMIT, SWIGLU_LIMIT)
        g = glu * jax.nn.sigmoid(SWIGLU_ALPHA * glu) * (lin + 1.0)
        ys_ref[...] = _dot(g.astype(BF16), w2_ref[0]) + b2_ref[0]

    @pl.when(valid <= 0)
    def _empty():
        ys_ref[...] = jnp.zeros_like(ys_ref)


def _experts(block_expert, block_valid, xs, w1g, w1l, b1g, b1l, w2, b2):
    n_blocks = xs.shape[0] // MOE_ROWS
    wspec = lambda a: pl.BlockSpec((1,) + a.shape[1:], lambda j, be, bv: (be[j], 0, 0))
    grid_spec = pltpu.PrefetchScalarGridSpec(
        num_scalar_prefetch=2,
        grid=(n_blocks,),
        in_specs=[pl.BlockSpec((MOE_ROWS, D_MODEL), lambda j, be, bv: (j, 0)),
                  wspec(w1g), wspec(w1l), wspec(b1g), wspec(b1l), wspec(w2), wspec(b2)],
        out_specs=pl.BlockSpec((MOE_ROWS, D_MODEL), lambda j, be, bv: (j, 0)),
    )
    return pl.pallas_call(
        _experts_body,
        grid_spec=grid_spec,
        out_shape=jax.ShapeDtypeStruct(xs.shape, F32),
        compiler_params=_cparams(("arbitrary",)),
        name="experts",
    )(block_expert, block_valid, xs, w1g, w1l, b1g, b1l, w2, b2)


def _combine_body(dest_ref, gate_ref, h_ref, nf_ref, ys_ref, y_ref, buf_ref, sem, *, tm):
    def issue(t, carry):
        for k in range(TOP_K):
            pltpu.make_async_copy(ys_ref.at[pl.ds(dest_ref[k, t], 1)], buf_ref.at[k, pl.ds(t, 1)], sem).start()
        return carry
    lax.fori_loop(0, tm, issue, 0)
    for k in range(TOP_K):
        pltpu.make_async_copy(ys_ref.at[pl.ds(0, tm)], buf_ref.at[k], sem).wait()
    acc = h_ref[...]
    for k in range(TOP_K):
        acc = acc + gate_ref[:, k:k + 1] * buf_ref[k]
    y_ref[...] = _rms(acc, nf_ref[...])


def _combine(dest, gates_rows, h, norm_final, ys, tm):
    n = h.shape[0]
    tm = min(tm, n)
    return pl.pallas_call(
        functools.partial(_combine_body, tm=tm),
        grid=(n // tm,),
        in_specs=[pl.BlockSpec((SUBLANES, tm), lambda i: (0, i), memory_space=pltpu.SMEM),
                  pl.BlockSpec((tm, TOP_K), lambda i: (i, 0)),
                  pl.BlockSpec((tm, D_MODEL), lambda i: (i, 0)),
                  pl.BlockSpec((1, D_MODEL), lambda i: (0, 0)),
                  pl.BlockSpec(memory_space=pl.ANY)],
        out_specs=pl.BlockSpec((tm, D_MODEL), lambda i: (i, 0)),
        out_shape=jax.ShapeDtypeStruct((n, D_MODEL), F32),
        scratch_shapes=[pltpu.VMEM((TOP_K, tm, D_MODEL), F32), pltpu.SemaphoreType.DMA(())],
        compiler_params=_cparams(("arbitrary",)),
        name="combine",
    )(dest, gates_rows, h, norm_final, ys)


def _expansion(width):
    h = np.arange(LANES)[:, None]
    c = np.arange(SSD_HEADS * width)[None, :] // width
    return jnp.asarray((h == c).astype(np.float32), dtype=BF16)


def _pad_lanes(v):
    return jnp.pad(v.astype(F32), (0, LANES - v.shape[0]))[None, :]


def kernel(x_prompt, x_sample, state_conv, state_ssm, cache_win_k, cache_win_v, norm_mix, w_in, conv_w, conv_b,
           dt_bias, a_log, d_skip, ssd_norm, att_norm, w_out, norm_ffn, router_w, router_b, w1, b1, w2, b2,
           norm_final):
    depth = w_in.shape[0]
    assert depth == 1
    bp, seq, _ = x_prompt.shape
    bs, t_new, _ = x_sample.shape
    n_p, n_s = bp * seq, bs * t_new
    l = 0

    o_dt = SSD_INNER + CONV_DIM
    o_q = o_dt + SSD_HEADS
    wl = w_in[l]
    w_cat = jnp.concatenate([wl[:, :o_dt], wl[:, o_q:], jnp.pad(wl[:, o_dt:o_q], ((0, 0), (0, DT_PAD - SSD_HEADS)))],
                            axis=1).astype(BF16)
    g_mix = norm_mix[l][None, :]
    ssd_consts = (conv_w[l], conv_b[l][None, :], _pad_lanes(dt_bias[l]), _pad_lanes(a_log[l]),
                  jnp.repeat(d_skip[l], SSD_HEAD_DIM)[None, :], ssd_norm[l][None, :],
                  _expansion(SSD_HEAD_DIM), _expansion(LANES))
    slopes = jnp.exp2(-8.0 * jnp.arange(1, ATT_HEADS + 1, dtype=F32) / ATT_HEADS)
    out_consts = (att_norm[l][None, :], w_out[l][:SSD_INNER].astype(BF16), w_out[l][SSD_INNER:].astype(BF16),
                  norm_ffn[l][None, :], router_w[l].T.astype(BF16), router_b[l][:, None].astype(F32))
    w1g = w1[l][:, :, 0::2].astype(BF16)
    w1l = w1[l][:, :, 1::2].astype(BF16)
    b1g = b1[l][:, None, 0::2]
    b1l = b1[l][:, None, 1::2]
    w2b = w2[l].astype(BF16)
    b2r = b2[l][:, None, :]

    zx, q, k, v, dt = _in_proj(x_prompt.reshape(n_p, D_MODEL), g_mix, w_cat, 256)
    y_p, conv_p, ssm_p = _ssd(zx.reshape(bp, seq, ZX_WIDTH), dt.reshape(bp, seq, DT_PAD), ssd_consts, None, BF16)
    o_p = _attn_prompt(q, k, v, slopes, bp, seq)
    keep = min(DILATED_PATTERNS[-1][0], seq)
    k_p = k.reshape(bp, seq, ATT_HEADS, ATT_HEAD_DIM)[:, seq - keep:]
    v_p = v.reshape(bp, seq, ATT_HEADS, ATT_HEAD_DIM)[:, seq - keep:]

    zx_s, q_s, k_s, v_s, dt_s = _in_proj(x_sample.reshape(n_s, D_MODEL), g_mix, w_cat, 256)
    init = (state_conv[l], state_ssm[l].reshape(bs, SSD_INNER, SSD_STATE))
    y_s, conv_s, ssm_s = _ssd(zx_s.reshape(bs, t_new, ZX_WIDTH), dt_s.reshape(bs, t_new, DT_PAD), ssd_consts, init, F32)
    m_cache = cache_win_k.shape[2]
    o_s = _attn_decode(q_s.reshape(bs, t_new, ATT_WIDTH), k_s.reshape(bs, t_new, ATT_WIDTH),
                       v_s.reshape(bs, t_new, ATT_WIDTH), cache_win_k[l].reshape(bs, m_cache, ATT_WIDTH),
                       cache_win_v[l].reshape(bs, m_cache, ATT_WIDTH))

    cnt0 = jnp.zeros((N_EXPERTS, LANES), F32)
    h_p, xn_p, idx_p, gate_p, rank_p, cnt_p = _out_proj(x_prompt.reshape(n_p, D_MODEL), y_p.reshape(n_p, SSD_INNER),
                                                        o_p, out_consts, cnt0, 256)
    h_s, xn_s, idx_s, gate_s, rank_s, cnt_all = _out_proj(x_sample.reshape(n_s, D_MODEL), y_s.reshape(n_s, SSD_INNER),
                                                          o_s.reshape(n_s, ATT_WIDTH), out_consts, cnt_p, 256)

    counts = cnt_all[:, 0].astype(jnp.int32)
    padded = (counts + MOE_ROWS - 1) // MOE_ROWS * MOE_ROWS
    pad_end = jnp.cumsum(padded)
    pad_start = pad_end - padded
    n_blocks = -(-((n_p + n_s) * TOP_K) // MOE_ROWS) + N_EXPERTS
    blk0 = jnp.arange(n_blocks, dtype=jnp.int32) * MOE_ROWS
    block_expert = jnp.minimum(jnp.searchsorted(pad_end, blk0, side="right"), N_EXPERTS - 1).astype(jnp.int32)
    block_valid = jnp.clip(counts[block_expert] - (blk0 - pad_start[block_expert]), 0, MOE_ROWS).astype(jnp.int32)
    dest_p = pad_start[idx_p] + rank_p
    dest_s = pad_start[idx_s] + rank_s

    xs = jnp.zeros((n_blocks * MOE_ROWS, D_MODEL), F32)
    xs = _dispatch(dest_p, xn_p, xs, 256)
    xs = _dispatch(dest_s, xn_s, xs, 256)
    ys = _experts(block_expert, block_valid, xs, w1g, w1l, b1g, b1l, w2b, b2r)
    nfin = norm_final[None, :]
    y_prompt = _combine(dest_p, gate_p[:TOP_K].T, h_p, nfin, ys, 256)
    y_sample = _combine(dest_s, gate_s[:TOP_K].T, h_s, nfin, ys, 256)

    return (y_prompt.reshape(bp, seq, D_MODEL), y_sample.reshape(bs, t_new, D_MODEL),
            conv_p[None], ssm_p.reshape(1, bp, SSD_HEADS, SSD_HEAD_DIM, SSD_STATE), k_p[None], v_p[None],
            conv_s[None], ssm_s.reshape(1, bs, SSD_HEADS, SSD_HEAD_DIM, SSD_STATE),
            k_s.reshape(1, bs, t_new, ATT_HEADS, ATT_HEAD_DIM), v_s.reshape(1, bs, t_new, ATT_HEADS, ATT_HEAD_DIM))
```

```python
import functools

import jax
import jax.numpy as jnp
import numpy as np
from jax import lax
from jax.experimental import pallas as pl
from jax.experimental.pallas import tpu as pltpu

F32 = jnp.float32
BF16 = jnp.bfloat16

D_MODEL = 1024
SSD_HEADS = 16
SSD_HEAD_DIM = 64
SSD_INNER = SSD_HEADS * SSD_HEAD_DIM
SSD_GROUPS = 2
SSD_STATE = 128
SSD_CONV = 4
SSD_CHUNK = 128
CONV_DIM = SSD_INNER + 2 * SSD_GROUPS * SSD_STATE
ATT_HEADS = 8
ATT_HEAD_DIM = 64
ATT_WIDTH = ATT_HEADS * ATT_HEAD_DIM
DILATED_PATTERNS = ((128, 1), (512, 4), (2048, 16))
ATT_BLOCK = 128
N_EXPERTS = 32
TOP_K = 4
SWIGLU_LIMIT = 7.0
SWIGLU_ALPHA = 1.702
NORM_EPS = 1e-5

LANES = 128
SUBLANES = 8
ZX_WIDTH = SSD_INNER + CONV_DIM
DT_PAD = LANES
W_CAT = ZX_WIDTH + 3 * ATT_WIDTH + DT_PAD
CONV_HIST = SUBLANES
MOE_ROWS = 512
VMEM_LIMIT = 56 * 1024 * 1024

NEG_INF = float("-inf")


def _cparams(sem):
    return pltpu.CompilerParams(dimension_semantics=sem, vmem_limit_bytes=VMEM_LIMIT)


def _rms(x, g):
    return x * lax.rsqrt(jnp.mean(x * x, axis=-1, keepdims=True) + NORM_EPS) * g


def _dot(a, b):
    return jnp.dot(a, b, preferred_element_type=F32)


def _dot_nt(a, b):
    return lax.dot_general(a, b, (((1,), (1,)), ((), ())), preferred_element_type=F32)


def _split3(v):
    hi = v.astype(BF16)
    r1 = v - hi.astype(F32)
    mid = r1.astype(BF16)
    lo = (r1 - mid.astype(F32)).astype(BF16)
    return hi, mid, lo


def _dot3(v, m):
    hi, mid, lo = _split3(v)
    return _dot(hi, m) + _dot(mid, m) + _dot(lo, m)


def _dot3_lhs(m, v):
    hi, mid, lo = _split3(v)
    return _dot(m, hi) + _dot(m, mid) + _dot(m, lo)


def _silu(x):
    return x * jax.nn.sigmoid(x)


def _in_proj_body(x_ref, g_ref, w_ref, zx_ref, q_ref, k_ref, v_ref, dt_ref, *t_refs):
    hn = _rms(x_ref[...], g_ref[...]).astype(BF16)
    o = 0
    for ref, width in ((zx_ref, ZX_WIDTH), (q_ref, ATT_WIDTH), (k_ref, ATT_WIDTH), (v_ref, ATT_WIDTH),
                       (dt_ref, DT_PAD)):
        ref[...] = _dot(hn, w_ref[:, o:o + width])
        o += width
    for src, dst in zip((k_ref, v_ref), t_refs):
        for j in range(ATT_WIDTH // LANES):
            dst[0, j * LANES:(j + 1) * LANES, :] = src[:, j * LANES:(j + 1) * LANES].T


def _in_proj(x2d, g, w_cat, tm, seq=None):
    n = x2d.shape[0]
    tm = min(tm, n)
    per = 1 if seq is None else seq // tm
    row = lambda w: pl.BlockSpec((tm, w), lambda b, i: (b * per + i, 0))
    full = lambda a: pl.BlockSpec(a.shape, lambda b, i: (0,) * a.ndim)
    widths = (ZX_WIDTH, ATT_WIDTH, ATT_WIDTH, ATT_WIDTH, DT_PAD)
    out_specs = [row(w) for w in widths]
    out_shape = [jax.ShapeDtypeStruct((n, w), F32) for w in widths]
    if seq is not None:
        out_specs += [pl.BlockSpec((1, ATT_WIDTH, tm), lambda b, i: (b, 0, i))] * 2
        out_shape += [jax.ShapeDtypeStruct((n // seq, ATT_WIDTH, seq), F32)] * 2
    return pl.pallas_call(
        _in_proj_body,
        grid=(n // (tm * per), per),
        in_specs=[row(D_MODEL), full(g), full(w_cat)],
        out_specs=out_specs,
        out_shape=out_shape,
        compiler_params=_cparams(("parallel", "parallel")),
        name="in_proj",
    )(x2d, g, w_cat)


def _ssd_body(*refs, l_blk, has_init):
    T = SSD_CHUNK
    if has_init:
        (zx_ref, dt_ref, cw_ref, cb_ref, dtb_ref, alog_ref, dskip_ref, norm_ref, e64_ref, e128_ref,
         cinit_ref, sinit_ref, y_ref, conv_out_ref, ssm_out_ref,
         ext_ref, act_ref, state_ref, cst_ref, ybuf_ref, zpad_ref, dtpad_ref) = refs
    else:
        (zx_ref, dt_ref, cw_ref, cb_ref, dtb_ref, alog_ref, dskip_ref, norm_ref, e64_ref, e128_ref,
         y_ref, conv_out_ref, ssm_out_ref,
         ext_ref, act_ref, state_ref, cst_ref, ybuf_ref) = refs
    c = pl.program_id(1)
    n_tile = SSD_INNER // LANES

    @pl.when(c == 0)
    def _start():
        if has_init:
            ext_ref[0:CONV_HIST, :] = jnp.zeros((CONV_HIST, CONV_DIM), F32)
            ext_ref[CONV_HIST - (SSD_CONV - 1):CONV_HIST, :] = cinit_ref[0]
            for j in range(n_tile):
                state_ref[:, j * LANES:(j + 1) * LANES] = sinit_ref[0, j * LANES:(j + 1) * LANES, :].T
        else:
            ext_ref[0:CONV_HIST, :] = jnp.zeros((CONV_HIST, CONV_DIM), F32)
            state_ref[...] = jnp.zeros_like(state_ref)

    if l_blk == T:
        ext_ref[CONV_HIST:CONV_HIST + T, :] = zx_ref[0, :, SSD_INNER:ZX_WIDTH]
        z_of = lambda sl: zx_ref[0, :, sl]
        dt_raw = dt_ref[0]
    else:
        ext_ref[CONV_HIST:CONV_HIST + T, :] = jnp.zeros((T, CONV_DIM), F32)
        ext_ref[CONV_HIST:CONV_HIST + l_blk, :] = zx_ref[0, :, SSD_INNER:ZX_WIDTH]
        zpad_ref[...] = jnp.zeros_like(zpad_ref)
        zpad_ref[0:l_blk, :] = zx_ref[0, :, 0:SSD_INNER]
        dtpad_ref[...] = jnp.zeros_like(dtpad_ref)
        dtpad_ref[0:l_blk, :] = dt_ref[0]
        z_of = lambda sl: zpad_ref[:, sl]
        dt_raw = dtpad_ref[...]

    cw = CONV_DIM // 3
    for cc in range(3):
        sl = slice(cc * cw, (cc + 1) * cw)
        acc = cb_ref[:, sl]
        for j in range(SSD_CONV):
            o = CONV_HIST - (SSD_CONV - 1) + j
            acc = acc + ext_ref[o:o + T, sl] * cw_ref[j:j + 1, sl]
        act_ref[:, sl] = _silu(acc)

    row = lax.broadcasted_iota(jnp.int32, (T, LANES), 0)
    col = lax.broadcasted_iota(jnp.int32, (T, LANES), 1)
    tri = row >= col
    tri_bf = jnp.where(tri, 1.0, 0.0).astype(BF16)
    even = col < SSD_HEAD_DIM

    xdt = dt_raw + dtb_ref[...]
    dtv = jnp.maximum(xdt, 0.0) + jnp.log1p(jnp.exp(-jnp.abs(xdt)))
    if l_blk < T:
        dtv = jnp.where(row < l_blk, dtv, 0.0)
    d_a = dtv * (-jnp.exp(alog_ref[...]))
    cs = _dot3_lhs(tri_bf, d_a)
    cs_last = cs[T - 1:T, :]
    dec = jnp.exp(cs_last - cs)
    ecs = jnp.exp(cs)
    cst_ref[...] = cs.T
    per_head = jnp.concatenate([dtv, dtv * dec, ecs], axis=0)
    ex = _dot3(per_head, e64_ref[...])
    dt_ex, dd_ex, ecs_ex = ex[0:T], ex[T:2 * T], ex[2 * T:3 * T]
    colb = _dot3(cs, e128_ref[...])

    gw = SSD_INNER // SSD_GROUPS
    heads_per_group = SSD_HEADS // SSD_GROUPS
    for g in range(SSD_GROUPS):
        gsl = slice(g * gw, (g + 1) * gw)
        b_g = act_ref[:, SSD_INNER + g * SSD_STATE:SSD_INNER + (g + 1) * SSD_STATE]
        c_off = SSD_INNER + SSD_GROUPS * SSD_STATE
        c_g = act_ref[:, c_off + g * SSD_STATE:c_off + (g + 1) * SSD_STATE].astype(BF16)
        cb = _dot_nt(c_g, b_g.astype(BF16))
        b_gt = b_g.T.astype(BF16)
        x_g = act_ref[:, gsl]
        x_dt = (x_g * dt_ex[:, gsl]).astype(BF16)
        x_dd = (x_g * dd_ex[:, gsl]).astype(BF16)
        st_old = state_ref[:, gsl]
        y_off = _dot(c_g, st_old.astype(BF16)) * ecs_ex[:, gsl]
        state_ref[:, gsl] = st_old * ecs_ex[T - 1:T, gsl] + _dot(b_gt, x_dd)
        for jp in range(heads_per_group // 2):
            h0 = g * heads_per_group + 2 * jp
            psl = slice(jp * LANES, (jp + 1) * LANES)
            yd = []
            for h in (h0, h0 + 1):
                seg = colb[:, h * LANES:(h + 1) * LANES] - cst_ref[h:h + 1, :]
                lmat = jnp.exp(jnp.where(tri, seg, NEG_INF))
                yd.append(_dot((cb * lmat).astype(BF16), x_dt[:, psl]))
            y_pair = jnp.where(even, yd[0], yd[1]) + y_off[:, psl]
            osl = slice(g * gw + jp * LANES, g * gw + (jp + 1) * LANES)
            ybuf_ref[:, osl] = y_pair + dskip_ref[:, osl] * x_g[:, psl]

    for g in range(SSD_GROUPS):
        gsl = slice(g * gw, (g + 1) * gw)
        yg = ybuf_ref[:, gsl] * _silu(z_of(gsl))
        yn = yg * lax.rsqrt(jnp.mean(yg * yg, axis=-1, keepdims=True) + NORM_EPS) * norm_ref[:, gsl]
        y_ref[0, :, gsl] = yn[0:l_blk].astype(y_ref.dtype)

    @pl.when(c == pl.num_programs(1) - 1)
    def _finish():
        lo = CONV_HIST + l_blk - (SSD_CONV - 1)
        conv_out_ref[0] = ext_ref[lo:lo + SSD_CONV - 1, :]
        for j in range(n_tile):
            ssm_out_ref[0, j * LANES:(j + 1) * LANES, :] = state_ref[:, j * LANES:(j + 1) * LANES].T

    ext_ref[0:CONV_HIST, :] = ext_ref[T:T + CONV_HIST, :]


def _ssd(zx3, dt3, consts, init, y_dtype):
    b, L, _ = zx3.shape
    T = SSD_CHUNK
    l_blk = T if L % T == 0 else L
    n_chunks = L // l_blk
    has_init = init is not None
    full = lambda a: pl.BlockSpec(a.shape, lambda i, c: (0,) * a.ndim)
    in_specs = [pl.BlockSpec((1, l_blk, ZX_WIDTH), lambda i, c: (i, c, 0)),
                pl.BlockSpec((1, l_blk, DT_PAD), lambda i, c: (i, c, 0))] + [full(a) for a in consts]
    args = [zx3, dt3, *consts]
    scratch = [pltpu.VMEM((T + CONV_HIST, CONV_DIM), F32),
               pltpu.VMEM((T, CONV_DIM), F32),
               pltpu.VMEM((SSD_STATE, SSD_INNER), F32),
               pltpu.VMEM((LANES, T), F32),
               pltpu.VMEM((T, SSD_INNER), F32)]
    if has_init:
        in_specs += [pl.BlockSpec((1, SSD_CONV - 1, CONV_DIM), lambda i, c: (i, 0, 0)),
                     pl.BlockSpec((1, SSD_INNER, SSD_STATE), lambda i, c: (i, 0, 0))]
        args += list(init)
        scratch += [pltpu.VMEM((T, SSD_INNER), F32), pltpu.VMEM((T, DT_PAD), F32)]
    return pl.pallas_call(
        functools.partial(_ssd_body, l_blk=l_blk, has_init=has_init),
        grid=(b, n_chunks),
        in_specs=in_specs,
        out_specs=[pl.BlockSpec((1, l_blk, SSD_INNER), lambda i, c: (i, c, 0)),
                   pl.BlockSpec((1, SSD_CONV - 1, CONV_DIM), lambda i, c: (i, 0, 0)),
                   pl.BlockSpec((1, SSD_INNER, SSD_STATE), lambda i, c: (i, 0, 0))],
        out_shape=[jax.ShapeDtypeStruct((b, L, SSD_INNER), y_dtype),
                   jax.ShapeDtypeStruct((b, SSD_CONV - 1, CONV_DIM), F32),
                   jax.ShapeDtypeStruct((b, SSD_INNER, SSD_STATE), F32)],
        scratch_shapes=scratch,
        compiler_params=_cparams(("parallel", "arbitrary")),
        name="ssd_init" if has_init else "ssd",
    )(*args)


def _attn_body(slopes_ref, q_ref, k_ref, v_ref, o_ref, op_ref, lp_ref, b2_ref, b1_ref, *, seq):
    B = ATT_BLOCK
    hp = pl.program_id(1)
    lane = lax.broadcasted_iota(jnp.int32, (B, LANES), 1)
    even = lane < ATT_HEAD_DIM
    scale = ATT_HEAD_DIM ** -0.5

    d2 = (lax.broadcasted_iota(jnp.int32, (B, 2 * B), 0) + B - lax.broadcasted_iota(jnp.int32, (B, 2 * B), 1))
    d1 = lax.broadcasted_iota(jnp.int32, (B, B), 0) - lax.broadcasted_iota(jnp.int32, (B, B), 1)
    for e in range(2):
        slope = slopes_ref[2 * hp + e]
        for p, (window, dil) in enumerate(DILATED_PATTERNS):
            span = window // dil
            if seq // dil > B:
                b2_ref[e, p] = jnp.where((d2 >= 0) & (d2 <= span), -slope * (dil * d2).astype(F32), NEG_INF)
            b1_ref[e, p] = jnp.where((d1 >= 0) & (d1 <= span), -slope * (dil * d1).astype(F32), NEG_INF)

    def block(p, dil, qs, ks, nk, bias_ref):
        def rows(start, n):
            return pl.ds(start, n) if dil == 1 else pl.ds(start, n, stride=dil)
        qb = q_ref[rows(qs, B), :] * scale
        kb = k_ref[rows(ks, nk), :].astype(BF16)
        vb = v_ref[rows(ks, nk), :].astype(BF16)
        outs, lses = [], []
        for e in range(2):
            qe = jnp.where(even if e == 0 else jnp.logical_not(even), qb, 0.0).astype(BF16)
            s = _dot_nt(qe, kb) + bias_ref[e, p]
            m = jnp.max(s, axis=-1, keepdims=True)
            pr = jnp.exp(s - m)
            l = jnp.sum(pr, axis=-1, keepdims=True)
            outs.append(_dot(pr.astype(BF16), vb) / l)
            lses.append(m + jnp.log(l))
        op_ref[p, rows(qs, B), :] = jnp.where(even, outs[0], outs[1])
        lp_ref[p, rows(qs, B), :] = jnp.where(even, lses[0], lses[1])

    for p, (_, dil) in enumerate(DILATED_PATTERNS):
        sc = seq // dil
        nb = sc // B

        def first(r, carry, p=p, dil=dil):
            block(p, dil, r, r, B, b1_ref)
            return carry
        lax.fori_loop(0, dil, first, 0)
        if nb > 1:
            def later(i, carry, p=p, dil=dil, nb=nb):
                r = i // (nb - 1)
                n = i % (nb - 1) + 1
                block(p, dil, r + dil * B * n, r + dil * B * (n - 1), 2 * B, b2_ref)
                return carry
            lax.fori_loop(0, dil * (nb - 1), later, 0)

    def merge(i, carry):
        sl = pl.ds(pl.multiple_of(i * B, B), B)
        l0, l1, l2 = lp_ref[0, sl, :], lp_ref[1, sl, :], lp_ref[2, sl, :]
        m = jnp.maximum(jnp.maximum(l0, l1), l2)
        w0, w1, w2 = jnp.exp(l0 - m), jnp.exp(l1 - m), jnp.exp(l2 - m)
        num = w0 * op_ref[0, sl, :] + w1 * op_ref[1, sl, :] + w2 * op_ref[2, sl, :]
        o_ref[sl, :] = num / (w0 + w1 + w2)
        return carry
    lax.fori_loop(0, seq // B, merge, 0)


def _attn_prompt(q, k, v, slopes, batch, seq):
    n_hp = ATT_HEADS // 2
    blk = pl.BlockSpec((seq, LANES), lambda b, h: (b, h))
    n_pat = len(DILATED_PATTERNS)
    return pl.pallas_call(
        functools.partial(_attn_body, seq=seq),
        grid=(batch, n_hp),
        in_specs=[pl.BlockSpec(memory_space=pltpu.SMEM), blk, blk, blk],
        out_specs=blk,
        out_shape=jax.ShapeDtypeStruct((batch * seq, ATT_WIDTH), F32),
        scratch_shapes=[pltpu.VMEM((n_pat, seq, LANES), F32), pltpu.VMEM((n_pat, seq, LANES), F32),
                        pltpu.VMEM((2, n_pat, ATT_BLOCK, 2 * ATT_BLOCK), F32),
                        pltpu.VMEM((2, n_pat, ATT_BLOCK, ATT_BLOCK), F32)],
        compiler_params=_cparams(("parallel", "parallel")),
        name="attn_prompt",
    )(slopes, q, k, v)


DEC_ROWS = SUBLANES


def _decode_tables(t_new, m_cache):
    slopes = 2.0 ** (-8.0 * np.arange(1, ATT_HEADS + 1) / ATT_HEADS)

    def mult(dist):
        return sum(1 for window, dil in DILATED_PATTERNS if 0 <= dist <= window and dist % dil == 0)

    bias_c = np.zeros((ATT_HEADS, DEC_ROWS, m_cache), np.float32)
    mult_c = np.ones((DEC_ROWS, m_cache), np.float32)
    bias_n = np.full((ATT_HEADS, DEC_ROWS, LANES), -np.inf, np.float32)
    mult_n = np.zeros((DEC_ROWS, LANES), np.float32)
    for t in range(t_new):
        dist = m_cache + t - np.arange(m_cache)
        mu = np.array([mult(d) for d in dist], np.float32)
        mult_c[t] = mu
        bias_c[:, t, :] = np.where(mu > 0, -slopes[:, None] * dist[None, :], -np.inf)
        for t2 in range(t_new):
            if mult(t - t2):
                bias_n[:, t, t2] = -slopes * (t - t2)
                mult_n[t, t2] = mult(t - t2)
    bias_n[:, t_new:, 0] = 0.0
    mult_n[t_new:, 0] = 1.0
    return jnp.asarray(bias_c), jnp.asarray(mult_c), jnp.asarray(bias_n), jnp.asarray(mult_n)


def _attn_dec_body(q_ref, kn_ref, vn_ref, ck_ref, cv_ref, bc_ref, mc_ref, bn_ref, mn_ref, o_ref, *, t_new):
    scale = ATT_HEAD_DIM ** -0.5
    for h in range(ATT_HEADS):
        qh = q_ref[0, h] * scale
        sc = _dot(qh.astype(BF16), ck_ref[0, h].astype(BF16)) + bc_ref[h]
        kn, vn = kn_ref[0, h], vn_ref[0, h]
        sn = [jnp.sum(qh * kn[t:t + 1, :], axis=-1, keepdims=True) + bn_ref[h, :, t:t + 1] for t in range(t_new)]
        m = functools.reduce(jnp.maximum, sn, jnp.max(sc, axis=-1, keepdims=True))
        pc = jnp.exp(sc - m) * mc_ref[...]
        l = jnp.sum(pc, axis=-1, keepdims=True)
        acc = _dot_nt(pc.astype(BF16), cv_ref[0, h].astype(BF16))
        for t in range(t_new):
            pn = jnp.exp(sn[t] - m) * mn_ref[:, t:t + 1]
            l = l + pn
            acc = acc + pn * vn[t:t + 1, :]
        o_ref[0, h] = acc / l


def _attn_decode(q4, kn4, vn4, cache_kt, cache_vt, t_new):
    b = q4.shape[0]
    m_cache = cache_kt.shape[3]
    assert t_new <= DEC_ROWS
    tables = _decode_tables(t_new, m_cache)
    new = pl.BlockSpec((1, ATT_HEADS, DEC_ROWS, ATT_HEAD_DIM), lambda i: (i, 0, 0, 0))
    cache = pl.BlockSpec((1, ATT_HEADS, ATT_HEAD_DIM, m_cache), lambda i: (i, 0, 0, 0))
    full = lambda a: pl.BlockSpec(a.shape, lambda i: (0,) * a.ndim)
    return pl.pallas_call(
        functools.partial(_attn_dec_body, t_new=t_new),
        grid=(b,),
        in_specs=[new, new, new, cache, cache] + [full(a) for a in tables],
        out_specs=new,
        out_shape=jax.ShapeDtypeStruct(q4.shape, F32),
        compiler_params=_cparams(("parallel",)),
        name="attn_decode",
    )(q4, kn4, vn4, cache_kt, cache_vt, *tables)


def _out_proj_body(x_ref, y_ref, o_ref, an_ref, wy_ref, wo_ref, nf_ref, rw_ref, rb_ref, cnt0_ref,
                   h_ref, xn_ref, idx_ref, gate_ref, rank_ref, cnt_ref, carry_ref, *, tm):
    i = pl.program_id(0)

    @pl.when(i == 0)
    def _start():
        carry_ref[...] = cnt0_ref[...]

    on = _rms(o_ref[...], an_ref[...]).astype(BF16)
    mixed = _dot(y_ref[...].astype(BF16), wy_ref[...]) + _dot(on, wo_ref[...])
    h = x_ref[...] + mixed
    h_ref[...] = h
    xn = _rms(h, nf_ref[...])
    xn_ref[...] = xn
    logits = _dot_nt(rw_ref[...], xn.astype(BF16)) + rb_ref[...]

    e_iota = lax.broadcasted_iota(jnp.int32, (N_EXPERTS, tm), 0)
    vals, idxs, sels = [], [], []
    cur = logits
    for _ in range(TOP_K):
        mx = jnp.max(cur, axis=0, keepdims=True)
        ix = jnp.min(jnp.where(cur == mx, e_iota, N_EXPERTS), axis=0, keepdims=True)
        sel = e_iota == ix
        cur = jnp.where(sel, NEG_INF, cur)
        vals.append(mx)
        idxs.append(ix)
        sels.append(sel)
    ex = [jnp.exp(v - vals[0]) for v in vals]
    den = functools.reduce(lambda a, b: a + b, ex)

    sel_any = functools.reduce(jnp.logical_or, sels)
    sel_f = jnp.where(sel_any, 1.0, 0.0)
    r = lax.broadcasted_iota(jnp.int32, (tm, tm), 0)
    c = lax.broadcasted_iota(jnp.int32, (tm, tm), 1)
    upper = jnp.where(r <= c, 1.0, 0.0).astype(BF16)
    cum = _dot(sel_f.astype(BF16), upper)
    before = carry_ref[:, 0:1] + cum - sel_f
    for k in range(TOP_K):
        idx_ref[k:k + 1, :] = idxs[k]
        gate_ref[k:k + 1, :] = ex[k] / den
        rank_ref[k:k + 1, :] = jnp.sum(jnp.where(sels[k], before, 0.0), axis=0, keepdims=True).astype(jnp.int32)
    idx_ref[TOP_K:SUBLANES, :] = jnp.zeros((SUBLANES - TOP_K, tm), jnp.int32)
    gate_ref[TOP_K:SUBLANES, :] = jnp.zeros((SUBLANES - TOP_K, tm), F32)
    rank_ref[TOP_K:SUBLANES, :] = jnp.zeros((SUBLANES - TOP_K, tm), jnp.int32)
    carry_ref[...] = carry_ref[...] + jnp.max(cum, axis=1, keepdims=True)
    cnt_ref[...] = carry_ref[...]


def _out_proj(x2d, y2d, o2d, consts, cnt0, tm):
    n = x2d.shape[0]
    tm = min(tm, n)
    row = lambda w: pl.BlockSpec((tm, w), lambda i: (i, 0))
    colb = pl.BlockSpec((SUBLANES, tm), lambda i: (0, i))
    full = lambda a: pl.BlockSpec(a.shape, lambda i: (0,) * a.ndim)
    return pl.pallas_call(
        functools.partial(_out_proj_body, tm=tm),
        grid=(n // tm,),
        in_specs=[row(D_MODEL), row(SSD_INNER), row(ATT_WIDTH)] + [full(a) for a in consts] + [full(cnt0)],
        out_specs=[row(D_MODEL), row(D_MODEL), colb, colb, colb, full(cnt0)],
        out_shape=[jax.ShapeDtypeStruct((n, D_MODEL), F32), jax.ShapeDtypeStruct((n, D_MODEL), F32),
                   jax.ShapeDtypeStruct((SUBLANES, n), jnp.int32), jax.ShapeDtypeStruct((SUBLANES, n), F32),
                   jax.ShapeDtypeStruct((SUBLANES, n), jnp.int32), jax.ShapeDtypeStruct(cnt0.shape, F32)],
        scratch_shapes=[pltpu.VMEM(cnt0.shape, F32)],
        compiler_params=_cparams(("arbitrary",)),
        name="out_proj",
    )(x2d, y2d, o2d, *consts, cnt0)


def _slot(idx_ref, rank_ref, start_ref, k, t):
    return start_ref[idx_ref[k, t]] + rank_ref[k, t]


def _dispatch_body(idx_ref, rank_ref, start_ref, from_ref, n_ref, tail_ref, xa_ref, xb_ref, xs_ref, zero_ref, sem,
                   *, tm, tiles_a, n_blocks):
    i = pl.program_id(0)

    @pl.when(i == 0)
    def _pad_fill():
        zero_ref[...] = jnp.zeros_like(zero_ref)
        one_row = lambda slot: pltpu.make_async_copy(zero_ref.at[pl.ds(0, 1)], xs_ref.at[pl.ds(slot, 1)], sem)
        block = lambda j: pltpu.make_async_copy(zero_ref, xs_ref.at[pl.ds(j * MOE_ROWS, MOE_ROWS)], sem)

        def per_expert(e, carry):
            def start(r, c):
                one_row(from_ref[e] + r).start()
                return c

            def wait(r, c):
                one_row(0).wait()
                return c
            lax.fori_loop(0, n_ref[e], start, 0)
            lax.fori_loop(0, n_ref[e], wait, 0)
            return carry
        lax.fori_loop(0, N_EXPERTS, per_expert, 0)

        def tail(j, carry):
            block(j).start()
            block(j).wait()
            return carry
        lax.fori_loop(tail_ref[0], n_blocks, tail, 0)

    def scatter_rows(src_ref):
        def issue(t, carry):
            for k in range(TOP_K):
                slot = _slot(idx_ref, rank_ref, start_ref, k, t)
                pltpu.make_async_copy(src_ref.at[pl.ds(t, 1)], xs_ref.at[pl.ds(slot, 1)], sem).start()
            return carry
        lax.fori_loop(0, tm, issue, 0)
        for _ in range(TOP_K):
            pltpu.make_async_copy(src_ref, xs_ref.at[pl.ds(0, tm)], sem).wait()

    @pl.when(i < tiles_a)
    def _first_group():
        scatter_rows(xa_ref)

    @pl.when(i >= tiles_a)
    def _second_group():
        scatter_rows(xb_ref)


def _smem_cols(tm, first=0):
    return pl.BlockSpec((SUBLANES, tm), lambda i: (0, i + first), memory_space=pltpu.SMEM)


def _dispatch(idx, rank, start, pad_from, pad_n, tail_block, xa, xb, n_blocks, tm):
    tiles_a, tiles_b = xa.shape[0] // tm, xb.shape[0] // tm
    assert tiles_a * tm == xa.shape[0] and tiles_b * tm == xb.shape[0]
    smem = pl.BlockSpec(memory_space=pltpu.SMEM)
    return pl.pallas_call(
        functools.partial(_dispatch_body, tm=tm, tiles_a=tiles_a, n_blocks=n_blocks),
        grid=(tiles_a + tiles_b,),
        in_specs=[_smem_cols(tm), _smem_cols(tm), smem, smem, smem, smem,
                  pl.BlockSpec((tm, D_MODEL), lambda i: (jnp.minimum(i, tiles_a - 1), 0)),
                  pl.BlockSpec((tm, D_MODEL), lambda i: (jnp.maximum(i - tiles_a, 0), 0))],
        out_specs=pl.BlockSpec(memory_space=pl.ANY),
        out_shape=jax.ShapeDtypeStruct((n_blocks * MOE_ROWS, D_MODEL), F32),
        scratch_shapes=[pltpu.VMEM((MOE_ROWS, D_MODEL), F32), pltpu.SemaphoreType.DMA(())],
        compiler_params=_cparams(("arbitrary",)),
        name="dispatch",
    )(idx, rank, start, pad_from, pad_n, tail_block, xa, xb)


MOE_COLS = 2 * LANES


def _w1_prep_body(w_ref, p_ref, o_ref):
    o_ref[0] = _dot(w_ref[0].astype(BF16), p_ref[...]).astype(BF16)


def _w1_prep(w1):
    e, d, n2 = w1.shape
    src = np.concatenate([np.arange(0, MOE_COLS, 2), np.arange(1, MOE_COLS, 2)])
    perm = np.zeros((MOE_COLS, MOE_COLS), np.float32)
    perm[src, np.arange(MOE_COLS)] = 1.0
    blk = pl.BlockSpec((1, d, MOE_COLS), lambda i, j: (i, 0, j))
    return pl.pallas_call(
        _w1_prep_body,
        grid=(e, n2 // MOE_COLS),
        in_specs=[blk, pl.BlockSpec((MOE_COLS, MOE_COLS), lambda i, j: (0, 0))],
        out_specs=blk,
        out_shape=jax.ShapeDtypeStruct(w1.shape, BF16),
        compiler_params=_cparams(("parallel", "parallel")),
        name="w1_prep",
    )(w1, jnp.asarray(perm, dtype=BF16))


def _experts_body(be_ref, bv_ref, xs_ref, w1_ref, b1_ref, w2_ref, b2_ref, ys_ref, g_ref):
    j = pl.program_id(0)
    valid = bv_ref[j]

    @pl.when(valid > 0)
    def _compute():
        rows = lax.broadcasted_iota(jnp.int32, (MOE_ROWS, 1), 0)
        x = jnp.where(rows < valid, xs_ref[...], 0.0).astype(BF16)
        for c in range(g_ref.shape[1] // LANES):
            cols = slice(c * MOE_COLS, (c + 1) * MOE_COLS)
            hc = _dot(x, w1_ref[0, :, cols]) + b1_ref[0, :, cols]
            glu = jnp.minimum(hc[:, :LANES], SWIGLU_LIMIT)
            lin = jnp.clip(hc[:, LANES:], -SWIGLU_LIMIT, SWIGLU_LIMIT)
            g_ref[:, c * LANES:(c + 1) * LANES] = (glu * jax.nn.sigmoid(SWIGLU_ALPHA * glu) * (lin + 1.0)).astype(BF16)
        ys_ref[...] = _dot(g_ref[...], w2_ref[0]) + b2_ref[0]

    @pl.when(valid <= 0)
    def _empty():
        ys_ref[...] = jnp.zeros_like(ys_ref)


def _experts(block_expert, block_valid, xs, w1p, b1p, w2, b2):
    n_blocks = xs.shape[0] // MOE_ROWS
    wspec = lambda a: pl.BlockSpec((1,) + a.shape[1:], lambda j, be, bv: (be[j], 0, 0))
    grid_spec = pltpu.PrefetchScalarGridSpec(
        num_scalar_prefetch=2,
        grid=(n_blocks,),
        in_specs=[pl.BlockSpec((MOE_ROWS, D_MODEL), lambda j, be, bv: (j, 0)),
                  wspec(w1p), wspec(b1p), wspec(w2), wspec(b2)],
        out_specs=pl.BlockSpec((MOE_ROWS, D_MODEL), lambda j, be, bv: (j, 0)),
        scratch_shapes=[pltpu.VMEM((MOE_ROWS, w2.shape[1]), BF16)],
    )
    return pl.pallas_call(
        _experts_body,
        grid_spec=grid_spec,
        out_shape=jax.ShapeDtypeStruct(xs.shape, F32),
        compiler_params=_cparams(("arbitrary",)),
        name="experts",
    )(block_expert, block_valid, xs, w1p, b1p, w2, b2)


def _combine_body(idx_ref, rank_ref, start_ref, gate_ref, h_ref, nf_ref, ys_ref, y_ref, buf_ref, sem, *, tm):
    def issue(t, carry):
        for k in range(TOP_K):
            slot = _slot(idx_ref, rank_ref, start_ref, k, t)
            pltpu.make_async_copy(ys_ref.at[pl.ds(slot, 1)], buf_ref.at[k, pl.ds(t, 1)], sem).start()
        return carry
    lax.fori_loop(0, tm, issue, 0)
    for k in range(TOP_K):
        pltpu.make_async_copy(ys_ref.at[pl.ds(0, tm)], buf_ref.at[k], sem).wait()
    acc = h_ref[...]
    for k in range(TOP_K):
        acc = acc + gate_ref[:, k:k + 1] * buf_ref[k]
    y_ref[...] = _rms(acc, nf_ref[...])


def _combine(idx, rank, start, gates_rows, h, norm_final, ys, tm, first_tile):
    n = h.shape[0]
    assert n % tm == 0
    return pl.pallas_call(
        functools.partial(_combine_body, tm=tm),
        grid=(n // tm,),
        in_specs=[_smem_cols(tm, first_tile), _smem_cols(tm, first_tile), pl.BlockSpec(memory_space=pltpu.SMEM),
                  pl.BlockSpec((tm, TOP_K), lambda i: (i, 0)),
                  pl.BlockSpec((tm, D_MODEL), lambda i: (i, 0)),
                  pl.BlockSpec((1, D_MODEL), lambda i: (0, 0)),
                  pl.BlockSpec(memory_space=pl.ANY)],
        out_specs=pl.BlockSpec((tm, D_MODEL), lambda i: (i, 0)),
        out_shape=jax.ShapeDtypeStruct((n, D_MODEL), F32),
        scratch_shapes=[pltpu.VMEM((TOP_K, tm, D_MODEL), F32), pltpu.SemaphoreType.DMA(())],
        compiler_params=_cparams(("arbitrary",)),
        name="combine",
    )(idx, rank, start, gates_rows, h, norm_final, ys)


def _expansion(width):
    h = np.arange(LANES)[:, None]
    c = np.arange(SSD_HEADS * width)[None, :] // width
    return jnp.asarray((h == c).astype(np.float32), dtype=BF16)


def _pad_lanes(v):
    return jnp.pad(v.astype(F32), (0, LANES - v.shape[0]))[None, :]


def kernel(x_prompt, x_sample, state_conv, state_ssm, cache_win_k, cache_win_v, norm_mix, w_in, conv_w, conv_b,
           dt_bias, a_log, d_skip, ssd_norm, att_norm, w_out, norm_ffn, router_w, router_b, w1, b1, w2, b2,
           norm_final):
    depth = w_in.shape[0]
    assert depth == 1
    bp, seq, _ = x_prompt.shape
    bs, t_new, _ = x_sample.shape
    n_p, n_s = bp * seq, bs * t_new
    l = 0

    o_dt = SSD_INNER + CONV_DIM
    o_q = o_dt + SSD_HEADS
    wl = w_in[l]
    w_cat = jnp.concatenate([wl[:, :o_dt], wl[:, o_q:], jnp.pad(wl[:, o_dt:o_q], ((0, 0), (0, DT_PAD - SSD_HEADS)))],
                            axis=1).astype(BF16)
    g_mix = norm_mix[l][None, :]
    ssd_consts = (conv_w[l], conv_b[l][None, :], _pad_lanes(dt_bias[l]), _pad_lanes(a_log[l]),
                  jnp.repeat(d_skip[l], SSD_HEAD_DIM)[None, :], ssd_norm[l][None, :],
                  _expansion(SSD_HEAD_DIM), _expansion(LANES))
    slopes = jnp.exp2(-8.0 * jnp.arange(1, ATT_HEADS + 1, dtype=F32) / ATT_HEADS)
    out_consts = (att_norm[l][None, :], w_out[l][:SSD_INNER].astype(BF16), w_out[l][SSD_INNER:].astype(BF16),
                  norm_ffn[l][None, :], router_w[l].T.astype(BF16), router_b[l][:, None].astype(F32))
    w1p = _w1_prep(w1[l])
    b1p = b1[l].reshape(N_EXPERTS, -1, LANES, 2).transpose(0, 1, 3, 2).reshape(N_EXPERTS, 1, -1)
    w2b = w2[l].astype(BF16)
    b2r = b2[l][:, None, :]

    zx, q, k, v, dt, kt, vt = _in_proj(x_prompt.reshape(n_p, D_MODEL), g_mix, w_cat, 256, seq)
    y_p, conv_p, ssm_p = _ssd(zx.reshape(bp, seq, ZX_WIDTH), dt.reshape(bp, seq, DT_PAD), ssd_consts, None, BF16)
    o_p = _attn_prompt(q, k, v, slopes, bp, seq)
    keep = min(DILATED_PATTERNS[-1][0], seq)
    k_p = kt.reshape(bp, ATT_HEADS, ATT_HEAD_DIM, seq).transpose(0, 3, 1, 2)[:, seq - keep:]
    v_p = vt.reshape(bp, ATT_HEADS, ATT_HEAD_DIM, seq).transpose(0, 3, 1, 2)[:, seq - keep:]

    zx_s, q_s, k_s, v_s, dt_s = _in_proj(x_sample.reshape(n_s, D_MODEL), g_mix, w_cat, 256)
    init = (state_conv[l], state_ssm[l].reshape(bs, SSD_INNER, SSD_STATE))
    y_s, conv_s, ssm_s = _ssd(zx_s.reshape(bs, t_new, ZX_WIDTH), dt_s.reshape(bs, t_new, DT_PAD), ssd_consts, init, F32)

    def head_major(a):
        a = a.reshape(bs, t_new, ATT_HEADS, ATT_HEAD_DIM).transpose(0, 2, 1, 3)
        return jnp.pad(a, ((0, 0), (0, 0), (0, DEC_ROWS - t_new), (0, 0)))
    o_s = _attn_decode(head_major(q_s), head_major(k_s), head_major(v_s),
                       cache_win_k[l].transpose(0, 2, 3, 1), cache_win_v[l].transpose(0, 2, 3, 1), t_new)
    o_s = o_s[:, :, :t_new].transpose(0, 2, 1, 3)

    cnt0 = jnp.zeros((N_EXPERTS, LANES), F32)
    h_p, xn_p, idx_p, gate_p, rank_p, cnt_p = _out_proj(x_prompt.reshape(n_p, D_MODEL), y_p.reshape(n_p, SSD_INNER),
                                                        o_p, out_consts, cnt0, 256)
    h_s, xn_s, idx_s, gate_s, rank_s, cnt_all = _out_proj(x_sample.reshape(n_s, D_MODEL), y_s.reshape(n_s, SSD_INNER),
                                                          o_s.reshape(n_s, ATT_WIDTH), out_consts, cnt_p, 256)

    counts = cnt_all[:, 0].astype(jnp.int32)
    padded = (counts + MOE_ROWS - 1) // MOE_ROWS * MOE_ROWS
    pad_end = jnp.cumsum(padded)
    pad_start = pad_end - padded
    n_blocks = -(-((n_p + n_s) * TOP_K) // MOE_ROWS) + N_EXPERTS
    blk0 = jnp.arange(n_blocks, dtype=jnp.int32) * MOE_ROWS
    owner = blk0[:, None] >= pad_end[None, :]
    block_expert = jnp.minimum(jnp.sum(owner, axis=1), N_EXPERTS - 1).astype(jnp.int32)
    onehot = block_expert[:, None] == jnp.arange(N_EXPERTS, dtype=jnp.int32)[None, :]
    used = jnp.sum(jnp.where(onehot, (blk0[:, None] - pad_start[None, :]), 0), axis=1)
    block_valid = jnp.clip(jnp.sum(jnp.where(onehot, counts[None, :], 0), axis=1) - used, 0, MOE_ROWS).astype(jnp.int32)

    tm = min(256, n_s)
    idx_all = jnp.concatenate([idx_p, idx_s], axis=1)
    rank_all = jnp.concatenate([rank_p, rank_s], axis=1)
    xs = _dispatch(idx_all, rank_all, pad_start, pad_start + counts, padded - counts,
                   pad_end[N_EXPERTS - 1:] // MOE_ROWS, xn_p, xn_s, n_blocks, tm)
    ys = _experts(block_expert, block_valid, xs, w1p, b1p, w2b, b2r)
    nfin = norm_final[None, :]
    y_prompt = _combine(idx_all, rank_all, pad_start, gate_p[:TOP_K].T, h_p, nfin, ys, tm, 0)
    y_sample = _combine(idx_all, rank_all, pad_start, gate_s[:TOP_K].T, h_s, nfin, ys, tm, n_p // tm)

    return (y_prompt.reshape(bp, seq, D_MODEL), y_sample.reshape(bs, t_new, D_MODEL),
            conv_p[None], ssm_p.reshape(1, bp, SSD_HEADS, SSD_HEAD_DIM, SSD_STATE), k_p[None], v_p[None],
            conv_s[None], ssm_s.reshape(1, bs, SSD_HEADS, SSD_HEAD_DIM, SSD_STATE),
            k_s.reshape(1, bs, t_new, ATT_HEADS, ATT_HEAD_DIM), v_s.reshape(1, bs, t_new, ATT_HEADS, ATT_HEAD_DIM))
```

```python
import functools

import jax
import jax.numpy as jnp
import numpy as np
from jax import lax
from jax.experimental import pallas as pl
from jax.experimental.pallas import tpu as pltpu

F32 = jnp.float32
BF16 = jnp.bfloat16

D_MODEL = 1024
SSD_HEADS = 16
SSD_HEAD_DIM = 64
SSD_INNER = SSD_HEADS * SSD_HEAD_DIM
SSD_GROUPS = 2
SSD_STATE = 128
SSD_CONV = 4
SSD_CHUNK = 128
CONV_DIM = SSD_INNER + 2 * SSD_GROUPS * SSD_STATE
ATT_HEADS = 8
ATT_HEAD_DIM = 64
ATT_WIDTH = ATT_HEADS * ATT_HEAD_DIM
DILATED_PATTERNS = ((128, 1), (512, 4), (2048, 16))
ATT_BLOCK = 128
N_EXPERTS = 32
TOP_K = 4
SWIGLU_LIMIT = 7.0
SWIGLU_ALPHA = 1.702
NORM_EPS = 1e-5

LANES = 128
SUBLANES = 8
ZX_WIDTH = SSD_INNER + CONV_DIM
DT_PAD = LANES
W_CAT = ZX_WIDTH + 3 * ATT_WIDTH + DT_PAD
CONV_HIST = SUBLANES
MOE_ROWS = 512
VMEM_LIMIT = 56 * 1024 * 1024

NEG_INF = float("-inf")


def _cparams(sem):
    return pltpu.CompilerParams(dimension_semantics=sem, vmem_limit_bytes=VMEM_LIMIT)


def _rms(x, g):
    return x * lax.rsqrt(jnp.mean(x * x, axis=-1, keepdims=True) + NORM_EPS) * g


def _dot(a, b):
    return jnp.dot(a, b, preferred_element_type=F32)


def _dot_nt(a, b):
    return lax.dot_general(a, b, (((1,), (1,)), ((), ())), preferred_element_type=F32)


def _split3(v):
    hi = v.astype(BF16)
    r1 = v - hi.astype(F32)
    mid = r1.astype(BF16)
    lo = (r1 - mid.astype(F32)).astype(BF16)
    return hi, mid, lo


def _dot3(v, m):
    hi, mid, lo = _split3(v)
    return _dot(hi, m) + _dot(mid, m) + _dot(lo, m)


def _dot3_lhs(m, v):
    hi, mid, lo = _split3(v)
    return _dot(m, hi) + _dot(m, mid) + _dot(m, lo)


def _silu(x):
    return x * jax.nn.sigmoid(x)


def _in_proj_body(x_ref, g_ref, w_ref, zx_ref, q_ref, k_ref, v_ref, dt_ref, *t_refs):
    hn = _rms(x_ref[...], g_ref[...]).astype(BF16)
    o = 0
    for ref, width in ((zx_ref, ZX_WIDTH), (q_ref, ATT_WIDTH), (k_ref, ATT_WIDTH), (v_ref, ATT_WIDTH),
                       (dt_ref, DT_PAD)):
        ref[...] = _dot(hn, w_ref[:, o:o + width])
        o += width
    for src, dst in zip((k_ref, v_ref), t_refs):
        for j in range(ATT_WIDTH // LANES):
            dst[0, j * LANES:(j + 1) * LANES, :] = src[:, j * LANES:(j + 1) * LANES].T


def _in_proj(x2d, g, w_cat, tm, seq=None):
    n = x2d.shape[0]
    tm = min(tm, n)
    per = 1 if seq is None else seq // tm
    row = lambda w: pl.BlockSpec((tm, w), lambda b, i: (b * per + i, 0))
    full = lambda a: pl.BlockSpec(a.shape, lambda b, i: (0,) * a.ndim)
    widths = (ZX_WIDTH, ATT_WIDTH, ATT_WIDTH, ATT_WIDTH, DT_PAD)
    out_specs = [row(w) for w in widths]
    out_shape = [jax.ShapeDtypeStruct((n, w), F32) for w in widths]
    if seq is not None:
        out_specs += [pl.BlockSpec((1, ATT_WIDTH, tm), lambda b, i: (b, 0, i))] * 2
        out_shape += [jax.ShapeDtypeStruct((n // seq, ATT_WIDTH, seq), F32)] * 2
    return pl.pallas_call(
        _in_proj_body,
        grid=(n // (tm * per), per),
        in_specs=[row(D_MODEL), full(g), full(w_cat)],
        out_specs=out_specs,
        out_shape=out_shape,
        compiler_params=_cparams(("parallel", "parallel")),
        name="in_proj",
    )(x2d, g, w_cat)


def _ssd_body(*refs, l_blk, has_init):
    T = SSD_CHUNK
    if has_init:
        (zx_ref, dt_ref, cw_ref, cb_ref, dtb_ref, alog_ref, dskip_ref, norm_ref, e64_ref, e128_ref,
         cinit_ref, sinit_ref, y_ref, conv_out_ref, ssm_out_ref,
         ext_ref, act_ref, state_ref, cst_ref, ybuf_ref, zpad_ref, dtpad_ref) = refs
    else:
        (zx_ref, dt_ref, cw_ref, cb_ref, dtb_ref, alog_ref, dskip_ref, norm_ref, e64_ref, e128_ref,
         y_ref, conv_out_ref, ssm_out_ref,
         ext_ref, act_ref, state_ref, cst_ref, ybuf_ref) = refs
    c = pl.program_id(1)
    n_tile = SSD_INNER // LANES

    @pl.when(c == 0)
    def _start():
        if has_init:
            ext_ref[0:CONV_HIST, :] = jnp.zeros((CONV_HIST, CONV_DIM), F32)
            ext_ref[CONV_HIST - (SSD_CONV - 1):CONV_HIST, :] = cinit_ref[0]
            for j in range(n_tile):
                state_ref[:, j * LANES:(j + 1) * LANES] = sinit_ref[0, j * LANES:(j + 1) * LANES, :].T
        else:
            ext_ref[0:CONV_HIST, :] = jnp.zeros((CONV_HIST, CONV_DIM), F32)
            state_ref[...] = jnp.zeros_like(state_ref)

    if l_blk == T:
        ext_ref[CONV_HIST:CONV_HIST + T, :] = zx_ref[0, :, SSD_INNER:ZX_WIDTH]
        z_of = lambda sl: zx_ref[0, :, sl]
        dt_raw = dt_ref[0]
    else:
        ext_ref[CONV_HIST:CONV_HIST + T, :] = jnp.zeros((T, CONV_DIM), F32)
        ext_ref[CONV_HIST:CONV_HIST + l_blk, :] = zx_ref[0, :, SSD_INNER:ZX_WIDTH]
        zpad_ref[...] = jnp.zeros_like(zpad_ref)
        zpad_ref[0:l_blk, :] = zx_ref[0, :, 0:SSD_INNER]
        dtpad_ref[...] = jnp.zeros_like(dtpad_ref)
        dtpad_ref[0:l_blk, :] = dt_ref[0]
        z_of = lambda sl: zpad_ref[:, sl]
        dt_raw = dtpad_ref[...]

    cw = CONV_DIM // 3
    for cc in range(3):
        sl = slice(cc * cw, (cc + 1) * cw)
        acc = cb_ref[:, sl]
        for j in range(SSD_CONV):
            o = CONV_HIST - (SSD_CONV - 1) + j
            acc = acc + ext_ref[o:o + T, sl] * cw_ref[j:j + 1, sl]
        act_ref[:, sl] = _silu(acc)

    row = lax.broadcasted_iota(jnp.int32, (T, LANES), 0)
    col = lax.broadcasted_iota(jnp.int32, (T, LANES), 1)
    tri = row >= col
    tri_bf = jnp.where(tri, 1.0, 0.0).astype(BF16)
    even = col < SSD_HEAD_DIM

    xdt = dt_raw + dtb_ref[...]
    dtv = jnp.maximum(xdt, 0.0) + jnp.log1p(jnp.exp(-jnp.abs(xdt)))
    if l_blk < T:
        dtv = jnp.where(row < l_blk, dtv, 0.0)
    d_a = dtv * (-jnp.exp(alog_ref[...]))
    cs = _dot3_lhs(tri_bf, d_a)
    cs_last = cs[T - 1:T, :]
    dec = jnp.exp(cs_last - cs)
    ecs = jnp.exp(cs)
    cst_ref[...] = cs.T
    per_head = jnp.concatenate([dtv, dtv * dec, ecs], axis=0)
    ex = _dot3(per_head, e64_ref[...])
    dt_ex, dd_ex, ecs_ex = ex[0:T], ex[T:2 * T], ex[2 * T:3 * T]
    colb = _dot3(cs, e128_ref[...])

    gw = SSD_INNER // SSD_GROUPS
    heads_per_group = SSD_HEADS // SSD_GROUPS
    for g in range(SSD_GROUPS):
        gsl = slice(g * gw, (g + 1) * gw)
        b_g = act_ref[:, SSD_INNER + g * SSD_STATE:SSD_INNER + (g + 1) * SSD_STATE]
        c_off = SSD_INNER + SSD_GROUPS * SSD_STATE
        c_g = act_ref[:, c_off + g * SSD_STATE:c_off + (g + 1) * SSD_STATE].astype(BF16)
        cb = _dot_nt(c_g, b_g.astype(BF16))
        b_gt = b_g.T.astype(BF16)
        x_g = act_ref[:, gsl]
        x_dt = (x_g * dt_ex[:, gsl]).astype(BF16)
        x_dd = (x_g * dd_ex[:, gsl]).astype(BF16)
        st_old = state_ref[:, gsl]
        y_off = _dot(c_g, st_old.astype(BF16)) * ecs_ex[:, gsl]
        state_ref[:, gsl] = st_old * ecs_ex[T - 1:T, gsl] + _dot(b_gt, x_dd)
        for jp in range(heads_per_group // 2):
            h0 = g * heads_per_group + 2 * jp
            psl = slice(jp * LANES, (jp + 1) * LANES)
            yd = []
            for h in (h0, h0 + 1):
                seg = colb[:, h * LANES:(h + 1) * LANES] - cst_ref[h:h + 1, :]
                lmat = jnp.exp(jnp.where(tri, seg, NEG_INF))
                yd.append(_dot((cb * lmat).astype(BF16), x_dt[:, psl]))
            y_pair = jnp.where(even, yd[0], yd[1]) + y_off[:, psl]
            osl = slice(g * gw + jp * LANES, g * gw + (jp + 1) * LANES)
            ybuf_ref[:, osl] = y_pair + dskip_ref[:, osl] * x_g[:, psl]

    for g in range(SSD_GROUPS):
        gsl = slice(g * gw, (g + 1) * gw)
        yg = ybuf_ref[:, gsl] * _silu(z_of(gsl))
        yn = yg * lax.rsqrt(jnp.mean(yg * yg, axis=-1, keepdims=True) + NORM_EPS) * norm_ref[:, gsl]
        y_ref[0, :, gsl] = yn[0:l_blk].astype(y_ref.dtype)

    @pl.when(c == pl.num_programs(1) - 1)
    def _finish():
        lo = CONV_HIST + l_blk - (SSD_CONV - 1)
        conv_out_ref[0] = ext_ref[lo:lo + SSD_CONV - 1, :]
        for j in range(n_tile):
            ssm_out_ref[0, j * LANES:(j + 1) * LANES, :] = state_ref[:, j * LANES:(j + 1) * LANES].T

    ext_ref[0:CONV_HIST, :] = ext_ref[T:T + CONV_HIST, :]


def _ssd(zx3, dt3, consts, init, y_dtype):
    b, L, _ = zx3.shape
    T = SSD_CHUNK
    l_blk = T if L % T == 0 else L
    n_chunks = L // l_blk
    has_init = init is not None
    full = lambda a: pl.BlockSpec(a.shape, lambda i, c: (0,) * a.ndim)
    in_specs = [pl.BlockSpec((1, l_blk, ZX_WIDTH), lambda i, c: (i, c, 0)),
                pl.BlockSpec((1, l_blk, DT_PAD), lambda i, c: (i, c, 0))] + [full(a) for a in consts]
    args = [zx3, dt3, *consts]
    scratch = [pltpu.VMEM((T + CONV_HIST, CONV_DIM), F32),
               pltpu.VMEM((T, CONV_DIM), F32),
               pltpu.VMEM((SSD_STATE, SSD_INNER), F32),
               pltpu.VMEM((LANES, T), F32),
               pltpu.VMEM((T, SSD_INNER), F32)]
    if has_init:
        in_specs += [pl.BlockSpec((1, SSD_CONV - 1, CONV_DIM), lambda i, c: (i, 0, 0)),
                     pl.BlockSpec((1, SSD_INNER, SSD_STATE), lambda i, c: (i, 0, 0))]
        args += list(init)
        scratch += [pltpu.VMEM((T, SSD_INNER), F32), pltpu.VMEM((T, DT_PAD), F32)]
    return pl.pallas_call(
        functools.partial(_ssd_body, l_blk=l_blk, has_init=has_init),
        grid=(b, n_chunks),
        in_specs=in_specs,
        out_specs=[pl.BlockSpec((1, l_blk, SSD_INNER), lambda i, c: (i, c, 0)),
                   pl.BlockSpec((1, SSD_CONV - 1, CONV_DIM), lambda i, c: (i, 0, 0)),
                   pl.BlockSpec((1, SSD_INNER, SSD_STATE), lambda i, c: (i, 0, 0))],
        out_shape=[jax.ShapeDtypeStruct((b, L, SSD_INNER), y_dtype),
                   jax.ShapeDtypeStruct((b, SSD_CONV - 1, CONV_DIM), F32),
                   jax.ShapeDtypeStruct((b, SSD_INNER, SSD_STATE), F32)],
        scratch_shapes=scratch,
        compiler_params=_cparams(("parallel", "arbitrary")),
        name="ssd_init" if has_init else "ssd",
    )(*args)


ATT_UNROLL = 5


def _unroll(n):
    return max(d for d in range(1, ATT_UNROLL + 1) if n % d == 0)


ATT_RES = DILATED_PATTERNS[-1][1]


def _attn_tables():
    B = ATT_BLOCK
    t1 = np.full((len(DILATED_PATTERNS), B, B), -np.inf, np.float32)
    t2 = np.full((len(DILATED_PATTERNS), B, 2 * B), -np.inf, np.float32)
    for p, (window, dil) in enumerate(DILATED_PATTERNS):
        m = ATT_RES // dil
        w = B // m
        rho = np.arange(B)
        c = m * (rho % w) + rho // w
        ck2 = np.concatenate([c - B, c])
        for tab, ck in ((t1, c), (t2, ck2)):
            delta = c[:, None] - ck[None, :]
            ok = (delta >= 0) & (delta <= window // dil)
            tab[p] = np.where(ok, -(dil * delta).astype(np.float32), -np.inf)
    return jnp.asarray(t1), jnp.asarray(t2)


def _attn_body(slopes_ref, t1_ref, t2_ref, q_ref, k_ref, v_ref, o_ref, qd_ref, kd_ref, vd_ref, op_ref, lp_ref,
               stage_ref, b1_ref, b2_ref, in_sem, out_sem, *, seq):
    B = ATT_BLOCK
    hp = pl.program_id(1)
    even = lax.broadcasted_iota(jnp.int32, (B, LANES), 1) < ATT_HEAD_DIM
    scale = ATT_HEAD_DIM ** -0.5

    def in_copy(a, r):
        src, dst = ((q_ref, qd_ref), (k_ref, kd_ref), (v_ref, vd_ref))[a]
        return pltpu.make_async_copy(src.at[:, r, :], dst.at[r], in_sem.at[a, r])
    for r in range(ATT_RES):
        for a in range(3):
            in_copy(a, r).start()

    for e in range(2):
        slope = slopes_ref[2 * hp + e]
        for p in range(len(DILATED_PATTERNS)):
            b1_ref[e * len(DILATED_PATTERNS) + p] = t1_ref[p] * slope
            b2_ref[e * len(DILATED_PATTERNS) + p] = t2_ref[p] * slope

    for r in range(ATT_RES):
        for a in range(3):
            in_copy(a, r).wait()

    def block(p, r_d, n, first):
        dil = DILATED_PATTERNS[p][1]
        m = ATT_RES // dil
        w = B // m

        def slab(j, nblk):
            start = nblk * w
            return r_d + dil * j, pl.ds(start if isinstance(start, int) else pl.multiple_of(start, w), w)

        def gather(src, nblk):
            parts = []
            for j in range(m):
                r, rows = slab(j, nblk)
                parts.append(src[r, rows, :])
            return parts[0] if m == 1 else jnp.concatenate(parts, axis=0)

        qb = gather(qd_ref, n) * scale
        if first:
            kb, vb = gather(kd_ref, n), gather(vd_ref, n)
            bias = lambda e: b1_ref[e * len(DILATED_PATTERNS) + p]
        else:
            kb = jnp.concatenate([gather(kd_ref, n - 1), gather(kd_ref, n)], axis=0)
            vb = jnp.concatenate([gather(vd_ref, n - 1), gather(vd_ref, n)], axis=0)
            bias = lambda e: b2_ref[e * len(DILATED_PATTERNS) + p]
        q2 = jnp.concatenate([jnp.where(even, qb, 0.0), jnp.where(even, 0.0, qb)], axis=0).astype(BF16)
        s = _dot_nt(q2, kb.astype(BF16)) + jnp.concatenate([bias(0), bias(1)], axis=0)
        mx = jnp.max(s, axis=-1, keepdims=True)
        pr = jnp.exp(s - mx)
        l = jnp.sum(pr, axis=-1, keepdims=True)
        o2 = _dot(pr.astype(BF16), vb.astype(BF16)) / l
        lse2 = jnp.broadcast_to(mx + jnp.log(l), (2 * B, LANES))
        o = jnp.where(even, o2[:B], o2[B:])
        lse = jnp.where(even, lse2[:B], lse2[B:])
        for j in range(m):
            r, rows = slab(j, n)
            op_ref[p, r, rows, :] = o[j * w:(j + 1) * w]
            lp_ref[p, r, rows, :] = lse[j * w:(j + 1) * w]

    for p, (_, dil) in enumerate(DILATED_PATTERNS):
        nb = seq // (dil * B)
        u_first = _unroll(dil)

        def first(i, carry, p=p, u_first=u_first):
            for u in range(u_first):
                block(p, i * u_first + u, 0, True)
            return carry
        lax.fori_loop(0, dil // u_first, first, 0)
        n_later = dil * (nb - 1)
        if n_later:
            u_later = _unroll(n_later)

            def later(i, carry, p=p, nb=nb, u_later=u_later):
                for u in range(u_later):
                    j = i * u_later + u
                    block(p, j // (nb - 1), j % (nb - 1) + 1, False)
                return carry
            lax.fori_loop(0, n_later // u_later, later, 0)

    def out_copy(r):
        return pltpu.make_async_copy(stage_ref.at[r], o_ref.at[:, r, :], out_sem.at[r])
    for r in range(ATT_RES):
        l0, l1, l2 = lp_ref[0, r], lp_ref[1, r], lp_ref[2, r]
        m = jnp.maximum(jnp.maximum(l0, l1), l2)
        w0, w1, w2 = jnp.exp(l0 - m), jnp.exp(l1 - m), jnp.exp(l2 - m)
        num = w0 * op_ref[0, r] + w1 * op_ref[1, r] + w2 * op_ref[2, r]
        stage_ref[r] = num / (w0 + w1 + w2)
        out_copy(r).start()
    for r in range(ATT_RES):
        out_copy(r).wait()


def _attn_prompt(q, k, v, slopes, batch, seq):
    n_hp = ATT_HEADS // 2
    n_pat = len(DILATED_PATTERNS)
    assert len(DILATED_PATTERNS) == 3 and all(ATT_RES % d == 0 and seq % (d * ATT_BLOCK) == 0 and w // d == ATT_BLOCK
                                              for w, d in DILATED_PATTERNS)
    per = seq // ATT_RES
    by_res = lambda a: a.reshape(batch * per, ATT_RES, ATT_WIDTH)
    blk = pl.BlockSpec((per, ATT_RES, LANES), lambda b, h: (b, 0, h))
    t1, t2 = _attn_tables()
    full = lambda a: pl.BlockSpec(a.shape, lambda b, h: (0,) * a.ndim)
    res = pltpu.VMEM((ATT_RES, per, LANES), F32)
    out = pl.pallas_call(
        functools.partial(_attn_body, seq=seq),
        grid=(batch, n_hp),
        in_specs=[pl.BlockSpec(memory_space=pltpu.SMEM), full(t1), full(t2), blk, blk, blk],
        out_specs=blk,
        out_shape=jax.ShapeDtypeStruct((batch * per, ATT_RES, ATT_WIDTH), F32),
        scratch_shapes=[res, res, res,
                        pltpu.VMEM((n_pat, ATT_RES, per, LANES), F32), pltpu.VMEM((n_pat, ATT_RES, per, LANES), F32),
                        res,
                        pltpu.VMEM((2 * n_pat, ATT_BLOCK, ATT_BLOCK), F32),
                        pltpu.VMEM((2 * n_pat, ATT_BLOCK, 2 * ATT_BLOCK), F32),
                        pltpu.SemaphoreType.DMA((3, ATT_RES)), pltpu.SemaphoreType.DMA((ATT_RES,))],
        compiler_params=_cparams(("parallel", "parallel")),
        name="attn_prompt",
    )(slopes, t1, t2, by_res(q), by_res(k), by_res(v))
    return out.reshape(batch * seq, ATT_WIDTH)


DEC_ROWS = SUBLANES


def _decode_tables(t_new, m_cache):
    slopes = 2.0 ** (-8.0 * np.arange(1, ATT_HEADS + 1) / ATT_HEADS)

    def mult(dist):
        return sum(1 for window, dil in DILATED_PATTERNS if 0 <= dist <= window and dist % dil == 0)

    bias_c = np.zeros((ATT_HEADS, DEC_ROWS, m_cache), np.float32)
    mult_c = np.ones((DEC_ROWS, m_cache), np.float32)
    bias_n = np.full((ATT_HEADS, DEC_ROWS, LANES), -np.inf, np.float32)
    mult_n = np.zeros((DEC_ROWS, LANES), np.float32)
    for t in range(t_new):
        dist = m_cache + t - np.arange(m_cache)
        mu = np.array([mult(d) for d in dist], np.float32)
        mult_c[t] = mu
        bias_c[:, t, :] = np.where(mu > 0, -slopes[:, None] * dist[None, :], -np.inf)
        for t2 in range(t_new):
            if mult(t - t2):
                bias_n[:, t, t2] = -slopes * (t - t2)
                mult_n[t, t2] = mult(t - t2)
    bias_n[:, t_new:, 0] = 0.0
    mult_n[t_new:, 0] = 1.0
    return jnp.asarray(bias_c), jnp.asarray(mult_c), jnp.asarray(bias_n), jnp.asarray(mult_n)


def _attn_dec_body(q_ref, kn_ref, vn_ref, ck_ref, cv_ref, bc_ref, mc_ref, bn_ref, mn_ref, o_ref, *, t_new):
    scale = ATT_HEAD_DIM ** -0.5
    for h in range(ATT_HEADS):
        qh = q_ref[0, h] * scale
        sc = _dot(qh.astype(BF16), ck_ref[0, h].astype(BF16)) + bc_ref[h]
        kn, vn = kn_ref[0, h], vn_ref[0, h]
        sn = [jnp.sum(qh * kn[t:t + 1, :], axis=-1, keepdims=True) + bn_ref[h, :, t:t + 1] for t in range(t_new)]
        m = functools.reduce(jnp.maximum, sn, jnp.max(sc, axis=-1, keepdims=True))
        pc = jnp.exp(sc - m) * mc_ref[...]
        l = jnp.sum(pc, axis=-1, keepdims=True)
        acc = _dot_nt(pc.astype(BF16), cv_ref[0, h].astype(BF16))
        for t in range(t_new):
            pn = jnp.exp(sn[t] - m) * mn_ref[:, t:t + 1]
            l = l + pn
            acc = acc + pn * vn[t:t + 1, :]
        o_ref[0, h] = acc / l


def _attn_decode(q4, kn4, vn4, cache_kt, cache_vt, t_new):
    b = q4.shape[0]
    m_cache = cache_kt.shape[3]
    assert t_new <= DEC_ROWS
    tables = _decode_tables(t_new, m_cache)
    new = pl.BlockSpec((1, ATT_HEADS, DEC_ROWS, ATT_HEAD_DIM), lambda i: (i, 0, 0, 0))
    cache = pl.BlockSpec((1, ATT_HEADS, ATT_HEAD_DIM, m_cache), lambda i: (i, 0, 0, 0))
    full = lambda a: pl.BlockSpec(a.shape, lambda i: (0,) * a.ndim)
    return pl.pallas_call(
        functools.partial(_attn_dec_body, t_new=t_new),
        grid=(b,),
        in_specs=[new, new, new, cache, cache] + [full(a) for a in tables],
        out_specs=new,
        out_shape=jax.ShapeDtypeStruct(q4.shape, F32),
        compiler_params=_cparams(("parallel",)),
        name="attn_decode",
    )(q4, kn4, vn4, cache_kt, cache_vt, *tables)


def _out_proj_body(x_ref, y_ref, o_ref, an_ref, wy_ref, wo_ref, nf_ref, rw_ref, rb_ref, cnt0_ref,
                   h_ref, xn_ref, idx_ref, gate_ref, rank_ref, cnt_ref, carry_ref, *, tm):
    i = pl.program_id(0)

    @pl.when(i == 0)
    def _start():
        carry_ref[...] = cnt0_ref[...]

    on = _rms(o_ref[...], an_ref[...]).astype(BF16)
    mixed = _dot(y_ref[...].astype(BF16), wy_ref[...]) + _dot(on, wo_ref[...])
    h = x_ref[...] + mixed
    h_ref[...] = h
    xn = _rms(h, nf_ref[...])
    xn_ref[...] = xn
    logits = _dot_nt(rw_ref[...], xn.astype(BF16)) + rb_ref[...]

    e_iota = lax.broadcasted_iota(jnp.int32, (N_EXPERTS, tm), 0)
    vals, idxs, sels = [], [], []
    cur = logits
    for _ in range(TOP_K):
        mx = jnp.max(cur, axis=0, keepdims=True)
        ix = jnp.min(jnp.where(cur == mx, e_iota, N_EXPERTS), axis=0, keepdims=True)
        sel = e_iota == ix
        cur = jnp.where(sel, NEG_INF, cur)
        vals.append(mx)
        idxs.append(ix)
        sels.append(sel)
    ex = [jnp.exp(v - vals[0]) for v in vals]
    den = functools.reduce(lambda a, b: a + b, ex)

    sel_any = functools.reduce(jnp.logical_or, sels)
    sel_f = jnp.where(sel_any, 1.0, 0.0)
    r = lax.broadcasted_iota(jnp.int32, (tm, tm), 0)
    c = lax.broadcasted_iota(jnp.int32, (tm, tm), 1)
    upper = jnp.where(r <= c, 1.0, 0.0).astype(BF16)
    cum = _dot(sel_f.astype(BF16), upper)
    before = carry_ref[:, 0:1] + cum - sel_f
    for k in range(TOP_K):
        idx_ref[k:k + 1, :] = idxs[k]
        gate_ref[k:k + 1, :] = ex[k] / den
        rank_ref[k:k + 1, :] = jnp.sum(jnp.where(sels[k], before, 0.0), axis=0, keepdims=True).astype(jnp.int32)
    idx_ref[TOP_K:SUBLANES, :] = jnp.zeros((SUBLANES - TOP_K, tm), jnp.int32)
    gate_ref[TOP_K:SUBLANES, :] = jnp.zeros((SUBLANES - TOP_K, tm), F32)
    rank_ref[TOP_K:SUBLANES, :] = jnp.zeros((SUBLANES - TOP_K, tm), jnp.int32)
    carry_ref[...] = carry_ref[...] + jnp.max(cum, axis=1, keepdims=True)
    cnt_ref[...] = carry_ref[...]


def _out_proj(x2d, y2d, o2d, consts, cnt0, tm):
    n = x2d.shape[0]
    tm = min(tm, n)
    row = lambda w: pl.BlockSpec((tm, w), lambda i: (i, 0))
    colb = pl.BlockSpec((SUBLANES, tm), lambda i: (0, i))
    full = lambda a: pl.BlockSpec(a.shape, lambda i: (0,) * a.ndim)
    return pl.pallas_call(
        functools.partial(_out_proj_body, tm=tm),
        grid=(n // tm,),
        in_specs=[row(D_MODEL), row(SSD_INNER), row(ATT_WIDTH)] + [full(a) for a in consts] + [full(cnt0)],
        out_specs=[row(D_MODEL), row(D_MODEL), colb, colb, colb, full(cnt0)],
        out_shape=[jax.ShapeDtypeStruct((n, D_MODEL), F32), jax.ShapeDtypeStruct((n, D_MODEL), F32),
                   jax.ShapeDtypeStruct((SUBLANES, n), jnp.int32), jax.ShapeDtypeStruct((SUBLANES, n), F32),
                   jax.ShapeDtypeStruct((SUBLANES, n), jnp.int32), jax.ShapeDtypeStruct(cnt0.shape, F32)],
        scratch_shapes=[pltpu.VMEM(cnt0.shape, F32)],
        compiler_params=_cparams(("arbitrary",)),
        name="out_proj",
    )(x2d, y2d, o2d, *consts, cnt0)


ISSUE_UNROLL = 8


def _dispatch_body(slot_ref, from_ref, n_ref, tail_ref, xa_ref, xb_ref, xs_ref, zero_ref, sem,
                   *, tm, tiles_a, n_blocks):
    i = pl.program_id(0)

    @pl.when(i == 0)
    def _pad_fill():
        zero_ref[...] = jnp.zeros_like(zero_ref)
        one_row = lambda slot: pltpu.make_async_copy(zero_ref.at[pl.ds(0, 1)], xs_ref.at[pl.ds(slot, 1)], sem)
        block = lambda j: pltpu.make_async_copy(zero_ref, xs_ref.at[pl.ds(j * MOE_ROWS, MOE_ROWS)], sem)

        def per_expert(e, carry):
            def start(r, c):
                one_row(from_ref[e] + r).start()
                return c

            def wait(r, c):
                one_row(0).wait()
                return c
            lax.fori_loop(0, n_ref[e], start, 0)
            lax.fori_loop(0, n_ref[e], wait, 0)
            return carry
        lax.fori_loop(0, N_EXPERTS, per_expert, 0)

        def tail(j, carry):
            block(j).start()
            block(j).wait()
            return carry
        lax.fori_loop(tail_ref[0], n_blocks, tail, 0)

    def scatter_rows(src_ref):
        def issue(t, carry):
            for k in range(TOP_K):
                pltpu.make_async_copy(src_ref.at[pl.ds(t, 1)], xs_ref.at[pl.ds(slot_ref[k, t], 1)], sem).start()
            return carry
        lax.fori_loop(0, tm, issue, 0, unroll=ISSUE_UNROLL)
        for _ in range(TOP_K):
            pltpu.make_async_copy(src_ref, xs_ref.at[pl.ds(0, tm)], sem).wait()

    @pl.when(i < tiles_a)
    def _first_group():
        scatter_rows(xa_ref)

    @pl.when(i >= tiles_a)
    def _second_group():
        scatter_rows(xb_ref)


def _smem_cols(tm, first=0):
    return pl.BlockSpec((SUBLANES, tm), lambda i: (0, i + first), memory_space=pltpu.SMEM)


def _dispatch(slot, pad_from, pad_n, tail_block, xa, xb, n_blocks, tm):
    tiles_a, tiles_b = xa.shape[0] // tm, xb.shape[0] // tm
    assert tiles_a * tm == xa.shape[0] and tiles_b * tm == xb.shape[0]
    smem = pl.BlockSpec(memory_space=pltpu.SMEM)
    return pl.pallas_call(
        functools.partial(_dispatch_body, tm=tm, tiles_a=tiles_a, n_blocks=n_blocks),
        grid=(tiles_a + tiles_b,),
        in_specs=[_smem_cols(tm), smem, smem, smem,
                  pl.BlockSpec((tm, D_MODEL), lambda i: (jnp.minimum(i, tiles_a - 1), 0)),
                  pl.BlockSpec((tm, D_MODEL), lambda i: (jnp.maximum(i - tiles_a, 0), 0))],
        out_specs=pl.BlockSpec(memory_space=pl.ANY),
        out_shape=jax.ShapeDtypeStruct((n_blocks * MOE_ROWS, D_MODEL), F32),
        scratch_shapes=[pltpu.VMEM((MOE_ROWS, D_MODEL), F32), pltpu.SemaphoreType.DMA(())],
        compiler_params=_cparams(("arbitrary",)),
        name="dispatch",
    )(slot, pad_from, pad_n, tail_block, xa, xb)


MOE_COLS = 2 * LANES


def _w1_prep_body(w_ref, p_ref, o_ref):
    o_ref[0] = _dot(w_ref[0].astype(BF16), p_ref[...]).astype(BF16)


def _w1_prep(w1):
    e, d, n2 = w1.shape
    src = np.concatenate([np.arange(0, MOE_COLS, 2), np.arange(1, MOE_COLS, 2)])
    perm = np.zeros((MOE_COLS, MOE_COLS), np.float32)
    perm[src, np.arange(MOE_COLS)] = 1.0
    blk = pl.BlockSpec((1, d, MOE_COLS), lambda i, j: (i, 0, j))
    return pl.pallas_call(
        _w1_prep_body,
        grid=(e, n2 // MOE_COLS),
        in_specs=[blk, pl.BlockSpec((MOE_COLS, MOE_COLS), lambda i, j: (0, 0))],
        out_specs=blk,
        out_shape=jax.ShapeDtypeStruct(w1.shape, BF16),
        compiler_params=_cparams(("parallel", "parallel")),
        name="w1_prep",
    )(w1, jnp.asarray(perm, dtype=BF16))


def _experts_body(be_ref, bv_ref, xs_ref, w1_ref, b1_ref, w2_ref, b2_ref, ys_ref, g_ref):
    j = pl.program_id(0)
    valid = bv_ref[j]

    @pl.when(valid > 0)
    def _compute():
        rows = lax.broadcasted_iota(jnp.int32, (MOE_ROWS, 1), 0)
        x = jnp.where(rows < valid, xs_ref[...], 0.0).astype(BF16)
        for c in range(g_ref.shape[1] // LANES):
            cols = slice(c * MOE_COLS, (c + 1) * MOE_COLS)
            hc = _dot(x, w1_ref[0, :, cols]) + b1_ref[0, :, cols]
            glu = jnp.minimum(hc[:, :LANES], SWIGLU_LIMIT)
            lin = jnp.clip(hc[:, LANES:], -SWIGLU_LIMIT, SWIGLU_LIMIT)
            g_ref[:, c * LANES:(c + 1) * LANES] = (glu * jax.nn.sigmoid(SWIGLU_ALPHA * glu) * (lin + 1.0)).astype(BF16)
        ys_ref[...] = _dot(g_ref[...], w2_ref[0]) + b2_ref[0]

    @pl.when(valid <= 0)
    def _empty():
        ys_ref[...] = jnp.zeros_like(ys_ref)


def _experts(block_expert, block_valid, xs, w1p, b1p, w2, b2):
    n_blocks = xs.shape[0] // MOE_ROWS
    wspec = lambda a: pl.BlockSpec((1,) + a.shape[1:], lambda j, be, bv: (be[j], 0, 0))
    grid_spec = pltpu.PrefetchScalarGridSpec(
        num_scalar_prefetch=2,
        grid=(n_blocks,),
        in_specs=[pl.BlockSpec((MOE_ROWS, D_MODEL), lambda j, be, bv: (j, 0)),
                  wspec(w1p), wspec(b1p), wspec(w2), wspec(b2)],
        out_specs=pl.BlockSpec((MOE_ROWS, D_MODEL), lambda j, be, bv: (j, 0)),
        scratch_shapes=[pltpu.VMEM((MOE_ROWS, w2.shape[1]), BF16)],
    )
    return pl.pallas_call(
        _experts_body,
        grid_spec=grid_spec,
        out_shape=jax.ShapeDtypeStruct(xs.shape, F32),
        compiler_params=_cparams(("arbitrary",)),
        name="experts",
    )(block_expert, block_valid, xs, w1p, b1p, w2, b2)


def _combine_body(slot_ref, next_ref, gate_ref, h_ref, nf_ref, ys_ref, y_ref, buf_ref, sem, *, tm):
    i = pl.program_id(0)
    n = pl.num_programs(0)

    def request(slots, half):
        def issue(t, carry):
            for k in range(TOP_K):
                pltpu.make_async_copy(ys_ref.at[pl.ds(slots[k, t], 1)], buf_ref.at[half, k, pl.ds(t, 1)],
                                      sem.at[half]).start()
            return carry
        lax.fori_loop(0, tm, issue, 0, unroll=ISSUE_UNROLL)

    def per_half(half):
        @pl.when(i + 1 < n)
        def _request_next():
            request(next_ref, 1 - half)
        for k in range(TOP_K):
            pltpu.make_async_copy(ys_ref.at[pl.ds(0, tm)], buf_ref.at[half, k], sem.at[half]).wait()
        acc = h_ref[...]
        for k in range(TOP_K):
            acc = acc + gate_ref[:, k:k + 1] * buf_ref[half, k]
        y_ref[...] = _rms(acc, nf_ref[...])

    @pl.when(i == 0)
    def _first():
        request(slot_ref, 0)

    @pl.when(i % 2 == 0)
    def _even():
        per_half(0)

    @pl.when(i % 2 == 1)
    def _odd():
        per_half(1)


def _combine(slot, gates_rows, h, norm_final, ys, tm, first_tile):
    n = h.shape[0]
    assert n % tm == 0
    last = first_tile + n // tm - 1
    nxt = pl.BlockSpec((SUBLANES, tm), lambda i: (0, jnp.minimum(i + first_tile + 1, last)), memory_space=pltpu.SMEM)
    return pl.pallas_call(
        functools.partial(_combine_body, tm=tm),
        grid=(n // tm,),
        in_specs=[_smem_cols(tm, first_tile), nxt,
                  pl.BlockSpec((tm, TOP_K), lambda i: (i, 0)),
                  pl.BlockSpec((tm, D_MODEL), lambda i: (i, 0)),
                  pl.BlockSpec((1, D_MODEL), lambda i: (0, 0)),
                  pl.BlockSpec(memory_space=pl.ANY)],
        out_specs=pl.BlockSpec((tm, D_MODEL), lambda i: (i, 0)),
        out_shape=jax.ShapeDtypeStruct((n, D_MODEL), F32),
        scratch_shapes=[pltpu.VMEM((2, TOP_K, tm, D_MODEL), F32), pltpu.SemaphoreType.DMA((2,))],
        compiler_params=_cparams(("arbitrary",)),
        name="combine",
    )(slot, slot, gates_rows, h, norm_final, ys)


def _expansion(width):
    h = np.arange(LANES)[:, None]
    c = np.arange(SSD_HEADS * width)[None, :] // width
    return jnp.asarray((h == c).astype(np.float32), dtype=BF16)


def _pad_lanes(v):
    return jnp.pad(v.astype(F32), (0, LANES - v.shape[0]))[None, :]


def kernel(x_prompt, x_sample, state_conv, state_ssm, cache_win_k, cache_win_v, norm_mix, w_in, conv_w, conv_b,
           dt_bias, a_log, d_skip, ssd_norm, att_norm, w_out, norm_ffn, router_w, router_b, w1, b1, w2, b2,
           norm_final):
    depth = w_in.shape[0]
    assert depth == 1
    bp, seq, _ = x_prompt.shape
    bs, t_new, _ = x_sample.shape
    n_p, n_s = bp * seq, bs * t_new
    l = 0

    o_dt = SSD_INNER + CONV_DIM
    o_q = o_dt + SSD_HEADS
    wl = w_in[l]
    w_cat = jnp.concatenate([wl[:, :o_dt], wl[:, o_q:], jnp.pad(wl[:, o_dt:o_q], ((0, 0), (0, DT_PAD - SSD_HEADS)))],
                            axis=1).astype(BF16)
    g_mix = norm_mix[l][None, :]
    ssd_consts = (conv_w[l], conv_b[l][None, :], _pad_lanes(dt_bias[l]), _pad_lanes(a_log[l]),
                  jnp.repeat(d_skip[l], SSD_HEAD_DIM)[None, :], ssd_norm[l][None, :],
                  _expansion(SSD_HEAD_DIM), _expansion(LANES))
    slopes = jnp.exp2(-8.0 * jnp.arange(1, ATT_HEADS + 1, dtype=F32) / ATT_HEADS)
    out_consts = (att_norm[l][None, :], w_out[l][:SSD_INNER].astype(BF16), w_out[l][SSD_INNER:].astype(BF16),
                  norm_ffn[l][None, :], router_w[l].T.astype(BF16), router_b[l][:, None].astype(F32))
    w1p = _w1_prep(w1[l])
    b1p = b1[l].reshape(N_EXPERTS, -1, LANES, 2).transpose(0, 1, 3, 2).reshape(N_EXPERTS, 1, -1)
    w2b = w2[l].astype(BF16)
    b2r = b2[l][:, None, :]

    zx, q, k, v, dt, kt, vt = _in_proj(x_prompt.reshape(n_p, D_MODEL), g_mix, w_cat, 256, seq)
    y_p, conv_p, ssm_p = _ssd(zx.reshape(bp, seq, ZX_WIDTH), dt.reshape(bp, seq, DT_PAD), ssd_consts, None, BF16)
    o_p = _attn_prompt(q, k, v, slopes, bp, seq)
    keep = min(DILATED_PATTERNS[-1][0], seq)
    k_p = kt.reshape(bp, ATT_HEADS, ATT_HEAD_DIM, seq).transpose(0, 3, 1, 2)[:, seq - keep:]
    v_p = vt.reshape(bp, ATT_HEADS, ATT_HEAD_DIM, seq).transpose(0, 3, 1, 2)[:, seq - keep:]

    zx_s, q_s, k_s, v_s, dt_s = _in_proj(x_sample.reshape(n_s, D_MODEL), g_mix, w_cat, 256)
    init = (state_conv[l], state_ssm[l].reshape(bs, SSD_INNER, SSD_STATE))
    y_s, conv_s, ssm_s = _ssd(zx_s.reshape(bs, t_new, ZX_WIDTH), dt_s.reshape(bs, t_new, DT_PAD), ssd_consts, init, F32)

    def head_major(a):
        a = a.reshape(bs, t_new, ATT_HEADS, ATT_HEAD_DIM).transpose(0, 2, 1, 3)
        return jnp.pad(a, ((0, 0), (0, 0), (0, DEC_ROWS - t_new), (0, 0)))
    o_s = _attn_decode(head_major(q_s), head_major(k_s), head_major(v_s),
                       cache_win_k[l].transpose(0, 2, 3, 1), cache_win_v[l].transpose(0, 2, 3, 1), t_new)
    o_s = o_s[:, :, :t_new].transpose(0, 2, 1, 3)

    cnt0 = jnp.zeros((N_EXPERTS, LANES), F32)
    h_p, xn_p, idx_p, gate_p, rank_p, cnt_p = _out_proj(x_prompt.reshape(n_p, D_MODEL), y_p.reshape(n_p, SSD_INNER),
                                                        o_p, out_consts, cnt0, 256)
    h_s, xn_s, idx_s, gate_s, rank_s, cnt_all = _out_proj(x_sample.reshape(n_s, D_MODEL), y_s.reshape(n_s, SSD_INNER),
                                                          o_s.reshape(n_s, ATT_WIDTH), out_consts, cnt_p, 256)

    counts = cnt_all[:, 0].astype(jnp.int32)
    padded = (counts + MOE_ROWS - 1) // MOE_ROWS * MOE_ROWS
    pad_end = jnp.cumsum(padded)
    pad_start = pad_end - padded
    n_blocks = -(-((n_p + n_s) * TOP_K) // MOE_ROWS) + N_EXPERTS
    blk0 = jnp.arange(n_blocks, dtype=jnp.int32) * MOE_ROWS
    owner = blk0[:, None] >= pad_end[None, :]
    block_expert = jnp.minimum(jnp.sum(owner, axis=1), N_EXPERTS - 1).astype(jnp.int32)
    onehot = block_expert[:, None] == jnp.arange(N_EXPERTS, dtype=jnp.int32)[None, :]
    used = jnp.sum(jnp.where(onehot, (blk0[:, None] - pad_start[None, :]), 0), axis=1)
    block_valid = jnp.clip(jnp.sum(jnp.where(onehot, counts[None, :], 0), axis=1) - used, 0, MOE_ROWS).astype(jnp.int32)

    tm = min(256, n_s)
    idx_all = jnp.concatenate([idx_p, idx_s], axis=1)
    rank_all = jnp.concatenate([rank_p, rank_s], axis=1)
    first_slot = functools.reduce(lambda acc, e: jnp.where(idx_all == e, pad_start[e], acc), range(N_EXPERTS),
                                  jnp.zeros_like(idx_all))
    slot_all = first_slot + rank_all
    xs = _dispatch(slot_all, pad_start + counts, padded - counts, pad_end[N_EXPERTS - 1:] // MOE_ROWS,
                   xn_p, xn_s, n_blocks, tm)
    ys = _experts(block_expert, block_valid, xs, w1p, b1p, w2b, b2r)
    nfin = norm_final[None, :]
    y_prompt = _combine(slot_all, gate_p[:TOP_K].T, h_p, nfin, ys, tm, 0)
    y_sample = _combine(slot_all, gate_s[:TOP_K].T, h_s, nfin, ys, tm, n_p // tm)

    return (y_prompt.reshape(bp, seq, D_MODEL), y_sample.reshape(bs, t_new, D_MODEL),
            conv_p[None], ssm_p.reshape(1, bp, SSD_HEADS, SSD_HEAD_DIM, SSD_STATE), k_p[None], v_p[None],
            conv_s[None], ssm_s.reshape(1, bs, SSD_HEADS, SSD_HEAD_DIM, SSD_STATE),
            k_s.reshape(1, bs, t_new, ATT_HEADS, ATT_HEAD_DIM), v_s.reshape(1, bs, t_new, ATT_HEADS, ATT_HEAD_DIM))
```

```python
import functools

import jax
import jax.numpy as jnp
import numpy as np
from jax import lax
from jax.experimental import pallas as pl
from jax.experimental.pallas import tpu as pltpu

F32 = jnp.float32
BF16 = jnp.bfloat16

D_MODEL = 1024
SSD_HEADS = 16
SSD_HEAD_DIM = 64
SSD_INNER = SSD_HEADS * SSD_HEAD_DIM
SSD_GROUPS = 2
SSD_STATE = 128
SSD_CONV = 4
SSD_CHUNK = 128
CONV_DIM = SSD_INNER + 2 * SSD_GROUPS * SSD_STATE
ATT_HEADS = 8
ATT_HEAD_DIM = 64
ATT_WIDTH = ATT_HEADS * ATT_HEAD_DIM
DILATED_PATTERNS = ((128, 1), (512, 4), (2048, 16))
ATT_BLOCK = 128
N_EXPERTS = 32
TOP_K = 4
SWIGLU_LIMIT = 7.0
SWIGLU_ALPHA = 1.702
NORM_EPS = 1e-5

LANES = 128
SUBLANES = 8
ZX_WIDTH = SSD_INNER + CONV_DIM
DT_PAD = LANES
W_CAT = ZX_WIDTH + 3 * ATT_WIDTH + DT_PAD
CONV_HIST = SUBLANES
MOE_ROWS = 512
VMEM_LIMIT = 56 * 1024 * 1024

NEG_INF = float("-inf")


def _cparams(sem):
    return pltpu.CompilerParams(dimension_semantics=sem, vmem_limit_bytes=VMEM_LIMIT)


def _rms(x, g):
    return x * lax.rsqrt(jnp.mean(x * x, axis=-1, keepdims=True) + NORM_EPS) * g


def _dot(a, b):
    return jnp.dot(a, b, preferred_element_type=F32)


def _dot_nt(a, b):
    return lax.dot_general(a, b, (((1,), (1,)), ((), ())), preferred_element_type=F32)


def _split3(v):
    hi = v.astype(BF16)
    r1 = v - hi.astype(F32)
    mid = r1.astype(BF16)
    lo = (r1 - mid.astype(F32)).astype(BF16)
    return hi, mid, lo


def _dot3(v, m):
    hi, mid, lo = _split3(v)
    return _dot(hi, m) + _dot(mid, m) + _dot(lo, m)


def _dot3_lhs(m, v):
    hi, mid, lo = _split3(v)
    return _dot(m, hi) + _dot(m, mid) + _dot(m, lo)


def _silu(x):
    half = 0.5 * x
    return half * (1.0 + jnp.tanh(half))


def _in_proj_body(x_ref, g_ref, w_ref, zx_ref, q_ref, k_ref, v_ref, dt_ref, *t_refs):
    hn = _rms(x_ref[...], g_ref[...]).astype(BF16)
    o = 0
    for ref, width in ((zx_ref, ZX_WIDTH), (q_ref, ATT_WIDTH), (k_ref, ATT_WIDTH), (v_ref, ATT_WIDTH),
                       (dt_ref, DT_PAD)):
        ref[...] = _dot(hn, w_ref[:, o:o + width])
        o += width
    for src, dst in zip((k_ref, v_ref), t_refs):
        for j in range(ATT_WIDTH // LANES):
            dst[0, j * LANES:(j + 1) * LANES, :] = src[:, j * LANES:(j + 1) * LANES].T


def _in_proj(x2d, g, w_cat, tm, seq=None):
    n = x2d.shape[0]
    tm = min(tm, n)
    per = 1 if seq is None else seq // tm
    row = lambda w: pl.BlockSpec((tm, w), lambda b, i: (b * per + i, 0))
    full = lambda a: pl.BlockSpec(a.shape, lambda b, i: (0,) * a.ndim)
    widths = (ZX_WIDTH, ATT_WIDTH, ATT_WIDTH, ATT_WIDTH, DT_PAD)
    out_specs = [row(w) for w in widths]
    out_shape = [jax.ShapeDtypeStruct((n, w), F32) for w in widths]
    if seq is not None:
        out_specs += [pl.BlockSpec((1, ATT_WIDTH, tm), lambda b, i: (b, 0, i))] * 2
        out_shape += [jax.ShapeDtypeStruct((n // seq, ATT_WIDTH, seq), F32)] * 2
    return pl.pallas_call(
        _in_proj_body,
        grid=(n // (tm * per), per),
        in_specs=[row(D_MODEL), full(g), full(w_cat)],
        out_specs=out_specs,
        out_shape=out_shape,
        compiler_params=_cparams(("parallel", "parallel")),
        name="in_proj",
    )(x2d, g, w_cat)


def _ssd_body(*refs, l_blk, has_init):
    T = SSD_CHUNK
    if has_init:
        (zx_ref, dt_ref, cw_ref, cb_ref, dtb_ref, alog_ref, dskip_ref, norm_ref, e64_ref,
         cinit_ref, sinit_ref, y_ref, conv_out_ref, ssm_out_ref,
         ext_ref, act_ref, state_ref, cst_ref, ybuf_ref, zpad_ref, dtpad_ref) = refs
    else:
        (zx_ref, dt_ref, cw_ref, cb_ref, dtb_ref, alog_ref, dskip_ref, norm_ref, e64_ref,
         y_ref, conv_out_ref, ssm_out_ref,
         ext_ref, act_ref, state_ref, cst_ref, ybuf_ref) = refs
    c = pl.program_id(1)
    n_tile = SSD_INNER // LANES

    @pl.when(c == 0)
    def _start():
        if has_init:
            ext_ref[0:CONV_HIST, :] = jnp.zeros((CONV_HIST, CONV_DIM), F32)
            ext_ref[CONV_HIST - (SSD_CONV - 1):CONV_HIST, :] = cinit_ref[0]
            for j in range(n_tile):
                state_ref[:, j * LANES:(j + 1) * LANES] = sinit_ref[0, j * LANES:(j + 1) * LANES, :].T
        else:
            ext_ref[0:CONV_HIST, :] = jnp.zeros((CONV_HIST, CONV_DIM), F32)
            state_ref[...] = jnp.zeros_like(state_ref)

    if l_blk == T:
        ext_ref[CONV_HIST:CONV_HIST + T, :] = zx_ref[0, :, SSD_INNER:ZX_WIDTH]
        z_of = lambda sl: zx_ref[0, :, sl]
        dt_raw = dt_ref[0]
    else:
        ext_ref[CONV_HIST:CONV_HIST + T, :] = jnp.zeros((T, CONV_DIM), F32)
        ext_ref[CONV_HIST:CONV_HIST + l_blk, :] = zx_ref[0, :, SSD_INNER:ZX_WIDTH]
        zpad_ref[...] = jnp.zeros_like(zpad_ref)
        zpad_ref[0:l_blk, :] = zx_ref[0, :, 0:SSD_INNER]
        dtpad_ref[...] = jnp.zeros_like(dtpad_ref)
        dtpad_ref[0:l_blk, :] = dt_ref[0]
        z_of = lambda sl: zpad_ref[:, sl]
        dt_raw = dtpad_ref[...]

    cw = CONV_DIM // 3
    for cc in range(3):
        sl = slice(cc * cw, (cc + 1) * cw)
        acc = cb_ref[:, sl]
        for j in range(SSD_CONV):
            o = CONV_HIST - (SSD_CONV - 1) + j
            acc = acc + ext_ref[o:o + T, sl] * cw_ref[j:j + 1, sl]
        act_ref[:, sl] = _silu(acc)

    row = lax.broadcasted_iota(jnp.int32, (T, LANES), 0)
    col = lax.broadcasted_iota(jnp.int32, (T, LANES), 1)
    tri = row >= col
    tri_bf = jnp.where(tri, 1.0, 0.0).astype(BF16)
    even = col < SSD_HEAD_DIM

    xdt = dt_raw + dtb_ref[...]
    dtv = jnp.maximum(xdt, 0.0) + jnp.log1p(jnp.exp(-jnp.abs(xdt)))
    if l_blk < T:
        dtv = jnp.where(row < l_blk, dtv, 0.0)
    d_a = dtv * (-jnp.exp(alog_ref[...]))
    cs = _dot3_lhs(tri_bf, d_a)
    cst_ref[...] = cs.T
    ex = _dot3(jnp.concatenate([dtv, cs], axis=0), e64_ref[...])
    dt_ex, cs_ex = ex[0:T], ex[T:2 * T]
    ecs_ex = jnp.exp(cs_ex)
    dd_ex = dt_ex * jnp.exp(cs_ex[T - 1:T] - cs_ex)

    gw = SSD_INNER // SSD_GROUPS
    heads_per_group = SSD_HEADS // SSD_GROUPS
    for g in range(SSD_GROUPS):
        gsl = slice(g * gw, (g + 1) * gw)
        b_g = act_ref[:, SSD_INNER + g * SSD_STATE:SSD_INNER + (g + 1) * SSD_STATE]
        c_off = SSD_INNER + SSD_GROUPS * SSD_STATE
        c_g = act_ref[:, c_off + g * SSD_STATE:c_off + (g + 1) * SSD_STATE].astype(BF16)
        cb = _dot_nt(c_g, b_g.astype(BF16))
        b_gt = b_g.T.astype(BF16)
        x_g = act_ref[:, gsl]
        x_dt = (x_g * dt_ex[:, gsl]).astype(BF16)
        x_dd = (x_g * dd_ex[:, gsl]).astype(BF16)
        st_old = state_ref[:, gsl]
        y_off = _dot(c_g, st_old.astype(BF16)) * ecs_ex[:, gsl]
        state_ref[:, gsl] = st_old * ecs_ex[T - 1:T, gsl] + _dot(b_gt, x_dd)
        for jp in range(heads_per_group // 2):
            h0 = g * heads_per_group + 2 * jp
            psl = slice(jp * LANES, (jp + 1) * LANES)
            osl = slice(g * gw + jp * LANES, g * gw + (jp + 1) * LANES)
            pair = cs_ex[:, osl]
            swapped = pltpu.roll(pair, SSD_HEAD_DIM, axis=1)
            cols = (jnp.where(even, pair, swapped), jnp.where(even, swapped, pair))
            yd = []
            for col, h in zip(cols, (h0, h0 + 1)):
                seg = col - cst_ref[h:h + 1, :]
                lmat = jnp.exp(jnp.where(tri, seg, NEG_INF))
                yd.append(_dot((cb * lmat).astype(BF16), x_dt[:, psl]))
            y_pair = jnp.where(even, yd[0], yd[1]) + y_off[:, psl]
            ybuf_ref[:, osl] = y_pair + dskip_ref[:, osl] * x_g[:, psl]

    for g in range(SSD_GROUPS):
        gsl = slice(g * gw, (g + 1) * gw)
        yg = ybuf_ref[:, gsl] * _silu(z_of(gsl))
        yn = yg * lax.rsqrt(jnp.mean(yg * yg, axis=-1, keepdims=True) + NORM_EPS) * norm_ref[:, gsl]
        y_ref[0, :, gsl] = yn[0:l_blk].astype(y_ref.dtype)

    @pl.when(c == pl.num_programs(1) - 1)
    def _finish():
        lo = CONV_HIST + l_blk - (SSD_CONV - 1)
        conv_out_ref[0] = ext_ref[lo:lo + SSD_CONV - 1, :]
        for j in range(n_tile):
            ssm_out_ref[0, j * LANES:(j + 1) * LANES, :] = state_ref[:, j * LANES:(j + 1) * LANES].T

    ext_ref[0:CONV_HIST, :] = ext_ref[T:T + CONV_HIST, :]


def _ssd(zx3, dt3, consts, init, y_dtype):
    b, L, _ = zx3.shape
    T = SSD_CHUNK
    l_blk = T if L % T == 0 else L
    n_chunks = L // l_blk
    has_init = init is not None
    full = lambda a: pl.BlockSpec(a.shape, lambda i, c: (0,) * a.ndim)
    in_specs = [pl.BlockSpec((1, l_blk, ZX_WIDTH), lambda i, c: (i, c, 0)),
                pl.BlockSpec((1, l_blk, DT_PAD), lambda i, c: (i, c, 0))] + [full(a) for a in consts]
    args = [zx3, dt3, *consts]
    scratch = [pltpu.VMEM((T + CONV_HIST, CONV_DIM), F32),
               pltpu.VMEM((T, CONV_DIM), F32),
               pltpu.VMEM((SSD_STATE, SSD_INNER), F32),
               pltpu.VMEM((LANES, T), F32),
               pltpu.VMEM((T, SSD_INNER), F32)]
    if has_init:
        in_specs += [pl.BlockSpec((1, SSD_CONV - 1, CONV_DIM), lambda i, c: (i, 0, 0)),
                     pl.BlockSpec((1, SSD_INNER, SSD_STATE), lambda i, c: (i, 0, 0))]
        args += list(init)
        scratch += [pltpu.VMEM((T, SSD_INNER), F32), pltpu.VMEM((T, DT_PAD), F32)]
    return pl.pallas_call(
        functools.partial(_ssd_body, l_blk=l_blk, has_init=has_init),
        grid=(b, n_chunks),
        in_specs=in_specs,
        out_specs=[pl.BlockSpec((1, l_blk, SSD_INNER), lambda i, c: (i, c, 0)),
                   pl.BlockSpec((1, SSD_CONV - 1, CONV_DIM), lambda i, c: (i, 0, 0)),
                   pl.BlockSpec((1, SSD_INNER, SSD_STATE), lambda i, c: (i, 0, 0))],
        out_shape=[jax.ShapeDtypeStruct((b, L, SSD_INNER), y_dtype),
                   jax.ShapeDtypeStruct((b, SSD_CONV - 1, CONV_DIM), F32),
                   jax.ShapeDtypeStruct((b, SSD_INNER, SSD_STATE), F32)],
        scratch_shapes=scratch,
        compiler_params=_cparams(("parallel", "arbitrary")),
        name="ssd_init" if has_init else "ssd",
    )(*args)


ATT_UNROLL = 16


def _unroll(n):
    return max(d for d in range(1, ATT_UNROLL + 1) if n % d == 0)


ATT_RES = DILATED_PATTERNS[-1][1]


def _attn_tables():
    B = ATT_BLOCK
    t1 = np.full((len(DILATED_PATTERNS), B, B), -np.inf, np.float32)
    t2 = np.full((len(DILATED_PATTERNS), B, 2 * B), -np.inf, np.float32)
    for p, (window, dil) in enumerate(DILATED_PATTERNS):
        m = ATT_RES // dil
        w = B // m
        rho = np.arange(B)
        c = m * (rho % w) + rho // w
        ck2 = np.concatenate([c - B, c])
        for tab, ck in ((t1, c), (t2, ck2)):
            delta = c[:, None] - ck[None, :]
            ok = (delta >= 0) & (delta <= window // dil)
            tab[p] = np.where(ok, -(dil * delta).astype(np.float32), -np.inf)
    return jnp.asarray(t1), jnp.asarray(t2)


def _attn_body(slopes_ref, t1_ref, t2_ref, q_ref, k_ref, v_ref, o_ref, qd_ref, kd_ref, vd_ref, op_ref, lp_ref,
               stage_ref, b1_ref, b2_ref, in_sem, out_sem, *, seq):
    B = ATT_BLOCK
    hp = pl.program_id(1)
    even = lax.broadcasted_iota(jnp.int32, (B, LANES), 1) < ATT_HEAD_DIM
    scale = ATT_HEAD_DIM ** -0.5

    def in_copy(a, r):
        src, dst = ((q_ref, qd_ref), (k_ref, kd_ref), (v_ref, vd_ref))[a]
        return pltpu.make_async_copy(src.at[:, r, :], dst.at[r], in_sem.at[a, r])
    for r in range(ATT_RES):
        for a in range(3):
            in_copy(a, r).start()

    for e in range(2):
        slope = slopes_ref[2 * hp + e]
        for p in range(len(DILATED_PATTERNS)):
            b1_ref[e * len(DILATED_PATTERNS) + p] = t1_ref[p] * slope
            b2_ref[e * len(DILATED_PATTERNS) + p] = t2_ref[p] * slope

    for r in range(ATT_RES):
        for a in range(3):
            in_copy(a, r).wait()

    def block(p, r_d, n, first):
        dil = DILATED_PATTERNS[p][1]
        m = ATT_RES // dil
        w = B // m

        def slab(j, nblk):
            start = nblk * w
            return r_d + dil * j, pl.ds(start if isinstance(start, int) else pl.multiple_of(start, w), w)

        def gather(src, nblk):
            parts = []
            for j in range(m):
                r, rows = slab(j, nblk)
                parts.append(src[r, rows, :])
            return parts[0] if m == 1 else jnp.concatenate(parts, axis=0)

        qb = gather(qd_ref, n) * scale
        if first:
            kb, vb = gather(kd_ref, n), gather(vd_ref, n)
            bias = lambda e: b1_ref[e * len(DILATED_PATTERNS) + p]
        else:
            kb = jnp.concatenate([gather(kd_ref, n - 1), gather(kd_ref, n)], axis=0)
            vb = jnp.concatenate([gather(vd_ref, n - 1), gather(vd_ref, n)], axis=0)
            bias = lambda e: b2_ref[e * len(DILATED_PATTERNS) + p]
        q2 = jnp.concatenate([jnp.where(even, qb, 0.0), jnp.where(even, 0.0, qb)], axis=0).astype(BF16)
        s = _dot_nt(q2, kb.astype(BF16)) + jnp.concatenate([bias(0), bias(1)], axis=0)
        mx = jnp.max(s, axis=-1, keepdims=True)
        pr = jnp.exp(s - mx)
        l = jnp.sum(pr, axis=-1, keepdims=True)
        o2 = _dot(pr.astype(BF16), vb.astype(BF16)) / l
        lse2 = jnp.broadcast_to(mx + jnp.log(l), (2 * B, LANES))
        o = jnp.where(even, o2[:B], o2[B:])
        lse = jnp.where(even, lse2[:B], lse2[B:])
        for j in range(m):
            r, rows = slab(j, n)
            op_ref[p, r, rows, :] = o[j * w:(j + 1) * w]
            lp_ref[p, r, rows, :] = lse[j * w:(j + 1) * w]

    for p, (_, dil) in enumerate(DILATED_PATTERNS):
        nb = seq // (dil * B)
        u_first = _unroll(dil)

        def first(i, carry, p=p, u_first=u_first):
            for u in range(u_first):
                block(p, i * u_first + u, 0, True)
            return carry
        lax.fori_loop(0, dil // u_first, first, 0)
        n_later = dil * (nb - 1)
        if n_later:
            u_later = _unroll(n_later)

            def later(i, carry, p=p, nb=nb, u_later=u_later):
                for u in range(u_later):
                    j = i * u_later + u
                    block(p, j // (nb - 1), j % (nb - 1) + 1, False)
                return carry
            lax.fori_loop(0, n_later // u_later, later, 0)

    def out_copy(r):
        return pltpu.make_async_copy(stage_ref.at[r], o_ref.at[:, r, :], out_sem.at[r])
    for r in range(ATT_RES):
        l0, l1, l2 = lp_ref[0, r], lp_ref[1, r], lp_ref[2, r]
        m = jnp.maximum(jnp.maximum(l0, l1), l2)
        w0, w1, w2 = jnp.exp(l0 - m), jnp.exp(l1 - m), jnp.exp(l2 - m)
        num = w0 * op_ref[0, r] + w1 * op_ref[1, r] + w2 * op_ref[2, r]
        stage_ref[r] = num / (w0 + w1 + w2)
        out_copy(r).start()
    for r in range(ATT_RES):
        out_copy(r).wait()


def _attn_prompt(q, k, v, slopes, batch, seq):
    n_hp = ATT_HEADS // 2
    n_pat = len(DILATED_PATTERNS)
    assert len(DILATED_PATTERNS) == 3 and all(ATT_RES % d == 0 and seq % (d * ATT_BLOCK) == 0 and w // d == ATT_BLOCK
                                              for w, d in DILATED_PATTERNS)
    per = seq // ATT_RES
    by_res = lambda a: a.reshape(batch * per, ATT_RES, ATT_WIDTH)
    blk = pl.BlockSpec((per, ATT_RES, LANES), lambda b, h: (b, 0, h))
    t1, t2 = _attn_tables()
    full = lambda a: pl.BlockSpec(a.shape, lambda b, h: (0,) * a.ndim)
    res = pltpu.VMEM((ATT_RES, per, LANES), F32)
    out = pl.pallas_call(
        functools.partial(_attn_body, seq=seq),
        grid=(batch, n_hp),
        in_specs=[pl.BlockSpec(memory_space=pltpu.SMEM), full(t1), full(t2), blk, blk, blk],
        out_specs=blk,
        out_shape=jax.ShapeDtypeStruct((batch * per, ATT_RES, ATT_WIDTH), F32),
        scratch_shapes=[res, res, res,
                        pltpu.VMEM((n_pat, ATT_RES, per, LANES), F32), pltpu.VMEM((n_pat, ATT_RES, per, LANES), F32),
                        res,
                        pltpu.VMEM((2 * n_pat, ATT_BLOCK, ATT_BLOCK), F32),
                        pltpu.VMEM((2 * n_pat, ATT_BLOCK, 2 * ATT_BLOCK), F32),
                        pltpu.SemaphoreType.DMA((3, ATT_RES)), pltpu.SemaphoreType.DMA((ATT_RES,))],
        compiler_params=_cparams(("parallel", "parallel")),
        name="attn_prompt",
    )(slopes, t1, t2, by_res(q), by_res(k), by_res(v))
    return out.reshape(batch * seq, ATT_WIDTH)


DEC_ROWS = SUBLANES


def _decode_tables(t_new, m_cache):
    slopes = 2.0 ** (-8.0 * np.arange(1, ATT_HEADS + 1) / ATT_HEADS)

    def mult(dist):
        return sum(1 for window, dil in DILATED_PATTERNS if 0 <= dist <= window and dist % dil == 0)

    bias_c = np.zeros((ATT_HEADS, DEC_ROWS, m_cache), np.float32)
    mult_c = np.ones((DEC_ROWS, m_cache), np.float32)
    bias_n = np.full((ATT_HEADS, DEC_ROWS, LANES), -np.inf, np.float32)
    mult_n = np.zeros((DEC_ROWS, LANES), np.float32)
    for t in range(t_new):
        dist = m_cache + t - np.arange(m_cache)
        mu = np.array([mult(d) for d in dist], np.float32)
        mult_c[t] = mu
        bias_c[:, t, :] = np.where(mu > 0, -slopes[:, None] * dist[None, :], -np.inf)
        for t2 in range(t_new):
            if mult(t - t2):
                bias_n[:, t, t2] = -slopes * (t - t2)
                mult_n[t, t2] = mult(t - t2)
    bias_n[:, t_new:, 0] = 0.0
    mult_n[t_new:, 0] = 1.0
    return jnp.asarray(bias_c), jnp.asarray(mult_c), jnp.asarray(bias_n), jnp.asarray(mult_n)


def _attn_dec_body(q_ref, kn_ref, vn_ref, ck_ref, cv_ref, bc_ref, mc_ref, bn_ref, mn_ref, o_ref, *, t_new):
    scale = ATT_HEAD_DIM ** -0.5
    for h in range(ATT_HEADS):
        qh = q_ref[0, h] * scale
        sc = _dot(qh.astype(BF16), ck_ref[0, h].astype(BF16)) + bc_ref[h]
        kn, vn = kn_ref[0, h], vn_ref[0, h]
        sn = [jnp.sum(qh * kn[t:t + 1, :], axis=-1, keepdims=True) + bn_ref[h, :, t:t + 1] for t in range(t_new)]
        m = functools.reduce(jnp.maximum, sn, jnp.max(sc, axis=-1, keepdims=True))
        pc = jnp.exp(sc - m) * mc_ref[...]
        l = jnp.sum(pc, axis=-1, keepdims=True)
        acc = _dot_nt(pc.astype(BF16), cv_ref[0, h].astype(BF16))
        for t in range(t_new):
            pn = jnp.exp(sn[t] - m) * mn_ref[:, t:t + 1]
            l = l + pn
            acc = acc + pn * vn[t:t + 1, :]
        o_ref[0, h] = acc / l


def _attn_decode(q4, kn4, vn4, cache_kt, cache_vt, t_new):
    b = q4.shape[0]
    m_cache = cache_kt.shape[3]
    assert t_new <= DEC_ROWS
    tables = _decode_tables(t_new, m_cache)
    new = pl.BlockSpec((1, ATT_HEADS, DEC_ROWS, ATT_HEAD_DIM), lambda i: (i, 0, 0, 0))
    cache = pl.BlockSpec((1, ATT_HEADS, ATT_HEAD_DIM, m_cache), lambda i: (i, 0, 0, 0))
    full = lambda a: pl.BlockSpec(a.shape, lambda i: (0,) * a.ndim)
    return pl.pallas_call(
        functools.partial(_attn_dec_body, t_new=t_new),
        grid=(b,),
        in_specs=[new, new, new, cache, cache] + [full(a) for a in tables],
        out_specs=new,
        out_shape=jax.ShapeDtypeStruct(q4.shape, F32),
        compiler_params=_cparams(("parallel",)),
        name="attn_decode",
    )(q4, kn4, vn4, cache_kt, cache_vt, *tables)


def _out_proj_body(x_ref, y_ref, o_ref, an_ref, wy_ref, wo_ref, nf_ref, rw_ref, rb_ref, cnt0_ref,
                   h_ref, xn_ref, idx_ref, gate_ref, rank_ref, cnt_ref, carry_ref, *, tm):
    i = pl.program_id(0)

    @pl.when(i == 0)
    def _start():
        carry_ref[...] = cnt0_ref[...]

    on = _rms(o_ref[...], an_ref[...]).astype(BF16)
    mixed = _dot(y_ref[...].astype(BF16), wy_ref[...]) + _dot(on, wo_ref[...])
    h = x_ref[...] + mixed
    h_ref[...] = h
    xn = _rms(h, nf_ref[...])
    xn_ref[...] = xn
    logits = _dot_nt(rw_ref[...], xn.astype(BF16)) + rb_ref[...]

    e_iota = lax.broadcasted_iota(jnp.int32, (N_EXPERTS, tm), 0)
    vals, idxs, sels = [], [], []
    cur = logits
    for _ in range(TOP_K):
        mx = jnp.max(cur, axis=0, keepdims=True)
        ix = jnp.min(jnp.where(cur == mx, e_iota, N_EXPERTS), axis=0, keepdims=True)
        sel = e_iota == ix
        cur = jnp.where(sel, NEG_INF, cur)
        vals.append(mx)
        idxs.append(ix)
        sels.append(sel)
    ex = [jnp.exp(v - vals[0]) for v in vals]
    den = functools.reduce(lambda a, b: a + b, ex)

    sel_any = functools.reduce(jnp.logical_or, sels)
    sel_f = jnp.where(sel_any, 1.0, 0.0)
    r = lax.broadcasted_iota(jnp.int32, (tm, tm), 0)
    c = lax.broadcasted_iota(jnp.int32, (tm, tm), 1)
    upper = jnp.where(r <= c, 1.0, 0.0).astype(BF16)
    cum = _dot(sel_f.astype(BF16), upper)
    before = carry_ref[:, 0:1] + cum - sel_f
    for k in range(TOP_K):
        idx_ref[k:k + 1, :] = idxs[k]
        gate_ref[k:k + 1, :] = ex[k] / den
        rank_ref[k:k + 1, :] = jnp.sum(jnp.where(sels[k], before, 0.0), axis=0, keepdims=True).astype(jnp.int32)
    idx_ref[TOP_K:SUBLANES, :] = jnp.zeros((SUBLANES - TOP_K, tm), jnp.int32)
    gate_ref[TOP_K:SUBLANES, :] = jnp.zeros((SUBLANES - TOP_K, tm), F32)
    rank_ref[TOP_K:SUBLANES, :] = jnp.zeros((SUBLANES - TOP_K, tm), jnp.int32)
    carry_ref[...] = carry_ref[...] + jnp.max(cum, axis=1, keepdims=True)
    cnt_ref[...] = carry_ref[...]


def _out_proj(x2d, y2d, o2d, consts, cnt0, tm):
    n = x2d.shape[0]
    tm = min(tm, n)
    row = lambda w: pl.BlockSpec((tm, w), lambda i: (i, 0))
    colb = pl.BlockSpec((SUBLANES, tm), lambda i: (0, i))
    full = lambda a: pl.BlockSpec(a.shape, lambda i: (0,) * a.ndim)
    return pl.pallas_call(
        functools.partial(_out_proj_body, tm=tm),
        grid=(n // tm,),
        in_specs=[row(D_MODEL), row(SSD_INNER), row(ATT_WIDTH)] + [full(a) for a in consts] + [full(cnt0)],
        out_specs=[row(D_MODEL), row(D_MODEL), colb, colb, colb, full(cnt0)],
        out_shape=[jax.ShapeDtypeStruct((n, D_MODEL), F32), jax.ShapeDtypeStruct((n, D_MODEL), F32),
                   jax.ShapeDtypeStruct((SUBLANES, n), jnp.int32), jax.ShapeDtypeStruct((SUBLANES, n), F32),
                   jax.ShapeDtypeStruct((SUBLANES, n), jnp.int32), jax.ShapeDtypeStruct(cnt0.shape, F32)],
        scratch_shapes=[pltpu.VMEM(cnt0.shape, F32)],
        compiler_params=_cparams(("arbitrary",)),
        name="out_proj",
    )(x2d, y2d, o2d, *consts, cnt0)


ISSUE_UNROLL = 8


def _dispatch_body(slot_ref, from_ref, n_ref, tail_ref, xa_ref, xb_ref, xs_ref, zero_ref, sem,
                   *, tm, tiles_a, n_blocks):
    i = pl.program_id(0)

    @pl.when(i == 0)
    def _pad_fill():
        zero_ref[...] = jnp.zeros_like(zero_ref)
        one_row = lambda slot: pltpu.make_async_copy(zero_ref.at[pl.ds(0, 1)], xs_ref.at[pl.ds(slot, 1)], sem)
        block = lambda j: pltpu.make_async_copy(zero_ref, xs_ref.at[pl.ds(j * MOE_ROWS, MOE_ROWS)], sem)

        def per_expert(e, carry):
            def start(r, c):
                one_row(from_ref[e] + r).start()
                return c

            def wait(r, c):
                one_row(0).wait()
                return c
            lax.fori_loop(0, n_ref[e], start, 0)
            lax.fori_loop(0, n_ref[e], wait, 0)
            return carry
        lax.fori_loop(0, N_EXPERTS, per_expert, 0)

        def tail(j, carry):
            block(j).start()
            block(j).wait()
            return carry
        lax.fori_loop(tail_ref[0], n_blocks, tail, 0)

    def scatter_rows(src_ref):
        def issue(t, carry):
            for k in range(TOP_K):
                pltpu.make_async_copy(src_ref.at[pl.ds(t, 1)], xs_ref.at[pl.ds(slot_ref[k, t], 1)], sem).start()
            return carry
        lax.fori_loop(0, tm, issue, 0, unroll=ISSUE_UNROLL)
        for _ in range(TOP_K):
            pltpu.make_async_copy(src_ref, xs_ref.at[pl.ds(0, tm)], sem).wait()

    @pl.when(i < tiles_a)
    def _first_group():
        scatter_rows(xa_ref)

    @pl.when(i >= tiles_a)
    def _second_group():
        scatter_rows(xb_ref)


def _smem_cols(tm, first=0):
    return pl.BlockSpec((SUBLANES, tm), lambda i: (0, i + first), memory_space=pltpu.SMEM)


def _dispatch(slot, pad_from, pad_n, tail_block, xa, xb, n_blocks, tm):
    tiles_a, tiles_b = xa.shape[0] // tm, xb.shape[0] // tm
    assert tiles_a * tm == xa.shape[0] and tiles_b * tm == xb.shape[0]
    smem = pl.BlockSpec(memory_space=pltpu.SMEM)
    return pl.pallas_call(
        functools.partial(_dispatch_body, tm=tm, tiles_a=tiles_a, n_blocks=n_blocks),
        grid=(tiles_a + tiles_b,),
        in_specs=[_smem_cols(tm), smem, smem, smem,
                  pl.BlockSpec((tm, D_MODEL), lambda i: (jnp.minimum(i, tiles_a - 1), 0)),
                  pl.BlockSpec((tm, D_MODEL), lambda i: (jnp.maximum(i - tiles_a, 0), 0))],
        out_specs=pl.BlockSpec(memory_space=pl.ANY),
        out_shape=jax.ShapeDtypeStruct((n_blocks * MOE_ROWS, D_MODEL), F32),
        scratch_shapes=[pltpu.VMEM((MOE_ROWS, D_MODEL), F32), pltpu.SemaphoreType.DMA(())],
        compiler_params=_cparams(("arbitrary",)),
        name="dispatch",
    )(slot, pad_from, pad_n, tail_block, xa, xb)


MOE_COLS = 2 * LANES


W1_PREP_COLS = 4 * MOE_COLS


def _w1_prep_body(w_ref, p_ref, o_ref):
    for c in range(W1_PREP_COLS // MOE_COLS):
        cols = slice(c * MOE_COLS, (c + 1) * MOE_COLS)
        o_ref[0, :, cols] = _dot(w_ref[0, :, cols].astype(BF16), p_ref[...]).astype(BF16)


def _w1_prep(w1):
    e, d, n2 = w1.shape
    src = np.concatenate([np.arange(0, MOE_COLS, 2), np.arange(1, MOE_COLS, 2)])
    perm = np.zeros((MOE_COLS, MOE_COLS), np.float32)
    perm[src, np.arange(MOE_COLS)] = 1.0
    blk = pl.BlockSpec((1, d, W1_PREP_COLS), lambda i, j: (i, 0, j))
    return pl.pallas_call(
        _w1_prep_body,
        grid=(e, n2 // W1_PREP_COLS),
        in_specs=[blk, pl.BlockSpec((MOE_COLS, MOE_COLS), lambda i, j: (0, 0))],
        out_specs=blk,
        out_shape=jax.ShapeDtypeStruct(w1.shape, BF16),
        compiler_params=_cparams(("parallel", "parallel")),
        name="w1_prep",
    )(w1, jnp.asarray(perm, dtype=BF16))


def _experts_body(be_ref, bv_ref, xs_ref, w1_ref, b1_ref, w2_ref, b2_ref, ys_ref, g_ref):
    j = pl.program_id(0)
    valid = bv_ref[j]

    @pl.when(valid > 0)
    def _compute():
        x = xs_ref[...].astype(BF16)
        for c in range(g_ref.shape[1] // LANES):
            cols = slice(c * MOE_COLS, (c + 1) * MOE_COLS)
            hc = _dot(x, w1_ref[0, :, cols]) + b1_ref[0, :, cols]
            glu = jnp.minimum(hc[:, :LANES], SWIGLU_LIMIT)
            lin = jnp.clip(hc[:, LANES:], -SWIGLU_LIMIT, SWIGLU_LIMIT)
            g_ref[:, c * LANES:(c + 1) * LANES] = (glu * jax.nn.sigmoid(SWIGLU_ALPHA * glu) * (lin + 1.0)).astype(BF16)
        ys_ref[...] = _dot(g_ref[...], w2_ref[0]) + b2_ref[0]

    @pl.when(valid <= 0)
    def _empty():
        ys_ref[...] = jnp.zeros_like(ys_ref)


def _experts(block_expert, block_valid, xs, w1p, b1p, w2, b2):
    n_blocks = xs.shape[0] // MOE_ROWS
    wspec = lambda a: pl.BlockSpec((1,) + a.shape[1:], lambda j, be, bv: (be[j], 0, 0))
    grid_spec = pltpu.PrefetchScalarGridSpec(
        num_scalar_prefetch=2,
        grid=(n_blocks,),
        in_specs=[pl.BlockSpec((MOE_ROWS, D_MODEL), lambda j, be, bv: (j, 0)),
                  wspec(w1p), wspec(b1p), wspec(w2), wspec(b2)],
        out_specs=pl.BlockSpec((MOE_ROWS, D_MODEL), lambda j, be, bv: (j, 0)),
        scratch_shapes=[pltpu.VMEM((MOE_ROWS, w2.shape[1]), BF16)],
    )
    return pl.pallas_call(
        _experts_body,
        grid_spec=grid_spec,
        out_shape=jax.ShapeDtypeStruct(xs.shape, F32),
        compiler_params=_cparams(("arbitrary",)),
        name="experts",
    )(block_expert, block_valid, xs, w1p, b1p, w2, b2)


def _combine_body(slot_ref, next_ref, gate_ref, h_ref, nf_ref, ys_ref, y_ref, buf_ref, sem, *, tm):
    i = pl.program_id(0)
    n = pl.num_programs(0)

    def request(slots, half):
        def issue(t, carry):
            for k in range(TOP_K):
                pltpu.make_async_copy(ys_ref.at[pl.ds(slots[k, t], 1)], buf_ref.at[half, k, pl.ds(t, 1)],
                                      sem.at[half]).start()
            return carry
        lax.fori_loop(0, tm, issue, 0, unroll=ISSUE_UNROLL)

    def per_half(half):
        @pl.when(i + 1 < n)
        def _request_next():
            request(next_ref, 1 - half)
        for k in range(TOP_K):
            pltpu.make_async_copy(ys_ref.at[pl.ds(0, tm)], buf_ref.at[half, k], sem.at[half]).wait()
        acc = h_ref[...]
        for k in range(TOP_K):
            acc = acc + gate_ref[:, k:k + 1] * buf_ref[half, k]
        y_ref[...] = _rms(acc, nf_ref[...])

    @pl.when(i == 0)
    def _first():
        request(slot_ref, 0)

    @pl.when(i % 2 == 0)
    def _even():
        per_half(0)

    @pl.when(i % 2 == 1)
    def _odd():
        per_half(1)


def _combine(slot, gates_rows, h, norm_final, ys, tm, first_tile):
    n = h.shape[0]
    assert n % tm == 0
    last = first_tile + n // tm - 1
    nxt = pl.BlockSpec((SUBLANES, tm), lambda i: (0, jnp.minimum(i + first_tile + 1, last)), memory_space=pltpu.SMEM)
    return pl.pallas_call(
        functools.partial(_combine_body, tm=tm),
        grid=(n // tm,),
        in_specs=[_smem_cols(tm, first_tile), nxt,
                  pl.BlockSpec((tm, TOP_K), lambda i: (i, 0)),
                  pl.BlockSpec((tm, D_MODEL), lambda i: (i, 0)),
                  pl.BlockSpec((1, D_MODEL), lambda i: (0, 0)),
                  pl.BlockSpec(memory_space=pl.ANY)],
        out_specs=pl.BlockSpec((tm, D_MODEL), lambda i: (i, 0)),
        out_shape=jax.ShapeDtypeStruct((n, D_MODEL), F32),
        scratch_shapes=[pltpu.VMEM((2, TOP_K, tm, D_MODEL), F32), pltpu.SemaphoreType.DMA((2,))],
        compiler_params=_cparams(("arbitrary",)),
        name="combine",
    )(slot, slot, gates_rows, h, norm_final, ys)


def _expansion(width):
    h = np.arange(LANES)[:, None]
    c = np.arange(SSD_HEADS * width)[None, :] // width
    return jnp.asarray((h == c).astype(np.float32), dtype=BF16)


def _pad_lanes(v):
    return jnp.pad(v.astype(F32), (0, LANES - v.shape[0]))[None, :]


def kernel(x_prompt, x_sample, state_conv, state_ssm, cache_win_k, cache_win_v, norm_mix, w_in, conv_w, conv_b,
           dt_bias, a_log, d_skip, ssd_norm, att_norm, w_out, norm_ffn, router_w, router_b, w1, b1, w2, b2,
           norm_final):
    depth = w_in.shape[0]
    assert depth == 1
    bp, seq, _ = x_prompt.shape
    bs, t_new, _ = x_sample.shape
    n_p, n_s = bp * seq, bs * t_new
    l = 0

    o_dt = SSD_INNER + CONV_DIM
    o_q = o_dt + SSD_HEADS
    wl = w_in[l]
    w_cat = jnp.concatenate([wl[:, :o_dt], wl[:, o_q:], jnp.pad(wl[:, o_dt:o_q], ((0, 0), (0, DT_PAD - SSD_HEADS)))],
                            axis=1).astype(BF16)
    g_mix = norm_mix[l][None, :]
    ssd_consts = (conv_w[l], conv_b[l][None, :], _pad_lanes(dt_bias[l]), _pad_lanes(a_log[l]),
                  jnp.repeat(d_skip[l], SSD_HEAD_DIM)[None, :], ssd_norm[l][None, :],
                  _expansion(SSD_HEAD_DIM))
    slopes = jnp.exp2(-8.0 * jnp.arange(1, ATT_HEADS + 1, dtype=F32) / ATT_HEADS)
    out_consts = (att_norm[l][None, :], w_out[l][:SSD_INNER].astype(BF16), w_out[l][SSD_INNER:].astype(BF16),
                  norm_ffn[l][None, :], router_w[l].T.astype(BF16), router_b[l][:, None].astype(F32))
    w1p = _w1_prep(w1[l])
    b1p = b1[l].reshape(N_EXPERTS, -1, LANES, 2).transpose(0, 1, 3, 2).reshape(N_EXPERTS, 1, -1)
    w2b = w2[l].astype(BF16)
    b2r = b2[l][:, None, :]

    zx, q, k, v, dt, kt, vt = _in_proj(x_prompt.reshape(n_p, D_MODEL), g_mix, w_cat, 256, seq)
    y_p, conv_p, ssm_p = _ssd(zx.reshape(bp, seq, ZX_WIDTH), dt.reshape(bp, seq, DT_PAD), ssd_consts, None, BF16)
    o_p = _attn_prompt(q, k, v, slopes, bp, seq)
    keep = min(DILATED_PATTERNS[-1][0], seq)
    k_p = kt.reshape(bp, ATT_HEADS, ATT_HEAD_DIM, seq).transpose(0, 3, 1, 2)[:, seq - keep:]
    v_p = vt.reshape(bp, ATT_HEADS, ATT_HEAD_DIM, seq).transpose(0, 3, 1, 2)[:, seq - keep:]

    zx_s, q_s, k_s, v_s, dt_s = _in_proj(x_sample.reshape(n_s, D_MODEL), g_mix, w_cat, 256)
    init = (state_conv[l], state_ssm[l].reshape(bs, SSD_INNER, SSD_STATE))
    y_s, conv_s, ssm_s = _ssd(zx_s.reshape(bs, t_new, ZX_WIDTH), dt_s.reshape(bs, t_new, DT_PAD), ssd_consts, init, F32)

    def head_major(a):
        a = a.reshape(bs, t_new, ATT_HEADS, ATT_HEAD_DIM).transpose(0, 2, 1, 3)
        return jnp.pad(a, ((0, 0), (0, 0), (0, DEC_ROWS - t_new), (0, 0)))
    o_s = _attn_decode(head_major(q_s), head_major(k_s), head_major(v_s),
                       cache_win_k[l].transpose(0, 2, 3, 1), cache_win_v[l].transpose(0, 2, 3, 1), t_new)
    o_s = o_s[:, :, :t_new].transpose(0, 2, 1, 3)

    cnt0 = jnp.zeros((N_EXPERTS, LANES), F32)
    h_p, xn_p, idx_p, gate_p, rank_p, cnt_p = _out_proj(x_prompt.reshape(n_p, D_MODEL), y_p.reshape(n_p, SSD_INNER),
                                                        o_p, out_consts, cnt0, 512)
    h_s, xn_s, idx_s, gate_s, rank_s, cnt_all = _out_proj(x_sample.reshape(n_s, D_MODEL), y_s.reshape(n_s, SSD_INNER),
                                                          o_s.reshape(n_s, ATT_WIDTH), out_consts, cnt_p, 512)

    counts = cnt_all[:, 0].astype(jnp.int32)
    padded = (counts + MOE_ROWS - 1) // MOE_ROWS * MOE_ROWS
    pad_end = jnp.cumsum(padded)
    pad_start = pad_end - padded
    n_blocks = -(-((n_p + n_s) * TOP_K) // MOE_ROWS) + N_EXPERTS
    blk0 = jnp.arange(n_blocks, dtype=jnp.int32) * MOE_ROWS
    owner = blk0[:, None] >= pad_end[None, :]
    block_expert = jnp.minimum(jnp.sum(owner, axis=1), N_EXPERTS - 1).astype(jnp.int32)
    onehot = block_expert[:, None] == jnp.arange(N_EXPERTS, dtype=jnp.int32)[None, :]
    used = jnp.sum(jnp.where(onehot, (blk0[:, None] - pad_start[None, :]), 0), axis=1)
    block_valid = jnp.clip(jnp.sum(jnp.where(onehot, counts[None, :], 0), axis=1) - used, 0, MOE_ROWS).astype(jnp.int32)

    tm = min(256, n_s)
    idx_all = jnp.concatenate([idx_p, idx_s], axis=1)
    rank_all = jnp.concatenate([rank_p, rank_s], axis=1)
    first_slot = functools.reduce(lambda acc, e: jnp.where(idx_all == e, pad_start[e], acc), range(N_EXPERTS),
                                  jnp.zeros_like(idx_all))
    slot_all = first_slot + rank_all
    xs = _dispatch(slot_all, pad_start + counts, padded - counts, pad_end[N_EXPERTS - 1:] // MOE_ROWS,
                   xn_p, xn_s, n_blocks, tm)
    ys = _experts(block_expert, block_valid, xs, w1p, b1p, w2b, b2r)
    nfin = norm_final[None, :]
    y_prompt = _combine(slot_all, gate_p[:TOP_K].T, h_p, nfin, ys, tm, 0)
    y_sample = _combine(slot_all, gate_s[:TOP_K].T, h_s, nfin, ys, tm, n_p // tm)

    return (y_prompt.reshape(bp, seq, D_MODEL), y_sample.reshape(bs, t_new, D_MODEL),
            conv_p[None], ssm_p.reshape(1, bp, SSD_HEADS, SSD_HEAD_DIM, SSD_STATE), k_p[None], v_p[None],
            conv_s[None], ssm_s.reshape(1, bs, SSD_HEADS, SSD_HEAD_DIM, SSD_STATE),
            k_s.reshape(1, bs, t_new, ATT_HEADS, ATT_HEAD_DIM), v_s.reshape(1, bs, t_new, ATT_HEADS, ATT_HEAD_DIM))
```

```python
import functools

import jax
import jax.numpy as jnp
import numpy as np
from jax import lax
from jax.experimental import pallas as pl
from jax.experimental.pallas import tpu as pltpu

F32 = jnp.float32
BF16 = jnp.bfloat16

D_MODEL = 1024
SSD_HEADS = 16
SSD_HEAD_DIM = 64
SSD_INNER = SSD_HEADS * SSD_HEAD_DIM
SSD_GROUPS = 2
SSD_STATE = 128
SSD_CONV = 4
SSD_CHUNK = 128
CONV_DIM = SSD_INNER + 2 * SSD_GROUPS * SSD_STATE
ATT_HEADS = 8
ATT_HEAD_DIM = 64
ATT_WIDTH = ATT_HEADS * ATT_HEAD_DIM
DILATED_PATTERNS = ((128, 1), (512, 4), (2048, 16))
ATT_BLOCK = 128
N_EXPERTS = 32
TOP_K = 4
SWIGLU_LIMIT = 7.0
SWIGLU_ALPHA = 1.702
NORM_EPS = 1e-5

LANES = 128
SUBLANES = 8
ZX_WIDTH = SSD_INNER + CONV_DIM
DT_PAD = LANES
W_CAT = ZX_WIDTH + 3 * ATT_WIDTH + DT_PAD
CONV_HIST = SUBLANES
MOE_ROWS = 512
VMEM_LIMIT = 56 * 1024 * 1024

NEG_INF = float("-inf")


def _cparams(sem):
    return pltpu.CompilerParams(dimension_semantics=sem, vmem_limit_bytes=VMEM_LIMIT)


def _rms(x, g):
    return x * lax.rsqrt(jnp.mean(x * x, axis=-1, keepdims=True) + NORM_EPS) * g


def _dot(a, b):
    return jnp.dot(a, b, preferred_element_type=F32)


def _dot_nt(a, b):
    return lax.dot_general(a, b, (((1,), (1,)), ((), ())), preferred_element_type=F32)


def _split3(v):
    hi = v.astype(BF16)
    r1 = v - hi.astype(F32)
    mid = r1.astype(BF16)
    lo = (r1 - mid.astype(F32)).astype(BF16)
    return hi, mid, lo


def _dot3(v, m):
    hi, mid, lo = _split3(v)
    return _dot(hi, m) + _dot(mid, m) + _dot(lo, m)


def _dot3_lhs(m, v):
    hi, mid, lo = _split3(v)
    return _dot(m, hi) + _dot(m, mid) + _dot(m, lo)


def _silu(x):
    half = 0.5 * x
    return half * (1.0 + jnp.tanh(half))


def _in_proj_body(x_ref, g_ref, w_ref, zx_ref, q_ref, k_ref, v_ref, dt_ref, *t_refs):
    hn = _rms(x_ref[...], g_ref[...]).astype(BF16)
    o = 0
    for ref, width in ((zx_ref, ZX_WIDTH), (q_ref, ATT_WIDTH), (k_ref, ATT_WIDTH), (v_ref, ATT_WIDTH),
                       (dt_ref, DT_PAD)):
        ref[...] = _dot(hn, w_ref[:, o:o + width])
        o += width
    for src, dst in zip((k_ref, v_ref), t_refs):
        for j in range(ATT_WIDTH // LANES):
            dst[0, j * LANES:(j + 1) * LANES, :] = src[:, j * LANES:(j + 1) * LANES].T


def _in_proj(x2d, g, w_cat, tm, seq=None):
    n = x2d.shape[0]
    tm = min(tm, n)
    per = 1 if seq is None else seq // tm
    row = lambda w: pl.BlockSpec((tm, w), lambda b, i: (b * per + i, 0))
    full = lambda a: pl.BlockSpec(a.shape, lambda b, i: (0,) * a.ndim)
    widths = (ZX_WIDTH, ATT_WIDTH, ATT_WIDTH, ATT_WIDTH, DT_PAD)
    out_specs = [row(w) for w in widths]
    out_shape = [jax.ShapeDtypeStruct((n, w), F32) for w in widths]
    if seq is not None:
        out_specs += [pl.BlockSpec((1, ATT_WIDTH, tm), lambda b, i: (b, 0, i))] * 2
        out_shape += [jax.ShapeDtypeStruct((n // seq, ATT_WIDTH, seq), F32)] * 2
    return pl.pallas_call(
        _in_proj_body,
        grid=(n // (tm * per), per),
        in_specs=[row(D_MODEL), full(g), full(w_cat)],
        out_specs=out_specs,
        out_shape=out_shape,
        compiler_params=_cparams(("parallel", "parallel")),
        name="in_proj",
    )(x2d, g, w_cat)


def _ssd_body(*refs, l_blk, has_init):
    T = SSD_CHUNK
    if has_init:
        (zx_ref, dt_ref, cw_ref, cb_ref, dtb_ref, alog_ref, dskip_ref, norm_ref, e64_ref,
         cinit_ref, sinit_ref, y_ref, conv_out_ref, ssm_out_ref,
         ext_ref, act_ref, state_ref, cst_ref, ybuf_ref, ex_ref, ecs_ref, xdt_ref, xdd_ref,
         zpad_ref, dtpad_ref) = refs
    else:
        (zx_ref, dt_ref, cw_ref, cb_ref, dtb_ref, alog_ref, dskip_ref, norm_ref, e64_ref,
         y_ref, conv_out_ref, ssm_out_ref,
         ext_ref, act_ref, state_ref, cst_ref, ybuf_ref, ex_ref, ecs_ref, xdt_ref, xdd_ref) = refs
    c = pl.program_id(1)
    n_tile = SSD_INNER // LANES

    @pl.when(c == 0)
    def _start():
        if has_init:
            ext_ref[0:CONV_HIST, :] = jnp.zeros((CONV_HIST, CONV_DIM), F32)
            ext_ref[CONV_HIST - (SSD_CONV - 1):CONV_HIST, :] = cinit_ref[0]
            for j in range(n_tile):
                state_ref[:, j * LANES:(j + 1) * LANES] = sinit_ref[0, j * LANES:(j + 1) * LANES, :].T
        else:
            ext_ref[0:CONV_HIST, :] = jnp.zeros((CONV_HIST, CONV_DIM), F32)
            state_ref[...] = jnp.zeros_like(state_ref)

    if l_blk == T:
        ext_ref[CONV_HIST:CONV_HIST + T, :] = zx_ref[0, :, SSD_INNER:ZX_WIDTH]
        z_of = lambda sl: zx_ref[0, :, sl]
        dt_raw = dt_ref[0]
    else:
        ext_ref[CONV_HIST:CONV_HIST + T, :] = jnp.zeros((T, CONV_DIM), F32)
        ext_ref[CONV_HIST:CONV_HIST + l_blk, :] = zx_ref[0, :, SSD_INNER:ZX_WIDTH]
        zpad_ref[...] = jnp.zeros_like(zpad_ref)
        zpad_ref[0:l_blk, :] = zx_ref[0, :, 0:SSD_INNER]
        dtpad_ref[...] = jnp.zeros_like(dtpad_ref)
        dtpad_ref[0:l_blk, :] = dt_ref[0]
        z_of = lambda sl: zpad_ref[:, sl]
        dt_raw = dtpad_ref[...]

    cw = CONV_DIM // 3
    for cc in range(3):
        sl = slice(cc * cw, (cc + 1) * cw)
        acc = cb_ref[:, sl]
        for j in range(SSD_CONV):
            o = CONV_HIST - (SSD_CONV - 1) + j
            acc = acc + ext_ref[o:o + T, sl] * cw_ref[j:j + 1, sl]
        act_ref[:, sl] = _silu(acc)

    row = lax.broadcasted_iota(jnp.int32, (T, LANES), 0)
    col = lax.broadcasted_iota(jnp.int32, (T, LANES), 1)
    tri = row >= col
    tri_bf = jnp.where(tri, 1.0, 0.0).astype(BF16)
    even = col < SSD_HEAD_DIM

    xdt = dt_raw + dtb_ref[...]
    dtv = jnp.maximum(xdt, 0.0) + jnp.log1p(jnp.exp(-jnp.abs(xdt)))
    if l_blk < T:
        dtv = jnp.where(row < l_blk, dtv, 0.0)
    d_a = dtv * (-jnp.exp(alog_ref[...]))
    cs = _dot3_lhs(tri_bf, d_a)
    cst_ref[...] = cs.T
    ex_ref[...] = _dot3(jnp.concatenate([dtv, cs], axis=0), e64_ref[...])
    for j in range(n_tile):
        sl = slice(j * LANES, (j + 1) * LANES)
        dt_e, cs_e = ex_ref[0:T, sl], ex_ref[T:2 * T, sl]
        x = act_ref[:, sl]
        ecs_ref[:, sl] = jnp.exp(cs_e)
        xdt_ref[:, sl] = (x * dt_e).astype(BF16)
        xdd_ref[:, sl] = (x * (dt_e * jnp.exp(cs_e[T - 1:T] - cs_e))).astype(BF16)

    gw = SSD_INNER // SSD_GROUPS
    heads_per_group = SSD_HEADS // SSD_GROUPS
    for g in range(SSD_GROUPS):
        gsl = slice(g * gw, (g + 1) * gw)
        b_g = act_ref[:, SSD_INNER + g * SSD_STATE:SSD_INNER + (g + 1) * SSD_STATE]
        c_off = SSD_INNER + SSD_GROUPS * SSD_STATE
        c_g = act_ref[:, c_off + g * SSD_STATE:c_off + (g + 1) * SSD_STATE].astype(BF16)
        cb = _dot_nt(c_g, b_g.astype(BF16))
        b_gt = b_g.T.astype(BF16)
        st_old = state_ref[:, gsl]
        ybuf_ref[:, gsl] = _dot(c_g, st_old.astype(BF16)) * ecs_ref[:, gsl] + dskip_ref[:, gsl] * act_ref[:, gsl]
        state_ref[:, gsl] = st_old * ecs_ref[T - 1:T, gsl] + _dot(b_gt, xdd_ref[:, gsl])
        for jp in range(heads_per_group // 2):
            h0 = g * heads_per_group + 2 * jp
            osl = slice(g * gw + jp * LANES, g * gw + (jp + 1) * LANES)
            pair = ex_ref[T:2 * T, osl]
            swapped = pltpu.roll(pair, SSD_HEAD_DIM, axis=1)
            cols = (jnp.where(even, pair, swapped), jnp.where(even, swapped, pair))
            yd = []
            for col, h in zip(cols, (h0, h0 + 1)):
                seg = col - cst_ref[h:h + 1, :]
                lmat = jnp.exp(jnp.where(tri, seg, NEG_INF))
                yd.append(_dot((cb * lmat).astype(BF16), xdt_ref[:, osl]))
            ybuf_ref[:, osl] = ybuf_ref[:, osl] + jnp.where(even, yd[0], yd[1])

    for g in range(SSD_GROUPS):
        gsl = slice(g * gw, (g + 1) * gw)
        yg = ybuf_ref[:, gsl] * _silu(z_of(gsl))
        yn = yg * lax.rsqrt(jnp.mean(yg * yg, axis=-1, keepdims=True) + NORM_EPS) * norm_ref[:, gsl]
        y_ref[0, :, gsl] = yn[0:l_blk].astype(y_ref.dtype)

    @pl.when(c == pl.num_programs(1) - 1)
    def _finish():
        lo = CONV_HIST + l_blk - (SSD_CONV - 1)
        conv_out_ref[0] = ext_ref[lo:lo + SSD_CONV - 1, :]
        for j in range(n_tile):
            ssm_out_ref[0, j * LANES:(j + 1) * LANES, :] = state_ref[:, j * LANES:(j + 1) * LANES].T

    ext_ref[0:CONV_HIST, :] = ext_ref[T:T + CONV_HIST, :]


def _ssd(zx3, dt3, consts, init, y_dtype):
    b, L, _ = zx3.shape
    T = SSD_CHUNK
    l_blk = T if L % T == 0 else L
    n_chunks = L // l_blk
    has_init = init is not None
    full = lambda a: pl.BlockSpec(a.shape, lambda i, c: (0,) * a.ndim)
    in_specs = [pl.BlockSpec((1, l_blk, ZX_WIDTH), lambda i, c: (i, c, 0)),
                pl.BlockSpec((1, l_blk, DT_PAD), lambda i, c: (i, c, 0))] + [full(a) for a in consts]
    args = [zx3, dt3, *consts]
    scratch = [pltpu.VMEM((T + CONV_HIST, CONV_DIM), F32),
               pltpu.VMEM((T, CONV_DIM), F32),
               pltpu.VMEM((SSD_STATE, SSD_INNER), F32),
               pltpu.VMEM((LANES, T), F32),
               pltpu.VMEM((T, SSD_INNER), F32),
               pltpu.VMEM((2 * T, SSD_INNER), F32),
               pltpu.VMEM((T, SSD_INNER), F32),
               pltpu.VMEM((T, SSD_INNER), BF16),
               pltpu.VMEM((T, SSD_INNER), BF16)]
    if has_init:
        in_specs += [pl.BlockSpec((1, SSD_CONV - 1, CONV_DIM), lambda i, c: (i, 0, 0)),
                     pl.BlockSpec((1, SSD_INNER, SSD_STATE), lambda i, c: (i, 0, 0))]
        args += list(init)
        scratch += [pltpu.VMEM((T, SSD_INNER), F32), pltpu.VMEM((T, DT_PAD), F32)]
    return pl.pallas_call(
        functools.partial(_ssd_body, l_blk=l_blk, has_init=has_init),
        grid=(b, n_chunks),
        in_specs=in_specs,
        out_specs=[pl.BlockSpec((1, l_blk, SSD_INNER), lambda i, c: (i, c, 0)),
                   pl.BlockSpec((1, SSD_CONV - 1, CONV_DIM), lambda i, c: (i, 0, 0)),
                   pl.BlockSpec((1, SSD_INNER, SSD_STATE), lambda i, c: (i, 0, 0))],
        out_shape=[jax.ShapeDtypeStruct((b, L, SSD_INNER), y_dtype),
                   jax.ShapeDtypeStruct((b, SSD_CONV - 1, CONV_DIM), F32),
                   jax.ShapeDtypeStruct((b, SSD_INNER, SSD_STATE), F32)],
        scratch_shapes=scratch,
        compiler_params=_cparams(("parallel", "arbitrary")),
        name="ssd_init" if has_init else "ssd",
    )(*args)


ATT_UNROLL = 16


def _unroll(n):
    return max(d for d in range(1, ATT_UNROLL + 1) if n % d == 0)


ATT_RES = DILATED_PATTERNS[-1][1]


def _attn_tables():
    B = ATT_BLOCK
    t1 = np.full((len(DILATED_PATTERNS), B, B), -np.inf, np.float32)
    t2 = np.full((len(DILATED_PATTERNS), B, 2 * B), -np.inf, np.float32)
    for p, (window, dil) in enumerate(DILATED_PATTERNS):
        m = ATT_RES // dil
        w = B // m
        rho = np.arange(B)
        c = m * (rho % w) + rho // w
        ck2 = np.concatenate([c - B, c])
        for tab, ck in ((t1, c), (t2, ck2)):
            delta = c[:, None] - ck[None, :]
            ok = (delta >= 0) & (delta <= window // dil)
            tab[p] = np.where(ok, -(dil * delta).astype(np.float32), -np.inf)
    return jnp.asarray(t1), jnp.asarray(t2)


def _attn_body(slopes_ref, t1_ref, t2_ref, q_ref, k_ref, v_ref, o_ref, qd_ref, kd_ref, vd_ref, op_ref, lp_ref,
               stage_ref, b1_ref, b2_ref, in_sem, out_sem, *, seq):
    B = ATT_BLOCK
    hp = pl.program_id(1)
    even = lax.broadcasted_iota(jnp.int32, (B, LANES), 1) < ATT_HEAD_DIM
    scale = ATT_HEAD_DIM ** -0.5

    def in_copy(a, r):
        src, dst = ((q_ref, qd_ref), (k_ref, kd_ref), (v_ref, vd_ref))[a]
        return pltpu.make_async_copy(src.at[:, r, :], dst.at[r], in_sem.at[a, r])
    for r in range(ATT_RES):
        for a in range(3):
            in_copy(a, r).start()

    for e in range(2):
        slope = slopes_ref[2 * hp + e]
        for p in range(len(DILATED_PATTERNS)):
            b1_ref[e * len(DILATED_PATTERNS) + p] = t1_ref[p] * slope
            b2_ref[e * len(DILATED_PATTERNS) + p] = t2_ref[p] * slope

    for r in range(ATT_RES):
        for a in range(3):
            in_copy(a, r).wait()

    def block(p, r_d, n, first):
        dil = DILATED_PATTERNS[p][1]
        m = ATT_RES // dil
        w = B // m

        def slab(j, nblk):
            start = nblk * w
            return r_d + dil * j, pl.ds(start if isinstance(start, int) else pl.multiple_of(start, w), w)

        def gather(src, nblk):
            parts = []
            for j in range(m):
                r, rows = slab(j, nblk)
                parts.append(src[r, rows, :])
            return parts[0] if m == 1 else jnp.concatenate(parts, axis=0)

        qb = gather(qd_ref, n) * scale
        if first:
            kb, vb = gather(kd_ref, n), gather(vd_ref, n)
            bias = lambda e: b1_ref[e * len(DILATED_PATTERNS) + p]
        else:
            kb = jnp.concatenate([gather(kd_ref, n - 1), gather(kd_ref, n)], axis=0)
            vb = jnp.concatenate([gather(vd_ref, n - 1), gather(vd_ref, n)], axis=0)
            bias = lambda e: b2_ref[e * len(DILATED_PATTERNS) + p]
        q2 = jnp.concatenate([jnp.where(even, qb, 0.0), jnp.where(even, 0.0, qb)], axis=0).astype(BF16)
        s = _dot_nt(q2, kb.astype(BF16)) + jnp.concatenate([bias(0), bias(1)], axis=0)
        mx = jnp.max(s, axis=-1, keepdims=True)
        pr = jnp.exp(s - mx)
        l = jnp.sum(pr, axis=-1, keepdims=True)
        o2 = _dot(pr.astype(BF16), vb.astype(BF16)) / l
        lse2 = jnp.broadcast_to(mx + jnp.log(l), (2 * B, LANES))
        o = jnp.where(even, o2[:B], o2[B:])
        lse = jnp.where(even, lse2[:B], lse2[B:])
        for j in range(m):
            r, rows = slab(j, n)
            op_ref[p, r, rows, :] = o[j * w:(j + 1) * w]
            lp_ref[p, r, rows, :] = lse[j * w:(j + 1) * w]

    for p, (_, dil) in enumerate(DILATED_PATTERNS):
        nb = seq // (dil * B)
        u_first = _unroll(dil)

        def first(i, carry, p=p, u_first=u_first):
            for u in range(u_first):
                block(p, i * u_first + u, 0, True)
            return carry
        lax.fori_loop(0, dil // u_first, first, 0)
        n_later = dil * (nb - 1)
        if n_later:
            u_later = _unroll(n_later)

            def later(i, carry, p=p, nb=nb, u_later=u_later):
                for u in range(u_later):
                    j = i * u_later + u
                    block(p, j // (nb - 1), j % (nb - 1) + 1, False)
                return carry
            lax.fori_loop(0, n_later // u_later, later, 0)

    def out_copy(r):
        return pltpu.make_async_copy(stage_ref.at[r], o_ref.at[:, r, :], out_sem.at[r])
    for r in range(ATT_RES):
        l0, l1, l2 = lp_ref[0, r], lp_ref[1, r], lp_ref[2, r]
        m = jnp.maximum(jnp.maximum(l0, l1), l2)
        w0, w1, w2 = jnp.exp(l0 - m), jnp.exp(l1 - m), jnp.exp(l2 - m)
        num = w0 * op_ref[0, r] + w1 * op_ref[1, r] + w2 * op_ref[2, r]
        stage_ref[r] = num / (w0 + w1 + w2)
        out_copy(r).start()
    for r in range(ATT_RES):
        out_copy(r).wait()


def _attn_prompt(q, k, v, slopes, batch, seq):
    n_hp = ATT_HEADS // 2
    n_pat = len(DILATED_PATTERNS)
    assert len(DILATED_PATTERNS) == 3 and all(ATT_RES % d == 0 and seq % (d * ATT_BLOCK) == 0 and w // d == ATT_BLOCK
                                              for w, d in DILATED_PATTERNS)
    per = seq // ATT_RES
    by_res = lambda a: a.reshape(batch * per, ATT_RES, ATT_WIDTH)
    blk = pl.BlockSpec((per, ATT_RES, LANES), lambda b, h: (b, 0, h))
    t1, t2 = _attn_tables()
    full = lambda a: pl.BlockSpec(a.shape, lambda b, h: (0,) * a.ndim)
    res = pltpu.VMEM((ATT_RES, per, LANES), F32)
    out = pl.pallas_call(
        functools.partial(_attn_body, seq=seq),
        grid=(batch, n_hp),
        in_specs=[pl.BlockSpec(memory_space=pltpu.SMEM), full(t1), full(t2), blk, blk, blk],
        out_specs=blk,
        out_shape=jax.ShapeDtypeStruct((batch * per, ATT_RES, ATT_WIDTH), F32),
        scratch_shapes=[res, res, res,
                        pltpu.VMEM((n_pat, ATT_RES, per, LANES), F32), pltpu.VMEM((n_pat, ATT_RES, per, LANES), F32),
                        res,
                        pltpu.VMEM((2 * n_pat, ATT_BLOCK, ATT_BLOCK), F32),
                        pltpu.VMEM((2 * n_pat, ATT_BLOCK, 2 * ATT_BLOCK), F32),
                        pltpu.SemaphoreType.DMA((3, ATT_RES)), pltpu.SemaphoreType.DMA((ATT_RES,))],
        compiler_params=_cparams(("parallel", "parallel")),
        name="attn_prompt",
    )(slopes, t1, t2, by_res(q), by_res(k), by_res(v))
    return out.reshape(batch * seq, ATT_WIDTH)


DEC_ROWS = SUBLANES


def _decode_tables(t_new, m_cache):
    slopes = 2.0 ** (-8.0 * np.arange(1, ATT_HEADS + 1) / ATT_HEADS)

    def mult(dist):
        return sum(1 for window, dil in DILATED_PATTERNS if 0 <= dist <= window and dist % dil == 0)

    bias_c = np.zeros((ATT_HEADS, DEC_ROWS, m_cache), np.float32)
    mult_c = np.ones((DEC_ROWS, m_cache), np.float32)
    bias_n = np.full((ATT_HEADS, DEC_ROWS, LANES), -np.inf, np.float32)
    mult_n = np.zeros((DEC_ROWS, LANES), np.float32)
    for t in range(t_new):
        dist = m_cache + t - np.arange(m_cache)
        mu = np.array([mult(d) for d in dist], np.float32)
        mult_c[t] = mu
        bias_c[:, t, :] = np.where(mu > 0, -slopes[:, None] * dist[None, :], -np.inf)
        for t2 in range(t_new):
            if mult(t - t2):
                bias_n[:, t, t2] = -slopes * (t - t2)
                mult_n[t, t2] = mult(t - t2)
    bias_n[:, t_new:, 0] = 0.0
    mult_n[t_new:, 0] = 1.0
    return jnp.asarray(bias_c), jnp.asarray(mult_c), jnp.asarray(bias_n), jnp.asarray(mult_n)


def _attn_dec_body(q_ref, knt_ref, vnt_ref, ck_ref, cv_ref, bc_ref, mc_ref, bn_ref, mn_ref, o_ref):
    scale = ATT_HEAD_DIM ** -0.5
    for h in range(ATT_HEADS):
        qh = (q_ref[0, h] * scale).astype(BF16)
        sc = _dot(qh, ck_ref[0, h].astype(BF16)) + bc_ref[h]
        sn = _dot(qh, knt_ref[0, h].astype(BF16)) + bn_ref[h]
        m = jnp.maximum(jnp.max(sc, axis=-1, keepdims=True), jnp.max(sn, axis=-1, keepdims=True))
        pc = jnp.exp(sc - m) * mc_ref[...]
        pn = jnp.exp(sn - m) * mn_ref[...]
        inv_l = 1.0 / (jnp.sum(pc, axis=-1, keepdims=True) + jnp.sum(pn, axis=-1, keepdims=True))
        o_ref[0, h] = (_dot_nt(cv_ref[0, h].astype(BF16), (pc * inv_l).astype(BF16))
                       + _dot_nt(vnt_ref[0, h].astype(BF16), (pn * inv_l).astype(BF16)))


def _attn_decode(q4, knt, vnt, cache_kt, cache_vt, t_new):
    b = q4.shape[0]
    m_cache = cache_kt.shape[3]
    assert t_new <= DEC_ROWS
    tables = _decode_tables(t_new, m_cache)
    per_b = lambda a: pl.BlockSpec((1,) + a.shape[1:], lambda i: (i, 0, 0, 0))
    full = lambda a: pl.BlockSpec(a.shape, lambda i: (0,) * a.ndim)
    out_shape = jax.ShapeDtypeStruct((b, ATT_HEADS, ATT_HEAD_DIM, DEC_ROWS), F32)
    return pl.pallas_call(
        _attn_dec_body,
        grid=(b,),
        in_specs=[per_b(a) for a in (q4, knt, vnt, cache_kt, cache_vt)] + [full(a) for a in tables],
        out_specs=per_b(out_shape),
        out_shape=out_shape,
        compiler_params=_cparams(("parallel",)),
        name="attn_decode",
    )(q4, knt, vnt, cache_kt, cache_vt, *tables)


def _out_proj_body(x_ref, y_ref, o_ref, an_ref, wy_ref, wo_ref, nf_ref, rw_ref, rb_ref, cnt0_ref,
                   h_ref, xn_ref, idx_ref, gate_ref, rank_ref, cnt_ref, carry_ref, *, tm):
    i = pl.program_id(0)

    @pl.when(i == 0)
    def _start():
        carry_ref[...] = cnt0_ref[...]

    on = _rms(o_ref[...], an_ref[...]).astype(BF16)
    mixed = _dot(y_ref[...].astype(BF16), wy_ref[...]) + _dot(on, wo_ref[...])
    h = x_ref[...] + mixed
    h_ref[...] = h
    xn = _rms(h, nf_ref[...])
    xn_ref[...] = xn
    logits = _dot_nt(rw_ref[...], xn.astype(BF16)) + rb_ref[...]

    e_iota = lax.broadcasted_iota(jnp.int32, (N_EXPERTS, tm), 0)
    vals, idxs, sels = [], [], []
    cur = logits
    for _ in range(TOP_K):
        mx = jnp.max(cur, axis=0, keepdims=True)
        ix = jnp.min(jnp.where(cur == mx, e_iota, N_EXPERTS), axis=0, keepdims=True)
        sel = e_iota == ix
        cur = jnp.where(sel, NEG_INF, cur)
        vals.append(mx)
        idxs.append(ix)
        sels.append(sel)
    ex = [jnp.exp(v - vals[0]) for v in vals]
    den = functools.reduce(lambda a, b: a + b, ex)

    sel_any = functools.reduce(jnp.logical_or, sels)
    sel_f = jnp.where(sel_any, 1.0, 0.0)
    r = lax.broadcasted_iota(jnp.int32, (tm, tm), 0)
    c = lax.broadcasted_iota(jnp.int32, (tm, tm), 1)
    upper = jnp.where(r <= c, 1.0, 0.0).astype(BF16)
    cum = _dot(sel_f.astype(BF16), upper)
    before = carry_ref[:, 0:1] + cum - sel_f
    for k in range(TOP_K):
        idx_ref[k:k + 1, :] = idxs[k]
        gate_ref[k:k + 1, :] = ex[k] / den
        rank_ref[k:k + 1, :] = jnp.sum(jnp.where(sels[k], before, 0.0), axis=0, keepdims=True).astype(jnp.int32)
    idx_ref[TOP_K:SUBLANES, :] = jnp.zeros((SUBLANES - TOP_K, tm), jnp.int32)
    gate_ref[TOP_K:SUBLANES, :] = jnp.zeros((SUBLANES - TOP_K, tm), F32)
    rank_ref[TOP_K:SUBLANES, :] = jnp.zeros((SUBLANES - TOP_K, tm), jnp.int32)
    carry_ref[...] = carry_ref[...] + jnp.max(cum, axis=1, keepdims=True)
    cnt_ref[...] = carry_ref[...]


def _out_proj(x2d, y2d, o2d, consts, cnt0, tm):
    n = x2d.shape[0]
    tm = min(tm, n)
    row = lambda w: pl.BlockSpec((tm, w), lambda i: (i, 0))
    colb = pl.BlockSpec((SUBLANES, tm), lambda i: (0, i))
    full = lambda a: pl.BlockSpec(a.shape, lambda i: (0,) * a.ndim)
    return pl.pallas_call(
        functools.partial(_out_proj_body, tm=tm),
        grid=(n // tm,),
        in_specs=[row(D_MODEL), row(SSD_INNER), row(ATT_WIDTH)] + [full(a) for a in consts] + [full(cnt0)],
        out_specs=[row(D_MODEL), row(D_MODEL), colb, colb, colb, full(cnt0)],
        out_shape=[jax.ShapeDtypeStruct((n, D_MODEL), F32), jax.ShapeDtypeStruct((n, D_MODEL), F32),
                   jax.ShapeDtypeStruct((SUBLANES, n), jnp.int32), jax.ShapeDtypeStruct((SUBLANES, n), F32),
                   jax.ShapeDtypeStruct((SUBLANES, n), jnp.int32), jax.ShapeDtypeStruct(cnt0.shape, F32)],
        scratch_shapes=[pltpu.VMEM(cnt0.shape, F32)],
        compiler_params=_cparams(("arbitrary",)),
        name="out_proj",
    )(x2d, y2d, o2d, *consts, cnt0)


ISSUE_UNROLL = 8


def _dispatch_body(slot_ref, from_ref, n_ref, tail_ref, xa_ref, xb_ref, xs_ref, zero_ref, sem,
                   *, tm, tiles_a, n_blocks):
    i = pl.program_id(0)

    @pl.when(i == 0)
    def _pad_fill():
        zero_ref[...] = jnp.zeros_like(zero_ref)
        one_row = lambda slot: pltpu.make_async_copy(zero_ref.at[pl.ds(0, 1)], xs_ref.at[pl.ds(slot, 1)], sem)
        block = lambda j: pltpu.make_async_copy(zero_ref, xs_ref.at[pl.ds(j * MOE_ROWS, MOE_ROWS)], sem)

        def per_expert(e, carry):
            def start(r, c):
                one_row(from_ref[e] + r).start()
                return c

            def wait(r, c):
                one_row(0).wait()
                return c
            lax.fori_loop(0, n_ref[e], start, 0)
            lax.fori_loop(0, n_ref[e], wait, 0)
            return carry
        lax.fori_loop(0, N_EXPERTS, per_expert, 0)

        def tail(j, carry):
            block(j).start()
            block(j).wait()
            return carry
        lax.fori_loop(tail_ref[0], n_blocks, tail, 0)

    def scatter_rows(src_ref):
        def issue(i, carry):
            t0 = pl.multiple_of(i * ISSUE_UNROLL, ISSUE_UNROLL)
            for u in range(ISSUE_UNROLL):
                for k in range(TOP_K):
                    pltpu.make_async_copy(src_ref.at[pl.ds(t0 + u, 1)], xs_ref.at[pl.ds(slot_ref[k, t0 + u], 1)],
                                          sem).start()
            return carry
        lax.fori_loop(0, tm // ISSUE_UNROLL, issue, 0)
        for _ in range(TOP_K):
            pltpu.make_async_copy(src_ref, xs_ref.at[pl.ds(0, tm)], sem).wait()

    @pl.when(i < tiles_a)
    def _first_group():
        scatter_rows(xa_ref)

    @pl.when(i >= tiles_a)
    def _second_group():
        scatter_rows(xb_ref)


def _smem_cols(tm, first=0):
    return pl.BlockSpec((SUBLANES, tm), lambda i: (0, i + first), memory_space=pltpu.SMEM)


def _dispatch(slot, pad_from, pad_n, tail_block, xa, xb, n_blocks, tm):
    tiles_a, tiles_b = xa.shape[0] // tm, xb.shape[0] // tm
    assert tiles_a * tm == xa.shape[0] and tiles_b * tm == xb.shape[0]
    smem = pl.BlockSpec(memory_space=pltpu.SMEM)
    return pl.pallas_call(
        functools.partial(_dispatch_body, tm=tm, tiles_a=tiles_a, n_blocks=n_blocks),
        grid=(tiles_a + tiles_b,),
        in_specs=[_smem_cols(tm), smem, smem, smem,
                  pl.BlockSpec((tm, D_MODEL), lambda i: (jnp.minimum(i, tiles_a - 1), 0)),
                  pl.BlockSpec((tm, D_MODEL), lambda i: (jnp.maximum(i - tiles_a, 0), 0))],
        out_specs=pl.BlockSpec(memory_space=pl.ANY),
        out_shape=jax.ShapeDtypeStruct((n_blocks * MOE_ROWS, D_MODEL), F32),
        scratch_shapes=[pltpu.VMEM((MOE_ROWS, D_MODEL), F32), pltpu.SemaphoreType.DMA(())],
        compiler_params=_cparams(("arbitrary",)),
        name="dispatch",
    )(slot, pad_from, pad_n, tail_block, xa, xb)


MOE_COLS = 2 * LANES


W1_PREP_COLS = 4 * MOE_COLS


def _w1_prep_body(w_ref, p_ref, o_ref):
    for c in range(W1_PREP_COLS // MOE_COLS):
        cols = slice(c * MOE_COLS, (c + 1) * MOE_COLS)
        o_ref[0, :, cols] = _dot(w_ref[0, :, cols].astype(BF16), p_ref[...]).astype(BF16)


def _w1_prep(w1):
    e, d, n2 = w1.shape
    src = np.concatenate([np.arange(0, MOE_COLS, 2), np.arange(1, MOE_COLS, 2)])
    perm = np.zeros((MOE_COLS, MOE_COLS), np.float32)
    perm[src, np.arange(MOE_COLS)] = 1.0
    blk = pl.BlockSpec((1, d, W1_PREP_COLS), lambda i, j: (i, 0, j))
    return pl.pallas_call(
        _w1_prep_body,
        grid=(e, n2 // W1_PREP_COLS),
        in_specs=[blk, pl.BlockSpec((MOE_COLS, MOE_COLS), lambda i, j: (0, 0))],
        out_specs=blk,
        out_shape=jax.ShapeDtypeStruct(w1.shape, BF16),
        compiler_params=_cparams(("parallel", "parallel")),
        name="w1_prep",
    )(w1, jnp.asarray(perm, dtype=BF16))


def _experts_body(be_ref, bv_ref, xs_ref, w1_ref, b1_ref, w2_ref, b2_ref, ys_ref, g_ref):
    j = pl.program_id(0)
    valid = bv_ref[j]

    @pl.when(valid > 0)
    def _compute():
        x = xs_ref[...].astype(BF16)
        for c in range(g_ref.shape[1] // LANES):
            cols = slice(c * MOE_COLS, (c + 1) * MOE_COLS)
            hc = _dot(x, w1_ref[0, :, cols]) + b1_ref[0, :, cols]
            glu = jnp.minimum(hc[:, :LANES], SWIGLU_LIMIT)
            lin = jnp.clip(hc[:, LANES:], -SWIGLU_LIMIT, SWIGLU_LIMIT)
            g_ref[:, c * LANES:(c + 1) * LANES] = (glu * jax.nn.sigmoid(SWIGLU_ALPHA * glu) * (lin + 1.0)).astype(BF16)
        ys_ref[...] = _dot(g_ref[...], w2_ref[0]) + b2_ref[0]

    @pl.when(valid <= 0)
    def _empty():
        ys_ref[...] = jnp.zeros_like(ys_ref)


def _experts(block_expert, block_valid, xs, w1p, b1p, w2, b2):
    n_blocks = xs.shape[0] // MOE_ROWS
    wspec = lambda a: pl.BlockSpec((1,) + a.shape[1:], lambda j, be, bv: (be[j], 0, 0))
    grid_spec = pltpu.PrefetchScalarGridSpec(
        num_scalar_prefetch=2,
        grid=(n_blocks,),
        in_specs=[pl.BlockSpec((MOE_ROWS, D_MODEL), lambda j, be, bv: (j, 0)),
                  wspec(w1p), wspec(b1p), wspec(w2), wspec(b2)],
        out_specs=pl.BlockSpec((MOE_ROWS, D_MODEL), lambda j, be, bv: (j, 0)),
        scratch_shapes=[pltpu.VMEM((MOE_ROWS, w2.shape[1]), BF16)],
    )
    return pl.pallas_call(
        _experts_body,
        grid_spec=grid_spec,
        out_shape=jax.ShapeDtypeStruct(xs.shape, F32),
        compiler_params=_cparams(("arbitrary",)),
        name="experts",
    )(block_expert, block_valid, xs, w1p, b1p, w2, b2)


COMBINE_TM = 256
RUN_UNIT = 16
RUN_BITS = tuple(1 << b for b in reversed(range((COMBINE_TM // RUN_UNIT).bit_length())))
BUF_ROWS = -(-(TOP_K * COMBINE_TM + N_EXPERTS * (RUN_UNIT - 1 + SUBLANES - 1)) // LANES) * LANES


def _combine_meta(idx_all, slot_all, pad_start):
    n_tok = idx_all.shape[1]
    tiles = n_tok // COMBINE_TM
    experts = jnp.arange(N_EXPERTS, dtype=jnp.int32)
    hit = idx_all[:TOP_K, :, None] == experts
    cnt = jnp.sum(hit.reshape(TOP_K, tiles, COMBINE_TM, N_EXPERTS), axis=(0, 2), dtype=jnp.int32)
    run_start = pad_start[None, :] + jnp.cumsum(cnt, axis=0) - cnt
    lead = run_start % SUBLANES
    units = jnp.where(cnt > 0, (cnt + lead + RUN_UNIT - 1) // RUN_UNIT, 0)
    first_row = RUN_UNIT * (jnp.cumsum(units, axis=1) - units)
    meta = jnp.concatenate([run_start - lead, units, first_row, jnp.zeros_like(cnt)], axis=1).astype(jnp.int32)
    shift = jnp.repeat(first_row + lead - run_start, COMBINE_TM, axis=0)
    col = slot_all[:TOP_K] + jnp.sum(jnp.where(hit, shift[None], 0), axis=2, dtype=jnp.int32)
    return meta.reshape(tiles, 1, 4 * N_EXPERTS), col.T


def _combine_body(meta_ref, next_ref, col_ref, gate_ref, h_ref, nf_ref, ys_ref, y_ref, buf_ref, sem, *, n):
    i = pl.program_id(0)

    def runs(meta, half, start):
        for e in range(N_EXPERTS):
            src0, units, dst0 = meta[0, 0, e], meta[0, 0, N_EXPERTS + e], meta[0, 0, 2 * N_EXPERTS + e]

            def pieces(bits, done):
                for b in bits:
                    rows = b * RUN_UNIT

                    @pl.when((units & b) != 0)
                    def _piece(done=done, rows=rows):
                        dst = pl.multiple_of(dst0 + done, RUN_UNIT)
                        cp = pltpu.make_async_copy(ys_ref.at[pl.ds(pl.multiple_of(src0 + done, SUBLANES), rows)],
                                                   buf_ref.at[half, pl.ds(dst, rows)], sem.at[half])
                        if start:
                            cp.start()
                        else:
                            cp.wait()
                    done = done + (units & b) * RUN_UNIT
                return done
            done = pieces(RUN_BITS[-2:], jnp.int32(0))

            @pl.when(units >= RUN_BITS[-3])
            def _long_run(done=done):
                pieces(RUN_BITS[:-2], done)

    def per_half(half):
        @pl.when(i + 1 < n)
        def _fetch_next():
            runs(next_ref, 1 - half, True)
        runs(meta_ref, half, False)
        rows = lax.broadcasted_iota(jnp.int32, (COMBINE_TM, BUF_ROWS), 1)
        pick = jnp.zeros((COMBINE_TM, BUF_ROWS), F32)
        for k in range(TOP_K):
            pick = jnp.where(rows == col_ref[:, k:k + 1], gate_ref[:, k:k + 1], pick)
        acc = h_ref[...] + _dot(pick.astype(BF16), buf_ref[half].astype(BF16))
        y_ref[...] = _rms(acc, nf_ref[...])

    @pl.when(i == 0)
    def _first():
        buf_ref[...] = jnp.zeros_like(buf_ref)
        runs(meta_ref, 0, True)

    @pl.when(i % 2 == 0)
    def _even():
        per_half(0)

    @pl.when(i % 2 == 1)
    def _odd():
        per_half(1)


def _combine(meta, col_rows, gates_rows, h, norm_final, ys, first_tile):
    n = h.shape[0]
    tm = COMBINE_TM
    assert n % tm == 0
    last = first_tile + n // tm - 1
    smem_tile = lambda f: pl.BlockSpec((1, 1, meta.shape[2]), lambda i: (f(i), 0, 0), memory_space=pltpu.SMEM)
    return pl.pallas_call(
        functools.partial(_combine_body, n=n // tm),
        grid=(n // tm,),
        in_specs=[smem_tile(lambda i: i + first_tile), smem_tile(lambda i: jnp.minimum(i + first_tile + 1, last)),
                  pl.BlockSpec((tm, TOP_K), lambda i: (i + first_tile, 0)),
                  pl.BlockSpec((tm, TOP_K), lambda i: (i, 0)),
                  pl.BlockSpec((tm, D_MODEL), lambda i: (i, 0)),
                  pl.BlockSpec((1, D_MODEL), lambda i: (0, 0)),
                  pl.BlockSpec(memory_space=pl.ANY)],
        out_specs=pl.BlockSpec((tm, D_MODEL), lambda i: (i, 0)),
        out_shape=jax.ShapeDtypeStruct((n, D_MODEL), F32),
        scratch_shapes=[pltpu.VMEM((2, BUF_ROWS, D_MODEL), F32), pltpu.SemaphoreType.DMA((2,))],
        compiler_params=_cparams(("arbitrary",)),
        name="combine",
    )(meta, meta, col_rows, gates_rows, h, norm_final, ys)


def _expansion(width):
    h = np.arange(LANES)[:, None]
    c = np.arange(SSD_HEADS * width)[None, :] // width
    return jnp.asarray((h == c).astype(np.float32), dtype=BF16)


def _pad_lanes(v):
    return jnp.pad(v.astype(F32), (0, LANES - v.shape[0]))[None, :]


def kernel(x_prompt, x_sample, state_conv, state_ssm, cache_win_k, cache_win_v, norm_mix, w_in, conv_w, conv_b,
           dt_bias, a_log, d_skip, ssd_norm, att_norm, w_out, norm_ffn, router_w, router_b, w1, b1, w2, b2,
           norm_final):
    depth = w_in.shape[0]
    assert depth == 1
    bp, seq, _ = x_prompt.shape
    bs, t_new, _ = x_sample.shape
    n_p, n_s = bp * seq, bs * t_new
    l = 0

    o_dt = SSD_INNER + CONV_DIM
    o_q = o_dt + SSD_HEADS
    wl = w_in[l]
    w_cat = jnp.concatenate([wl[:, :o_dt], wl[:, o_q:], jnp.pad(wl[:, o_dt:o_q], ((0, 0), (0, DT_PAD - SSD_HEADS)))],
                            axis=1).astype(BF16)
    g_mix = norm_mix[l][None, :]
    ssd_consts = (conv_w[l], conv_b[l][None, :], _pad_lanes(dt_bias[l]), _pad_lanes(a_log[l]),
                  jnp.repeat(d_skip[l], SSD_HEAD_DIM)[None, :], ssd_norm[l][None, :],
                  _expansion(SSD_HEAD_DIM))
    slopes = jnp.exp2(-8.0 * jnp.arange(1, ATT_HEADS + 1, dtype=F32) / ATT_HEADS)
    out_consts = (att_norm[l][None, :], w_out[l][:SSD_INNER].astype(BF16), w_out[l][SSD_INNER:].astype(BF16),
                  norm_ffn[l][None, :], router_w[l].T.astype(BF16), router_b[l][:, None].astype(F32))
    w1p = _w1_prep(w1[l])
    b1p = b1[l].reshape(N_EXPERTS, -1, LANES, 2).transpose(0, 1, 3, 2).reshape(N_EXPERTS, 1, -1)
    w2b = w2[l].astype(BF16)
    b2r = b2[l][:, None, :]

    zx, q, k, v, dt, kt, vt = _in_proj(x_prompt.reshape(n_p, D_MODEL), g_mix, w_cat, 256, seq)
    y_p, conv_p, ssm_p = _ssd(zx.reshape(bp, seq, ZX_WIDTH), dt.reshape(bp, seq, DT_PAD), ssd_consts, None, BF16)
    o_p = _attn_prompt(q, k, v, slopes, bp, seq)
    keep = min(DILATED_PATTERNS[-1][0], seq)
    k_p = kt.reshape(bp, ATT_HEADS, ATT_HEAD_DIM, seq).transpose(0, 3, 1, 2)[:, seq - keep:]
    v_p = vt.reshape(bp, ATT_HEADS, ATT_HEAD_DIM, seq).transpose(0, 3, 1, 2)[:, seq - keep:]

    zx_s, q_s, k_s, v_s, dt_s = _in_proj(x_sample.reshape(n_s, D_MODEL), g_mix, w_cat, 256)
    init = (state_conv[l], state_ssm[l].reshape(bs, SSD_INNER, SSD_STATE))
    y_s, conv_s, ssm_s = _ssd(zx_s.reshape(bs, t_new, ZX_WIDTH), dt_s.reshape(bs, t_new, DT_PAD), ssd_consts, init, F32)

    def head_major(a):
        a = a.reshape(bs, t_new, ATT_HEADS, ATT_HEAD_DIM).transpose(0, 2, 1, 3)
        return jnp.pad(a, ((0, 0), (0, 0), (0, DEC_ROWS - t_new), (0, 0)))

    def head_major_t(a):
        a = a.reshape(bs, t_new, ATT_HEADS, ATT_HEAD_DIM).transpose(0, 2, 3, 1)
        return jnp.pad(a, ((0, 0), (0, 0), (0, 0), (0, LANES - t_new)))
    o_s = _attn_decode(head_major(q_s), head_major_t(k_s), head_major_t(v_s),
                       cache_win_k[l].transpose(0, 2, 3, 1), cache_win_v[l].transpose(0, 2, 3, 1), t_new)
    o_s = o_s[:, :, :, :t_new].transpose(0, 3, 1, 2)

    cnt0 = jnp.zeros((N_EXPERTS, LANES), F32)
    h_p, xn_p, idx_p, gate_p, rank_p, cnt_p = _out_proj(x_prompt.reshape(n_p, D_MODEL), y_p.reshape(n_p, SSD_INNER),
                                                        o_p, out_consts, cnt0, 512)
    h_s, xn_s, idx_s, gate_s, rank_s, cnt_all = _out_proj(x_sample.reshape(n_s, D_MODEL), y_s.reshape(n_s, SSD_INNER),
                                                          o_s.reshape(n_s, ATT_WIDTH), out_consts, cnt_p, 512)

    counts = cnt_all[:, 0].astype(jnp.int32)
    padded = (counts + MOE_ROWS - 1) // MOE_ROWS * MOE_ROWS
    pad_end = jnp.cumsum(padded)
    pad_start = pad_end - padded
    n_blocks = -(-((n_p + n_s) * TOP_K) // MOE_ROWS) + N_EXPERTS
    blk0 = jnp.arange(n_blocks, dtype=jnp.int32) * MOE_ROWS
    owner = blk0[:, None] >= pad_end[None, :]
    block_expert = jnp.minimum(jnp.sum(owner, axis=1), N_EXPERTS - 1).astype(jnp.int32)
    onehot = block_expert[:, None] == jnp.arange(N_EXPERTS, dtype=jnp.int32)[None, :]
    used = jnp.sum(jnp.where(onehot, (blk0[:, None] - pad_start[None, :]), 0), axis=1)
    block_valid = jnp.clip(jnp.sum(jnp.where(onehot, counts[None, :], 0), axis=1) - used, 0, MOE_ROWS).astype(jnp.int32)

    tm = min(256, n_s)
    idx_all = jnp.concatenate([idx_p, idx_s], axis=1)
    rank_all = jnp.concatenate([rank_p, rank_s], axis=1)
    first_slot = functools.reduce(lambda acc, e: jnp.where(idx_all == e, pad_start[e], acc), range(N_EXPERTS),
                                  jnp.zeros_like(idx_all))
    slot_all = first_slot + rank_all
    xs = _dispatch(slot_all, pad_start + counts, padded - counts, pad_end[N_EXPERTS - 1:] // MOE_ROWS,
                   xn_p, xn_s, n_blocks, tm)
    ys = _experts(block_expert, block_valid, xs, w1p, b1p, w2b, b2r)
    nfin = norm_final[None, :]
    meta, col_rows = _combine_meta(idx_all, slot_all, pad_start)
    y_prompt = _combine(meta, col_rows, gate_p[:TOP_K].T, h_p, nfin, ys, 0)
    y_sample = _combine(meta, col_rows, gate_s[:TOP_K].T, h_s, nfin, ys, n_p // COMBINE_TM)

    return (y_prompt.reshape(bp, seq, D_MODEL), y_sample.reshape(bs, t_new, D_MODEL),
            conv_p[None], ssm_p.reshape(1, bp, SSD_HEADS, SSD_HEAD_DIM, SSD_STATE), k_p[None], v_p[None],
            conv_s[None], ssm_s.reshape(1, bs, SSD_HEADS, SSD_HEAD_DIM, SSD_STATE),
            k_s.reshape(1, bs, t_new, ATT_HEADS, ATT_HEAD_DIM), v_s.reshape(1, bs, t_new, ATT_HEADS, ATT_HEAD_DIM))
```

```python
import functools

import jax
import jax.numpy as jnp
import numpy as np
from jax import lax
from jax.experimental import pallas as pl
from jax.experimental.pallas import tpu as pltpu

F32 = jnp.float32
BF16 = jnp.bfloat16

D_MODEL = 1024
SSD_HEADS = 16
SSD_HEAD_DIM = 64
SSD_INNER = SSD_HEADS * SSD_HEAD_DIM
SSD_GROUPS = 2
SSD_STATE = 128
SSD_CONV = 4
SSD_CHUNK = 128
CONV_DIM = SSD_INNER + 2 * SSD_GROUPS * SSD_STATE
ATT_HEADS = 8
ATT_HEAD_DIM = 64
ATT_WIDTH = ATT_HEADS * ATT_HEAD_DIM
DILATED_PATTERNS = ((128, 1), (512, 4), (2048, 16))
ATT_BLOCK = 128
N_EXPERTS = 32
TOP_K = 4
SWIGLU_LIMIT = 7.0
SWIGLU_ALPHA = 1.702
NORM_EPS = 1e-5

LANES = 128
SUBLANES = 8
ZX_WIDTH = SSD_INNER + CONV_DIM
DT_PAD = LANES
CONV_HIST = SUBLANES
MOE_ROWS = 512
VMEM_LIMIT = 56 * 1024 * 1024

NEG_INF = float("-inf")


def _cparams(sem):
    return pltpu.CompilerParams(dimension_semantics=sem, vmem_limit_bytes=VMEM_LIMIT)


def _rms(x, g):
    return x * lax.rsqrt(jnp.mean(x * x, axis=-1, keepdims=True) + NORM_EPS) * g


def _dot(a, b):
    return jnp.dot(a, b, preferred_element_type=F32)


def _dot_nt(a, b):
    return lax.dot_general(a, b, (((1,), (1,)), ((), ())), preferred_element_type=F32)


def _split3(v):
    hi = v.astype(BF16)
    r1 = v - hi.astype(F32)
    mid = r1.astype(BF16)
    lo = (r1 - mid.astype(F32)).astype(BF16)
    return hi, mid, lo


def _dot3(v, m):
    hi, mid, lo = _split3(v)
    return _dot(hi, m) + _dot(mid, m) + _dot(lo, m)


def _dot3_lhs(m, v):
    hi, mid, lo = _split3(v)
    return _dot(m, hi) + _dot(m, mid) + _dot(m, lo)


def _silu(x):
    half = 0.5 * x
    return half * (1.0 + jnp.tanh(half))


def _in_proj_body(x_ref, g_ref, w_ref, zx_ref, q_ref, k_ref, v_ref, dt_ref, *t_refs):
    hn = _rms(x_ref[...], g_ref[...]).astype(BF16)
    o = 0
    for ref, width in ((zx_ref, ZX_WIDTH), (q_ref, ATT_WIDTH), (k_ref, ATT_WIDTH), (v_ref, ATT_WIDTH),
                       (dt_ref, DT_PAD)):
        ref[...] = _dot(hn, w_ref[:, o:o + width])
        o += width
    for src, dst in zip((k_ref, v_ref), t_refs):
        for j in range(ATT_WIDTH // LANES):
            dst[0, j * LANES:(j + 1) * LANES, :] = src[:, j * LANES:(j + 1) * LANES].T


def _in_proj(x2d, g, w_cat, tm, seq=None):
    n = x2d.shape[0]
    tm = min(tm, n)
    per = 1 if seq is None else seq // tm
    row = lambda w: pl.BlockSpec((tm, w), lambda b, i: (b * per + i, 0))
    full = lambda a: pl.BlockSpec(a.shape, lambda b, i: (0,) * a.ndim)
    widths = (ZX_WIDTH, ATT_WIDTH, ATT_WIDTH, ATT_WIDTH, DT_PAD)
    out_specs = [row(w) for w in widths]
    out_shape = [jax.ShapeDtypeStruct((n, w), F32) for w in widths]
    if seq is not None:
        out_specs += [pl.BlockSpec((1, ATT_WIDTH, tm), lambda b, i: (b, 0, i))] * 2
        out_shape += [jax.ShapeDtypeStruct((n // seq, ATT_WIDTH, seq), F32)] * 2
    return pl.pallas_call(
        _in_proj_body,
        grid=(n // (tm * per), per),
        in_specs=[row(D_MODEL), full(g), full(w_cat)],
        out_specs=out_specs,
        out_shape=out_shape,
        compiler_params=_cparams(("parallel", "parallel")),
        name="in_proj",
    )(x2d, g, w_cat)


def _ssd_body(*refs, l_blk, has_init):
    T = SSD_CHUNK
    if has_init:
        (zx_ref, dt_ref, cw_ref, cb_ref, dtb_ref, alog_ref, dskip_ref, norm_ref, e64_ref,
         cinit_ref, sinit_ref, y_ref, conv_out_ref, ssm_out_ref,
         ext_ref, act_ref, state_ref, cst_ref, ybuf_ref, zpad_ref, dtpad_ref) = refs
    else:
        (zx_ref, dt_ref, cw_ref, cb_ref, dtb_ref, alog_ref, dskip_ref, norm_ref, e64_ref,
         y_ref, conv_out_ref, ssm_out_ref,
         ext_ref, act_ref, state_ref, cst_ref, ybuf_ref) = refs
    c = pl.program_id(1)
    n_tile = SSD_INNER // LANES

    @pl.when(c == 0)
    def _start():
        if has_init:
            ext_ref[0:CONV_HIST, :] = jnp.zeros((CONV_HIST, CONV_DIM), F32)
            ext_ref[CONV_HIST - (SSD_CONV - 1):CONV_HIST, :] = cinit_ref[0]
            for j in range(n_tile):
                state_ref[:, j * LANES:(j + 1) * LANES] = sinit_ref[0, j * LANES:(j + 1) * LANES, :].T
        else:
            ext_ref[0:CONV_HIST, :] = jnp.zeros((CONV_HIST, CONV_DIM), F32)
            state_ref[...] = jnp.zeros_like(state_ref)

    if l_blk == T:
        ext_ref[CONV_HIST:CONV_HIST + T, :] = zx_ref[0, :, SSD_INNER:ZX_WIDTH]
        z_of = lambda sl: zx_ref[0, :, sl]
        dt_raw = dt_ref[0]
    else:
        ext_ref[CONV_HIST:CONV_HIST + T, :] = jnp.zeros((T, CONV_DIM), F32)
        ext_ref[CONV_HIST:CONV_HIST + l_blk, :] = zx_ref[0, :, SSD_INNER:ZX_WIDTH]
        zpad_ref[...] = jnp.zeros_like(zpad_ref)
        zpad_ref[0:l_blk, :] = zx_ref[0, :, 0:SSD_INNER]
        dtpad_ref[...] = jnp.zeros_like(dtpad_ref)
        dtpad_ref[0:l_blk, :] = dt_ref[0]
        z_of = lambda sl: zpad_ref[:, sl]
        dt_raw = dtpad_ref[...]

    cw = CONV_DIM // 3
    for cc in range(3):
        sl = slice(cc * cw, (cc + 1) * cw)
        acc = cb_ref[:, sl]
        for j in range(SSD_CONV):
            o = CONV_HIST - (SSD_CONV - 1) + j
            acc = acc + ext_ref[o:o + T, sl] * cw_ref[j:j + 1, sl]
        act_ref[:, sl] = _silu(acc)

    row = lax.broadcasted_iota(jnp.int32, (T, LANES), 0)
    col = lax.broadcasted_iota(jnp.int32, (T, LANES), 1)
    tri = row >= col
    tri_bf = jnp.where(tri, 1.0, 0.0).astype(BF16)
    even = col < SSD_HEAD_DIM

    xdt = dt_raw + dtb_ref[...]
    dtv = jnp.maximum(xdt, 0.0) + jnp.log1p(jnp.exp(-jnp.abs(xdt)))
    if l_blk < T:
        dtv = jnp.where(row < l_blk, dtv, 0.0)
    d_a = dtv * (-jnp.exp(alog_ref[...]))
    cs = _dot3_lhs(tri_bf, d_a)
    cst_ref[...] = cs.T
    ex = _dot3(jnp.concatenate([dtv, cs], axis=0), e64_ref[...])
    dt_ex, cs_ex = ex[0:T], ex[T:2 * T]
    ecs_ex = jnp.exp(cs_ex)
    dd_ex = dt_ex * jnp.exp(cs_ex[T - 1:T] - cs_ex)

    gw = SSD_INNER // SSD_GROUPS
    heads_per_group = SSD_HEADS // SSD_GROUPS
    for g in range(SSD_GROUPS):
        gsl = slice(g * gw, (g + 1) * gw)
        b_g = act_ref[:, SSD_INNER + g * SSD_STATE:SSD_INNER + (g + 1) * SSD_STATE]
        c_off = SSD_INNER + SSD_GROUPS * SSD_STATE
        c_g = act_ref[:, c_off + g * SSD_STATE:c_off + (g + 1) * SSD_STATE].astype(BF16)
        cb = _dot_nt(c_g, b_g.astype(BF16))
        b_gt = b_g.T.astype(BF16)
        x_g = act_ref[:, gsl]
        x_dt = (x_g * dt_ex[:, gsl]).astype(BF16)
        x_dd = (x_g * dd_ex[:, gsl]).astype(BF16)
        st_old = state_ref[:, gsl]
        y_off = _dot(c_g, st_old.astype(BF16)) * ecs_ex[:, gsl]
        state_ref[:, gsl] = st_old * ecs_ex[T - 1:T, gsl] + _dot(b_gt, x_dd)
        for jp in range(heads_per_group // 2):
            h0 = g * heads_per_group + 2 * jp
            psl = slice(jp * LANES, (jp + 1) * LANES)
            osl = slice(g * gw + jp * LANES, g * gw + (jp + 1) * LANES)
            pair = cs_ex[:, osl]
            swapped = pltpu.roll(pair, SSD_HEAD_DIM, axis=1)
            cols = (jnp.where(even, pair, swapped), jnp.where(even, swapped, pair))
            yd = []
            for col, h in zip(cols, (h0, h0 + 1)):
                seg = col - cst_ref[h:h + 1, :]
                lmat = jnp.exp(jnp.where(tri, seg, NEG_INF))
                yd.append(_dot((cb * lmat).astype(BF16), x_dt[:, psl]))
            y_pair = jnp.where(even, yd[0], yd[1]) + y_off[:, psl]
            ybuf_ref[:, osl] = y_pair + dskip_ref[:, osl] * x_g[:, psl]

    for g in range(SSD_GROUPS):
        gsl = slice(g * gw, (g + 1) * gw)
        yg = ybuf_ref[:, gsl] * _silu(z_of(gsl))
        yn = yg * lax.rsqrt(jnp.mean(yg * yg, axis=-1, keepdims=True) + NORM_EPS) * norm_ref[:, gsl]
        y_ref[0, :, gsl] = yn[0:l_blk].astype(y_ref.dtype)

    @pl.when(c == pl.num_programs(1) - 1)
    def _finish():
        lo = CONV_HIST + l_blk - (SSD_CONV - 1)
        conv_out_ref[0] = ext_ref[lo:lo + SSD_CONV - 1, :]
        for j in range(n_tile):
            ssm_out_ref[0, j * LANES:(j + 1) * LANES, :] = state_ref[:, j * LANES:(j + 1) * LANES].T

    ext_ref[0:CONV_HIST, :] = ext_ref[T:T + CONV_HIST, :]


def _ssd(zx3, dt3, consts, init, y_dtype):
    b, L, _ = zx3.shape
    T = SSD_CHUNK
    l_blk = T if L % T == 0 else L
    n_chunks = L // l_blk
    has_init = init is not None
    full = lambda a: pl.BlockSpec(a.shape, lambda i, c: (0,) * a.ndim)
    in_specs = [pl.BlockSpec((1, l_blk, ZX_WIDTH), lambda i, c: (i, c, 0)),
                pl.BlockSpec((1, l_blk, DT_PAD), lambda i, c: (i, c, 0))] + [full(a) for a in consts]
    args = [zx3, dt3, *consts]
    scratch = [pltpu.VMEM((T + CONV_HIST, CONV_DIM), F32),
               pltpu.VMEM((T, CONV_DIM), F32),
               pltpu.VMEM((SSD_STATE, SSD_INNER), F32),
               pltpu.VMEM((LANES, T), F32),
               pltpu.VMEM((T, SSD_INNER), F32)]
    if has_init:
        in_specs += [pl.BlockSpec((1, SSD_CONV - 1, CONV_DIM), lambda i, c: (i, 0, 0)),
                     pl.BlockSpec((1, SSD_INNER, SSD_STATE), lambda i, c: (i, 0, 0))]
        args += list(init)
        scratch += [pltpu.VMEM((T, SSD_INNER), F32), pltpu.VMEM((T, DT_PAD), F32)]
    return pl.pallas_call(
        functools.partial(_ssd_body, l_blk=l_blk, has_init=has_init),
        grid=(b, n_chunks),
        in_specs=in_specs,
        out_specs=[pl.BlockSpec((1, l_blk, SSD_INNER), lambda i, c: (i, c, 0)),
                   pl.BlockSpec((1, SSD_CONV - 1, CONV_DIM), lambda i, c: (i, 0, 0)),
                   pl.BlockSpec((1, SSD_INNER, SSD_STATE), lambda i, c: (i, 0, 0))],
        out_shape=[jax.ShapeDtypeStruct((b, L, SSD_INNER), y_dtype),
                   jax.ShapeDtypeStruct((b, SSD_CONV - 1, CONV_DIM), F32),
                   jax.ShapeDtypeStruct((b, SSD_INNER, SSD_STATE), F32)],
        scratch_shapes=scratch,
        compiler_params=_cparams(("parallel", "arbitrary")),
        name="ssd_init" if has_init else "ssd",
    )(*args)


ATT_UNROLL = 16


def _unroll(n):
    return max(d for d in range(1, ATT_UNROLL + 1) if n % d == 0)


ATT_RES = DILATED_PATTERNS[-1][1]


def _attn_tables():
    B = ATT_BLOCK
    t1 = np.full((len(DILATED_PATTERNS), B, B), -np.inf, np.float32)
    t2 = np.full((len(DILATED_PATTERNS), B, 2 * B), -np.inf, np.float32)
    for p, (window, dil) in enumerate(DILATED_PATTERNS):
        m = ATT_RES // dil
        w = B // m
        rho = np.arange(B)
        c = m * (rho % w) + rho // w
        ck2 = np.concatenate([c - B, c])
        for tab, ck in ((t1, c), (t2, ck2)):
            delta = c[:, None] - ck[None, :]
            ok = (delta >= 0) & (delta <= window // dil)
            tab[p] = np.where(ok, -(dil * delta).astype(np.float32), -np.inf)
    return jnp.asarray(t1), jnp.asarray(t2)


def _attn_body(slopes_ref, t1_ref, t2_ref, q_ref, k_ref, v_ref, o_ref, qd_ref, kd_ref, vd_ref, op_ref, lp_ref,
               stage_ref, b1_ref, b2_ref, in_sem, out_sem, *, seq):
    B = ATT_BLOCK
    hp = pl.program_id(1)
    even = lax.broadcasted_iota(jnp.int32, (B, LANES), 1) < ATT_HEAD_DIM
    scale = ATT_HEAD_DIM ** -0.5

    def in_copy(a, r):
        src, dst = ((q_ref, qd_ref), (k_ref, kd_ref), (v_ref, vd_ref))[a]
        return pltpu.make_async_copy(src.at[:, r, :], dst.at[r], in_sem.at[a, r])
    for r in range(ATT_RES):
        for a in range(3):
            in_copy(a, r).start()

    for e in range(2):
        slope = slopes_ref[2 * hp + e]
        for p in range(len(DILATED_PATTERNS)):
            b1_ref[e * len(DILATED_PATTERNS) + p] = t1_ref[p] * slope
            b2_ref[e * len(DILATED_PATTERNS) + p] = t2_ref[p] * slope

    for r in range(ATT_RES):
        for a in range(3):
            in_copy(a, r).wait()

    def block(p, r_d, n, first):
        dil = DILATED_PATTERNS[p][1]
        m = ATT_RES // dil
        w = B // m

        def slab(j, nblk):
            start = nblk * w
            return r_d + dil * j, pl.ds(start if isinstance(start, int) else pl.multiple_of(start, w), w)

        def gather(src, nblk):
            parts = []
            for j in range(m):
                r, rows = slab(j, nblk)
                parts.append(src[r, rows, :])
            return parts[0] if m == 1 else jnp.concatenate(parts, axis=0)

        qb = gather(qd_ref, n) * scale
        if first:
            kb, vb = gather(kd_ref, n), gather(vd_ref, n)
            bias = lambda e: b1_ref[e * len(DILATED_PATTERNS) + p]
        else:
            kb = jnp.concatenate([gather(kd_ref, n - 1), gather(kd_ref, n)], axis=0)
            vb = jnp.concatenate([gather(vd_ref, n - 1), gather(vd_ref, n)], axis=0)
            bias = lambda e: b2_ref[e * len(DILATED_PATTERNS) + p]
        q2 = jnp.concatenate([jnp.where(even, qb, 0.0), jnp.where(even, 0.0, qb)], axis=0).astype(BF16)
        s = _dot_nt(q2, kb.astype(BF16)) + jnp.concatenate([bias(0), bias(1)], axis=0)
        mx = jnp.max(s, axis=-1, keepdims=True)
        pr = jnp.exp(s - mx)
        l = jnp.sum(pr, axis=-1, keepdims=True)
        o2 = _dot(pr.astype(BF16), vb.astype(BF16)) / l
        lse2 = jnp.broadcast_to(mx + jnp.log(l), (2 * B, LANES))
        o = jnp.where(even, o2[:B], o2[B:])
        lse = jnp.where(even, lse2[:B], lse2[B:])
        for j in range(m):
            r, rows = slab(j, n)
            op_ref[p, r, rows, :] = o[j * w:(j + 1) * w]
            lp_ref[p, r, rows, :] = lse[j * w:(j + 1) * w]

    for p, (_, dil) in enumerate(DILATED_PATTERNS):
        nb = seq // (dil * B)
        u_first = _unroll(dil)

        def first(i, carry, p=p, u_first=u_first):
            for u in range(u_first):
                block(p, i * u_first + u, 0, True)
            return carry
        lax.fori_loop(0, dil // u_first, first, 0)
        n_later = dil * (nb - 1)
        if n_later:
            u_later = _unroll(n_later)

            def later(i, carry, p=p, nb=nb, u_later=u_later):
                for u in range(u_later):
                    j = i * u_later + u
                    block(p, j // (nb - 1), j % (nb - 1) + 1, False)
                return carry
            lax.fori_loop(0, n_later // u_later, later, 0)

    def out_copy(r):
        return pltpu.make_async_copy(stage_ref.at[r], o_ref.at[:, r, :], out_sem.at[r])
    for r in range(ATT_RES):
        l0, l1, l2 = lp_ref[0, r], lp_ref[1, r], lp_ref[2, r]
        m = jnp.maximum(jnp.maximum(l0, l1), l2)
        w0, w1, w2 = jnp.exp(l0 - m), jnp.exp(l1 - m), jnp.exp(l2 - m)
        num = w0 * op_ref[0, r] + w1 * op_ref[1, r] + w2 * op_ref[2, r]
        stage_ref[r] = num / (w0 + w1 + w2)
        out_copy(r).start()
    for r in range(ATT_RES):
        out_copy(r).wait()


def _attn_prompt(q, k, v, slopes, batch, seq):
    n_hp = ATT_HEADS // 2
    n_pat = len(DILATED_PATTERNS)
    assert len(DILATED_PATTERNS) == 3 and all(ATT_RES % d == 0 and seq % (d * ATT_BLOCK) == 0 and w // d == ATT_BLOCK
                                              for w, d in DILATED_PATTERNS)
    per = seq // ATT_RES
    by_res = lambda a: a.reshape(batch * per, ATT_RES, ATT_WIDTH)
    blk = pl.BlockSpec((per, ATT_RES, LANES), lambda b, h: (b, 0, h))
    t1, t2 = _attn_tables()
    full = lambda a: pl.BlockSpec(a.shape, lambda b, h: (0,) * a.ndim)
    res = pltpu.VMEM((ATT_RES, per, LANES), F32)
    out = pl.pallas_call(
        functools.partial(_attn_body, seq=seq),
        grid=(batch, n_hp),
        in_specs=[pl.BlockSpec(memory_space=pltpu.SMEM), full(t1), full(t2), blk, blk, blk],
        out_specs=blk,
        out_shape=jax.ShapeDtypeStruct((batch * per, ATT_RES, ATT_WIDTH), F32),
        scratch_shapes=[res, res, res,
                        pltpu.VMEM((n_pat, ATT_RES, per, LANES), F32), pltpu.VMEM((n_pat, ATT_RES, per, LANES), F32),
                        res,
                        pltpu.VMEM((2 * n_pat, ATT_BLOCK, ATT_BLOCK), F32),
                        pltpu.VMEM((2 * n_pat, ATT_BLOCK, 2 * ATT_BLOCK), F32),
                        pltpu.SemaphoreType.DMA((3, ATT_RES)), pltpu.SemaphoreType.DMA((ATT_RES,))],
        compiler_params=_cparams(("parallel", "parallel")),
        name="attn_prompt",
    )(slopes, t1, t2, by_res(q), by_res(k), by_res(v))
    return out.reshape(batch * seq, ATT_WIDTH)


DEC_ROWS = SUBLANES


def _decode_tables(t_new, m_cache):
    slopes = 2.0 ** (-8.0 * np.arange(1, ATT_HEADS + 1) / ATT_HEADS)

    def mult(dist):
        return sum(1 for window, dil in DILATED_PATTERNS if 0 <= dist <= window and dist % dil == 0)

    bias_c = np.zeros((ATT_HEADS, DEC_ROWS, m_cache), np.float32)
    mult_c = np.ones((DEC_ROWS, m_cache), np.float32)
    bias_n = np.full((ATT_HEADS, DEC_ROWS, LANES), -np.inf, np.float32)
    mult_n = np.zeros((DEC_ROWS, LANES), np.float32)
    for t in range(t_new):
        dist = m_cache + t - np.arange(m_cache)
        mu = np.array([mult(d) for d in dist], np.float32)
        mult_c[t] = mu
        bias_c[:, t, :] = np.where(mu > 0, -slopes[:, None] * dist[None, :], -np.inf)
        for t2 in range(t_new):
            if mult(t - t2):
                bias_n[:, t, t2] = -slopes * (t - t2)
                mult_n[t, t2] = mult(t - t2)
    bias_n[:, t_new:, 0] = 0.0
    mult_n[t_new:, 0] = 1.0
    return jnp.asarray(bias_c), jnp.asarray(mult_c), jnp.asarray(bias_n), jnp.asarray(mult_n)


def _attn_dec_body(q_ref, knt_ref, vnt_ref, ck_ref, cv_ref, bc_ref, mc_ref, bn_ref, mn_ref, o_ref):
    scale = ATT_HEAD_DIM ** -0.5
    for h in range(ATT_HEADS):
        qh = (q_ref[0, h] * scale).astype(BF16)
        sc = _dot(qh, ck_ref[0, h].astype(BF16)) + bc_ref[h]
        sn = _dot(qh, knt_ref[0, h].astype(BF16)) + bn_ref[h]
        m = jnp.maximum(jnp.max(sc, axis=-1, keepdims=True), jnp.max(sn, axis=-1, keepdims=True))
        pc = jnp.exp(sc - m) * mc_ref[...]
        pn = jnp.exp(sn - m) * mn_ref[...]
        inv_l = 1.0 / (jnp.sum(pc, axis=-1, keepdims=True) + jnp.sum(pn, axis=-1, keepdims=True))
        o_ref[0, h] = (_dot_nt(cv_ref[0, h].astype(BF16), (pc * inv_l).astype(BF16))
                       + _dot_nt(vnt_ref[0, h].astype(BF16), (pn * inv_l).astype(BF16)))


def _attn_decode(q4, knt, vnt, cache_kt, cache_vt, t_new):
    b = q4.shape[0]
    m_cache = cache_kt.shape[3]
    assert t_new <= DEC_ROWS
    tables = _decode_tables(t_new, m_cache)
    per_b = lambda a: pl.BlockSpec((1,) + a.shape[1:], lambda i: (i, 0, 0, 0))
    full = lambda a: pl.BlockSpec(a.shape, lambda i: (0,) * a.ndim)
    out_shape = jax.ShapeDtypeStruct((b, ATT_HEADS, ATT_HEAD_DIM, DEC_ROWS), F32)
    return pl.pallas_call(
        _attn_dec_body,
        grid=(b,),
        in_specs=[per_b(a) for a in (q4, knt, vnt, cache_kt, cache_vt)] + [full(a) for a in tables],
        out_specs=per_b(out_shape),
        out_shape=out_shape,
        compiler_params=_cparams(("parallel",)),
        name="attn_decode",
    )(q4, knt, vnt, cache_kt, cache_vt, *tables)


def _out_proj_body(x_ref, y_ref, o_ref, an_ref, wy_ref, wo_ref, nf_ref, rw_ref, rb_ref, cnt0_ref,
                   h_ref, xn_ref, idx_ref, gate_ref, rank_ref, cnt_ref, carry_ref, *, tm):
    i = pl.program_id(0)

    @pl.when(i == 0)
    def _start():
        carry_ref[...] = cnt0_ref[...]

    on = _rms(o_ref[...], an_ref[...]).astype(BF16)
    mixed = _dot(y_ref[...].astype(BF16), wy_ref[...]) + _dot(on, wo_ref[...])
    h = x_ref[...] + mixed
    h_ref[...] = h
    xn = _rms(h, nf_ref[...])
    xn_ref[...] = xn
    logits = _dot_nt(rw_ref[...], xn.astype(BF16)) + rb_ref[...]

    e_iota = lax.broadcasted_iota(jnp.int32, (N_EXPERTS, tm), 0)
    vals, idxs, sels = [], [], []
    cur = logits
    for _ in range(TOP_K):
        mx = jnp.max(cur, axis=0, keepdims=True)
        ix = jnp.min(jnp.where(cur == mx, e_iota, N_EXPERTS), axis=0, keepdims=True)
        sel = e_iota == ix
        cur = jnp.where(sel, NEG_INF, cur)
        vals.append(mx)
        idxs.append(ix)
        sels.append(sel)
    ex = [jnp.exp(v - vals[0]) for v in vals]
    den = functools.reduce(lambda a, b: a + b, ex)

    sel_any = functools.reduce(jnp.logical_or, sels)
    sel_f = jnp.where(sel_any, 1.0, 0.0)
    r = lax.broadcasted_iota(jnp.int32, (tm, tm), 0)
    c = lax.broadcasted_iota(jnp.int32, (tm, tm), 1)
    upper = jnp.where(r <= c, 1.0, 0.0).astype(BF16)
    cum = _dot(sel_f.astype(BF16), upper)
    before = carry_ref[:, 0:1] + cum - sel_f
    for k in range(TOP_K):
        idx_ref[k:k + 1, :] = idxs[k]
        gate_ref[k:k + 1, :] = ex[k] / den
        rank_ref[k:k + 1, :] = jnp.sum(jnp.where(sels[k], before, 0.0), axis=0, keepdims=True).astype(jnp.int32)
    idx_ref[TOP_K:SUBLANES, :] = jnp.zeros((SUBLANES - TOP_K, tm), jnp.int32)
    gate_ref[TOP_K:SUBLANES, :] = jnp.zeros((SUBLANES - TOP_K, tm), F32)
    rank_ref[TOP_K:SUBLANES, :] = jnp.zeros((SUBLANES - TOP_K, tm), jnp.int32)
    carry_ref[...] = carry_ref[...] + jnp.max(cum, axis=1, keepdims=True)
    cnt_ref[...] = carry_ref[...]


def _out_proj(x2d, y2d, o2d, consts, cnt0, tm):
    n = x2d.shape[0]
    tm = min(tm, n)
    row = lambda w: pl.BlockSpec((tm, w), lambda i: (i, 0))
    colb = pl.BlockSpec((SUBLANES, tm), lambda i: (0, i))
    full = lambda a: pl.BlockSpec(a.shape, lambda i: (0,) * a.ndim)
    return pl.pallas_call(
        functools.partial(_out_proj_body, tm=tm),
        grid=(n // tm,),
        in_specs=[row(D_MODEL), row(SSD_INNER), row(ATT_WIDTH)] + [full(a) for a in consts] + [full(cnt0)],
        out_specs=[row(D_MODEL), row(D_MODEL), colb, colb, colb, full(cnt0)],
        out_shape=[jax.ShapeDtypeStruct((n, D_MODEL), F32), jax.ShapeDtypeStruct((n, D_MODEL), F32),
                   jax.ShapeDtypeStruct((SUBLANES, n), jnp.int32), jax.ShapeDtypeStruct((SUBLANES, n), F32),
                   jax.ShapeDtypeStruct((SUBLANES, n), jnp.int32), jax.ShapeDtypeStruct(cnt0.shape, F32)],
        scratch_shapes=[pltpu.VMEM(cnt0.shape, F32)],
        compiler_params=_cparams(("arbitrary",)),
        name="out_proj",
    )(x2d, y2d, o2d, *consts, cnt0)


ISSUE_UNROLL = 8


def _dispatch_body(slot_ref, from_ref, n_ref, tail_ref, xa_ref, xb_ref, xs_ref, zero_ref, stage_ref, sem, row_sems,
                   *, tm, tiles_a, tiles_b, n_blocks):
    i = pl.program_id(0)

    @pl.when(i == 0)
    def _pad_fill():
        zero_ref[...] = jnp.zeros_like(zero_ref)
        one_row = lambda slot: pltpu.make_async_copy(zero_ref.at[pl.ds(0, 1)], xs_ref.at[pl.ds(slot, 1)], sem)
        block = lambda j: pltpu.make_async_copy(zero_ref, xs_ref.at[pl.ds(j * MOE_ROWS, MOE_ROWS)], sem)

        def per_expert(e, carry):
            def start(r, c):
                one_row(from_ref[e] + r).start()
                return c

            def wait(r, c):
                one_row(0).wait()
                return c
            lax.fori_loop(0, n_ref[e], start, 0)
            lax.fori_loop(0, n_ref[e], wait, 0)
            return carry
        lax.fori_loop(0, N_EXPERTS, per_expert, 0)

        def tail(j, carry):
            block(j).start()
            block(j).wait()
            return carry
        lax.fori_loop(tail_ref[0], n_blocks, tail, 0)

    def drain(parity):
        for _ in range(TOP_K):
            pltpu.make_async_copy(stage_ref.at[parity], xs_ref.at[pl.ds(0, tm)], row_sems.at[parity]).wait()

    for parity in range(2):
        @pl.when(i % 2 == parity)
        def _step(parity=parity):
            @pl.when(i < tiles_a)
            def _first_group():
                stage_ref[parity] = xa_ref[...]

            @pl.when(i >= tiles_a)
            def _second_group():
                stage_ref[parity] = xb_ref[...]

            def issue(j, carry):
                t0 = pl.multiple_of(j * ISSUE_UNROLL, ISSUE_UNROLL)
                for u in range(ISSUE_UNROLL):
                    for k in range(TOP_K):
                        pltpu.make_async_copy(stage_ref.at[parity, pl.ds(t0 + u, 1)],
                                              xs_ref.at[pl.ds(slot_ref[k, t0 + u], 1)], row_sems.at[parity]).start()
                return carry
            lax.fori_loop(0, tm // ISSUE_UNROLL, issue, 0)

            @pl.when(i > 0)
            def _previous():
                drain(1 - parity)

            @pl.when(i == tiles_a + tiles_b - 1)
            def _last():
                drain(parity)


def _smem_cols(tm, first=0):
    return pl.BlockSpec((SUBLANES, tm), lambda i: (0, i + first), memory_space=pltpu.SMEM)


def _dispatch(slot, pad_from, pad_n, tail_block, xa, xb, n_blocks, tm):
    tiles_a, tiles_b = xa.shape[0] // tm, xb.shape[0] // tm
    assert tiles_a * tm == xa.shape[0] and tiles_b * tm == xb.shape[0]
    smem = pl.BlockSpec(memory_space=pltpu.SMEM)
    hbm = pl.BlockSpec(memory_space=pl.ANY)
    return pl.pallas_call(
        functools.partial(_dispatch_body, tm=tm, tiles_a=tiles_a, tiles_b=tiles_b, n_blocks=n_blocks),
        grid=(tiles_a + tiles_b,),
        in_specs=[_smem_cols(tm), smem, smem, smem,
                  pl.BlockSpec((tm, D_MODEL), lambda i: (jnp.minimum(i, tiles_a - 1), 0)),
                  pl.BlockSpec((tm, D_MODEL), lambda i: (jnp.maximum(i - tiles_a, 0), 0))],
        out_specs=hbm,
        out_shape=jax.ShapeDtypeStruct((n_blocks * MOE_ROWS, D_MODEL), F32),
        scratch_shapes=[pltpu.VMEM((MOE_ROWS, D_MODEL), F32), pltpu.VMEM((2, tm, D_MODEL), F32),
                        pltpu.SemaphoreType.DMA(()), pltpu.SemaphoreType.DMA((2,))],
        compiler_params=_cparams(("arbitrary",)),
        name="dispatch",
    )(slot, pad_from, pad_n, tail_block, xa, xb)


MOE_COLS = 2 * LANES


W1_PREP_COLS = 4 * MOE_COLS


def _w1_prep_body(w_ref, p_ref, o_ref):
    for c in range(W1_PREP_COLS // MOE_COLS):
        cols = slice(c * MOE_COLS, (c + 1) * MOE_COLS)
        o_ref[0, :, cols] = _dot(w_ref[0, :, cols].astype(BF16), p_ref[...]).astype(BF16)


def _w1_prep(w1):
    e, d, n2 = w1.shape
    src = np.concatenate([np.arange(0, MOE_COLS, 2), np.arange(1, MOE_COLS, 2)])
    perm = np.zeros((MOE_COLS, MOE_COLS), np.float32)
    perm[src, np.arange(MOE_COLS)] = 1.0
    blk = pl.BlockSpec((1, d, W1_PREP_COLS), lambda i, j: (i, 0, j))
    return pl.pallas_call(
        _w1_prep_body,
        grid=(e, n2 // W1_PREP_COLS),
        in_specs=[blk, pl.BlockSpec((MOE_COLS, MOE_COLS), lambda i, j: (0, 0))],
        out_specs=blk,
        out_shape=jax.ShapeDtypeStruct(w1.shape, BF16),
        compiler_params=_cparams(("parallel", "parallel")),
        name="w1_prep",
    )(w1, jnp.asarray(perm, dtype=BF16))


def _experts_body(be_ref, bv_ref, xs_ref, w1_ref, b1_ref, w2_ref, b2_ref, ys_ref, g_ref):
    j = pl.program_id(0)
    valid = bv_ref[j]

    @pl.when(valid > 0)
    def _compute():
        x = xs_ref[...].astype(BF16)
        for c in range(g_ref.shape[1] // LANES):
            cols = slice(c * MOE_COLS, (c + 1) * MOE_COLS)
            hc = _dot(x, w1_ref[0, :, cols]) + b1_ref[0, :, cols]
            glu = jnp.minimum(hc[:, :LANES], SWIGLU_LIMIT)
            lin = jnp.clip(hc[:, LANES:], -SWIGLU_LIMIT, SWIGLU_LIMIT)
            g_ref[:, c * LANES:(c + 1) * LANES] = (glu * jax.nn.sigmoid(SWIGLU_ALPHA * glu) * (lin + 1.0)).astype(BF16)
        ys_ref[...] = _dot(g_ref[...], w2_ref[0]) + b2_ref[0]

    @pl.when(valid <= 0)
    def _empty():
        ys_ref[...] = jnp.zeros_like(ys_ref)


def _experts(block_expert, block_valid, xs, w1p, b1p, w2, b2):
    n_blocks = xs.shape[0] // MOE_ROWS
    wspec = lambda a: pl.BlockSpec((1,) + a.shape[1:], lambda j, be, bv: (be[j], 0, 0))
    grid_spec = pltpu.PrefetchScalarGridSpec(
        num_scalar_prefetch=2,
        grid=(n_blocks,),
        in_specs=[pl.BlockSpec((MOE_ROWS, D_MODEL), lambda j, be, bv: (j, 0)),
                  wspec(w1p), wspec(b1p), wspec(w2), wspec(b2)],
        out_specs=pl.BlockSpec((MOE_ROWS, D_MODEL), lambda j, be, bv: (j, 0)),
        scratch_shapes=[pltpu.VMEM((MOE_ROWS, w2.shape[1]), BF16)],
    )
    return pl.pallas_call(
        _experts_body,
        grid_spec=grid_spec,
        out_shape=jax.ShapeDtypeStruct(xs.shape, F32),
        compiler_params=_cparams(("arbitrary",)),
        name="experts",
    )(block_expert, block_valid, xs, w1p, b1p, w2, b2)


COMBINE_TM = 256
RUN_UNIT = 16
RUN_BITS = tuple(1 << b for b in reversed(range((COMBINE_TM // RUN_UNIT).bit_length())))
BUF_ROWS = -(-(TOP_K * COMBINE_TM + N_EXPERTS * (RUN_UNIT - 1 + SUBLANES - 1)) // LANES) * LANES


def _combine_meta(idx_all, slot_all, pad_start):
    n_tok = idx_all.shape[1]
    tiles = n_tok // COMBINE_TM
    experts = jnp.arange(N_EXPERTS, dtype=jnp.int32)
    hit = idx_all[:TOP_K, :, None] == experts
    cnt = jnp.sum(hit.reshape(TOP_K, tiles, COMBINE_TM, N_EXPERTS), axis=(0, 2), dtype=jnp.int32)
    run_start = pad_start[None, :] + jnp.cumsum(cnt, axis=0) - cnt
    lead = run_start % SUBLANES
    units = jnp.where(cnt > 0, (cnt + lead + RUN_UNIT - 1) // RUN_UNIT, 0)
    first_row = RUN_UNIT * (jnp.cumsum(units, axis=1) - units)
    meta = jnp.concatenate([run_start - lead, units, first_row, jnp.zeros_like(cnt)], axis=1).astype(jnp.int32)
    shift = jnp.repeat(first_row + lead - run_start, COMBINE_TM, axis=0)
    col = slot_all[:TOP_K] + jnp.sum(jnp.where(hit, shift[None], 0), axis=2, dtype=jnp.int32)
    return meta.reshape(tiles, 1, 4 * N_EXPERTS), col.T


def _combine_body(meta_ref, next_ref, col_ref, gate_ref, h_ref, nf_ref, ys_ref, y_ref, buf_ref, sem, *, n):
    i = pl.program_id(0)

    def runs(meta, half, start):
        for e in range(N_EXPERTS):
            src0, units, dst0 = meta[0, 0, e], meta[0, 0, N_EXPERTS + e], meta[0, 0, 2 * N_EXPERTS + e]

            def pieces(bits, done):
                for b in bits:
                    rows = b * RUN_UNIT

                    @pl.when((units & b) != 0)
                    def _piece(done=done, rows=rows):
                        dst = pl.multiple_of(dst0 + done, RUN_UNIT)
                        cp = pltpu.make_async_copy(ys_ref.at[pl.ds(pl.multiple_of(src0 + done, SUBLANES), rows)],
                                                   buf_ref.at[half, pl.ds(dst, rows)], sem.at[half])
                        if start:
                            cp.start()
                        else:
                            cp.wait()
                    done = done + (units & b) * RUN_UNIT
                return done
            done = pieces(RUN_BITS[-2:], jnp.int32(0))

            @pl.when(units >= RUN_BITS[-3])
            def _long_run(done=done):
                pieces(RUN_BITS[:-2], done)

    def per_half(half):
        @pl.when(i + 1 < n)
        def _fetch_next():
            runs(next_ref, 1 - half, True)
        runs(meta_ref, half, False)
        rows = lax.broadcasted_iota(jnp.int32, (COMBINE_TM, BUF_ROWS), 1)
        pick = jnp.zeros((COMBINE_TM, BUF_ROWS), F32)
        for k in range(TOP_K):
            pick = jnp.where(rows == col_ref[:, k:k + 1], gate_ref[:, k:k + 1], pick)
        acc = h_ref[...] + _dot(pick.astype(BF16), buf_ref[half].astype(BF16))
        y_ref[...] = _rms(acc, nf_ref[...])

    @pl.when(i == 0)
    def _first():
        buf_ref[...] = jnp.zeros_like(buf_ref)
        runs(meta_ref, 0, True)

    @pl.when(i % 2 == 0)
    def _even():
        per_half(0)

    @pl.when(i % 2 == 1)
    def _odd():
        per_half(1)


def _combine(meta, col_rows, gates_rows, h, norm_final, ys, first_tile):
    n = h.shape[0]
    tm = COMBINE_TM
    assert n % tm == 0
    last = first_tile + n // tm - 1
    smem_tile = lambda f: pl.BlockSpec((1, 1, meta.shape[2]), lambda i: (f(i), 0, 0), memory_space=pltpu.SMEM)
    return pl.pallas_call(
        functools.partial(_combine_body, n=n // tm),
        grid=(n // tm,),
        in_specs=[smem_tile(lambda i: i + first_tile), smem_tile(lambda i: jnp.minimum(i + first_tile + 1, last)),
                  pl.BlockSpec((tm, TOP_K), lambda i: (i + first_tile, 0)),
                  pl.BlockSpec((tm, TOP_K), lambda i: (i, 0)),
                  pl.BlockSpec((tm, D_MODEL), lambda i: (i, 0)),
                  pl.BlockSpec((1, D_MODEL), lambda i: (0, 0)),
                  pl.BlockSpec(memory_space=pl.ANY)],
        out_specs=pl.BlockSpec((tm, D_MODEL), lambda i: (i, 0)),
        out_shape=jax.ShapeDtypeStruct((n, D_MODEL), F32),
        scratch_shapes=[pltpu.VMEM((2, BUF_ROWS, D_MODEL), F32), pltpu.SemaphoreType.DMA((2,))],
        compiler_params=_cparams(("arbitrary",)),
        name="combine",
    )(meta, meta, col_rows, gates_rows, h, norm_final, ys)


def _expansion(width):
    h = np.arange(LANES)[:, None]
    c = np.arange(SSD_HEADS * width)[None, :] // width
    return jnp.asarray((h == c).astype(np.float32), dtype=BF16)


def _pad_lanes(v):
    return jnp.pad(v.astype(F32), (0, LANES - v.shape[0]))[None, :]


def kernel(x_prompt, x_sample, state_conv, state_ssm, cache_win_k, cache_win_v, norm_mix, w_in, conv_w, conv_b,
           dt_bias, a_log, d_skip, ssd_norm, att_norm, w_out, norm_ffn, router_w, router_b, w1, b1, w2, b2,
           norm_final):
    depth = w_in.shape[0]
    assert depth == 1
    bp, seq, _ = x_prompt.shape
    bs, t_new, _ = x_sample.shape
    n_p, n_s = bp * seq, bs * t_new
    l = 0

    o_dt = SSD_INNER + CONV_DIM
    o_q = o_dt + SSD_HEADS
    wl = w_in[l]
    w_cat = jnp.concatenate([wl[:, :o_dt], wl[:, o_q:], jnp.pad(wl[:, o_dt:o_q], ((0, 0), (0, DT_PAD - SSD_HEADS)))],
                            axis=1).astype(BF16)
    g_mix = norm_mix[l][None, :]
    ssd_consts = (conv_w[l], conv_b[l][None, :], _pad_lanes(dt_bias[l]), _pad_lanes(a_log[l]),
                  jnp.repeat(d_skip[l], SSD_HEAD_DIM)[None, :], ssd_norm[l][None, :],
                  _expansion(SSD_HEAD_DIM))
    slopes = jnp.exp2(-8.0 * jnp.arange(1, ATT_HEADS + 1, dtype=F32) / ATT_HEADS)
    out_consts = (att_norm[l][None, :], w_out[l][:SSD_INNER].astype(BF16), w_out[l][SSD_INNER:].astype(BF16),
                  norm_ffn[l][None, :], router_w[l].T.astype(BF16), router_b[l][:, None].astype(F32))
    w1p = _w1_prep(w1[l])
    b1p = b1[l].reshape(N_EXPERTS, -1, LANES, 2).transpose(0, 1, 3, 2).reshape(N_EXPERTS, 1, -1)
    w2b = w2[l].astype(BF16)
    b2r = b2[l][:, None, :]

    zx, q, k, v, dt, kt, vt = _in_proj(x_prompt.reshape(n_p, D_MODEL), g_mix, w_cat, 256, seq)
    y_p, conv_p, ssm_p = _ssd(zx.reshape(bp, seq, ZX_WIDTH), dt.reshape(bp, seq, DT_PAD), ssd_consts, None, BF16)
    o_p = _attn_prompt(q, k, v, slopes, bp, seq)
    keep = min(DILATED_PATTERNS[-1][0], seq)
    k_p = kt.reshape(bp, ATT_HEADS, ATT_HEAD_DIM, seq).transpose(0, 3, 1, 2)[:, seq - keep:]
    v_p = vt.reshape(bp, ATT_HEADS, ATT_HEAD_DIM, seq).transpose(0, 3, 1, 2)[:, seq - keep:]

    zx_s, q_s, k_s, v_s, dt_s = _in_proj(x_sample.reshape(n_s, D_MODEL), g_mix, w_cat, 256)
    init = (state_conv[l], state_ssm[l].reshape(bs, SSD_INNER, SSD_STATE))
    y_s, conv_s, ssm_s = _ssd(zx_s.reshape(bs, t_new, ZX_WIDTH), dt_s.reshape(bs, t_new, DT_PAD), ssd_consts, init, F32)

    def head_major(a):
        a = a.reshape(bs, t_new, ATT_HEADS, ATT_HEAD_DIM).transpose(0, 2, 1, 3)
        return jnp.pad(a, ((0, 0), (0, 0), (0, DEC_ROWS - t_new), (0, 0)))

    def head_major_t(a):
        a = a.reshape(bs, t_new, ATT_HEADS, ATT_HEAD_DIM).transpose(0, 2, 3, 1)
        return jnp.pad(a, ((0, 0), (0, 0), (0, 0), (0, LANES - t_new)))
    o_s = _attn_decode(head_major(q_s), head_major_t(k_s), head_major_t(v_s),
                       cache_win_k[l].transpose(0, 2, 3, 1), cache_win_v[l].transpose(0, 2, 3, 1), t_new)
    o_s = o_s[:, :, :, :t_new].transpose(0, 3, 1, 2)

    cnt0 = jnp.zeros((N_EXPERTS, LANES), F32)
    h_p, xn_p, idx_p, gate_p, rank_p, cnt_p = _out_proj(x_prompt.reshape(n_p, D_MODEL), y_p.reshape(n_p, SSD_INNER),
                                                        o_p, out_consts, cnt0, 512)
    h_s, xn_s, idx_s, gate_s, rank_s, cnt_all = _out_proj(x_sample.reshape(n_s, D_MODEL), y_s.reshape(n_s, SSD_INNER),
                                                          o_s.reshape(n_s, ATT_WIDTH), out_consts, cnt_p, 512)

    counts = cnt_all[:, 0].astype(jnp.int32)
    padded = (counts + MOE_ROWS - 1) // MOE_ROWS * MOE_ROWS
    pad_end = jnp.cumsum(padded)
    pad_start = pad_end - padded
    n_blocks = -(-((n_p + n_s) * TOP_K) // MOE_ROWS) + N_EXPERTS
    blk0 = jnp.arange(n_blocks, dtype=jnp.int32) * MOE_ROWS
    owner = blk0[:, None] >= pad_end[None, :]
    block_expert = jnp.minimum(jnp.sum(owner, axis=1), N_EXPERTS - 1).astype(jnp.int32)
    onehot = block_expert[:, None] == jnp.arange(N_EXPERTS, dtype=jnp.int32)[None, :]
    used = jnp.sum(jnp.where(onehot, (blk0[:, None] - pad_start[None, :]), 0), axis=1)
    block_valid = jnp.clip(jnp.sum(jnp.where(onehot, counts[None, :], 0), axis=1) - used, 0, MOE_ROWS).astype(jnp.int32)

    tm = min(256, n_s)
    idx_all = jnp.concatenate([idx_p, idx_s], axis=1)
    rank_all = jnp.concatenate([rank_p, rank_s], axis=1)
    first_slot = functools.reduce(lambda acc, e: jnp.where(idx_all == e, pad_start[e], acc), range(N_EXPERTS),
                                  jnp.zeros_like(idx_all))
    slot_all = first_slot + rank_all
    xs = _dispatch(slot_all, pad_start + counts, padded - counts, pad_end[N_EXPERTS - 1:] // MOE_ROWS,
                   xn_p, xn_s, n_blocks, tm)
    ys = _experts(block_expert, block_valid, xs, w1p, b1p, w2b, b2r)
    nfin = norm_final[None, :]
    meta, col_rows = _combine_meta(idx_all, slot_all, pad_start)
    y_prompt = _combine(meta, col_rows, gate_p[:TOP_K].T, h_p, nfin, ys, 0)
    y_sample = _combine(meta, col_rows, gate_s[:TOP_K].T, h_s, nfin, ys, n_p // COMBINE_TM)

    return (y_prompt.reshape(bp, seq, D_MODEL), y_sample.reshape(bs, t_new, D_MODEL),
            conv_p[None], ssm_p.reshape(1, bp, SSD_HEADS, SSD_HEAD_DIM, SSD_STATE), k_p[None], v_p[None],
            conv_s[None], ssm_s.reshape(1, bs, SSD_HEADS, SSD_HEAD_DIM, SSD_STATE),
            k_s.reshape(1, bs, t_new, ATT_HEADS, ATT_HEAD_DIM), v_s.reshape(1, bs, t_new, ATT_HEADS, ATT_HEAD_DIM))
```

```python
import functools

import jax
import jax.numpy as jnp
import numpy as np
from jax import lax
from jax.experimental import pallas as pl
from jax.experimental.pallas import tpu as pltpu

F32 = jnp.float32
BF16 = jnp.bfloat16

D_MODEL = 1024
SSD_HEADS = 16
SSD_HEAD_DIM = 64
SSD_INNER = SSD_HEADS * SSD_HEAD_DIM
SSD_GROUPS = 2
SSD_STATE = 128
SSD_CONV = 4
SSD_CHUNK = 128
CONV_DIM = SSD_INNER + 2 * SSD_GROUPS * SSD_STATE
ATT_HEADS = 8
ATT_HEAD_DIM = 64
ATT_WIDTH = ATT_HEADS * ATT_HEAD_DIM
DILATED_PATTERNS = ((128, 1), (512, 4), (2048, 16))
ATT_BLOCK = 128
N_EXPERTS = 32
TOP_K = 4
SWIGLU_LIMIT = 7.0
SWIGLU_ALPHA = 1.702
NORM_EPS = 1e-5

LANES = 128
SUBLANES = 8
ZX_WIDTH = SSD_INNER + CONV_DIM
DT_PAD = LANES
CONV_HIST = SUBLANES
MOE_ROWS = 512
VMEM_LIMIT = 56 * 1024 * 1024

NEG_INF = float("-inf")


def _cparams(sem):
    return pltpu.CompilerParams(dimension_semantics=sem, vmem_limit_bytes=VMEM_LIMIT)


def _rms(x, g):
    return x * lax.rsqrt(jnp.mean(x * x, axis=-1, keepdims=True) + NORM_EPS) * g


def _dot(a, b):
    return jnp.dot(a, b, preferred_element_type=F32)


def _dot_nt(a, b):
    return lax.dot_general(a, b, (((1,), (1,)), ((), ())), preferred_element_type=F32)


def _split3(v):
    hi = v.astype(BF16)
    r1 = v - hi.astype(F32)
    mid = r1.astype(BF16)
    lo = (r1 - mid.astype(F32)).astype(BF16)
    return hi, mid, lo


def _dot3(v, m):
    hi, mid, lo = _split3(v)
    return _dot(hi, m) + _dot(mid, m) + _dot(lo, m)


def _dot3_lhs(m, v):
    hi, mid, lo = _split3(v)
    return _dot(m, hi) + _dot(m, mid) + _dot(m, lo)


def _silu(x):
    half = 0.5 * x
    return half * (1.0 + jnp.tanh(half))


def _in_proj_body(x_ref, g_ref, w_ref, zx_ref, q_ref, k_ref, v_ref, dt_ref, *t_refs):
    hn = _rms(x_ref[...], g_ref[...]).astype(BF16)
    o = 0
    for ref, width in ((zx_ref, ZX_WIDTH), (q_ref, ATT_WIDTH), (k_ref, ATT_WIDTH), (v_ref, ATT_WIDTH),
                       (dt_ref, DT_PAD)):
        ref[...] = _dot(hn, w_ref[:, o:o + width])
        o += width
    for src, dst in zip((k_ref, v_ref), t_refs):
        for j in range(ATT_WIDTH // LANES):
            dst[0, j * LANES:(j + 1) * LANES, :] = src[:, j * LANES:(j + 1) * LANES].T


def _in_proj(x2d, g, w_cat, tm, seq=None):
    n = x2d.shape[0]
    tm = min(tm, n)
    per = 1 if seq is None else seq // tm
    row = lambda w: pl.BlockSpec((tm, w), lambda b, i: (b * per + i, 0))
    full = lambda a: pl.BlockSpec(a.shape, lambda b, i: (0,) * a.ndim)
    widths = (ZX_WIDTH, ATT_WIDTH, ATT_WIDTH, ATT_WIDTH, DT_PAD)
    out_specs = [row(w) for w in widths]
    out_shape = [jax.ShapeDtypeStruct((n, w), F32) for w in widths]
    if seq is not None:
        out_specs += [pl.BlockSpec((1, ATT_WIDTH, tm), lambda b, i: (b, 0, i))] * 2
        out_shape += [jax.ShapeDtypeStruct((n // seq, ATT_WIDTH, seq), F32)] * 2
    return pl.pallas_call(
        _in_proj_body,
        grid=(n // (tm * per), per),
        in_specs=[row(D_MODEL), full(g), full(w_cat)],
        out_specs=out_specs,
        out_shape=out_shape,
        compiler_params=_cparams(("parallel", "parallel")),
        name="in_proj",
    )(x2d, g, w_cat)


def _ssd_body(*refs, l_blk, has_init):
    T = SSD_CHUNK
    if has_init:
        (zx_ref, dt_ref, cw_ref, cb_ref, dtb_ref, alog_ref, dskip_ref, norm_ref, e64_ref,
         cinit_ref, sinit_ref, y_ref, conv_out_ref, ssm_out_ref,
         ext_ref, act_ref, state_ref, cst_ref, ybuf_ref, zpad_ref, dtpad_ref) = refs
    else:
        (zx_ref, dt_ref, cw_ref, cb_ref, dtb_ref, alog_ref, dskip_ref, norm_ref, e64_ref,
         y_ref, conv_out_ref, ssm_out_ref,
         ext_ref, act_ref, state_ref, cst_ref, ybuf_ref) = refs
    c = pl.program_id(1)
    n_tile = SSD_INNER // LANES

    @pl.when(c == 0)
    def _start():
        if has_init:
            ext_ref[0:CONV_HIST, :] = jnp.zeros((CONV_HIST, CONV_DIM), F32)
            ext_ref[CONV_HIST - (SSD_CONV - 1):CONV_HIST, :] = cinit_ref[0]
            for j in range(n_tile):
                state_ref[:, j * LANES:(j + 1) * LANES] = sinit_ref[0, j * LANES:(j + 1) * LANES, :].T
        else:
            ext_ref[0:CONV_HIST, :] = jnp.zeros((CONV_HIST, CONV_DIM), F32)
            state_ref[...] = jnp.zeros_like(state_ref)

    if l_blk == T:
        ext_ref[CONV_HIST:CONV_HIST + T, :] = zx_ref[0, :, SSD_INNER:ZX_WIDTH]
        z_of = lambda sl: zx_ref[0, :, sl]
        dt_raw = dt_ref[0]
    else:
        ext_ref[CONV_HIST:CONV_HIST + T, :] = jnp.zeros((T, CONV_DIM), F32)
        ext_ref[CONV_HIST:CONV_HIST + l_blk, :] = zx_ref[0, :, SSD_INNER:ZX_WIDTH]
        zpad_ref[...] = jnp.zeros_like(zpad_ref)
        zpad_ref[0:l_blk, :] = zx_ref[0, :, 0:SSD_INNER]
        dtpad_ref[...] = jnp.zeros_like(dtpad_ref)
        dtpad_ref[0:l_blk, :] = dt_ref[0]
        z_of = lambda sl: zpad_ref[:, sl]
        dt_raw = dtpad_ref[...]

    cw = CONV_DIM // 3
    for cc in range(3):
        sl = slice(cc * cw, (cc + 1) * cw)
        acc = cb_ref[:, sl]
        for j in range(SSD_CONV):
            o = CONV_HIST - (SSD_CONV - 1) + j
            acc = acc + ext_ref[o:o + T, sl] * cw_ref[j:j + 1, sl]
        act_ref[:, sl] = _silu(acc)

    row = lax.broadcasted_iota(jnp.int32, (T, LANES), 0)
    col = lax.broadcasted_iota(jnp.int32, (T, LANES), 1)
    tri = row >= col
    tri_bf = jnp.where(tri, 1.0, 0.0).astype(BF16)
    even = col < SSD_HEAD_DIM

    xdt = dt_raw + dtb_ref[...]
    dtv = jnp.maximum(xdt, 0.0) + jnp.log1p(jnp.exp(-jnp.abs(xdt)))
    if l_blk < T:
        dtv = jnp.where(row < l_blk, dtv, 0.0)
    d_a = dtv * (-jnp.exp(alog_ref[...]))
    cs = _dot3_lhs(tri_bf, d_a)
    cst_ref[...] = cs.T
    ex = _dot3(jnp.concatenate([dtv, cs], axis=0), e64_ref[...])
    dt_ex, cs_ex = ex[0:T], ex[T:2 * T]
    ecs_ex = jnp.exp(cs_ex)
    dd_ex = dt_ex * jnp.exp(cs_ex[T - 1:T] - cs_ex)

    gw = SSD_INNER // SSD_GROUPS
    heads_per_group = SSD_HEADS // SSD_GROUPS
    for g in range(SSD_GROUPS):
        gsl = slice(g * gw, (g + 1) * gw)
        b_g = act_ref[:, SSD_INNER + g * SSD_STATE:SSD_INNER + (g + 1) * SSD_STATE]
        c_off = SSD_INNER + SSD_GROUPS * SSD_STATE
        c_g = act_ref[:, c_off + g * SSD_STATE:c_off + (g + 1) * SSD_STATE].astype(BF16)
        cb = _dot_nt(c_g, b_g.astype(BF16))
        b_gt = b_g.T.astype(BF16)
        x_g = act_ref[:, gsl]
        x_dt = (x_g * dt_ex[:, gsl]).astype(BF16)
        x_dd = (x_g * dd_ex[:, gsl]).astype(BF16)
        st_old = state_ref[:, gsl]
        y_off = _dot(c_g, st_old.astype(BF16)) * ecs_ex[:, gsl]
        state_ref[:, gsl] = st_old * ecs_ex[T - 1:T, gsl] + _dot(b_gt, x_dd)
        for jp in range(heads_per_group // 2):
            h0 = g * heads_per_group + 2 * jp
            psl = slice(jp * LANES, (jp + 1) * LANES)
            osl = slice(g * gw + jp * LANES, g * gw + (jp + 1) * LANES)
            pair = cs_ex[:, osl]
            swapped = pltpu.roll(pair, SSD_HEAD_DIM, axis=1)
            cols = (jnp.where(even, pair, swapped), jnp.where(even, swapped, pair))
            yd = []
            for col, h in zip(cols, (h0, h0 + 1)):
                seg = col - cst_ref[h:h + 1, :]
                lmat = jnp.exp(jnp.where(tri, seg, NEG_INF))
                yd.append(_dot((cb * lmat).astype(BF16), x_dt[:, psl]))
            y_pair = jnp.where(even, yd[0], yd[1]) + y_off[:, psl]
            ybuf_ref[:, osl] = y_pair + dskip_ref[:, osl] * x_g[:, psl]

    for g in range(SSD_GROUPS):
        gsl = slice(g * gw, (g + 1) * gw)
        yg = ybuf_ref[:, gsl] * _silu(z_of(gsl))
        yn = yg * lax.rsqrt(jnp.mean(yg * yg, axis=-1, keepdims=True) + NORM_EPS) * norm_ref[:, gsl]
        y_ref[0, :, gsl] = yn[0:l_blk].astype(y_ref.dtype)

    @pl.when(c == pl.num_programs(1) - 1)
    def _finish():
        lo = CONV_HIST + l_blk - (SSD_CONV - 1)
        conv_out_ref[0] = ext_ref[lo:lo + SSD_CONV - 1, :]
        for j in range(n_tile):
            ssm_out_ref[0, j * LANES:(j + 1) * LANES, :] = state_ref[:, j * LANES:(j + 1) * LANES].T

    ext_ref[0:CONV_HIST, :] = ext_ref[T:T + CONV_HIST, :]


def _ssd(zx3, dt3, consts, init, y_dtype):
    b, L, _ = zx3.shape
    T = SSD_CHUNK
    l_blk = T if L % T == 0 else L
    n_chunks = L // l_blk
    has_init = init is not None
    full = lambda a: pl.BlockSpec(a.shape, lambda i, c: (0,) * a.ndim)
    in_specs = [pl.BlockSpec((1, l_blk, ZX_WIDTH), lambda i, c: (i, c, 0)),
                pl.BlockSpec((1, l_blk, DT_PAD), lambda i, c: (i, c, 0))] + [full(a) for a in consts]
    args = [zx3, dt3, *consts]
    scratch = [pltpu.VMEM((T + CONV_HIST, CONV_DIM), F32),
               pltpu.VMEM((T, CONV_DIM), F32),
               pltpu.VMEM((SSD_STATE, SSD_INNER), F32),
               pltpu.VMEM((LANES, T), F32),
               pltpu.VMEM((T, SSD_INNER), F32)]
    if has_init:
        in_specs += [pl.BlockSpec((1, SSD_CONV - 1, CONV_DIM), lambda i, c: (i, 0, 0)),
                     pl.BlockSpec((1, SSD_INNER, SSD_STATE), lambda i, c: (i, 0, 0))]
        args += list(init)
        scratch += [pltpu.VMEM((T, SSD_INNER), F32), pltpu.VMEM((T, DT_PAD), F32)]
    return pl.pallas_call(
        functools.partial(_ssd_body, l_blk=l_blk, has_init=has_init),
        grid=(b, n_chunks),
        in_specs=in_specs,
        out_specs=[pl.BlockSpec((1, l_blk, SSD_INNER), lambda i, c: (i, c, 0)),
                   pl.BlockSpec((1, SSD_CONV - 1, CONV_DIM), lambda i, c: (i, 0, 0)),
                   pl.BlockSpec((1, SSD_INNER, SSD_STATE), lambda i, c: (i, 0, 0))],
        out_shape=[jax.ShapeDtypeStruct((b, L, SSD_INNER), y_dtype),
                   jax.ShapeDtypeStruct((b, SSD_CONV - 1, CONV_DIM), F32),
                   jax.ShapeDtypeStruct((b, SSD_INNER, SSD_STATE), F32)],
        scratch_shapes=scratch,
        compiler_params=_cparams(("parallel", "arbitrary")),
        name="ssd_init" if has_init else "ssd",
    )(*args)


ATT_UNROLL = 16


def _unroll(n):
    return max(d for d in range(1, ATT_UNROLL + 1) if n % d == 0)


ATT_RES = DILATED_PATTERNS[-1][1]


def _attn_tables():
    B = ATT_BLOCK
    t1 = np.full((len(DILATED_PATTERNS), B, B), -np.inf, np.float32)
    t2 = np.full((len(DILATED_PATTERNS), B, 2 * B), -np.inf, np.float32)
    for p, (window, dil) in enumerate(DILATED_PATTERNS):
        m = ATT_RES // dil
        w = B // m
        rho = np.arange(B)
        c = m * (rho % w) + rho // w
        ck2 = np.concatenate([c - B, c])
        for tab, ck in ((t1, c), (t2, ck2)):
            delta = c[:, None] - ck[None, :]
            ok = (delta >= 0) & (delta <= window // dil)
            tab[p] = np.where(ok, -(dil * delta).astype(np.float32), -np.inf)
    return jnp.asarray(t1), jnp.asarray(t2)


def _attn_body(slopes_ref, t1_ref, t2_ref, q_ref, k_ref, v_ref, o_ref, qd_ref, kd_ref, vd_ref, op_ref, lp_ref,
               stage_ref, b1_ref, b2_ref, in_sem, out_sem, *, seq):
    B = ATT_BLOCK
    hp = pl.program_id(1)
    even = lax.broadcasted_iota(jnp.int32, (B, LANES), 1) < ATT_HEAD_DIM
    scale = ATT_HEAD_DIM ** -0.5

    def in_copy(a, r):
        src, dst = ((q_ref, qd_ref), (k_ref, kd_ref), (v_ref, vd_ref))[a]
        return pltpu.make_async_copy(src.at[:, r, :], dst.at[r], in_sem.at[a, r])
    for r in range(ATT_RES):
        for a in range(3):
            in_copy(a, r).start()

    for e in range(2):
        slope = slopes_ref[2 * hp + e]
        for p in range(len(DILATED_PATTERNS)):
            b1_ref[e * len(DILATED_PATTERNS) + p] = t1_ref[p] * slope
            b2_ref[e * len(DILATED_PATTERNS) + p] = t2_ref[p] * slope

    for r in range(ATT_RES):
        for a in range(3):
            in_copy(a, r).wait()

    def block(p, r_d, n, first):
        dil = DILATED_PATTERNS[p][1]
        m = ATT_RES // dil
        w = B // m

        def slab(j, nblk):
            start = nblk * w
            return r_d + dil * j, pl.ds(start if isinstance(start, int) else pl.multiple_of(start, w), w)

        def gather(src, nblk):
            parts = []
            for j in range(m):
                r, rows = slab(j, nblk)
                parts.append(src[r, rows, :])
            return parts[0] if m == 1 else jnp.concatenate(parts, axis=0)

        qb = gather(qd_ref, n) * scale
        if first:
            kb, vb = gather(kd_ref, n), gather(vd_ref, n)
            bias = lambda e: b1_ref[e * len(DILATED_PATTERNS) + p]
        else:
            kb = jnp.concatenate([gather(kd_ref, n - 1), gather(kd_ref, n)], axis=0)
            vb = jnp.concatenate([gather(vd_ref, n - 1), gather(vd_ref, n)], axis=0)
            bias = lambda e: b2_ref[e * len(DILATED_PATTERNS) + p]
        q2 = jnp.concatenate([jnp.where(even, qb, 0.0), jnp.where(even, 0.0, qb)], axis=0).astype(BF16)
        s = _dot_nt(q2, kb.astype(BF16)) + jnp.concatenate([bias(0), bias(1)], axis=0)
        mx = jnp.max(s, axis=-1, keepdims=True)
        pr = jnp.exp(s - mx)
        l = jnp.sum(pr, axis=-1, keepdims=True)
        o2 = _dot(pr.astype(BF16), vb.astype(BF16)) / l
        lse2 = jnp.broadcast_to(mx + jnp.log(l), (2 * B, LANES))
        o = jnp.where(even, o2[:B], o2[B:])
        lse = jnp.where(even, lse2[:B], lse2[B:])
        for j in range(m):
            r, rows = slab(j, n)
            op_ref[p, r, rows, :] = o[j * w:(j + 1) * w]
            lp_ref[p, r, rows, :] = lse[j * w:(j + 1) * w]

    for p, (_, dil) in enumerate(DILATED_PATTERNS):
        nb = seq // (dil * B)
        u_first = _unroll(dil)

        def first(i, carry, p=p, u_first=u_first):
            for u in range(u_first):
                block(p, i * u_first + u, 0, True)
            return carry
        lax.fori_loop(0, dil // u_first, first, 0)
        n_later = dil * (nb - 1)
        if n_later:
            u_later = _unroll(n_later)

            def later(i, carry, p=p, nb=nb, u_later=u_later):
                for u in range(u_later):
                    j = i * u_later + u
                    block(p, j // (nb - 1), j % (nb - 1) + 1, False)
                return carry
            lax.fori_loop(0, n_later // u_later, later, 0)

    def out_copy(r):
        return pltpu.make_async_copy(stage_ref.at[r], o_ref.at[:, r, :], out_sem.at[r])
    for r in range(ATT_RES):
        l0, l1, l2 = lp_ref[0, r], lp_ref[1, r], lp_ref[2, r]
        m = jnp.maximum(jnp.maximum(l0, l1), l2)
        w0, w1, w2 = jnp.exp(l0 - m), jnp.exp(l1 - m), jnp.exp(l2 - m)
        num = w0 * op_ref[0, r] + w1 * op_ref[1, r] + w2 * op_ref[2, r]
        stage_ref[r] = num / (w0 + w1 + w2)
        out_copy(r).start()
    for r in range(ATT_RES):
        out_copy(r).wait()


def _attn_prompt(q, k, v, slopes, batch, seq):
    n_hp = ATT_HEADS // 2
    n_pat = len(DILATED_PATTERNS)
    assert len(DILATED_PATTERNS) == 3 and all(ATT_RES % d == 0 and seq % (d * ATT_BLOCK) == 0 and w // d == ATT_BLOCK
                                              for w, d in DILATED_PATTERNS)
    per = seq // ATT_RES
    by_res = lambda a: a.reshape(batch * per, ATT_RES, ATT_WIDTH)
    blk = pl.BlockSpec((per, ATT_RES, LANES), lambda b, h: (b, 0, h))
    t1, t2 = _attn_tables()
    full = lambda a: pl.BlockSpec(a.shape, lambda b, h: (0,) * a.ndim)
    res = pltpu.VMEM((ATT_RES, per, LANES), F32)
    out = pl.pallas_call(
        functools.partial(_attn_body, seq=seq),
        grid=(batch, n_hp),
        in_specs=[pl.BlockSpec(memory_space=pltpu.SMEM), full(t1), full(t2), blk, blk, blk],
        out_specs=blk,
        out_shape=jax.ShapeDtypeStruct((batch * per, ATT_RES, ATT_WIDTH), F32),
        scratch_shapes=[res, res, res,
                        pltpu.VMEM((n_pat, ATT_RES, per, LANES), F32), pltpu.VMEM((n_pat, ATT_RES, per, LANES), F32),
                        res,
                        pltpu.VMEM((2 * n_pat, ATT_BLOCK, ATT_BLOCK), F32),
                        pltpu.VMEM((2 * n_pat, ATT_BLOCK, 2 * ATT_BLOCK), F32),
                        pltpu.SemaphoreType.DMA((3, ATT_RES)), pltpu.SemaphoreType.DMA((ATT_RES,))],
        compiler_params=_cparams(("parallel", "parallel")),
        name="attn_prompt",
    )(slopes, t1, t2, by_res(q), by_res(k), by_res(v))
    return out.reshape(batch * seq, ATT_WIDTH)


DEC_ROWS = SUBLANES


def _decode_tables(t_new, m_cache):
    slopes = 2.0 ** (-8.0 * np.arange(1, ATT_HEADS + 1) / ATT_HEADS)

    def mult(dist):
        return sum(1 for window, dil in DILATED_PATTERNS if 0 <= dist <= window and dist % dil == 0)

    bias_c = np.zeros((ATT_HEADS, DEC_ROWS, m_cache), np.float32)
    mult_c = np.ones((DEC_ROWS, m_cache), np.float32)
    bias_n = np.full((ATT_HEADS, DEC_ROWS, LANES), -np.inf, np.float32)
    mult_n = np.zeros((DEC_ROWS, LANES), np.float32)
    for t in range(t_new):
        dist = m_cache + t - np.arange(m_cache)
        mu = np.array([mult(d) for d in dist], np.float32)
        mult_c[t] = mu
        bias_c[:, t, :] = np.where(mu > 0, -slopes[:, None] * dist[None, :], -np.inf)
        for t2 in range(t_new):
            if mult(t - t2):
                bias_n[:, t, t2] = -slopes * (t - t2)
                mult_n[t, t2] = mult(t - t2)
    bias_n[:, t_new:, 0] = 0.0
    mult_n[t_new:, 0] = 1.0
    return jnp.asarray(bias_c), jnp.asarray(mult_c), jnp.asarray(bias_n), jnp.asarray(mult_n)


def _attn_dec_body(q_ref, knt_ref, vnt_ref, ck_ref, cv_ref, bc_ref, mc_ref, bn_ref, mn_ref, o_ref):
    scale = ATT_HEAD_DIM ** -0.5
    for h in range(ATT_HEADS):
        qh = (q_ref[0, h] * scale).astype(BF16)
        sc = _dot(qh, ck_ref[0, h].astype(BF16)) + bc_ref[h]
        sn = _dot(qh, knt_ref[0, h].astype(BF16)) + bn_ref[h]
        m = jnp.maximum(jnp.max(sc, axis=-1, keepdims=True), jnp.max(sn, axis=-1, keepdims=True))
        pc = jnp.exp(sc - m) * mc_ref[...]
        pn = jnp.exp(sn - m) * mn_ref[...]
        inv_l = 1.0 / (jnp.sum(pc, axis=-1, keepdims=True) + jnp.sum(pn, axis=-1, keepdims=True))
        o_ref[0, h] = (_dot_nt(cv_ref[0, h].astype(BF16), (pc * inv_l).astype(BF16))
                       + _dot_nt(vnt_ref[0, h].astype(BF16), (pn * inv_l).astype(BF16)))


def _attn_decode(q4, knt, vnt, cache_kt, cache_vt, t_new):
    b = q4.shape[0]
    m_cache = cache_kt.shape[3]
    assert t_new <= DEC_ROWS
    tables = _decode_tables(t_new, m_cache)
    per_b = lambda a: pl.BlockSpec((1,) + a.shape[1:], lambda i: (i, 0, 0, 0))
    full = lambda a: pl.BlockSpec(a.shape, lambda i: (0,) * a.ndim)
    out_shape = jax.ShapeDtypeStruct((b, ATT_HEADS, ATT_HEAD_DIM, DEC_ROWS), F32)
    return pl.pallas_call(
        _attn_dec_body,
        grid=(b,),
        in_specs=[per_b(a) for a in (q4, knt, vnt, cache_kt, cache_vt)] + [full(a) for a in tables],
        out_specs=per_b(out_shape),
        out_shape=out_shape,
        compiler_params=_cparams(("parallel",)),
        name="attn_decode",
    )(q4, knt, vnt, cache_kt, cache_vt, *tables)


def _out_proj_body(x_ref, y_ref, o_ref, an_ref, wy_ref, wo_ref, nf_ref, rw_ref, rb_ref, cnt0_ref,
                   h_ref, xn_ref, idx_ref, gate_ref, rank_ref, cnt_ref, carry_ref, *, tm):
    i = pl.program_id(0)

    @pl.when(i == 0)
    def _start():
        carry_ref[...] = cnt0_ref[...]

    on = _rms(o_ref[...], an_ref[...]).astype(BF16)
    mixed = _dot(y_ref[...].astype(BF16), wy_ref[...]) + _dot(on, wo_ref[...])
    h = x_ref[...] + mixed
    h_ref[...] = h
    xn = _rms(h, nf_ref[...])
    xn_ref[...] = xn
    logits = _dot_nt(rw_ref[...], xn.astype(BF16)) + rb_ref[...]

    e_iota = lax.broadcasted_iota(jnp.int32, (N_EXPERTS, tm), 0)
    vals, idxs, sels = [], [], []
    cur = logits
    for _ in range(TOP_K):
        mx = jnp.max(cur, axis=0, keepdims=True)
        ix = jnp.min(jnp.where(cur == mx, e_iota, N_EXPERTS), axis=0, keepdims=True)
        sel = e_iota == ix
        cur = jnp.where(sel, NEG_INF, cur)
        vals.append(mx)
        idxs.append(ix)
        sels.append(sel)
    ex = [jnp.exp(v - vals[0]) for v in vals]
    den = functools.reduce(lambda a, b: a + b, ex)

    sel_any = functools.reduce(jnp.logical_or, sels)
    sel_f = jnp.where(sel_any, 1.0, 0.0)
    r = lax.broadcasted_iota(jnp.int32, (tm, tm), 0)
    c = lax.broadcasted_iota(jnp.int32, (tm, tm), 1)
    upper = jnp.where(r <= c, 1.0, 0.0).astype(BF16)
    cum = _dot(sel_f.astype(BF16), upper)
    before = carry_ref[:, 0:1] + cum - sel_f
    for k in range(TOP_K):
        idx_ref[k:k + 1, :] = idxs[k]
        gate_ref[k:k + 1, :] = ex[k] / den
        rank_ref[k:k + 1, :] = jnp.sum(jnp.where(sels[k], before, 0.0), axis=0, keepdims=True).astype(jnp.int32)
    idx_ref[TOP_K:SUBLANES, :] = jnp.zeros((SUBLANES - TOP_K, tm), jnp.int32)
    gate_ref[TOP_K:SUBLANES, :] = jnp.zeros((SUBLANES - TOP_K, tm), F32)
    rank_ref[TOP_K:SUBLANES, :] = jnp.zeros((SUBLANES - TOP_K, tm), jnp.int32)
    carry_ref[...] = carry_ref[...] + jnp.max(cum, axis=1, keepdims=True)
    cnt_ref[...] = carry_ref[...]


def _out_proj(x2d, y2d, o2d, consts, cnt0, tm):
    n = x2d.shape[0]
    tm = min(tm, n)
    row = lambda w: pl.BlockSpec((tm, w), lambda i: (i, 0))
    colb = pl.BlockSpec((SUBLANES, tm), lambda i: (0, i))
    full = lambda a: pl.BlockSpec(a.shape, lambda i: (0,) * a.ndim)
    return pl.pallas_call(
        functools.partial(_out_proj_body, tm=tm),
        grid=(n // tm,),
        in_specs=[row(D_MODEL), row(SSD_INNER), row(ATT_WIDTH)] + [full(a) for a in consts] + [full(cnt0)],
        out_specs=[row(D_MODEL), row(D_MODEL), colb, colb, colb, full(cnt0)],
        out_shape=[jax.ShapeDtypeStruct((n, D_MODEL), F32), jax.ShapeDtypeStruct((n, D_MODEL), F32),
                   jax.ShapeDtypeStruct((SUBLANES, n), jnp.int32), jax.ShapeDtypeStruct((SUBLANES, n), F32),
                   jax.ShapeDtypeStruct((SUBLANES, n), jnp.int32), jax.ShapeDtypeStruct(cnt0.shape, F32)],
        scratch_shapes=[pltpu.VMEM(cnt0.shape, F32)],
        compiler_params=_cparams(("arbitrary",)),
        name="out_proj",
    )(x2d, y2d, o2d, *consts, cnt0)


ISSUE_UNROLL = 8


def _dispatch_body(slot_ref, from_ref, n_ref, tail_ref, xa_ref, xb_ref, xs_ref, zero_ref, stage_ref, sem, row_sems,
                   *, tm, tiles_a, tiles_b, n_blocks):
    i = pl.program_id(0)

    @pl.when(i == 0)
    def _pad_fill():
        zero_ref[...] = jnp.zeros_like(zero_ref)
        one_row = lambda slot: pltpu.make_async_copy(zero_ref.at[pl.ds(0, 1)], xs_ref.at[pl.ds(slot, 1)], sem)
        block = lambda j: pltpu.make_async_copy(zero_ref, xs_ref.at[pl.ds(j * MOE_ROWS, MOE_ROWS)], sem)

        def per_expert(e, carry):
            def start(r, c):
                one_row(from_ref[e] + r).start()
                return c

            def wait(r, c):
                one_row(0).wait()
                return c
            lax.fori_loop(0, n_ref[e], start, 0)
            lax.fori_loop(0, n_ref[e], wait, 0)
            return carry
        lax.fori_loop(0, N_EXPERTS, per_expert, 0)

        def tail(j, carry):
            block(j).start()
            block(j).wait()
            return carry
        lax.fori_loop(tail_ref[0], n_blocks, tail, 0)

    def drain(parity):
        for _ in range(TOP_K):
            pltpu.make_async_copy(stage_ref.at[parity], xs_ref.at[pl.ds(0, tm)], row_sems.at[parity]).wait()

    for parity in range(2):
        @pl.when(i % 2 == parity)
        def _step(parity=parity):
            @pl.when(i < tiles_a)
            def _first_group():
                stage_ref[parity] = xa_ref[...]

            @pl.when(i >= tiles_a)
            def _second_group():
                stage_ref[parity] = xb_ref[...]

            def issue(j, carry):
                t0 = pl.multiple_of(j * ISSUE_UNROLL, ISSUE_UNROLL)
                for u in range(ISSUE_UNROLL):
                    for k in range(TOP_K):
                        slot = slot_ref[t0 * TOP_K + (u * TOP_K + k)]
                        pltpu.make_async_copy(stage_ref.at[parity, pl.ds(t0 + u, 1)],
                                              xs_ref.at[pl.ds(slot, 1)], row_sems.at[parity]).start()
                return carry
            lax.fori_loop(0, tm // ISSUE_UNROLL, issue, 0)

            @pl.when(i > 0)
            def _previous():
                drain(1 - parity)

            @pl.when(i == tiles_a + tiles_b - 1)
            def _last():
                drain(parity)


def _dispatch(slot, pad_from, pad_n, tail_block, xa, xb, n_blocks, tm):
    tiles_a, tiles_b = xa.shape[0] // tm, xb.shape[0] // tm
    assert tiles_a * tm == xa.shape[0] and tiles_b * tm == xb.shape[0]
    smem = pl.BlockSpec(memory_space=pltpu.SMEM)
    hbm = pl.BlockSpec(memory_space=pl.ANY)
    return pl.pallas_call(
        functools.partial(_dispatch_body, tm=tm, tiles_a=tiles_a, tiles_b=tiles_b, n_blocks=n_blocks),
        grid=(tiles_a + tiles_b,),
        in_specs=[pl.BlockSpec((tm * TOP_K,), lambda i: (i,), memory_space=pltpu.SMEM), smem, smem, smem,
                  pl.BlockSpec((tm, D_MODEL), lambda i: (jnp.minimum(i, tiles_a - 1), 0)),
                  pl.BlockSpec((tm, D_MODEL), lambda i: (jnp.maximum(i - tiles_a, 0), 0))],
        out_specs=hbm,
        out_shape=jax.ShapeDtypeStruct((n_blocks * MOE_ROWS, D_MODEL), F32),
        scratch_shapes=[pltpu.VMEM((MOE_ROWS, D_MODEL), F32), pltpu.VMEM((2, tm, D_MODEL), F32),
                        pltpu.SemaphoreType.DMA(()), pltpu.SemaphoreType.DMA((2,))],
        compiler_params=_cparams(("arbitrary",)),
        name="dispatch",
    )(slot, pad_from, pad_n, tail_block, xa, xb)


MOE_COLS = 2 * LANES


W1_PREP_COLS = 4 * MOE_COLS


def _w1_prep_body(w_ref, p_ref, o_ref):
    for c in range(W1_PREP_COLS // MOE_COLS):
        cols = slice(c * MOE_COLS, (c + 1) * MOE_COLS)
        o_ref[0, :, cols] = _dot(w_ref[0, :, cols].astype(BF16), p_ref[...]).astype(BF16)


def _w1_prep(w1):
    e, d, n2 = w1.shape
    src = np.concatenate([np.arange(0, MOE_COLS, 2), np.arange(1, MOE_COLS, 2)])
    perm = np.zeros((MOE_COLS, MOE_COLS), np.float32)
    perm[src, np.arange(MOE_COLS)] = 1.0
    blk = pl.BlockSpec((1, d, W1_PREP_COLS), lambda i, j: (i, 0, j))
    return pl.pallas_call(
        _w1_prep_body,
        grid=(e, n2 // W1_PREP_COLS),
        in_specs=[blk, pl.BlockSpec((MOE_COLS, MOE_COLS), lambda i, j: (0, 0))],
        out_specs=blk,
        out_shape=jax.ShapeDtypeStruct(w1.shape, BF16),
        compiler_params=_cparams(("parallel", "parallel")),
        name="w1_prep",
    )(w1, jnp.asarray(perm, dtype=BF16))


def _experts_body(be_ref, bv_ref, xs_ref, w1_ref, b1_ref, w2_ref, b2_ref, ys_ref, g_ref):
    j = pl.program_id(0)
    valid = bv_ref[j]

    @pl.when(valid > 0)
    def _compute():
        x = xs_ref[...].astype(BF16)
        for c in range(g_ref.shape[1] // LANES):
            cols = slice(c * MOE_COLS, (c + 1) * MOE_COLS)
            hc = _dot(x, w1_ref[0, :, cols]) + b1_ref[0, :, cols]
            glu = jnp.minimum(hc[:, :LANES], SWIGLU_LIMIT)
            lin = jnp.clip(hc[:, LANES:], -SWIGLU_LIMIT, SWIGLU_LIMIT)
            g_ref[:, c * LANES:(c + 1) * LANES] = (glu * jax.nn.sigmoid(SWIGLU_ALPHA * glu) * (lin + 1.0)).astype(BF16)
        ys_ref[...] = _dot(g_ref[...], w2_ref[0]) + b2_ref[0]

    @pl.when(valid <= 0)
    def _empty():
        ys_ref[...] = jnp.zeros_like(ys_ref)


def _experts(block_expert, block_valid, xs, w1p, b1p, w2, b2):
    n_blocks = xs.shape[0] // MOE_ROWS
    wspec = lambda a: pl.BlockSpec((1,) + a.shape[1:], lambda j, be, bv: (be[j], 0, 0))
    grid_spec = pltpu.PrefetchScalarGridSpec(
        num_scalar_prefetch=2,
        grid=(n_blocks,),
        in_specs=[pl.BlockSpec((MOE_ROWS, D_MODEL), lambda j, be, bv: (j, 0)),
                  wspec(w1p), wspec(b1p), wspec(w2), wspec(b2)],
        out_specs=pl.BlockSpec((MOE_ROWS, D_MODEL), lambda j, be, bv: (j, 0)),
        scratch_shapes=[pltpu.VMEM((MOE_ROWS, w2.shape[1]), BF16)],
    )
    return pl.pallas_call(
        _experts_body,
        grid_spec=grid_spec,
        out_shape=jax.ShapeDtypeStruct(xs.shape, F32),
        compiler_params=_cparams(("arbitrary",)),
        name="experts",
    )(block_expert, block_valid, xs, w1p, b1p, w2, b2)


COMBINE_TM = 256
RUN_UNIT = 16
RUN_BITS = tuple(1 << b for b in reversed(range((COMBINE_TM // RUN_UNIT).bit_length())))
BUF_ROWS = -(-(TOP_K * COMBINE_TM + N_EXPERTS * (RUN_UNIT - 1 + SUBLANES - 1)) // LANES) * LANES


def _combine_meta(idx_all, slot_all, pad_start):
    n_tok = idx_all.shape[1]
    tiles = n_tok // COMBINE_TM
    experts = jnp.arange(N_EXPERTS, dtype=jnp.int32)
    hit = idx_all[:TOP_K, :, None] == experts
    cnt = jnp.sum(hit.reshape(TOP_K, tiles, COMBINE_TM, N_EXPERTS), axis=(0, 2), dtype=jnp.int32)
    run_start = pad_start[None, :] + jnp.cumsum(cnt, axis=0) - cnt
    lead = run_start % SUBLANES
    units = jnp.where(cnt > 0, (cnt + lead + RUN_UNIT - 1) // RUN_UNIT, 0)
    first_row = RUN_UNIT * (jnp.cumsum(units, axis=1) - units)
    meta = jnp.concatenate([run_start - lead, units, first_row, jnp.zeros_like(cnt)], axis=1).astype(jnp.int32)
    shift = jnp.repeat(first_row + lead - run_start, COMBINE_TM, axis=0)
    col = slot_all[:TOP_K] + jnp.sum(jnp.where(hit, shift[None], 0), axis=2, dtype=jnp.int32)
    return meta.reshape(tiles, 1, 4 * N_EXPERTS), col.T


def _combine_body(meta_ref, next_ref, col_ref, gate_ref, h_ref, nf_ref, ys_ref, y_ref, buf_ref, sem, *, n):
    i = pl.program_id(0)

    def runs(meta, half, start):
        for e in range(N_EXPERTS):
            src0, units, dst0 = meta[0, 0, e], meta[0, 0, N_EXPERTS + e], meta[0, 0, 2 * N_EXPERTS + e]

            def pieces(bits, done):
                for b in bits:
                    rows = b * RUN_UNIT

                    @pl.when((units & b) != 0)
                    def _piece(done=done, rows=rows):
                        dst = pl.multiple_of(dst0 + done, RUN_UNIT)
                        cp = pltpu.make_async_copy(ys_ref.at[pl.ds(pl.multiple_of(src0 + done, SUBLANES), rows)],
                                                   buf_ref.at[half, pl.ds(dst, rows)], sem.at[half])
                        if start:
                            cp.start()
                        else:
                            cp.wait()
                    done = done + (units & b) * RUN_UNIT
                return done
            done = pieces(RUN_BITS[-2:], jnp.int32(0))

            @pl.when(units >= RUN_BITS[-3])
            def _long_run(done=done):
                pieces(RUN_BITS[:-2], done)

    def per_half(half):
        @pl.when(i + 1 < n)
        def _fetch_next():
            runs(next_ref, 1 - half, True)
        runs(meta_ref, half, False)
        rows = lax.broadcasted_iota(jnp.int32, (COMBINE_TM, BUF_ROWS), 1)
        pick = jnp.zeros((COMBINE_TM, BUF_ROWS), F32)
        for k in range(TOP_K):
            pick = jnp.where(rows == col_ref[:, k:k + 1], gate_ref[:, k:k + 1], pick)
        acc = h_ref[...] + _dot(pick.astype(BF16), buf_ref[half].astype(BF16))
        y_ref[...] = _rms(acc, nf_ref[...])

    @pl.when(i == 0)
    def _first():
        buf_ref[...] = jnp.zeros_like(buf_ref)
        runs(meta_ref, 0, True)

    @pl.when(i % 2 == 0)
    def _even():
        per_half(0)

    @pl.when(i % 2 == 1)
    def _odd():
        per_half(1)


def _combine(meta, col_rows, gates_rows, h, norm_final, ys, first_tile):
    n = h.shape[0]
    tm = COMBINE_TM
    assert n % tm == 0
    last = first_tile + n // tm - 1
    smem_tile = lambda f: pl.BlockSpec((1, 1, meta.shape[2]), lambda i: (f(i), 0, 0), memory_space=pltpu.SMEM)
    return pl.pallas_call(
        functools.partial(_combine_body, n=n // tm),
        grid=(n // tm,),
        in_specs=[smem_tile(lambda i: i + first_tile), smem_tile(lambda i: jnp.minimum(i + first_tile + 1, last)),
                  pl.BlockSpec((tm, TOP_K), lambda i: (i + first_tile, 0)),
                  pl.BlockSpec((tm, TOP_K), lambda i: (i, 0)),
                  pl.BlockSpec((tm, D_MODEL), lambda i: (i, 0)),
                  pl.BlockSpec((1, D_MODEL), lambda i: (0, 0)),
                  pl.BlockSpec(memory_space=pl.ANY)],
        out_specs=pl.BlockSpec((tm, D_MODEL), lambda i: (i, 0)),
        out_shape=jax.ShapeDtypeStruct((n, D_MODEL), F32),
        scratch_shapes=[pltpu.VMEM((2, BUF_ROWS, D_MODEL), F32), pltpu.SemaphoreType.DMA((2,))],
        compiler_params=_cparams(("arbitrary",)),
        name="combine",
    )(meta, meta, col_rows, gates_rows, h, norm_final, ys)


def _expansion(width):
    h = np.arange(LANES)[:, None]
    c = np.arange(SSD_HEADS * width)[None, :] // width
    return jnp.asarray((h == c).astype(np.float32), dtype=BF16)


def _pad_lanes(v):
    return jnp.pad(v.astype(F32), (0, LANES - v.shape[0]))[None, :]


def kernel(x_prompt, x_sample, state_conv, state_ssm, cache_win_k, cache_win_v, norm_mix, w_in, conv_w, conv_b,
           dt_bias, a_log, d_skip, ssd_norm, att_norm, w_out, norm_ffn, router_w, router_b, w1, b1, w2, b2,
           norm_final):
    depth = w_in.shape[0]
    assert depth == 1
    bp, seq, _ = x_prompt.shape
    bs, t_new, _ = x_sample.shape
    n_p, n_s = bp * seq, bs * t_new
    l = 0

    o_dt = SSD_INNER + CONV_DIM
    o_q = o_dt + SSD_HEADS
    wl = w_in[l]
    w_cat = jnp.concatenate([wl[:, :o_dt], wl[:, o_q:], jnp.pad(wl[:, o_dt:o_q], ((0, 0), (0, DT_PAD - SSD_HEADS)))],
                            axis=1).astype(BF16)
    g_mix = norm_mix[l][None, :]
    ssd_consts = (conv_w[l], conv_b[l][None, :], _pad_lanes(dt_bias[l]), _pad_lanes(a_log[l]),
                  jnp.repeat(d_skip[l], SSD_HEAD_DIM)[None, :], ssd_norm[l][None, :],
                  _expansion(SSD_HEAD_DIM))
    slopes = jnp.exp2(-8.0 * jnp.arange(1, ATT_HEADS + 1, dtype=F32) / ATT_HEADS)
    out_consts = (att_norm[l][None, :], w_out[l][:SSD_INNER].astype(BF16), w_out[l][SSD_INNER:].astype(BF16),
                  norm_ffn[l][None, :], router_w[l].T.astype(BF16), router_b[l][:, None].astype(F32))
    w1p = _w1_prep(w1[l])
    b1p = b1[l].reshape(N_EXPERTS, -1, LANES, 2).transpose(0, 1, 3, 2).reshape(N_EXPERTS, 1, -1)
    w2b = w2[l].astype(BF16)
    b2r = b2[l][:, None, :]

    zx, q, k, v, dt, kt, vt = _in_proj(x_prompt.reshape(n_p, D_MODEL), g_mix, w_cat, 512, seq)
    y_p, conv_p, ssm_p = _ssd(zx.reshape(bp, seq, ZX_WIDTH), dt.reshape(bp, seq, DT_PAD), ssd_consts, None, BF16)
    o_p = _attn_prompt(q, k, v, slopes, bp, seq)
    keep = min(DILATED_PATTERNS[-1][0], seq)
    k_p = kt.reshape(bp, ATT_HEADS, ATT_HEAD_DIM, seq).transpose(0, 3, 1, 2)[:, seq - keep:]
    v_p = vt.reshape(bp, ATT_HEADS, ATT_HEAD_DIM, seq).transpose(0, 3, 1, 2)[:, seq - keep:]

    zx_s, q_s, k_s, v_s, dt_s = _in_proj(x_sample.reshape(n_s, D_MODEL), g_mix, w_cat, 256)
    init = (state_conv[l], state_ssm[l].reshape(bs, SSD_INNER, SSD_STATE))
    y_s, conv_s, ssm_s = _ssd(zx_s.reshape(bs, t_new, ZX_WIDTH), dt_s.reshape(bs, t_new, DT_PAD), ssd_consts, init, F32)

    def head_major(a):
        a = a.reshape(bs, t_new, ATT_HEADS, ATT_HEAD_DIM).transpose(0, 2, 1, 3)
        return jnp.pad(a, ((0, 0), (0, 0), (0, DEC_ROWS - t_new), (0, 0)))

    def head_major_t(a):
        a = a.reshape(bs, t_new, ATT_HEADS, ATT_HEAD_DIM).transpose(0, 2, 3, 1)
        return jnp.pad(a, ((0, 0), (0, 0), (0, 0), (0, LANES - t_new)))
    o_s = _attn_decode(head_major(q_s), head_major_t(k_s), head_major_t(v_s),
                       cache_win_k[l].transpose(0, 2, 3, 1), cache_win_v[l].transpose(0, 2, 3, 1), t_new)
    o_s = o_s[:, :, :, :t_new].transpose(0, 3, 1, 2)

    cnt0 = jnp.zeros((N_EXPERTS, LANES), F32)
    h_p, xn_p, idx_p, gate_p, rank_p, cnt_p = _out_proj(x_prompt.reshape(n_p, D_MODEL), y_p.reshape(n_p, SSD_INNER),
                                                        o_p, out_consts, cnt0, 512)
    h_s, xn_s, idx_s, gate_s, rank_s, cnt_all = _out_proj(x_sample.reshape(n_s, D_MODEL), y_s.reshape(n_s, SSD_INNER),
                                                          o_s.reshape(n_s, ATT_WIDTH), out_consts, cnt_p, 512)

    counts = cnt_all[:, 0].astype(jnp.int32)
    padded = (counts + MOE_ROWS - 1) // MOE_ROWS * MOE_ROWS
    pad_end = jnp.cumsum(padded)
    pad_start = pad_end - padded
    n_blocks = -(-((n_p + n_s) * TOP_K) // MOE_ROWS) + N_EXPERTS
    blk0 = jnp.arange(n_blocks, dtype=jnp.int32) * MOE_ROWS
    owner = blk0[:, None] >= pad_end[None, :]
    block_expert = jnp.minimum(jnp.sum(owner, axis=1), N_EXPERTS - 1).astype(jnp.int32)
    onehot = block_expert[:, None] == jnp.arange(N_EXPERTS, dtype=jnp.int32)[None, :]
    used = jnp.sum(jnp.where(onehot, (blk0[:, None] - pad_start[None, :]), 0), axis=1)
    block_valid = jnp.clip(jnp.sum(jnp.where(onehot, counts[None, :], 0), axis=1) - used, 0, MOE_ROWS).astype(jnp.int32)

    tm = min(256, n_s)
    idx_all = jnp.concatenate([idx_p, idx_s], axis=1)
    rank_all = jnp.concatenate([rank_p, rank_s], axis=1)
    first_slot = functools.reduce(lambda acc, e: jnp.where(idx_all == e, pad_start[e], acc), range(N_EXPERTS),
                                  jnp.zeros_like(idx_all))
    slot_all = first_slot + rank_all
    xs = _dispatch(slot_all[:TOP_K].T.reshape(-1), pad_start + counts, padded - counts, pad_end[N_EXPERTS - 1:] // MOE_ROWS,
                   xn_p, xn_s, n_blocks, tm)
    ys = _experts(block_expert, block_valid, xs, w1p, b1p, w2b, b2r)
    nfin = norm_final[None, :]
    meta, col_rows = _combine_meta(idx_all, slot_all, pad_start)
    y_prompt = _combine(meta, col_rows, gate_p[:TOP_K].T, h_p, nfin, ys, 0)
    y_sample = _combine(meta, col_rows, gate_s[:TOP_K].T, h_s, nfin, ys, n_p // COMBINE_TM)

    return (y_prompt.reshape(bp, seq, D_MODEL), y_sample.reshape(bs, t_new, D_MODEL),
            conv_p[None], ssm_p.reshape(1, bp, SSD_HEADS, SSD_HEAD_DIM, SSD_STATE), k_p[None], v_p[None],
            conv_s[None], ssm_s.reshape(1, bs, SSD_HEADS, SSD_HEAD_DIM, SSD_STATE),
            k_s.reshape(1, bs, t_new, ATT_HEADS, ATT_HEAD_DIM), v_s.reshape(1, bs, t_new, ATT_HEADS, ATT_HEAD_DIM))
```

```python
import functools

import jax
import jax.numpy as jnp
import numpy as np
from jax import lax
from jax.experimental import pallas as pl
from jax.experimental.pallas import tpu as pltpu

F32 = jnp.float32
BF16 = jnp.bfloat16

D_MODEL = 1024
SSD_HEADS = 16
SSD_HEAD_DIM = 64
SSD_INNER = SSD_HEADS * SSD_HEAD_DIM
SSD_GROUPS = 2
SSD_STATE = 128
SSD_CONV = 4
SSD_CHUNK = 128
CONV_DIM = SSD_INNER + 2 * SSD_GROUPS * SSD_STATE
ATT_HEADS = 8
ATT_HEAD_DIM = 64
ATT_WIDTH = ATT_HEADS * ATT_HEAD_DIM
DILATED_PATTERNS = ((128, 1), (512, 4), (2048, 16))
ATT_BLOCK = 128
N_EXPERTS = 32
TOP_K = 4
SWIGLU_LIMIT = 7.0
SWIGLU_ALPHA = 1.702
NORM_EPS = 1e-5

LANES = 128
SUBLANES = 8
ZX_WIDTH = SSD_INNER + CONV_DIM
DT_PAD = LANES
CONV_HIST = SUBLANES
MOE_ROWS = 512
VMEM_LIMIT = 56 * 1024 * 1024

NEG_INF = float("-inf")


def _cparams(sem):
    return pltpu.CompilerParams(dimension_semantics=sem, vmem_limit_bytes=VMEM_LIMIT)


def _rms(x, g):
    return x * lax.rsqrt(jnp.mean(x * x, axis=-1, keepdims=True) + NORM_EPS) * g


def _dot(a, b):
    return jnp.dot(a, b, preferred_element_type=F32)


def _dot_nt(a, b):
    return lax.dot_general(a, b, (((1,), (1,)), ((), ())), preferred_element_type=F32)


def _split3(v):
    hi = v.astype(BF16)
    r1 = v - hi.astype(F32)
    mid = r1.astype(BF16)
    lo = (r1 - mid.astype(F32)).astype(BF16)
    return hi, mid, lo


def _dot3(v, m):
    hi, mid, lo = _split3(v)
    return _dot(hi, m) + _dot(mid, m) + _dot(lo, m)


def _dot3_lhs(m, v):
    hi, mid, lo = _split3(v)
    return _dot(m, hi) + _dot(m, mid) + _dot(m, lo)


def _silu(x):
    half = 0.5 * x
    return half * (1.0 + jnp.tanh(half))


def _in_proj_body(x_ref, g_ref, w_ref, zx_ref, q_ref, k_ref, v_ref, dt_ref, *t_refs):
    hn = _rms(x_ref[...], g_ref[...]).astype(BF16)
    o = 0
    for ref, width in ((zx_ref, ZX_WIDTH), (q_ref, ATT_WIDTH), (k_ref, ATT_WIDTH), (v_ref, ATT_WIDTH),
                       (dt_ref, DT_PAD)):
        ref[...] = _dot(hn, w_ref[:, o:o + width])
        o += width
    for src, dst in zip((k_ref, v_ref), t_refs):
        for j in range(ATT_WIDTH // LANES):
            dst[0, j * LANES:(j + 1) * LANES, :] = src[:, j * LANES:(j + 1) * LANES].T


def _in_proj(x2d, g, w_cat, tm, seq=None):
    n = x2d.shape[0]
    tm = min(tm, n)
    per = 1 if seq is None else seq // tm
    row = lambda w: pl.BlockSpec((tm, w), lambda b, i: (b * per + i, 0))
    full = lambda a: pl.BlockSpec(a.shape, lambda b, i: (0,) * a.ndim)
    widths = (ZX_WIDTH, ATT_WIDTH, ATT_WIDTH, ATT_WIDTH, DT_PAD)
    out_specs = [row(w) for w in widths]
    out_shape = [jax.ShapeDtypeStruct((n, w), F32) for w in widths]
    if seq is not None:
        out_specs += [pl.BlockSpec((1, ATT_WIDTH, tm), lambda b, i: (b, 0, i))] * 2
        out_shape += [jax.ShapeDtypeStruct((n // seq, ATT_WIDTH, seq), F32)] * 2
    return pl.pallas_call(
        _in_proj_body,
        grid=(n // (tm * per), per),
        in_specs=[row(D_MODEL), full(g), full(w_cat)],
        out_specs=out_specs,
        out_shape=out_shape,
        compiler_params=_cparams(("parallel", "parallel")),
        name="in_proj",
    )(x2d, g, w_cat)


def _ssd_body(*refs, l_blk, has_init):
    T = SSD_CHUNK
    if has_init:
        (zx_ref, dt_ref, cw_ref, cb_ref, dtb_ref, alog_ref, dskip_ref, norm_ref, e64_ref,
         cinit_ref, sinit_ref, y_ref, conv_out_ref, ssm_out_ref,
         ext_ref, act_ref, state_ref, cst_ref, ybuf_ref, zpad_ref, dtpad_ref) = refs
    else:
        (zx_ref, dt_ref, cw_ref, cb_ref, dtb_ref, alog_ref, dskip_ref, norm_ref, e64_ref,
         y_ref, conv_out_ref, ssm_out_ref,
         ext_ref, act_ref, state_ref, cst_ref, ybuf_ref) = refs
    c = pl.program_id(1)
    n_tile = SSD_INNER // LANES

    @pl.when(c == 0)
    def _start():
        if has_init:
            ext_ref[0:CONV_HIST, :] = jnp.zeros((CONV_HIST, CONV_DIM), F32)
            ext_ref[CONV_HIST - (SSD_CONV - 1):CONV_HIST, :] = cinit_ref[0]
            for j in range(n_tile):
                state_ref[:, j * LANES:(j + 1) * LANES] = sinit_ref[0, j * LANES:(j + 1) * LANES, :].T
        else:
            ext_ref[0:CONV_HIST, :] = jnp.zeros((CONV_HIST, CONV_DIM), F32)
            state_ref[...] = jnp.zeros_like(state_ref)

    if l_blk == T:
        ext_ref[CONV_HIST:CONV_HIST + T, :] = zx_ref[0, :, SSD_INNER:ZX_WIDTH]
        z_of = lambda sl: zx_ref[0, :, sl]
        dt_raw = dt_ref[0]
    else:
        ext_ref[CONV_HIST:CONV_HIST + T, :] = jnp.zeros((T, CONV_DIM), F32)
        ext_ref[CONV_HIST:CONV_HIST + l_blk, :] = zx_ref[0, :, SSD_INNER:ZX_WIDTH]
        zpad_ref[...] = jnp.zeros_like(zpad_ref)
        zpad_ref[0:l_blk, :] = zx_ref[0, :, 0:SSD_INNER]
        dtpad_ref[...] = jnp.zeros_like(dtpad_ref)
        dtpad_ref[0:l_blk, :] = dt_ref[0]
        z_of = lambda sl: zpad_ref[:, sl]
        dt_raw = dtpad_ref[...]

    cw = CONV_DIM // 3
    for cc in range(3):
        sl = slice(cc * cw, (cc + 1) * cw)
        acc = cb_ref[:, sl]
        for j in range(SSD_CONV):
            o = CONV_HIST - (SSD_CONV - 1) + j
            acc = acc + ext_ref[o:o + T, sl] * cw_ref[j:j + 1, sl]
        act_ref[:, sl] = _silu(acc)

    row = lax.broadcasted_iota(jnp.int32, (T, LANES), 0)
    col = lax.broadcasted_iota(jnp.int32, (T, LANES), 1)
    tri = row >= col
    tri_bf = jnp.where(tri, 1.0, 0.0).astype(BF16)
    even = col < SSD_HEAD_DIM

    xdt = dt_raw + dtb_ref[...]
    dtv = jnp.maximum(xdt, 0.0) + jnp.log1p(jnp.exp(-jnp.abs(xdt)))
    if l_blk < T:
        dtv = jnp.where(row < l_blk, dtv, 0.0)
    d_a = dtv * (-jnp.exp(alog_ref[...]))
    cs = _dot3_lhs(tri_bf, d_a)
    cst_ref[...] = cs.T
    ex = _dot3(jnp.concatenate([dtv, cs], axis=0), e64_ref[...])
    dt_ex, cs_ex = ex[0:T], ex[T:2 * T]
    ecs_ex = jnp.exp(cs_ex)
    dd_ex = dt_ex * jnp.exp(cs_ex[T - 1:T] - cs_ex)

    gw = SSD_INNER // SSD_GROUPS
    heads_per_group = SSD_HEADS // SSD_GROUPS
    for g in range(SSD_GROUPS):
        gsl = slice(g * gw, (g + 1) * gw)
        b_g = act_ref[:, SSD_INNER + g * SSD_STATE:SSD_INNER + (g + 1) * SSD_STATE]
        c_off = SSD_INNER + SSD_GROUPS * SSD_STATE
        c_g = act_ref[:, c_off + g * SSD_STATE:c_off + (g + 1) * SSD_STATE].astype(BF16)
        cb = _dot_nt(c_g, b_g.astype(BF16))
        b_gt = b_g.T.astype(BF16)
        x_g = act_ref[:, gsl]
        x_dt = (x_g * dt_ex[:, gsl]).astype(BF16)
        x_dd = (x_g * dd_ex[:, gsl]).astype(BF16)
        st_old = state_ref[:, gsl]
        y_off = _dot(c_g, st_old.astype(BF16)) * ecs_ex[:, gsl]
        state_ref[:, gsl] = st_old * ecs_ex[T - 1:T, gsl] + _dot(b_gt, x_dd)
        for jp in range(heads_per_group // 2):
            h0 = g * heads_per_group + 2 * jp
            psl = slice(jp * LANES, (jp + 1) * LANES)
            osl = slice(g * gw + jp * LANES, g * gw + (jp + 1) * LANES)
            pair = cs_ex[:, osl]
            swapped = pltpu.roll(pair, SSD_HEAD_DIM, axis=1)
            cols = (jnp.where(even, pair, swapped), jnp.where(even, swapped, pair))
            yd = []
            for col, h in zip(cols, (h0, h0 + 1)):
                seg = col - cst_ref[h:h + 1, :]
                lmat = jnp.exp(jnp.where(tri, seg, NEG_INF))
                yd.append(_dot((cb * lmat).astype(BF16), x_dt[:, psl]))
            y_pair = jnp.where(even, yd[0], yd[1]) + y_off[:, psl]
            ybuf_ref[:, osl] = y_pair + dskip_ref[:, osl] * x_g[:, psl]

    for g in range(SSD_GROUPS):
        gsl = slice(g * gw, (g + 1) * gw)
        yg = ybuf_ref[:, gsl] * _silu(z_of(gsl))
        yn = yg * lax.rsqrt(jnp.mean(yg * yg, axis=-1, keepdims=True) + NORM_EPS) * norm_ref[:, gsl]
        y_ref[0, :, gsl] = yn[0:l_blk].astype(y_ref.dtype)

    @pl.when(c == pl.num_programs(1) - 1)
    def _finish():
        lo = CONV_HIST + l_blk - (SSD_CONV - 1)
        conv_out_ref[0] = ext_ref[lo:lo + SSD_CONV - 1, :]
        for j in range(n_tile):
            ssm_out_ref[0, j * LANES:(j + 1) * LANES, :] = state_ref[:, j * LANES:(j + 1) * LANES].T

    ext_ref[0:CONV_HIST, :] = ext_ref[T:T + CONV_HIST, :]


def _ssd(zx3, dt3, consts, init, y_dtype):
    b, L, _ = zx3.shape
    T = SSD_CHUNK
    l_blk = T if L % T == 0 else L
    n_chunks = L // l_blk
    has_init = init is not None
    full = lambda a: pl.BlockSpec(a.shape, lambda i, c: (0,) * a.ndim)
    in_specs = [pl.BlockSpec((1, l_blk, ZX_WIDTH), lambda i, c: (i, c, 0)),
                pl.BlockSpec((1, l_blk, DT_PAD), lambda i, c: (i, c, 0))] + [full(a) for a in consts]
    args = [zx3, dt3, *consts]
    scratch = [pltpu.VMEM((T + CONV_HIST, CONV_DIM), F32),
               pltpu.VMEM((T, CONV_DIM), F32),
               pltpu.VMEM((SSD_STATE, SSD_INNER), F32),
               pltpu.VMEM((LANES, T), F32),
               pltpu.VMEM((T, SSD_INNER), F32)]
    if has_init:
        in_specs += [pl.BlockSpec((1, SSD_CONV - 1, CONV_DIM), lambda i, c: (i, 0, 0)),
                     pl.BlockSpec((1, SSD_INNER, SSD_STATE), lambda i, c: (i, 0, 0))]
        args += list(init)
        scratch += [pltpu.VMEM((T, SSD_INNER), F32), pltpu.VMEM((T, DT_PAD), F32)]
    return pl.pallas_call(
        functools.partial(_ssd_body, l_blk=l_blk, has_init=has_init),
        grid=(b, n_chunks),
        in_specs=in_specs,
        out_specs=[pl.BlockSpec((1, l_blk, SSD_INNER), lambda i, c: (i, c, 0)),
                   pl.BlockSpec((1, SSD_CONV - 1, CONV_DIM), lambda i, c: (i, 0, 0)),
                   pl.BlockSpec((1, SSD_INNER, SSD_STATE), lambda i, c: (i, 0, 0))],
        out_shape=[jax.ShapeDtypeStruct((b, L, SSD_INNER), y_dtype),
                   jax.ShapeDtypeStruct((b, SSD_CONV - 1, CONV_DIM), F32),
                   jax.ShapeDtypeStruct((b, SSD_INNER, SSD_STATE), F32)],
        scratch_shapes=scratch,
        compiler_params=_cparams(("parallel", "arbitrary")),
        name="ssd_init" if has_init else "ssd",
    )(*args)


ATT_UNROLL = 16


def _unroll(n):
    return max(d for d in range(1, ATT_UNROLL + 1) if n % d == 0)


ATT_RES = DILATED_PATTERNS[-1][1]


def _attn_tables():
    B = ATT_BLOCK
    t1 = np.full((len(DILATED_PATTERNS), B, B), -np.inf, np.float32)
    t2 = np.full((len(DILATED_PATTERNS), B, 2 * B), -np.inf, np.float32)
    for p, (window, dil) in enumerate(DILATED_PATTERNS):
        m = ATT_RES // dil
        w = B // m
        rho = np.arange(B)
        c = m * (rho % w) + rho // w
        ck2 = np.concatenate([c - B, c])
        for tab, ck in ((t1, c), (t2, ck2)):
            delta = c[:, None] - ck[None, :]
            ok = (delta >= 0) & (delta <= window // dil)
            tab[p] = np.where(ok, -(dil * delta).astype(np.float32), -np.inf)
    return jnp.asarray(t1), jnp.asarray(t2)


def _attn_body(slopes_ref, t1_ref, t2_ref, q_ref, k_ref, v_ref, o_ref, qd_ref, kd_ref, vd_ref, op_ref, lp_ref,
               stage_ref, b1_ref, b2_ref, in_sem, out_sem, *, seq):
    B = ATT_BLOCK
    hp = pl.program_id(1)
    even = lax.broadcasted_iota(jnp.int32, (B, LANES), 1) < ATT_HEAD_DIM
    scale = ATT_HEAD_DIM ** -0.5

    def in_copy(a, r):
        src, dst = ((q_ref, qd_ref), (k_ref, kd_ref), (v_ref, vd_ref))[a]
        return pltpu.make_async_copy(src.at[:, r, :], dst.at[r], in_sem.at[a, r])
    for r in range(ATT_RES):
        for a in range(3):
            in_copy(a, r).start()

    for e in range(2):
        slope = slopes_ref[2 * hp + e]
        for p in range(len(DILATED_PATTERNS)):
            b1_ref[e * len(DILATED_PATTERNS) + p] = t1_ref[p] * slope
            b2_ref[e * len(DILATED_PATTERNS) + p] = t2_ref[p] * slope

    for r in range(ATT_RES):
        for a in range(3):
            in_copy(a, r).wait()

    def block(p, r_d, n, first):
        dil = DILATED_PATTERNS[p][1]
        m = ATT_RES // dil
        w = B // m

        def slab(j, nblk):
            start = nblk * w
            return r_d + dil * j, pl.ds(start if isinstance(start, int) else pl.multiple_of(start, w), w)

        def gather(src, nblk):
            parts = []
            for j in range(m):
                r, rows = slab(j, nblk)
                parts.append(src[r, rows, :])
            return parts[0] if m == 1 else jnp.concatenate(parts, axis=0)

        qb = gather(qd_ref, n) * scale
        if first:
            kb, vb = gather(kd_ref, n), gather(vd_ref, n)
            bias = lambda e: b1_ref[e * len(DILATED_PATTERNS) + p]
        else:
            kb = jnp.concatenate([gather(kd_ref, n - 1), gather(kd_ref, n)], axis=0)
            vb = jnp.concatenate([gather(vd_ref, n - 1), gather(vd_ref, n)], axis=0)
            bias = lambda e: b2_ref[e * len(DILATED_PATTERNS) + p]
        q2 = jnp.concatenate([jnp.where(even, qb, 0.0), jnp.where(even, 0.0, qb)], axis=0).astype(BF16)
        s = _dot_nt(q2, kb.astype(BF16)) + jnp.concatenate([bias(0), bias(1)], axis=0)
        mx = jnp.max(s, axis=-1, keepdims=True)
        pr = jnp.exp(s - mx)
        l = jnp.sum(pr, axis=-1, keepdims=True)
        o2 = _dot(pr.astype(BF16), vb.astype(BF16)) / l
        lse2 = jnp.broadcast_to(mx + jnp.log(l), (2 * B, LANES))
        o = jnp.where(even, o2[:B], o2[B:])
        lse = jnp.where(even, lse2[:B], lse2[B:])
        for j in range(m):
            r, rows = slab(j, n)
            op_ref[p, r, rows, :] = o[j * w:(j + 1) * w]
            lp_ref[p, r, rows, :] = lse[j * w:(j + 1) * w]

    for p, (_, dil) in enumerate(DILATED_PATTERNS):
        nb = seq // (dil * B)
        u_first = _unroll(dil)

        def first(i, carry, p=p, u_first=u_first):
            for u in range(u_first):
                block(p, i * u_first + u, 0, True)
            return carry
        lax.fori_loop(0, dil // u_first, first, 0)
        n_later = dil * (nb - 1)
        if n_later:
            u_later = _unroll(n_later)

            def later(i, carry, p=p, nb=nb, u_later=u_later):
                for u in range(u_later):
                    j = i * u_later + u
                    block(p, j // (nb - 1), j % (nb - 1) + 1, False)
                return carry
            lax.fori_loop(0, n_later // u_later, later, 0)

    def out_copy(r):
        return pltpu.make_async_copy(stage_ref.at[r], o_ref.at[:, r, :], out_sem.at[r])
    for r in range(ATT_RES):
        l0, l1, l2 = lp_ref[0, r], lp_ref[1, r], lp_ref[2, r]
        m = jnp.maximum(jnp.maximum(l0, l1), l2)
        w0, w1, w2 = jnp.exp(l0 - m), jnp.exp(l1 - m), jnp.exp(l2 - m)
        num = w0 * op_ref[0, r] + w1 * op_ref[1, r] + w2 * op_ref[2, r]
        stage_ref[r] = num / (w0 + w1 + w2)
        out_copy(r).start()
    for r in range(ATT_RES):
        out_copy(r).wait()


def _attn_prompt(q, k, v, slopes, batch, seq):
    n_hp = ATT_HEADS // 2
    n_pat = len(DILATED_PATTERNS)
    assert len(DILATED_PATTERNS) == 3 and all(ATT_RES % d == 0 and seq % (d * ATT_BLOCK) == 0 and w // d == ATT_BLOCK
                                              for w, d in DILATED_PATTERNS)
    per = seq // ATT_RES
    by_res = lambda a: a.reshape(batch * per, ATT_RES, ATT_WIDTH)
    blk = pl.BlockSpec((per, ATT_RES, LANES), lambda b, h: (b, 0, h))
    t1, t2 = _attn_tables()
    full = lambda a: pl.BlockSpec(a.shape, lambda b, h: (0,) * a.ndim)
    res = pltpu.VMEM((ATT_RES, per, LANES), F32)
    out = pl.pallas_call(
        functools.partial(_attn_body, seq=seq),
        grid=(batch, n_hp),
        in_specs=[pl.BlockSpec(memory_space=pltpu.SMEM), full(t1), full(t2), blk, blk, blk],
        out_specs=blk,
        out_shape=jax.ShapeDtypeStruct((batch * per, ATT_RES, ATT_WIDTH), F32),
        scratch_shapes=[res, res, res,
                        pltpu.VMEM((n_pat, ATT_RES, per, LANES), F32), pltpu.VMEM((n_pat, ATT_RES, per, LANES), F32),
                        res,
                        pltpu.VMEM((2 * n_pat, ATT_BLOCK, ATT_BLOCK), F32),
                        pltpu.VMEM((2 * n_pat, ATT_BLOCK, 2 * ATT_BLOCK), F32),
                        pltpu.SemaphoreType.DMA((3, ATT_RES)), pltpu.SemaphoreType.DMA((ATT_RES,))],
        compiler_params=_cparams(("parallel", "parallel")),
        name="attn_prompt",
    )(slopes, t1, t2, by_res(q), by_res(k), by_res(v))
    return out.reshape(batch * seq, ATT_WIDTH)


DEC_ROWS = SUBLANES


def _decode_tables(t_new, m_cache):
    slopes = 2.0 ** (-8.0 * np.arange(1, ATT_HEADS + 1) / ATT_HEADS)

    def mult(dist):
        return sum(1 for window, dil in DILATED_PATTERNS if 0 <= dist <= window and dist % dil == 0)

    bias_c = np.zeros((ATT_HEADS, DEC_ROWS, m_cache), np.float32)
    mult_c = np.ones((DEC_ROWS, m_cache), np.float32)
    bias_n = np.full((ATT_HEADS, DEC_ROWS, LANES), -np.inf, np.float32)
    mult_n = np.zeros((DEC_ROWS, LANES), np.float32)
    for t in range(t_new):
        dist = m_cache + t - np.arange(m_cache)
        mu = np.array([mult(d) for d in dist], np.float32)
        mult_c[t] = mu
        bias_c[:, t, :] = np.where(mu > 0, -slopes[:, None] * dist[None, :], -np.inf)
        for t2 in range(t_new):
            if mult(t - t2):
                bias_n[:, t, t2] = -slopes * (t - t2)
                mult_n[t, t2] = mult(t - t2)
    bias_n[:, t_new:, 0] = 0.0
    mult_n[t_new:, 0] = 1.0
    return jnp.asarray(bias_c), jnp.asarray(mult_c), jnp.asarray(bias_n), jnp.asarray(mult_n)


def _attn_dec_body(q_ref, knt_ref, vnt_ref, ck_ref, cv_ref, bc_ref, mc_ref, bn_ref, mn_ref, o_ref):
    scale = ATT_HEAD_DIM ** -0.5
    for h in range(ATT_HEADS):
        qh = (q_ref[0, h] * scale).astype(BF16)
        sc = _dot(qh, ck_ref[0, h].astype(BF16)) + bc_ref[h]
        sn = _dot(qh, knt_ref[0, h].astype(BF16)) + bn_ref[h]
        m = jnp.maximum(jnp.max(sc, axis=-1, keepdims=True), jnp.max(sn, axis=-1, keepdims=True))
        pc = jnp.exp(sc - m) * mc_ref[...]
        pn = jnp.exp(sn - m) * mn_ref[...]
        inv_l = 1.0 / (jnp.sum(pc, axis=-1, keepdims=True) + jnp.sum(pn, axis=-1, keepdims=True))
        o_ref[0, h] = (_dot_nt(cv_ref[0, h].astype(BF16), (pc * inv_l).astype(BF16))
                       + _dot_nt(vnt_ref[0, h].astype(BF16), (pn * inv_l).astype(BF16)))


def _attn_decode(q4, knt, vnt, cache_kt, cache_vt, t_new):
    b = q4.shape[0]
    m_cache = cache_kt.shape[3]
    assert t_new <= DEC_ROWS
    tables = _decode_tables(t_new, m_cache)
    per_b = lambda a: pl.BlockSpec((1,) + a.shape[1:], lambda i: (i, 0, 0, 0))
    full = lambda a: pl.BlockSpec(a.shape, lambda i: (0,) * a.ndim)
    out_shape = jax.ShapeDtypeStruct((b, ATT_HEADS, ATT_HEAD_DIM, DEC_ROWS), F32)
    return pl.pallas_call(
        _attn_dec_body,
        grid=(b,),
        in_specs=[per_b(a) for a in (q4, knt, vnt, cache_kt, cache_vt)] + [full(a) for a in tables],
        out_specs=per_b(out_shape),
        out_shape=out_shape,
        compiler_params=_cparams(("parallel",)),
        name="attn_decode",
    )(q4, knt, vnt, cache_kt, cache_vt, *tables)


def _out_proj_body(x_ref, y_ref, o_ref, an_ref, wy_ref, wo_ref, nf_ref, rw_ref, rb_ref, cnt0_ref,
                   h_ref, xn_ref, idx_ref, gate_ref, rank_ref, cnt_ref, carry_ref, *, tm):
    i = pl.program_id(0)

    @pl.when(i == 0)
    def _start():
        carry_ref[...] = cnt0_ref[...]

    on = _rms(o_ref[...], an_ref[...]).astype(BF16)
    mixed = _dot(y_ref[...].astype(BF16), wy_ref[...]) + _dot(on, wo_ref[...])
    h = x_ref[...] + mixed
    h_ref[...] = h
    xn = _rms(h, nf_ref[...])
    xn_ref[...] = xn
    logits = _dot_nt(rw_ref[...], xn.astype(BF16)) + rb_ref[...]

    e_iota = lax.broadcasted_iota(jnp.int32, (N_EXPERTS, tm), 0)
    vals, idxs, sels = [], [], []
    cur = logits
    for _ in range(TOP_K):
        mx = jnp.max(cur, axis=0, keepdims=True)
        ix = jnp.min(jnp.where(cur == mx, e_iota, N_EXPERTS), axis=0, keepdims=True)
        sel = e_iota == ix
        cur = jnp.where(sel, NEG_INF, cur)
        vals.append(mx)
        idxs.append(ix)
        sels.append(sel)
    ex = [jnp.exp(v - vals[0]) for v in vals]
    den = functools.reduce(lambda a, b: a + b, ex)

    sel_any = functools.reduce(jnp.logical_or, sels)
    sel_f = jnp.where(sel_any, 1.0, 0.0)
    r = lax.broadcasted_iota(jnp.int32, (tm, tm), 0)
    c = lax.broadcasted_iota(jnp.int32, (tm, tm), 1)
    upper = jnp.where(r <= c, 1.0, 0.0).astype(BF16)
    cum = _dot(sel_f.astype(BF16), upper)
    before = carry_ref[:, 0:1] + cum - sel_f
    for k in range(TOP_K):
        idx_ref[k:k + 1, :] = idxs[k]
        gate_ref[k:k + 1, :] = ex[k] / den
        rank_ref[k:k + 1, :] = jnp.sum(jnp.where(sels[k], before, 0.0), axis=0, keepdims=True).astype(jnp.int32)
    idx_ref[TOP_K:SUBLANES, :] = jnp.zeros((SUBLANES - TOP_K, tm), jnp.int32)
    gate_ref[TOP_K:SUBLANES, :] = jnp.zeros((SUBLANES - TOP_K, tm), F32)
    rank_ref[TOP_K:SUBLANES, :] = jnp.zeros((SUBLANES - TOP_K, tm), jnp.int32)
    carry_ref[...] = carry_ref[...] + jnp.max(cum, axis=1, keepdims=True)
    cnt_ref[...] = carry_ref[...]


def _out_proj(x2d, y2d, o2d, consts, cnt0, tm):
    n = x2d.shape[0]
    tm = min(tm, n)
    row = lambda w: pl.BlockSpec((tm, w), lambda i: (i, 0))
    colb = pl.BlockSpec((SUBLANES, tm), lambda i: (0, i))
    full = lambda a: pl.BlockSpec(a.shape, lambda i: (0,) * a.ndim)
    return pl.pallas_call(
        functools.partial(_out_proj_body, tm=tm),
        grid=(n // tm,),
        in_specs=[row(D_MODEL), row(SSD_INNER), row(ATT_WIDTH)] + [full(a) for a in consts] + [full(cnt0)],
        out_specs=[row(D_MODEL), row(D_MODEL), colb, colb, colb, full(cnt0)],
        out_shape=[jax.ShapeDtypeStruct((n, D_MODEL), F32), jax.ShapeDtypeStruct((n, D_MODEL), F32),
                   jax.ShapeDtypeStruct((SUBLANES, n), jnp.int32), jax.ShapeDtypeStruct((SUBLANES, n), F32),
                   jax.ShapeDtypeStruct((SUBLANES, n), jnp.int32), jax.ShapeDtypeStruct(cnt0.shape, F32)],
        scratch_shapes=[pltpu.VMEM(cnt0.shape, F32)],
        compiler_params=_cparams(("arbitrary",)),
        name="out_proj",
    )(x2d, y2d, o2d, *consts, cnt0)


ISSUE_UNROLL = 8


def _dispatch_body(slot_ref, from_ref, n_ref, tail_ref, xa_ref, xb_ref, xs_ref, zero_ref, stage_ref, sem, row_sems,
                   *, tm, tiles_a, tiles_b, n_blocks):
    i = pl.program_id(0)

    @pl.when(i == 0)
    def _pad_fill():
        zero_ref[...] = jnp.zeros_like(zero_ref)
        one_row = lambda slot: pltpu.make_async_copy(zero_ref.at[pl.ds(0, 1)], xs_ref.at[pl.ds(slot, 1)], sem)
        block = lambda j: pltpu.make_async_copy(zero_ref, xs_ref.at[pl.ds(j * MOE_ROWS, MOE_ROWS)], sem)

        def per_expert(e, carry):
            def start(r, c):
                one_row(from_ref[e] + r).start()
                return c

            def wait(r, c):
                one_row(0).wait()
                return c
            lax.fori_loop(0, n_ref[e], start, 0)
            lax.fori_loop(0, n_ref[e], wait, 0)
            return carry
        lax.fori_loop(0, N_EXPERTS, per_expert, 0)

        def tail(j, carry):
            block(j).start()
            block(j).wait()
            return carry
        lax.fori_loop(tail_ref[0], n_blocks, tail, 0)

    def drain(parity):
        for _ in range(TOP_K):
            pltpu.make_async_copy(stage_ref.at[parity], xs_ref.at[pl.ds(0, tm)], row_sems.at[parity]).wait()

    for parity in range(2):
        @pl.when(i % 2 == parity)
        def _step(parity=parity):
            @pl.when(i < tiles_a)
            def _first_group():
                stage_ref[parity] = xa_ref[...]

            @pl.when(i >= tiles_a)
            def _second_group():
                stage_ref[parity] = xb_ref[...]

            def issue(j, carry):
                t0 = pl.multiple_of(j * ISSUE_UNROLL, ISSUE_UNROLL)
                for u in range(ISSUE_UNROLL):
                    for k in range(TOP_K):
                        slot = slot_ref[t0 * TOP_K + (u * TOP_K + k)]
                        pltpu.make_async_copy(stage_ref.at[parity, pl.ds(t0 + u, 1)],
                                              xs_ref.at[pl.ds(slot, 1)], row_sems.at[parity]).start()
                return carry
            lax.fori_loop(0, tm // ISSUE_UNROLL, issue, 0)

            @pl.when(i > 0)
            def _previous():
                drain(1 - parity)

            @pl.when(i == tiles_a + tiles_b - 1)
            def _last():
                drain(parity)


def _dispatch(slot, pad_from, pad_n, tail_block, xa, xb, n_blocks, tm):
    tiles_a, tiles_b = xa.shape[0] // tm, xb.shape[0] // tm
    assert tiles_a * tm == xa.shape[0] and tiles_b * tm == xb.shape[0]
    smem = pl.BlockSpec(memory_space=pltpu.SMEM)
    hbm = pl.BlockSpec(memory_space=pl.ANY)
    return pl.pallas_call(
        functools.partial(_dispatch_body, tm=tm, tiles_a=tiles_a, tiles_b=tiles_b, n_blocks=n_blocks),
        grid=(tiles_a + tiles_b,),
        in_specs=[pl.BlockSpec((tm * TOP_K,), lambda i: (i,), memory_space=pltpu.SMEM), smem, smem, smem,
                  pl.BlockSpec((tm, D_MODEL), lambda i: (jnp.minimum(i, tiles_a - 1), 0)),
                  pl.BlockSpec((tm, D_MODEL), lambda i: (jnp.maximum(i - tiles_a, 0), 0))],
        out_specs=hbm,
        out_shape=jax.ShapeDtypeStruct((n_blocks * MOE_ROWS, D_MODEL), F32),
        scratch_shapes=[pltpu.VMEM((MOE_ROWS, D_MODEL), F32), pltpu.VMEM((2, tm, D_MODEL), F32),
                        pltpu.SemaphoreType.DMA(()), pltpu.SemaphoreType.DMA((2,))],
        compiler_params=_cparams(("arbitrary",)),
        name="dispatch",
    )(slot, pad_from, pad_n, tail_block, xa, xb)


MOE_COLS = 2 * LANES


def _glu_perm():
    src = np.concatenate([np.arange(0, MOE_COLS, 2), np.arange(1, MOE_COLS, 2)])
    perm = np.zeros((MOE_COLS, MOE_COLS), np.float32)
    perm[src, np.arange(MOE_COLS)] = 1.0
    return jnp.asarray(perm, dtype=BF16)


def _experts_body(be_ref, bv_ref, xs_ref, w1_ref, p_ref, b1_ref, w2_ref, b2_ref, ys_ref, w1s_ref, w2s_ref, g_ref):
    j = pl.program_id(0)
    valid = bv_ref[j]

    @pl.when(jnp.logical_and(valid > 0, jnp.logical_or(j == 0, be_ref[j] != be_ref[jnp.maximum(j - 1, 0)])))
    def _new_expert():
        for c in range(w1s_ref.shape[1] // MOE_COLS):
            cols = slice(c * MOE_COLS, (c + 1) * MOE_COLS)
            w1s_ref[:, cols] = _dot(w1_ref[0, :, cols].astype(BF16), p_ref[...]).astype(BF16)
        w2s_ref[...] = w2_ref[0].astype(BF16)

    @pl.when(valid > 0)
    def _compute():
        x = xs_ref[...].astype(BF16)
        for c in range(g_ref.shape[1] // LANES):
            cols = slice(c * MOE_COLS, (c + 1) * MOE_COLS)
            hc = _dot(x, w1s_ref[:, cols]) + b1_ref[0, :, cols]
            glu = jnp.minimum(hc[:, :LANES], SWIGLU_LIMIT)
            lin = jnp.clip(hc[:, LANES:], -SWIGLU_LIMIT, SWIGLU_LIMIT)
            g_ref[:, c * LANES:(c + 1) * LANES] = (glu * jax.nn.sigmoid(SWIGLU_ALPHA * glu) * (lin + 1.0)).astype(BF16)
        ys_ref[...] = _dot(g_ref[...], w2s_ref[...]) + b2_ref[0]

    @pl.when(valid <= 0)
    def _empty():
        ys_ref[...] = jnp.zeros_like(ys_ref)


def _experts(block_expert, block_valid, xs, w1, b1p, w2, b2):
    n_blocks = xs.shape[0] // MOE_ROWS
    perm = _glu_perm()
    wspec = lambda a: pl.BlockSpec((1,) + a.shape[1:], lambda j, be, bv: (be[j], 0, 0))
    grid_spec = pltpu.PrefetchScalarGridSpec(
        num_scalar_prefetch=2,
        grid=(n_blocks,),
        in_specs=[pl.BlockSpec((MOE_ROWS, D_MODEL), lambda j, be, bv: (j, 0)),
                  wspec(w1), pl.BlockSpec(perm.shape, lambda j, be, bv: (0, 0)), wspec(b1p), wspec(w2), wspec(b2)],
        out_specs=pl.BlockSpec((MOE_ROWS, D_MODEL), lambda j, be, bv: (j, 0)),
        scratch_shapes=[pltpu.VMEM(w1.shape[1:], BF16), pltpu.VMEM(w2.shape[1:], BF16),
                        pltpu.VMEM((MOE_ROWS, w2.shape[1]), BF16)],
    )
    return pl.pallas_call(
        _experts_body,
        grid_spec=grid_spec,
        out_shape=jax.ShapeDtypeStruct(xs.shape, F32),
        compiler_params=_cparams(("arbitrary",)),
        name="experts",
    )(block_expert, block_valid, xs, w1, perm, b1p, w2, b2)


COMBINE_TM = 256
RUN_UNIT = 16
RUN_BITS = tuple(1 << b for b in reversed(range((COMBINE_TM // RUN_UNIT).bit_length())))
BUF_ROWS = -(-(TOP_K * COMBINE_TM + N_EXPERTS * (RUN_UNIT - 1 + SUBLANES - 1)) // LANES) * LANES


def _combine_meta(idx_all, slot_all, pad_start):
    n_tok = idx_all.shape[1]
    tiles = n_tok // COMBINE_TM
    experts = jnp.arange(N_EXPERTS, dtype=jnp.int32)
    hit = idx_all[:TOP_K, :, None] == experts
    cnt = jnp.sum(hit.reshape(TOP_K, tiles, COMBINE_TM, N_EXPERTS), axis=(0, 2), dtype=jnp.int32)
    run_start = pad_start[None, :] + jnp.cumsum(cnt, axis=0) - cnt
    lead = run_start % SUBLANES
    units = jnp.where(cnt > 0, (cnt + lead + RUN_UNIT - 1) // RUN_UNIT, 0)
    first_row = RUN_UNIT * (jnp.cumsum(units, axis=1) - units)
    meta = jnp.concatenate([run_start - lead, units, first_row, jnp.zeros_like(cnt)], axis=1).astype(jnp.int32)
    shift = jnp.repeat(first_row + lead - run_start, COMBINE_TM, axis=0)
    col = slot_all[:TOP_K] + jnp.sum(jnp.where(hit, shift[None], 0), axis=2, dtype=jnp.int32)
    return meta.reshape(tiles, 1, 4 * N_EXPERTS), col.T


def _combine_body(meta_ref, next_ref, col_ref, gate_ref, h_ref, nf_ref, ys_ref, y_ref, buf_ref, sem, *, n):
    i = pl.program_id(0)

    def runs(meta, half, start):
        for e in range(N_EXPERTS):
            src0, units, dst0 = meta[0, 0, e], meta[0, 0, N_EXPERTS + e], meta[0, 0, 2 * N_EXPERTS + e]

            def pieces(bits, done):
                for b in bits:
                    rows = b * RUN_UNIT

                    @pl.when((units & b) != 0)
                    def _piece(done=done, rows=rows):
                        dst = pl.multiple_of(dst0 + done, RUN_UNIT)
                        cp = pltpu.make_async_copy(ys_ref.at[pl.ds(pl.multiple_of(src0 + done, SUBLANES), rows)],
                                                   buf_ref.at[half, pl.ds(dst, rows)], sem.at[half])
                        if start:
                            cp.start()
                        else:
                            cp.wait()
                    done = done + (units & b) * RUN_UNIT
                return done
            done = pieces(RUN_BITS[-2:], jnp.int32(0))

            @pl.when(units >= RUN_BITS[-3])
            def _long_run(done=done):
                pieces(RUN_BITS[:-2], done)

    def per_half(half):
        @pl.when(i + 1 < n)
        def _fetch_next():
            runs(next_ref, 1 - half, True)
        runs(meta_ref, half, False)
        rows = lax.broadcasted_iota(jnp.int32, (COMBINE_TM, BUF_ROWS), 1)
        pick = jnp.zeros((COMBINE_TM, BUF_ROWS), F32)
        for k in range(TOP_K):
            pick = jnp.where(rows == col_ref[:, k:k + 1], gate_ref[:, k:k + 1], pick)
        acc = h_ref[...] + _dot(pick.astype(BF16), buf_ref[half].astype(BF16))
        y_ref[...] = _rms(acc, nf_ref[...])

    @pl.when(i == 0)
    def _first():
        buf_ref[...] = jnp.zeros_like(buf_ref)
        runs(meta_ref, 0, True)

    @pl.when(i % 2 == 0)
    def _even():
        per_half(0)

    @pl.when(i % 2 == 1)
    def _odd():
        per_half(1)


def _combine(meta, col_rows, gates_rows, h, norm_final, ys, first_tile):
    n = h.shape[0]
    tm = COMBINE_TM
    assert n % tm == 0
    last = first_tile + n // tm - 1
    smem_tile = lambda f: pl.BlockSpec((1, 1, meta.shape[2]), lambda i: (f(i), 0, 0), memory_space=pltpu.SMEM)
    return pl.pallas_call(
        functools.partial(_combine_body, n=n // tm),
        grid=(n // tm,),
        in_specs=[smem_tile(lambda i: i + first_tile), smem_tile(lambda i: jnp.minimum(i + first_tile + 1, last)),
                  pl.BlockSpec((tm, TOP_K), lambda i: (i + first_tile, 0)),
                  pl.BlockSpec((tm, TOP_K), lambda i: (i, 0)),
                  pl.BlockSpec((tm, D_MODEL), lambda i: (i, 0)),
                  pl.BlockSpec((1, D_MODEL), lambda i: (0, 0)),
                  pl.BlockSpec(memory_space=pl.ANY)],
        out_specs=pl.BlockSpec((tm, D_MODEL), lambda i: (i, 0)),
        out_shape=jax.ShapeDtypeStruct((n, D_MODEL), F32),
        scratch_shapes=[pltpu.VMEM((2, BUF_ROWS, D_MODEL), F32), pltpu.SemaphoreType.DMA((2,))],
        compiler_params=_cparams(("arbitrary",)),
        name="combine",
    )(meta, meta, col_rows, gates_rows, h, norm_final, ys)


def _expansion(width):
    h = np.arange(LANES)[:, None]
    c = np.arange(SSD_HEADS * width)[None, :] // width
    return jnp.asarray((h == c).astype(np.float32), dtype=BF16)


def _pad_lanes(v):
    return jnp.pad(v.astype(F32), (0, LANES - v.shape[0]))[None, :]


def kernel(x_prompt, x_sample, state_conv, state_ssm, cache_win_k, cache_win_v, norm_mix, w_in, conv_w, conv_b,
           dt_bias, a_log, d_skip, ssd_norm, att_norm, w_out, norm_ffn, router_w, router_b, w1, b1, w2, b2,
           norm_final):
    depth = w_in.shape[0]
    assert depth == 1
    bp, seq, _ = x_prompt.shape
    bs, t_new, _ = x_sample.shape
    n_p, n_s = bp * seq, bs * t_new
    l = 0

    o_dt = SSD_INNER + CONV_DIM
    o_q = o_dt + SSD_HEADS
    wl = w_in[l]
    w_cat = jnp.concatenate([wl[:, :o_dt], wl[:, o_q:], jnp.pad(wl[:, o_dt:o_q], ((0, 0), (0, DT_PAD - SSD_HEADS)))],
                            axis=1).astype(BF16)
    g_mix = norm_mix[l][None, :]
    ssd_consts = (conv_w[l], conv_b[l][None, :], _pad_lanes(dt_bias[l]), _pad_lanes(a_log[l]),
                  jnp.repeat(d_skip[l], SSD_HEAD_DIM)[None, :], ssd_norm[l][None, :],
                  _expansion(SSD_HEAD_DIM))
    slopes = jnp.exp2(-8.0 * jnp.arange(1, ATT_HEADS + 1, dtype=F32) / ATT_HEADS)
    out_consts = (att_norm[l][None, :], w_out[l][:SSD_INNER].astype(BF16), w_out[l][SSD_INNER:].astype(BF16),
                  norm_ffn[l][None, :], router_w[l].T.astype(BF16), router_b[l][:, None].astype(F32))
    b1p = b1[l].reshape(N_EXPERTS, -1, LANES, 2).transpose(0, 1, 3, 2).reshape(N_EXPERTS, 1, -1)
    b2r = b2[l][:, None, :]

    zx, q, k, v, dt, kt, vt = _in_proj(x_prompt.reshape(n_p, D_MODEL), g_mix, w_cat, 512, seq)
    y_p, conv_p, ssm_p = _ssd(zx.reshape(bp, seq, ZX_WIDTH), dt.reshape(bp, seq, DT_PAD), ssd_consts, None, BF16)
    o_p = _attn_prompt(q, k, v, slopes, bp, seq)
    keep = min(DILATED_PATTERNS[-1][0], seq)
    k_p = kt.reshape(bp, ATT_HEADS, ATT_HEAD_DIM, seq).transpose(0, 3, 1, 2)[:, seq - keep:]
    v_p = vt.reshape(bp, ATT_HEADS, ATT_HEAD_DIM, seq).transpose(0, 3, 1, 2)[:, seq - keep:]

    zx_s, q_s, k_s, v_s, dt_s = _in_proj(x_sample.reshape(n_s, D_MODEL), g_mix, w_cat, 256)
    init = (state_conv[l], state_ssm[l].reshape(bs, SSD_INNER, SSD_STATE))
    y_s, conv_s, ssm_s = _ssd(zx_s.reshape(bs, t_new, ZX_WIDTH), dt_s.reshape(bs, t_new, DT_PAD), ssd_consts, init, F32)

    def head_major(a):
        a = a.reshape(bs, t_new, ATT_HEADS, ATT_HEAD_DIM).transpose(0, 2, 1, 3)
        return jnp.pad(a, ((0, 0), (0, 0), (0, DEC_ROWS - t_new), (0, 0)))

    def head_major_t(a):
        a = a.reshape(bs, t_new, ATT_HEADS, ATT_HEAD_DIM).transpose(0, 2, 3, 1)
        return jnp.pad(a, ((0, 0), (0, 0), (0, 0), (0, LANES - t_new)))
    o_s = _attn_decode(head_major(q_s), head_major_t(k_s), head_major_t(v_s),
                       cache_win_k[l].transpose(0, 2, 3, 1), cache_win_v[l].transpose(0, 2, 3, 1), t_new)
    o_s = o_s[:, :, :, :t_new].transpose(0, 3, 1, 2)

    cnt0 = jnp.zeros((N_EXPERTS, LANES), F32)
    h_p, xn_p, idx_p, gate_p, rank_p, cnt_p = _out_proj(x_prompt.reshape(n_p, D_MODEL), y_p.reshape(n_p, SSD_INNER),
                                                        o_p, out_consts, cnt0, 512)
    h_s, xn_s, idx_s, gate_s, rank_s, cnt_all = _out_proj(x_sample.reshape(n_s, D_MODEL), y_s.reshape(n_s, SSD_INNER),
                                                          o_s.reshape(n_s, ATT_WIDTH), out_consts, cnt_p, 512)

    counts = cnt_all[:, 0].astype(jnp.int32)
    padded = (counts + MOE_ROWS - 1) // MOE_ROWS * MOE_ROWS
    pad_end = jnp.cumsum(padded)
    pad_start = pad_end - padded
    n_blocks = -(-((n_p + n_s) * TOP_K) // MOE_ROWS) + N_EXPERTS
    blk0 = jnp.arange(n_blocks, dtype=jnp.int32) * MOE_ROWS
    owner = blk0[:, None] >= pad_end[None, :]
    block_expert = jnp.minimum(jnp.sum(owner, axis=1), N_EXPERTS - 1).astype(jnp.int32)
    onehot = block_expert[:, None] == jnp.arange(N_EXPERTS, dtype=jnp.int32)[None, :]
    used = jnp.sum(jnp.where(onehot, (blk0[:, None] - pad_start[None, :]), 0), axis=1)
    block_valid = jnp.clip(jnp.sum(jnp.where(onehot, counts[None, :], 0), axis=1) - used, 0, MOE_ROWS).astype(jnp.int32)

    tm = min(256, n_s)
    idx_all = jnp.concatenate([idx_p, idx_s], axis=1)
    rank_all = jnp.concatenate([rank_p, rank_s], axis=1)
    first_slot = functools.reduce(lambda acc, e: jnp.where(idx_all == e, pad_start[e], acc), range(N_EXPERTS),
                                  jnp.zeros_like(idx_all))
    slot_all = first_slot + rank_all
    xs = _dispatch(slot_all[:TOP_K].T.reshape(-1), pad_start + counts, padded - counts, pad_end[N_EXPERTS - 1:] // MOE_ROWS,
                   xn_p, xn_s, n_blocks, tm)
    ys = _experts(block_expert, block_valid, xs, w1[l], b1p, w2[l], b2r)
    nfin = norm_final[None, :]
    meta, col_rows = _combine_meta(idx_all, slot_all, pad_start)
    y_prompt = _combine(meta, col_rows, gate_p[:TOP_K].T, h_p, nfin, ys, 0)
    y_sample = _combine(meta, col_rows, gate_s[:TOP_K].T, h_s, nfin, ys, n_p // COMBINE_TM)

    return (y_prompt.reshape(bp, seq, D_MODEL), y_sample.reshape(bs, t_new, D_MODEL),
            conv_p[None], ssm_p.reshape(1, bp, SSD_HEADS, SSD_HEAD_DIM, SSD_STATE), k_p[None], v_p[None],
            conv_s[None], ssm_s.reshape(1, bs, SSD_HEADS, SSD_HEAD_DIM, SSD_STATE),
            k_s.reshape(1, bs, t_new, ATT_HEADS, ATT_HEAD_DIM), v_s.reshape(1, bs, t_new, ATT_HEADS, ATT_HEAD_DIM))
```

```python
import functools

import jax
import jax.numpy as jnp
import numpy as np
from jax import lax
from jax.experimental import pallas as pl
from jax.experimental.pallas import tpu as pltpu

F32 = jnp.float32
BF16 = jnp.bfloat16

D_MODEL = 1024
SSD_HEADS = 16
SSD_HEAD_DIM = 64
SSD_INNER = SSD_HEADS * SSD_HEAD_DIM
SSD_GROUPS = 2
SSD_STATE = 128
SSD_CONV = 4
SSD_CHUNK = 128
CONV_DIM = SSD_INNER + 2 * SSD_GROUPS * SSD_STATE
ATT_HEADS = 8
ATT_HEAD_DIM = 64
ATT_WIDTH = ATT_HEADS * ATT_HEAD_DIM
DILATED_PATTERNS = ((128, 1), (512, 4), (2048, 16))
ATT_BLOCK = 128
N_EXPERTS = 32
TOP_K = 4
SWIGLU_LIMIT = 7.0
SWIGLU_ALPHA = 1.702
NORM_EPS = 1e-5

LANES = 128
SUBLANES = 8
ZX_WIDTH = SSD_INNER + CONV_DIM
DT_PAD = LANES
CONV_HIST = SUBLANES
MOE_ROWS = 512
VMEM_LIMIT = 56 * 1024 * 1024

NEG_INF = float("-inf")


def _cparams(sem):
    return pltpu.CompilerParams(dimension_semantics=sem, vmem_limit_bytes=VMEM_LIMIT)


def _rms(x, g):
    return x * lax.rsqrt(jnp.mean(x * x, axis=-1, keepdims=True) + NORM_EPS) * g


def _dot(a, b):
    return jnp.dot(a, b, preferred_element_type=F32)


def _dot_nt(a, b):
    return lax.dot_general(a, b, (((1,), (1,)), ((), ())), preferred_element_type=F32)


def _split3(v):
    hi = v.astype(BF16)
    r1 = v - hi.astype(F32)
    mid = r1.astype(BF16)
    lo = (r1 - mid.astype(F32)).astype(BF16)
    return hi, mid, lo


def _dot3(v, m):
    hi, mid, lo = _split3(v)
    return _dot(hi, m) + _dot(mid, m) + _dot(lo, m)


def _dot3_lhs(m, v):
    hi, mid, lo = _split3(v)
    return _dot(m, hi) + _dot(m, mid) + _dot(m, lo)


def _silu(x):
    half = 0.5 * x
    return half * (1.0 + jnp.tanh(half))


def _in_proj_body(x_ref, g_ref, w_ref, zx_ref, q_ref, k_ref, v_ref, dt_ref, *t_refs):
    hn = _rms(x_ref[...], g_ref[...]).astype(BF16)
    o = 0
    for ref, width in ((zx_ref, ZX_WIDTH), (q_ref, ATT_WIDTH), (k_ref, ATT_WIDTH), (v_ref, ATT_WIDTH),
                       (dt_ref, DT_PAD)):
        ref[...] = _dot(hn, w_ref[:, o:o + width])
        o += width
    for src, dst in zip((k_ref, v_ref), t_refs):
        for j in range(ATT_WIDTH // LANES):
            dst[0, j * LANES:(j + 1) * LANES, :] = src[:, j * LANES:(j + 1) * LANES].T


def _in_proj(x2d, g, w_cat, tm, seq=None):
    n = x2d.shape[0]
    tm = min(tm, n)
    per = 1 if seq is None else seq // tm
    row = lambda w: pl.BlockSpec((tm, w), lambda b, i: (b * per + i, 0))
    full = lambda a: pl.BlockSpec(a.shape, lambda b, i: (0,) * a.ndim)
    widths = (ZX_WIDTH, ATT_WIDTH, ATT_WIDTH, ATT_WIDTH, DT_PAD)
    out_specs = [row(w) for w in widths]
    out_shape = [jax.ShapeDtypeStruct((n, w), F32) for w in widths]
    if seq is not None:
        out_specs += [pl.BlockSpec((1, ATT_WIDTH, tm), lambda b, i: (b, 0, i))] * 2
        out_shape += [jax.ShapeDtypeStruct((n // seq, ATT_WIDTH, seq), F32)] * 2
    return pl.pallas_call(
        _in_proj_body,
        grid=(n // (tm * per), per),
        in_specs=[row(D_MODEL), full(g), full(w_cat)],
        out_specs=out_specs,
        out_shape=out_shape,
        compiler_params=_cparams(("parallel", "parallel")),
        name="in_proj",
    )(x2d, g, w_cat)


def _ssd_body(*refs, l_blk, has_init):
    T = SSD_CHUNK
    if has_init:
        (zx_ref, dt_ref, cw_ref, cb_ref, dtb_ref, alog_ref, dskip_ref, norm_ref, e64_ref,
         cinit_ref, sinit_ref, y_ref, conv_out_ref, ssm_out_ref,
         ext_ref, act_ref, state_ref, cst_ref, ybuf_ref, zpad_ref, dtpad_ref) = refs
    else:
        (zx_ref, dt_ref, cw_ref, cb_ref, dtb_ref, alog_ref, dskip_ref, norm_ref, e64_ref,
         y_ref, conv_out_ref, ssm_out_ref,
         ext_ref, act_ref, state_ref, cst_ref, ybuf_ref) = refs
    c = pl.program_id(1)
    n_tile = SSD_INNER // LANES

    @pl.when(c == 0)
    def _start():
        if has_init:
            ext_ref[0:CONV_HIST, :] = jnp.zeros((CONV_HIST, CONV_DIM), F32)
            ext_ref[CONV_HIST - (SSD_CONV - 1):CONV_HIST, :] = cinit_ref[0]
            for j in range(n_tile):
                state_ref[:, j * LANES:(j + 1) * LANES] = sinit_ref[0, j * LANES:(j + 1) * LANES, :].T
        else:
            ext_ref[0:CONV_HIST, :] = jnp.zeros((CONV_HIST, CONV_DIM), F32)
            state_ref[...] = jnp.zeros_like(state_ref)

    if l_blk == T:
        ext_ref[CONV_HIST:CONV_HIST + T, :] = zx_ref[0, :, SSD_INNER:ZX_WIDTH]
        z_of = lambda sl: zx_ref[0, :, sl]
        dt_raw = dt_ref[0]
    else:
        ext_ref[CONV_HIST:CONV_HIST + T, :] = jnp.zeros((T, CONV_DIM), F32)
        ext_ref[CONV_HIST:CONV_HIST + l_blk, :] = zx_ref[0, :, SSD_INNER:ZX_WIDTH]
        zpad_ref[...] = jnp.zeros_like(zpad_ref)
        zpad_ref[0:l_blk, :] = zx_ref[0, :, 0:SSD_INNER]
        dtpad_ref[...] = jnp.zeros_like(dtpad_ref)
        dtpad_ref[0:l_blk, :] = dt_ref[0]
        z_of = lambda sl: zpad_ref[:, sl]
        dt_raw = dtpad_ref[...]

    cw = CONV_DIM // 3
    for cc in range(3):
        sl = slice(cc * cw, (cc + 1) * cw)
        acc = cb_ref[:, sl]
        for j in range(SSD_CONV):
            o = CONV_HIST - (SSD_CONV - 1) + j
            acc = acc + ext_ref[o:o + T, sl] * cw_ref[j:j + 1, sl]
        act_ref[:, sl] = _silu(acc)

    row = lax.broadcasted_iota(jnp.int32, (T, LANES), 0)
    col = lax.broadcasted_iota(jnp.int32, (T, LANES), 1)
    tri = row >= col
    tri_bf = jnp.where(tri, 1.0, 0.0).astype(BF16)
    even = col < SSD_HEAD_DIM

    xdt = dt_raw + dtb_ref[...]
    dtv = jnp.maximum(xdt, 0.0) + jnp.log1p(jnp.exp(-jnp.abs(xdt)))
    if l_blk < T:
        dtv = jnp.where(row < l_blk, dtv, 0.0)
    d_a = dtv * (-jnp.exp(alog_ref[...]))
    cs = _dot3_lhs(tri_bf, d_a)
    cst_ref[...] = cs.T
    ex = _dot3(jnp.concatenate([dtv, cs], axis=0), e64_ref[...])
    dt_ex, cs_ex = ex[0:T], ex[T:2 * T]
    ecs_ex = jnp.exp(cs_ex)
    dd_ex = dt_ex * jnp.exp(cs_ex[T - 1:T] - cs_ex)

    gw = SSD_INNER // SSD_GROUPS
    heads_per_group = SSD_HEADS // SSD_GROUPS
    for g in range(SSD_GROUPS):
        gsl = slice(g * gw, (g + 1) * gw)
        b_g = act_ref[:, SSD_INNER + g * SSD_STATE:SSD_INNER + (g + 1) * SSD_STATE]
        c_off = SSD_INNER + SSD_GROUPS * SSD_STATE
        c_g = act_ref[:, c_off + g * SSD_STATE:c_off + (g + 1) * SSD_STATE].astype(BF16)
        cb = _dot_nt(c_g, b_g.astype(BF16))
        b_gt = b_g.T.astype(BF16)
        x_g = act_ref[:, gsl]
        x_dt = (x_g * dt_ex[:, gsl]).astype(BF16)
        x_dd = (x_g * dd_ex[:, gsl]).astype(BF16)
        st_old = state_ref[:, gsl]
        y_off = _dot(c_g, st_old.astype(BF16)) * ecs_ex[:, gsl]
        state_ref[:, gsl] = st_old * ecs_ex[T - 1:T, gsl] + _dot(b_gt, x_dd)
        for jp in range(heads_per_group // 2):
            h0 = g * heads_per_group + 2 * jp
            psl = slice(jp * LANES, (jp + 1) * LANES)
            osl = slice(g * gw + jp * LANES, g * gw + (jp + 1) * LANES)
            pair = cs_ex[:, osl]
            swapped = pltpu.roll(pair, SSD_HEAD_DIM, axis=1)
            cols = (jnp.where(even, pair, swapped), jnp.where(even, swapped, pair))
            yd = []
            for col, h in zip(cols, (h0, h0 + 1)):
                seg = col - cst_ref[h:h + 1, :]
                lmat = jnp.exp(jnp.where(tri, seg, NEG_INF))
                yd.append(_dot((cb * lmat).astype(BF16), x_dt[:, psl]))
            y_pair = jnp.where(even, yd[0], yd[1]) + y_off[:, psl]
            ybuf_ref[:, osl] = y_pair + dskip_ref[:, osl] * x_g[:, psl]

    for g in range(SSD_GROUPS):
        gsl = slice(g * gw, (g + 1) * gw)
        yg = ybuf_ref[:, gsl] * _silu(z_of(gsl))
        yn = yg * lax.rsqrt(jnp.mean(yg * yg, axis=-1, keepdims=True) + NORM_EPS) * norm_ref[:, gsl]
        y_ref[0, :, gsl] = yn[0:l_blk].astype(y_ref.dtype)

    @pl.when(c == pl.num_programs(1) - 1)
    def _finish():
        lo = CONV_HIST + l_blk - (SSD_CONV - 1)
        conv_out_ref[0] = ext_ref[lo:lo + SSD_CONV - 1, :]
        for j in range(n_tile):
            ssm_out_ref[0, j * LANES:(j + 1) * LANES, :] = state_ref[:, j * LANES:(j + 1) * LANES].T

    ext_ref[0:CONV_HIST, :] = ext_ref[T:T + CONV_HIST, :]


def _ssd(zx3, dt3, consts, init, y_dtype):
    b, L, _ = zx3.shape
    T = SSD_CHUNK
    l_blk = T if L % T == 0 else L
    n_chunks = L // l_blk
    has_init = init is not None
    full = lambda a: pl.BlockSpec(a.shape, lambda i, c: (0,) * a.ndim)
    in_specs = [pl.BlockSpec((1, l_blk, ZX_WIDTH), lambda i, c: (i, c, 0)),
                pl.BlockSpec((1, l_blk, DT_PAD), lambda i, c: (i, c, 0))] + [full(a) for a in consts]
    args = [zx3, dt3, *consts]
    scratch = [pltpu.VMEM((T + CONV_HIST, CONV_DIM), F32),
               pltpu.VMEM((T, CONV_DIM), F32),
               pltpu.VMEM((SSD_STATE, SSD_INNER), F32),
               pltpu.VMEM((LANES, T), F32),
               pltpu.VMEM((T, SSD_INNER), F32)]
    if has_init:
        in_specs += [pl.BlockSpec((1, SSD_CONV - 1, CONV_DIM), lambda i, c: (i, 0, 0)),
                     pl.BlockSpec((1, SSD_INNER, SSD_STATE), lambda i, c: (i, 0, 0))]
        args += list(init)
        scratch += [pltpu.VMEM((T, SSD_INNER), F32), pltpu.VMEM((T, DT_PAD), F32)]
    return pl.pallas_call(
        functools.partial(_ssd_body, l_blk=l_blk, has_init=has_init),
        grid=(b, n_chunks),
        in_specs=in_specs,
        out_specs=[pl.BlockSpec((1, l_blk, SSD_INNER), lambda i, c: (i, c, 0)),
                   pl.BlockSpec((1, SSD_CONV - 1, CONV_DIM), lambda i, c: (i, 0, 0)),
                   pl.BlockSpec((1, SSD_INNER, SSD_STATE), lambda i, c: (i, 0, 0))],
        out_shape=[jax.ShapeDtypeStruct((b, L, SSD_INNER), y_dtype),
                   jax.ShapeDtypeStruct((b, SSD_CONV - 1, CONV_DIM), F32),
                   jax.ShapeDtypeStruct((b, SSD_INNER, SSD_STATE), F32)],
        scratch_shapes=scratch,
        compiler_params=_cparams(("parallel", "arbitrary")),
        name="ssd_init" if has_init else "ssd",
    )(*args)


ATT_UNROLL = 16


def _unroll(n):
    return max(d for d in range(1, ATT_UNROLL + 1) if n % d == 0)


ATT_RES = DILATED_PATTERNS[-1][1]


def _attn_tables():
    B = ATT_BLOCK
    t1 = np.full((len(DILATED_PATTERNS), B, B), -np.inf, np.float32)
    t2 = np.full((len(DILATED_PATTERNS), B, 2 * B), -np.inf, np.float32)
    for p, (window, dil) in enumerate(DILATED_PATTERNS):
        m = ATT_RES // dil
        w = B // m
        rho = np.arange(B)
        c = m * (rho % w) + rho // w
        ck2 = np.concatenate([c - B, c])
        for tab, ck in ((t1, c), (t2, ck2)):
            delta = c[:, None] - ck[None, :]
            ok = (delta >= 0) & (delta <= window // dil)
            tab[p] = np.where(ok, -(dil * delta).astype(np.float32), -np.inf)
    return jnp.asarray(t1), jnp.asarray(t2)


def _attn_body(slopes_ref, t1_ref, t2_ref, q_ref, k_ref, v_ref, o_ref, qd_ref, kd_ref, vd_ref, op_ref, lp_ref,
               stage_ref, b1_ref, b2_ref, in_sem, out_sem, *, seq):
    B = ATT_BLOCK
    hp = pl.program_id(1)
    even = lax.broadcasted_iota(jnp.int32, (B, LANES), 1) < ATT_HEAD_DIM
    scale = ATT_HEAD_DIM ** -0.5

    def in_copy(a, r):
        src, dst = ((q_ref, qd_ref), (k_ref, kd_ref), (v_ref, vd_ref))[a]
        return pltpu.make_async_copy(src.at[:, r, :], dst.at[r], in_sem.at[a, r])
    for r in range(ATT_RES):
        for a in range(3):
            in_copy(a, r).start()

    for e in range(2):
        slope = slopes_ref[2 * hp + e]
        for p in range(len(DILATED_PATTERNS)):
            b1_ref[e * len(DILATED_PATTERNS) + p] = t1_ref[p] * slope
            b2_ref[e * len(DILATED_PATTERNS) + p] = t2_ref[p] * slope

    for r in range(ATT_RES):
        for a in range(3):
            in_copy(a, r).wait()

    def block(p, r_d, n, first):
        dil = DILATED_PATTERNS[p][1]
        m = ATT_RES // dil
        w = B // m

        def slab(j, nblk):
            start = nblk * w
            return r_d + dil * j, pl.ds(start if isinstance(start, int) else pl.multiple_of(start, w), w)

        def gather(src, nblk):
            parts = []
            for j in range(m):
                r, rows = slab(j, nblk)
                parts.append(src[r, rows, :])
            return parts[0] if m == 1 else jnp.concatenate(parts, axis=0)

        qb = gather(qd_ref, n) * scale
        if first:
            kb, vb = gather(kd_ref, n), gather(vd_ref, n)
            bias = lambda e: b1_ref[e * len(DILATED_PATTERNS) + p]
        else:
            kb = jnp.concatenate([gather(kd_ref, n - 1), gather(kd_ref, n)], axis=0)
            vb = jnp.concatenate([gather(vd_ref, n - 1), gather(vd_ref, n)], axis=0)
            bias = lambda e: b2_ref[e * len(DILATED_PATTERNS) + p]
        q2 = jnp.concatenate([jnp.where(even, qb, 0.0), jnp.where(even, 0.0, qb)], axis=0).astype(BF16)
        s = _dot_nt(q2, kb.astype(BF16)) + jnp.concatenate([bias(0), bias(1)], axis=0)
        mx = jnp.max(s, axis=-1, keepdims=True)
        pr = jnp.exp(s - mx)
        l = jnp.sum(pr, axis=-1, keepdims=True)
        o2 = _dot(pr.astype(BF16), vb.astype(BF16)) / l
        lse2 = jnp.broadcast_to(mx + jnp.log(l), (2 * B, LANES))
        o = jnp.where(even, o2[:B], o2[B:])
        lse = jnp.where(even, lse2[:B], lse2[B:])
        for j in range(m):
            r, rows = slab(j, n)
            op_ref[p, r, rows, :] = o[j * w:(j + 1) * w]
            lp_ref[p, r, rows, :] = lse[j * w:(j + 1) * w]

    for p, (_, dil) in enumerate(DILATED_PATTERNS):
        nb = seq // (dil * B)
        u_first = _unroll(dil)

        def first(i, carry, p=p, u_first=u_first):
            for u in range(u_first):
                block(p, i * u_first + u, 0, True)
            return carry
        lax.fori_loop(0, dil // u_first, first, 0)
        n_later = dil * (nb - 1)
        if n_later:
            u_later = _unroll(n_later)

            def later(i, carry, p=p, nb=nb, u_later=u_later):
                for u in range(u_later):
                    j = i * u_later + u
                    block(p, j // (nb - 1), j % (nb - 1) + 1, False)
                return carry
            lax.fori_loop(0, n_later // u_later, later, 0)

    def out_copy(r):
        return pltpu.make_async_copy(stage_ref.at[r], o_ref.at[:, r, :], out_sem.at[r])
    for r in range(ATT_RES):
        l0, l1, l2 = lp_ref[0, r], lp_ref[1, r], lp_ref[2, r]
        m = jnp.maximum(jnp.maximum(l0, l1), l2)
        w0, w1, w2 = jnp.exp(l0 - m), jnp.exp(l1 - m), jnp.exp(l2 - m)
        num = w0 * op_ref[0, r] + w1 * op_ref[1, r] + w2 * op_ref[2, r]
        stage_ref[r] = num / (w0 + w1 + w2)
        out_copy(r).start()
    for r in range(ATT_RES):
        out_copy(r).wait()


def _attn_prompt(q, k, v, slopes, batch, seq):
    n_hp = ATT_HEADS // 2
    n_pat = len(DILATED_PATTERNS)
    assert len(DILATED_PATTERNS) == 3 and all(ATT_RES % d == 0 and seq % (d * ATT_BLOCK) == 0 and w // d == ATT_BLOCK
                                              for w, d in DILATED_PATTERNS)
    per = seq // ATT_RES
    by_res = lambda a: a.reshape(batch * per, ATT_RES, ATT_WIDTH)
    blk = pl.BlockSpec((per, ATT_RES, LANES), lambda b, h: (b, 0, h))
    t1, t2 = _attn_tables()
    full = lambda a: pl.BlockSpec(a.shape, lambda b, h: (0,) * a.ndim)
    res = pltpu.VMEM((ATT_RES, per, LANES), F32)
    out = pl.pallas_call(
        functools.partial(_attn_body, seq=seq),
        grid=(batch, n_hp),
        in_specs=[pl.BlockSpec(memory_space=pltpu.SMEM), full(t1), full(t2), blk, blk, blk],
        out_specs=blk,
        out_shape=jax.ShapeDtypeStruct((batch * per, ATT_RES, ATT_WIDTH), F32),
        scratch_shapes=[res, res, res,
                        pltpu.VMEM((n_pat, ATT_RES, per, LANES), F32), pltpu.VMEM((n_pat, ATT_RES, per, LANES), F32),
                        res,
                        pltpu.VMEM((2 * n_pat, ATT_BLOCK, ATT_BLOCK), F32),
                        pltpu.VMEM((2 * n_pat, ATT_BLOCK, 2 * ATT_BLOCK), F32),
                        pltpu.SemaphoreType.DMA((3, ATT_RES)), pltpu.SemaphoreType.DMA((ATT_RES,))],
        compiler_params=_cparams(("parallel", "parallel")),
        name="attn_prompt",
    )(slopes, t1, t2, by_res(q), by_res(k), by_res(v))
    return out.reshape(batch * seq, ATT_WIDTH)


DEC_ROWS = SUBLANES


def _decode_tables(t_new, m_cache):
    slopes = 2.0 ** (-8.0 * np.arange(1, ATT_HEADS + 1) / ATT_HEADS)

    def mult(dist):
        return sum(1 for window, dil in DILATED_PATTERNS if 0 <= dist <= window and dist % dil == 0)

    bias_c = np.zeros((ATT_HEADS, DEC_ROWS, m_cache), np.float32)
    mult_c = np.ones((DEC_ROWS, m_cache), np.float32)
    bias_n = np.full((ATT_HEADS, DEC_ROWS, LANES), -np.inf, np.float32)
    mult_n = np.zeros((DEC_ROWS, LANES), np.float32)
    for t in range(t_new):
        dist = m_cache + t - np.arange(m_cache)
        mu = np.array([mult(d) for d in dist], np.float32)
        mult_c[t] = mu
        bias_c[:, t, :] = np.where(mu > 0, -slopes[:, None] * dist[None, :], -np.inf)
        for t2 in range(t_new):
            if mult(t - t2):
                bias_n[:, t, t2] = -slopes * (t - t2)
                mult_n[t, t2] = mult(t - t2)
    bias_n[:, t_new:, 0] = 0.0
    mult_n[t_new:, 0] = 1.0
    return jnp.asarray(bias_c), jnp.asarray(mult_c), jnp.asarray(bias_n), jnp.asarray(mult_n)


def _attn_dec_body(q_ref, knt_ref, vnt_ref, ck_ref, cv_ref, bc_ref, mc_ref, bn_ref, mn_ref, o_ref):
    scale = ATT_HEAD_DIM ** -0.5
    for h in range(ATT_HEADS):
        qh = (q_ref[0, h] * scale).astype(BF16)
        sc = _dot(qh, ck_ref[0, h].astype(BF16)) + bc_ref[h]
        sn = _dot(qh, knt_ref[0, h].astype(BF16)) + bn_ref[h]
        m = jnp.maximum(jnp.max(sc, axis=-1, keepdims=True), jnp.max(sn, axis=-1, keepdims=True))
        pc = jnp.exp(sc - m) * mc_ref[...]
        pn = jnp.exp(sn - m) * mn_ref[...]
        inv_l = 1.0 / (jnp.sum(pc, axis=-1, keepdims=True) + jnp.sum(pn, axis=-1, keepdims=True))
        o_ref[0, h] = (_dot_nt(cv_ref[0, h].astype(BF16), (pc * inv_l).astype(BF16))
                       + _dot_nt(vnt_ref[0, h].astype(BF16), (pn * inv_l).astype(BF16)))


def _attn_decode(q4, knt, vnt, cache_kt, cache_vt, t_new):
    b = q4.shape[0]
    m_cache = cache_kt.shape[3]
    assert t_new <= DEC_ROWS
    tables = _decode_tables(t_new, m_cache)
    per_b = lambda a: pl.BlockSpec((1,) + a.shape[1:], lambda i: (i, 0, 0, 0))
    full = lambda a: pl.BlockSpec(a.shape, lambda i: (0,) * a.ndim)
    out_shape = jax.ShapeDtypeStruct((b, ATT_HEADS, ATT_HEAD_DIM, DEC_ROWS), F32)
    return pl.pallas_call(
        _attn_dec_body,
        grid=(b,),
        in_specs=[per_b(a) for a in (q4, knt, vnt, cache_kt, cache_vt)] + [full(a) for a in tables],
        out_specs=per_b(out_shape),
        out_shape=out_shape,
        compiler_params=_cparams(("parallel",)),
        name="attn_decode",
    )(q4, knt, vnt, cache_kt, cache_vt, *tables)


def _out_proj_body(x_ref, y_ref, o_ref, an_ref, wy_ref, wo_ref, nf_ref, rw_ref, rb_ref, cnt0_ref,
                   h_ref, xn_ref, idx_ref, gate_ref, rank_ref, cnt_ref, carry_ref, *, tm):
    i = pl.program_id(0)

    @pl.when(i == 0)
    def _start():
        carry_ref[...] = cnt0_ref[...]

    on = _rms(o_ref[...], an_ref[...]).astype(BF16)
    mixed = _dot(y_ref[...].astype(BF16), wy_ref[...]) + _dot(on, wo_ref[...])
    h = x_ref[...] + mixed
    h_ref[...] = h
    xn = _rms(h, nf_ref[...])
    xn_ref[...] = xn
    logits = _dot_nt(rw_ref[...], xn.astype(BF16)) + rb_ref[...]

    e_iota = lax.broadcasted_iota(jnp.int32, (N_EXPERTS, tm), 0)
    vals, idxs, sels = [], [], []
    cur = logits
    for _ in range(TOP_K):
        mx = jnp.max(cur, axis=0, keepdims=True)
        ix = jnp.min(jnp.where(cur == mx, e_iota, N_EXPERTS), axis=0, keepdims=True)
        sel = e_iota == ix
        cur = jnp.where(sel, NEG_INF, cur)
        vals.append(mx)
        idxs.append(ix)
        sels.append(sel)
    ex = [jnp.exp(v - vals[0]) for v in vals]
    den = functools.reduce(lambda a, b: a + b, ex)

    sel_any = functools.reduce(jnp.logical_or, sels)
    sel_f = jnp.where(sel_any, 1.0, 0.0)
    r = lax.broadcasted_iota(jnp.int32, (tm, tm), 0)
    c = lax.broadcasted_iota(jnp.int32, (tm, tm), 1)
    upper = jnp.where(r <= c, 1.0, 0.0).astype(BF16)
    cum = _dot(sel_f.astype(BF16), upper)
    before = carry_ref[:, 0:1] + cum - sel_f
    for k in range(TOP_K):
        idx_ref[k:k + 1, :] = idxs[k]
        gate_ref[k:k + 1, :] = ex[k] / den
        rank_ref[k:k + 1, :] = jnp.sum(jnp.where(sels[k], before, 0.0), axis=0, keepdims=True).astype(jnp.int32)
    idx_ref[TOP_K:SUBLANES, :] = jnp.zeros((SUBLANES - TOP_K, tm), jnp.int32)
    gate_ref[TOP_K:SUBLANES, :] = jnp.zeros((SUBLANES - TOP_K, tm), F32)
    rank_ref[TOP_K:SUBLANES, :] = jnp.zeros((SUBLANES - TOP_K, tm), jnp.int32)
    carry_ref[...] = carry_ref[...] + jnp.max(cum, axis=1, keepdims=True)
    cnt_ref[...] = carry_ref[...]


def _out_proj(x2d, y2d, o2d, consts, cnt0, tm):
    n = x2d.shape[0]
    tm = min(tm, n)
    row = lambda w: pl.BlockSpec((tm, w), lambda i: (i, 0))
    colb = pl.BlockSpec((SUBLANES, tm), lambda i: (0, i))
    full = lambda a: pl.BlockSpec(a.shape, lambda i: (0,) * a.ndim)
    return pl.pallas_call(
        functools.partial(_out_proj_body, tm=tm),
        grid=(n // tm,),
        in_specs=[row(D_MODEL), row(SSD_INNER), row(ATT_WIDTH)] + [full(a) for a in consts] + [full(cnt0)],
        out_specs=[row(D_MODEL), row(D_MODEL), colb, colb, colb, full(cnt0)],
        out_shape=[jax.ShapeDtypeStruct((n, D_MODEL), F32), jax.ShapeDtypeStruct((n, D_MODEL), F32),
                   jax.ShapeDtypeStruct((SUBLANES, n), jnp.int32), jax.ShapeDtypeStruct((SUBLANES, n), F32),
                   jax.ShapeDtypeStruct((SUBLANES, n), jnp.int32), jax.ShapeDtypeStruct(cnt0.shape, F32)],
        scratch_shapes=[pltpu.VMEM(cnt0.shape, F32)],
        compiler_params=_cparams(("arbitrary",)),
        name="out_proj",
    )(x2d, y2d, o2d, *consts, cnt0)


ISSUE_UNROLL = 8


def _dispatch_body(slot_ref, from_ref, n_ref, tail_ref, xa_ref, xb_ref, xs_ref, zero_ref, stage_ref, sem, row_sems,
                   *, tm, tiles_a, tiles_b, n_blocks):
    i = pl.program_id(0)

    @pl.when(i == 0)
    def _pad_fill():
        zero_ref[...] = jnp.zeros_like(zero_ref)
        one_row = lambda slot: pltpu.make_async_copy(zero_ref.at[pl.ds(0, 1)], xs_ref.at[pl.ds(slot, 1)], sem)
        block = lambda j: pltpu.make_async_copy(zero_ref, xs_ref.at[pl.ds(j * MOE_ROWS, MOE_ROWS)], sem)

        def per_expert(e, carry):
            def start(r, c):
                one_row(from_ref[e] + r).start()
                return c

            def wait(r, c):
                one_row(0).wait()
                return c
            lax.fori_loop(0, n_ref[e], start, 0)
            lax.fori_loop(0, n_ref[e], wait, 0)
            return carry
        lax.fori_loop(0, N_EXPERTS, per_expert, 0)

        def tail(j, carry):
            block(j).start()
            block(j).wait()
            return carry
        lax.fori_loop(tail_ref[0], n_blocks, tail, 0)

    def drain(parity):
        for _ in range(TOP_K):
            pltpu.make_async_copy(stage_ref.at[parity], xs_ref.at[pl.ds(0, tm)], row_sems.at[parity]).wait()

    for parity in range(2):
        @pl.when(i % 2 == parity)
        def _step(parity=parity):
            @pl.when(i < tiles_a)
            def _first_group():
                stage_ref[parity] = xa_ref[...]

            @pl.when(i >= tiles_a)
            def _second_group():
                stage_ref[parity] = xb_ref[...]

            def issue(j, carry):
                t0 = pl.multiple_of(j * ISSUE_UNROLL, ISSUE_UNROLL)
                for u in range(ISSUE_UNROLL):
                    for k in range(TOP_K):
                        slot = slot_ref[t0 * TOP_K + (u * TOP_K + k)]
                        pltpu.make_async_copy(stage_ref.at[parity, pl.ds(t0 + u, 1)],
                                              xs_ref.at[pl.ds(slot, 1)], row_sems.at[parity]).start()
                return carry
            lax.fori_loop(0, tm // ISSUE_UNROLL, issue, 0)

            @pl.when(i > 0)
            def _previous():
                drain(1 - parity)

            @pl.when(i == tiles_a + tiles_b - 1)
            def _last():
                drain(parity)


def _dispatch(slot, pad_from, pad_n, tail_block, xa, xb, n_blocks, tm):
    tiles_a, tiles_b = xa.shape[0] // tm, xb.shape[0] // tm
    assert tiles_a * tm == xa.shape[0] and tiles_b * tm == xb.shape[0]
    smem = pl.BlockSpec(memory_space=pltpu.SMEM)
    hbm = pl.BlockSpec(memory_space=pl.ANY)
    return pl.pallas_call(
        functools.partial(_dispatch_body, tm=tm, tiles_a=tiles_a, tiles_b=tiles_b, n_blocks=n_blocks),
        grid=(tiles_a + tiles_b,),
        in_specs=[pl.BlockSpec((tm * TOP_K,), lambda i: (i,), memory_space=pltpu.SMEM), smem, smem, smem,
                  pl.BlockSpec((tm, D_MODEL), lambda i: (jnp.minimum(i, tiles_a - 1), 0)),
                  pl.BlockSpec((tm, D_MODEL), lambda i: (jnp.maximum(i - tiles_a, 0), 0))],
        out_specs=hbm,
        out_shape=jax.ShapeDtypeStruct((n_blocks * MOE_ROWS, D_MODEL), F32),
        scratch_shapes=[pltpu.VMEM((MOE_ROWS, D_MODEL), F32), pltpu.VMEM((2, tm, D_MODEL), F32),
                        pltpu.SemaphoreType.DMA(()), pltpu.SemaphoreType.DMA((2,))],
        compiler_params=_cparams(("arbitrary",)),
        name="dispatch",
    )(slot, pad_from, pad_n, tail_block, xa, xb)


MOE_COLS = 2 * LANES


def _glu_perm():
    src = np.concatenate([np.arange(0, MOE_COLS, 2), np.arange(1, MOE_COLS, 2)])
    perm = np.zeros((MOE_COLS, MOE_COLS), np.float32)
    perm[src, np.arange(MOE_COLS)] = 1.0
    return jnp.asarray(perm, dtype=BF16)


def _experts_body(be_ref, bv_ref, nx_ref, xs_ref, w1_ref, p_ref, b1_ref, w2_ref, b2_ref, ys_ref,
                  w1f_ref, w1s_ref, w2s_ref, g_ref, sem):
    j = pl.program_id(0)
    valid = bv_ref[j]
    e = be_ref[j]

    @pl.when(jnp.logical_and(valid > 0, jnp.logical_or(j == 0, e != be_ref[jnp.maximum(j - 1, 0)])))
    def _new_expert():
        slot, nxt = nx_ref[1, e], nx_ref[0, e]
        fetch = lambda ex, s: pltpu.make_async_copy(w1_ref.at[ex], w1f_ref.at[s], sem.at[s])

        @pl.when(j == 0)
        def _first_fetch():
            fetch(e, slot).start()
        fetch(e, slot).wait()

        @pl.when(nxt >= 0)
        def _next_fetch():
            fetch(nxt, 1 - slot).start()
        for c in range(w1s_ref.shape[1] // MOE_COLS):
            cols = slice(c * MOE_COLS, (c + 1) * MOE_COLS)
            w1s_ref[:, cols] = _dot(w1f_ref[slot, :, cols].astype(BF16), p_ref[...]).astype(BF16)
        w2s_ref[...] = w2_ref[0].astype(BF16)

    @pl.when(valid > 0)
    def _compute():
        x = xs_ref[...].astype(BF16)
        for c in range(g_ref.shape[1] // LANES):
            cols = slice(c * MOE_COLS, (c + 1) * MOE_COLS)
            hc = _dot(x, w1s_ref[:, cols]) + b1_ref[0, :, cols]
            glu = jnp.minimum(hc[:, :LANES], SWIGLU_LIMIT)
            lin = jnp.clip(hc[:, LANES:], -SWIGLU_LIMIT, SWIGLU_LIMIT)
            g_ref[:, c * LANES:(c + 1) * LANES] = (glu * jax.nn.sigmoid(SWIGLU_ALPHA * glu) * (lin + 1.0)).astype(BF16)
        ys_ref[...] = _dot(g_ref[...], w2s_ref[...]) + b2_ref[0]

    @pl.when(valid <= 0)
    def _empty():
        ys_ref[...] = jnp.zeros_like(ys_ref)


def _experts(block_expert, block_valid, counts, xs, w1, b1p, w2, b2):
    n_blocks = xs.shape[0] // MOE_ROWS
    perm = _glu_perm()
    active = counts > 0
    ids = jnp.arange(N_EXPERTS, dtype=jnp.int32)
    later = jnp.where(active[None, :] & (ids[None, :] > ids[:, None]), ids[None, :], N_EXPERTS)
    nxt = jnp.min(later, axis=1)
    nx = jnp.stack([jnp.where(nxt < N_EXPERTS, nxt, -1), (jnp.cumsum(active) - 1) % 2]).astype(jnp.int32)
    wspec = lambda a: pl.BlockSpec((1,) + a.shape[1:], lambda j, be, bv, nx: (be[j], 0, 0))
    grid_spec = pltpu.PrefetchScalarGridSpec(
        num_scalar_prefetch=3,
        grid=(n_blocks,),
        in_specs=[pl.BlockSpec((MOE_ROWS, D_MODEL), lambda j, be, bv, nx: (j, 0)),
                  pl.BlockSpec(memory_space=pl.ANY), pl.BlockSpec(perm.shape, lambda j, be, bv, nx: (0, 0)),
                  wspec(b1p), wspec(w2), wspec(b2)],
        out_specs=pl.BlockSpec((MOE_ROWS, D_MODEL), lambda j, be, bv, nx: (j, 0)),
        scratch_shapes=[pltpu.VMEM((2,) + w1.shape[1:], F32), pltpu.VMEM(w1.shape[1:], BF16),
                        pltpu.VMEM(w2.shape[1:], BF16), pltpu.VMEM((MOE_ROWS, w2.shape[1]), BF16),
                        pltpu.SemaphoreType.DMA((2,))],
    )
    return pl.pallas_call(
        _experts_body,
        grid_spec=grid_spec,
        out_shape=jax.ShapeDtypeStruct(xs.shape, F32),
        compiler_params=_cparams(("arbitrary",)),
        name="experts",
    )(block_expert, block_valid, nx, xs, w1, perm, b1p, w2, b2)


COMBINE_TM = 256
RUN_UNIT = 16
RUN_BITS = tuple(1 << b for b in reversed(range((COMBINE_TM // RUN_UNIT).bit_length())))
BUF_ROWS = -(-(TOP_K * COMBINE_TM + N_EXPERTS * (RUN_UNIT - 1 + SUBLANES - 1)) // LANES) * LANES


def _combine_meta(idx_all, slot_all, pad_start):
    n_tok = idx_all.shape[1]
    tiles = n_tok // COMBINE_TM
    experts = jnp.arange(N_EXPERTS, dtype=jnp.int32)
    hit = idx_all[:TOP_K, :, None] == experts
    cnt = jnp.sum(hit.reshape(TOP_K, tiles, COMBINE_TM, N_EXPERTS), axis=(0, 2), dtype=jnp.int32)
    run_start = pad_start[None, :] + jnp.cumsum(cnt, axis=0) - cnt
    lead = run_start % SUBLANES
    units = jnp.where(cnt > 0, (cnt + lead + RUN_UNIT - 1) // RUN_UNIT, 0)
    first_row = RUN_UNIT * (jnp.cumsum(units, axis=1) - units)
    meta = jnp.concatenate([run_start - lead, units, first_row, jnp.zeros_like(cnt)], axis=1).astype(jnp.int32)
    shift = jnp.repeat(first_row + lead - run_start, COMBINE_TM, axis=0)
    col = slot_all[:TOP_K] + jnp.sum(jnp.where(hit, shift[None], 0), axis=2, dtype=jnp.int32)
    return meta.reshape(tiles, 1, 4 * N_EXPERTS), col.T


def _combine_body(meta_ref, next_ref, col_ref, gate_ref, h_ref, nf_ref, ys_ref, y_ref, buf_ref, sem, *, n):
    i = pl.program_id(0)

    def runs(meta, half, start):
        for e in range(N_EXPERTS):
            src0, units, dst0 = meta[0, 0, e], meta[0, 0, N_EXPERTS + e], meta[0, 0, 2 * N_EXPERTS + e]

            def pieces(bits, done):
                for b in bits:
                    rows = b * RUN_UNIT

                    @pl.when((units & b) != 0)
                    def _piece(done=done, rows=rows):
                        dst = pl.multiple_of(dst0 + done, RUN_UNIT)
                        cp = pltpu.make_async_copy(ys_ref.at[pl.ds(pl.multiple_of(src0 + done, SUBLANES), rows)],
                                                   buf_ref.at[half, pl.ds(dst, rows)], sem.at[half])
                        if start:
                            cp.start()
                        else:
                            cp.wait()
                    done = done + (units & b) * RUN_UNIT
                return done
            done = pieces(RUN_BITS[-2:], jnp.int32(0))

            @pl.when(units >= RUN_BITS[-3])
            def _long_run(done=done):
                pieces(RUN_BITS[:-2], done)

    def per_half(half):
        @pl.when(i + 1 < n)
        def _fetch_next():
            runs(next_ref, 1 - half, True)
        runs(meta_ref, half, False)
        rows = lax.broadcasted_iota(jnp.int32, (COMBINE_TM, BUF_ROWS), 1)
        pick = jnp.zeros((COMBINE_TM, BUF_ROWS), F32)
        for k in range(TOP_K):
            pick = jnp.where(rows == col_ref[:, k:k + 1], gate_ref[:, k:k + 1], pick)
        acc = h_ref[...] + _dot(pick.astype(BF16), buf_ref[half].astype(BF16))
        y_ref[...] = _rms(acc, nf_ref[...])

    @pl.when(i == 0)
    def _first():
        buf_ref[...] = jnp.zeros_like(buf_ref)
        runs(meta_ref, 0, True)

    @pl.when(i % 2 == 0)
    def _even():
        per_half(0)

    @pl.when(i % 2 == 1)
    def _odd():
        per_half(1)


def _combine(meta, col_rows, gates_rows, h, norm_final, ys, first_tile):
    n = h.shape[0]
    tm = COMBINE_TM
    assert n % tm == 0
    last = first_tile + n // tm - 1
    smem_tile = lambda f: pl.BlockSpec((1, 1, meta.shape[2]), lambda i: (f(i), 0, 0), memory_space=pltpu.SMEM)
    return pl.pallas_call(
        functools.partial(_combine_body, n=n // tm),
        grid=(n // tm,),
        in_specs=[smem_tile(lambda i: i + first_tile), smem_tile(lambda i: jnp.minimum(i + first_tile + 1, last)),
                  pl.BlockSpec((tm, TOP_K), lambda i: (i + first_tile, 0)),
                  pl.BlockSpec((tm, TOP_K), lambda i: (i, 0)),
                  pl.BlockSpec((tm, D_MODEL), lambda i: (i, 0)),
                  pl.BlockSpec((1, D_MODEL), lambda i: (0, 0)),
                  pl.BlockSpec(memory_space=pl.ANY)],
        out_specs=pl.BlockSpec((tm, D_MODEL), lambda i: (i, 0)),
        out_shape=jax.ShapeDtypeStruct((n, D_MODEL), F32),
        scratch_shapes=[pltpu.VMEM((2, BUF_ROWS, D_MODEL), F32), pltpu.SemaphoreType.DMA((2,))],
        compiler_params=_cparams(("arbitrary",)),
        name="combine",
    )(meta, meta, col_rows, gates_rows, h, norm_final, ys)


def _expansion(width):
    h = np.arange(LANES)[:, None]
    c = np.arange(SSD_HEADS * width)[None, :] // width
    return jnp.asarray((h == c).astype(np.float32), dtype=BF16)


def _pad_lanes(v):
    return jnp.pad(v.astype(F32), (0, LANES - v.shape[0]))[None, :]


def kernel(x_prompt, x_sample, state_conv, state_ssm, cache_win_k, cache_win_v, norm_mix, w_in, conv_w, conv_b,
           dt_bias, a_log, d_skip, ssd_norm, att_norm, w_out, norm_ffn, router_w, router_b, w1, b1, w2, b2,
           norm_final):
    depth = w_in.shape[0]
    assert depth == 1
    bp, seq, _ = x_prompt.shape
    bs, t_new, _ = x_sample.shape
    n_p, n_s = bp * seq, bs * t_new
    l = 0

    o_dt = SSD_INNER + CONV_DIM
    o_q = o_dt + SSD_HEADS
    wl = w_in[l]
    w_cat = jnp.concatenate([wl[:, :o_dt], wl[:, o_q:], jnp.pad(wl[:, o_dt:o_q], ((0, 0), (0, DT_PAD - SSD_HEADS)))],
                            axis=1).astype(BF16)
    g_mix = norm_mix[l][None, :]
    ssd_consts = (conv_w[l], conv_b[l][None, :], _pad_lanes(dt_bias[l]), _pad_lanes(a_log[l]),
                  jnp.repeat(d_skip[l], SSD_HEAD_DIM)[None, :], ssd_norm[l][None, :],
                  _expansion(SSD_HEAD_DIM))
    slopes = jnp.exp2(-8.0 * jnp.arange(1, ATT_HEADS + 1, dtype=F32) / ATT_HEADS)
    out_consts = (att_norm[l][None, :], w_out[l][:SSD_INNER].astype(BF16), w_out[l][SSD_INNER:].astype(BF16),
                  norm_ffn[l][None, :], router_w[l].T.astype(BF16), router_b[l][:, None].astype(F32))
    b1p = b1[l].reshape(N_EXPERTS, -1, LANES, 2).transpose(0, 1, 3, 2).reshape(N_EXPERTS, 1, -1)
    b2r = b2[l][:, None, :]

    zx, q, k, v, dt, kt, vt = _in_proj(x_prompt.reshape(n_p, D_MODEL), g_mix, w_cat, 512, seq)
    y_p, conv_p, ssm_p = _ssd(zx.reshape(bp, seq, ZX_WIDTH), dt.reshape(bp, seq, DT_PAD), ssd_consts, None, BF16)
    o_p = _attn_prompt(q, k, v, slopes, bp, seq)
    keep = min(DILATED_PATTERNS[-1][0], seq)
    k_p = kt.reshape(bp, ATT_HEADS, ATT_HEAD_DIM, seq).transpose(0, 3, 1, 2)[:, seq - keep:]
    v_p = vt.reshape(bp, ATT_HEADS, ATT_HEAD_DIM, seq).transpose(0, 3, 1, 2)[:, seq - keep:]

    zx_s, q_s, k_s, v_s, dt_s = _in_proj(x_sample.reshape(n_s, D_MODEL), g_mix, w_cat, 256)
    init = (state_conv[l], state_ssm[l].reshape(bs, SSD_INNER, SSD_STATE))
    y_s, conv_s, ssm_s = _ssd(zx_s.reshape(bs, t_new, ZX_WIDTH), dt_s.reshape(bs, t_new, DT_PAD), ssd_consts, init, F32)

    def head_major(a):
        a = a.reshape(bs, t_new, ATT_HEADS, ATT_HEAD_DIM).transpose(0, 2, 1, 3)
        return jnp.pad(a, ((0, 0), (0, 0), (0, DEC_ROWS - t_new), (0, 0)))

    def head_major_t(a):
        a = a.reshape(bs, t_new, ATT_HEADS, ATT_HEAD_DIM).transpose(0, 2, 3, 1)
        return jnp.pad(a, ((0, 0), (0, 0), (0, 0), (0, LANES - t_new)))
    o_s = _attn_decode(head_major(q_s), head_major_t(k_s), head_major_t(v_s),
                       cache_win_k[l].transpose(0, 2, 3, 1), cache_win_v[l].transpose(0, 2, 3, 1), t_new)
    o_s = o_s[:, :, :, :t_new].transpose(0, 3, 1, 2)

    cnt0 = jnp.zeros((N_EXPERTS, LANES), F32)
    h_p, xn_p, idx_p, gate_p, rank_p, cnt_p = _out_proj(x_prompt.reshape(n_p, D_MODEL), y_p.reshape(n_p, SSD_INNER),
                                                        o_p, out_consts, cnt0, 512)
    h_s, xn_s, idx_s, gate_s, rank_s, cnt_all = _out_proj(x_sample.reshape(n_s, D_MODEL), y_s.reshape(n_s, SSD_INNER),
                                                          o_s.reshape(n_s, ATT_WIDTH), out_consts, cnt_p, 512)

    counts = cnt_all[:, 0].astype(jnp.int32)
    padded = (counts + MOE_ROWS - 1) // MOE_ROWS * MOE_ROWS
    pad_end = jnp.cumsum(padded)
    pad_start = pad_end - padded
    n_blocks = -(-((n_p + n_s) * TOP_K) // MOE_ROWS) + N_EXPERTS
    blk0 = jnp.arange(n_blocks, dtype=jnp.int32) * MOE_ROWS
    owner = blk0[:, None] >= pad_end[None, :]
    block_expert = jnp.minimum(jnp.sum(owner, axis=1), N_EXPERTS - 1).astype(jnp.int32)
    onehot = block_expert[:, None] == jnp.arange(N_EXPERTS, dtype=jnp.int32)[None, :]
    used = jnp.sum(jnp.where(onehot, (blk0[:, None] - pad_start[None, :]), 0), axis=1)
    block_valid = jnp.clip(jnp.sum(jnp.where(onehot, counts[None, :], 0), axis=1) - used, 0, MOE_ROWS).astype(jnp.int32)

    tm = min(256, n_s)
    idx_all = jnp.concatenate([idx_p, idx_s], axis=1)
    rank_all = jnp.concatenate([rank_p, rank_s], axis=1)
    first_slot = functools.reduce(lambda acc, e: jnp.where(idx_all == e, pad_start[e], acc), range(N_EXPERTS),
                                  jnp.zeros_like(idx_all))
    slot_all = first_slot + rank_all
    xs = _dispatch(slot_all[:TOP_K].T.reshape(-1), pad_start + counts, padded - counts, pad_end[N_EXPERTS - 1:] // MOE_ROWS,
                   xn_p, xn_s, n_blocks, tm)
    ys = _experts(block_expert, block_valid, counts, xs, w1[l], b1p, w2[l], b2r)
    nfin = norm_final[None, :]
    meta, col_rows = _combine_meta(idx_all, slot_all, pad_start)
    y_prompt = _combine(meta, col_rows, gate_p[:TOP_K].T, h_p, nfin, ys, 0)
    y_sample = _combine(meta, col_rows, gate_s[:TOP_K].T, h_s, nfin, ys, n_p // COMBINE_TM)

    return (y_prompt.reshape(bp, seq, D_MODEL), y_sample.reshape(bs, t_new, D_MODEL),
            conv_p[None], ssm_p.reshape(1, bp, SSD_HEADS, SSD_HEAD_DIM, SSD_STATE), k_p[None], v_p[None],
            conv_s[None], ssm_s.reshape(1, bs, SSD_HEADS, SSD_HEAD_DIM, SSD_STATE),
            k_s.reshape(1, bs, t_new, ATT_HEADS, ATT_HEAD_DIM), v_s.reshape(1, bs, t_new, ATT_HEADS, ATT_HEAD_DIM))
```

```python
import functools

import jax
import jax.numpy as jnp
import numpy as np
from jax import lax
from jax.experimental import pallas as pl
from jax.experimental.pallas import tpu as pltpu

F32 = jnp.float32
BF16 = jnp.bfloat16

D_MODEL = 1024
SSD_HEADS = 16
SSD_HEAD_DIM = 64
SSD_INNER = SSD_HEADS * SSD_HEAD_DIM
SSD_GROUPS = 2
SSD_STATE = 128
SSD_CONV = 4
SSD_CHUNK = 128
CONV_DIM = SSD_INNER + 2 * SSD_GROUPS * SSD_STATE
ATT_HEADS = 8
ATT_HEAD_DIM = 64
ATT_WIDTH = ATT_HEADS * ATT_HEAD_DIM
DILATED_PATTERNS = ((128, 1), (512, 4), (2048, 16))
ATT_BLOCK = 128
N_EXPERTS = 32
TOP_K = 4
SWIGLU_LIMIT = 7.0
SWIGLU_ALPHA = 1.702
NORM_EPS = 1e-5

LANES = 128
SUBLANES = 8
ZX_WIDTH = SSD_INNER + CONV_DIM
DT_PAD = LANES
CONV_HIST = SUBLANES
MOE_ROWS = 1024
VMEM_LIMIT = 56 * 1024 * 1024

NEG_INF = float("-inf")


def _cparams(sem):
    return pltpu.CompilerParams(dimension_semantics=sem, vmem_limit_bytes=VMEM_LIMIT)


def _rms(x, g):
    return x * lax.rsqrt(jnp.mean(x * x, axis=-1, keepdims=True) + NORM_EPS) * g


def _dot(a, b):
    return jnp.dot(a, b, preferred_element_type=F32)


def _dot_nt(a, b):
    return lax.dot_general(a, b, (((1,), (1,)), ((), ())), preferred_element_type=F32)


def _split3(v):
    hi = v.astype(BF16)
    r1 = v - hi.astype(F32)
    mid = r1.astype(BF16)
    lo = (r1 - mid.astype(F32)).astype(BF16)
    return hi, mid, lo


def _dot3(v, m):
    hi, mid, lo = _split3(v)
    return _dot(hi, m) + _dot(mid, m) + _dot(lo, m)


def _dot3_lhs(m, v):
    hi, mid, lo = _split3(v)
    return _dot(m, hi) + _dot(m, mid) + _dot(m, lo)


def _silu(x):
    half = 0.5 * x
    return half * (1.0 + jnp.tanh(half))


def _in_proj_body(x_ref, g_ref, w_ref, zx_ref, q_ref, k_ref, v_ref, dt_ref, *t_refs):
    hn = _rms(x_ref[...], g_ref[...]).astype(BF16)
    o = 0
    for ref, width in ((zx_ref, ZX_WIDTH), (q_ref, ATT_WIDTH), (k_ref, ATT_WIDTH), (v_ref, ATT_WIDTH),
                       (dt_ref, DT_PAD)):
        ref[...] = _dot(hn, w_ref[:, o:o + width])
        o += width
    for src, dst in zip((k_ref, v_ref), t_refs):
        for j in range(ATT_WIDTH // LANES):
            dst[0, j * LANES:(j + 1) * LANES, :] = src[:, j * LANES:(j + 1) * LANES].T


def _in_proj(x2d, g, w_cat, tm, seq=None):
    n = x2d.shape[0]
    tm = min(tm, n)
    per = 1 if seq is None else seq // tm
    row = lambda w: pl.BlockSpec((tm, w), lambda b, i: (b * per + i, 0))
    full = lambda a: pl.BlockSpec(a.shape, lambda b, i: (0,) * a.ndim)
    widths = (ZX_WIDTH, ATT_WIDTH, ATT_WIDTH, ATT_WIDTH, DT_PAD)
    out_specs = [row(w) for w in widths]
    out_shape = [jax.ShapeDtypeStruct((n, w), F32) for w in widths]
    if seq is not None:
        out_specs += [pl.BlockSpec((1, ATT_WIDTH, tm), lambda b, i: (b, 0, i))] * 2
        out_shape += [jax.ShapeDtypeStruct((n // seq, ATT_WIDTH, seq), F32)] * 2
    return pl.pallas_call(
        _in_proj_body,
        grid=(n // (tm * per), per),
        in_specs=[row(D_MODEL), full(g), full(w_cat)],
        out_specs=out_specs,
        out_shape=out_shape,
        compiler_params=_cparams(("parallel", "parallel")),
        name="in_proj",
    )(x2d, g, w_cat)


def _ssd_body(*refs, l_blk, has_init):
    T = SSD_CHUNK
    if has_init:
        (zx_ref, dt_ref, cw_ref, cb_ref, dtb_ref, alog_ref, dskip_ref, norm_ref, e64_ref,
         cinit_ref, sinit_ref, y_ref, conv_out_ref, ssm_out_ref,
         ext_ref, act_ref, state_ref, cst_ref, ybuf_ref, zpad_ref, dtpad_ref) = refs
    else:
        (zx_ref, dt_ref, cw_ref, cb_ref, dtb_ref, alog_ref, dskip_ref, norm_ref, e64_ref,
         y_ref, conv_out_ref, ssm_out_ref,
         ext_ref, act_ref, state_ref, cst_ref, ybuf_ref) = refs
    c = pl.program_id(1)
    n_tile = SSD_INNER // LANES

    @pl.when(c == 0)
    def _start():
        if has_init:
            ext_ref[0:CONV_HIST, :] = jnp.zeros((CONV_HIST, CONV_DIM), F32)
            ext_ref[CONV_HIST - (SSD_CONV - 1):CONV_HIST, :] = cinit_ref[0]
            for j in range(n_tile):
                state_ref[:, j * LANES:(j + 1) * LANES] = sinit_ref[0, j * LANES:(j + 1) * LANES, :].T
        else:
            ext_ref[0:CONV_HIST, :] = jnp.zeros((CONV_HIST, CONV_DIM), F32)
            state_ref[...] = jnp.zeros_like(state_ref)

    if l_blk == T:
        ext_ref[CONV_HIST:CONV_HIST + T, :] = zx_ref[0, :, SSD_INNER:ZX_WIDTH]
        z_of = lambda sl: zx_ref[0, :, sl]
        dt_raw = dt_ref[0]
    else:
        ext_ref[CONV_HIST:CONV_HIST + T, :] = jnp.zeros((T, CONV_DIM), F32)
        ext_ref[CONV_HIST:CONV_HIST + l_blk, :] = zx_ref[0, :, SSD_INNER:ZX_WIDTH]
        zpad_ref[...] = jnp.zeros_like(zpad_ref)
        zpad_ref[0:l_blk, :] = zx_ref[0, :, 0:SSD_INNER]
        dtpad_ref[...] = jnp.zeros_like(dtpad_ref)
        dtpad_ref[0:l_blk, :] = dt_ref[0]
        z_of = lambda sl: zpad_ref[:, sl]
        dt_raw = dtpad_ref[...]

    cw = CONV_DIM // 3
    for cc in range(3):
        sl = slice(cc * cw, (cc + 1) * cw)
        acc = cb_ref[:, sl]
        for j in range(SSD_CONV):
            o = CONV_HIST - (SSD_CONV - 1) + j
            acc = acc + ext_ref[o:o + T, sl] * cw_ref[j:j + 1, sl]
        act_ref[:, sl] = _silu(acc)

    row = lax.broadcasted_iota(jnp.int32, (T, LANES), 0)
    col = lax.broadcasted_iota(jnp.int32, (T, LANES), 1)
    tri = row >= col
    tri_bf = jnp.where(tri, 1.0, 0.0).astype(BF16)
    even = col < SSD_HEAD_DIM

    xdt = dt_raw + dtb_ref[...]
    dtv = jnp.maximum(xdt, 0.0) + jnp.log1p(jnp.exp(-jnp.abs(xdt)))
    if l_blk < T:
        dtv = jnp.where(row < l_blk, dtv, 0.0)
    d_a = dtv * (-jnp.exp(alog_ref[...]))
    cs = _dot3_lhs(tri_bf, d_a)
    cst_ref[...] = cs.T
    ex = _dot3(jnp.concatenate([dtv, cs], axis=0), e64_ref[...])
    dt_ex, cs_ex = ex[0:T], ex[T:2 * T]
    ecs_ex = jnp.exp(cs_ex)
    dd_ex = dt_ex * jnp.exp(cs_ex[T - 1:T] - cs_ex)

    gw = SSD_INNER // SSD_GROUPS
    heads_per_group = SSD_HEADS // SSD_GROUPS
    for g in range(SSD_GROUPS):
        gsl = slice(g * gw, (g + 1) * gw)
        b_g = act_ref[:, SSD_INNER + g * SSD_STATE:SSD_INNER + (g + 1) * SSD_STATE]
        c_off = SSD_INNER + SSD_GROUPS * SSD_STATE
        c_g = act_ref[:, c_off + g * SSD_STATE:c_off + (g + 1) * SSD_STATE].astype(BF16)
        cb = _dot_nt(c_g, b_g.astype(BF16))
        b_gt = b_g.T.astype(BF16)
        x_g = act_ref[:, gsl]
        x_dt = (x_g * dt_ex[:, gsl]).astype(BF16)
        x_dd = (x_g * dd_ex[:, gsl]).astype(BF16)
        st_old = state_ref[:, gsl]
        y_off = _dot(c_g, st_old.astype(BF16)) * ecs_ex[:, gsl]
        state_ref[:, gsl] = st_old * ecs_ex[T - 1:T, gsl] + _dot(b_gt, x_dd)
        for jp in range(heads_per_group // 2):
            h0 = g * heads_per_group + 2 * jp
            psl = slice(jp * LANES, (jp + 1) * LANES)
            osl = slice(g * gw + jp * LANES, g * gw + (jp + 1) * LANES)
            pair = cs_ex[:, osl]
            swapped = pltpu.roll(pair, SSD_HEAD_DIM, axis=1)
            cols = (jnp.where(even, pair, swapped), jnp.where(even, swapped, pair))
            yd = []
            for col, h in zip(cols, (h0, h0 + 1)):
                seg = col - cst_ref[h:h + 1, :]
                lmat = jnp.exp(jnp.where(tri, seg, NEG_INF))
                yd.append(_dot((cb * lmat).astype(BF16), x_dt[:, psl]))
            y_pair = jnp.where(even, yd[0], yd[1]) + y_off[:, psl]
            ybuf_ref[:, osl] = y_pair + dskip_ref[:, osl] * x_g[:, psl]

    for g in range(SSD_GROUPS):
        gsl = slice(g * gw, (g + 1) * gw)
        yg = ybuf_ref[:, gsl] * _silu(z_of(gsl))
        yn = yg * lax.rsqrt(jnp.mean(yg * yg, axis=-1, keepdims=True) + NORM_EPS) * norm_ref[:, gsl]
        y_ref[0, :, gsl] = yn[0:l_blk].astype(y_ref.dtype)

    @pl.when(c == pl.num_programs(1) - 1)
    def _finish():
        lo = CONV_HIST + l_blk - (SSD_CONV - 1)
        conv_out_ref[0] = ext_ref[lo:lo + SSD_CONV - 1, :]
        for j in range(n_tile):
            ssm_out_ref[0, j * LANES:(j + 1) * LANES, :] = state_ref[:, j * LANES:(j + 1) * LANES].T

    ext_ref[0:CONV_HIST, :] = ext_ref[T:T + CONV_HIST, :]


def _ssd(zx3, dt3, consts, init, y_dtype):
    b, L, _ = zx3.shape
    T = SSD_CHUNK
    l_blk = T if L % T == 0 else L
    n_chunks = L // l_blk
    has_init = init is not None
    full = lambda a: pl.BlockSpec(a.shape, lambda i, c: (0,) * a.ndim)
    in_specs = [pl.BlockSpec((1, l_blk, ZX_WIDTH), lambda i, c: (i, c, 0)),
                pl.BlockSpec((1, l_blk, DT_PAD), lambda i, c: (i, c, 0))] + [full(a) for a in consts]
    args = [zx3, dt3, *consts]
    scratch = [pltpu.VMEM((T + CONV_HIST, CONV_DIM), F32),
               pltpu.VMEM((T, CONV_DIM), F32),
               pltpu.VMEM((SSD_STATE, SSD_INNER), F32),
               pltpu.VMEM((LANES, T), F32),
               pltpu.VMEM((T, SSD_INNER), F32)]
    if has_init:
        in_specs += [pl.BlockSpec((1, SSD_CONV - 1, CONV_DIM), lambda i, c: (i, 0, 0)),
                     pl.BlockSpec((1, SSD_INNER, SSD_STATE), lambda i, c: (i, 0, 0))]
        args += list(init)
        scratch += [pltpu.VMEM((T, SSD_INNER), F32), pltpu.VMEM((T, DT_PAD), F32)]
    return pl.pallas_call(
        functools.partial(_ssd_body, l_blk=l_blk, has_init=has_init),
        grid=(b, n_chunks),
        in_specs=in_specs,
        out_specs=[pl.BlockSpec((1, l_blk, SSD_INNER), lambda i, c: (i, c, 0)),
                   pl.BlockSpec((1, SSD_CONV - 1, CONV_DIM), lambda i, c: (i, 0, 0)),
                   pl.BlockSpec((1, SSD_INNER, SSD_STATE), lambda i, c: (i, 0, 0))],
        out_shape=[jax.ShapeDtypeStruct((b, L, SSD_INNER), y_dtype),
                   jax.ShapeDtypeStruct((b, SSD_CONV - 1, CONV_DIM), F32),
                   jax.ShapeDtypeStruct((b, SSD_INNER, SSD_STATE), F32)],
        scratch_shapes=scratch,
        compiler_params=_cparams(("parallel", "arbitrary")),
        name="ssd_init" if has_init else "ssd",
    )(*args)


ATT_UNROLL = 16


def _unroll(n):
    return max(d for d in range(1, ATT_UNROLL + 1) if n % d == 0)


ATT_RES = DILATED_PATTERNS[-1][1]


def _attn_tables():
    B = ATT_BLOCK
    t1 = np.full((len(DILATED_PATTERNS), B, B), -np.inf, np.float32)
    t2 = np.full((len(DILATED_PATTERNS), B, 2 * B), -np.inf, np.float32)
    for p, (window, dil) in enumerate(DILATED_PATTERNS):
        m = ATT_RES // dil
        w = B // m
        rho = np.arange(B)
        c = m * (rho % w) + rho // w
        ck2 = np.concatenate([c - B, c])
        for tab, ck in ((t1, c), (t2, ck2)):
            delta = c[:, None] - ck[None, :]
            ok = (delta >= 0) & (delta <= window // dil)
            tab[p] = np.where(ok, -(dil * delta).astype(np.float32), -np.inf)
    return jnp.asarray(t1), jnp.asarray(t2)


def _attn_body(slopes_ref, t1_ref, t2_ref, q_ref, k_ref, v_ref, o_ref, qd_ref, kd_ref, vd_ref, op_ref, lp_ref,
               stage_ref, b1_ref, b2_ref, in_sem, out_sem, *, seq):
    B = ATT_BLOCK
    hp = pl.program_id(1)
    even = lax.broadcasted_iota(jnp.int32, (B, LANES), 1) < ATT_HEAD_DIM
    scale = ATT_HEAD_DIM ** -0.5

    def in_copy(a, r):
        src, dst = ((q_ref, qd_ref), (k_ref, kd_ref), (v_ref, vd_ref))[a]
        return pltpu.make_async_copy(src.at[:, r, :], dst.at[r], in_sem.at[a, r])
    for r in range(ATT_RES):
        for a in range(3):
            in_copy(a, r).start()

    for e in range(2):
        slope = slopes_ref[2 * hp + e]
        for p in range(len(DILATED_PATTERNS)):
            b1_ref[e * len(DILATED_PATTERNS) + p] = t1_ref[p] * slope
            b2_ref[e * len(DILATED_PATTERNS) + p] = t2_ref[p] * slope

    for r in range(ATT_RES):
        for a in range(3):
            in_copy(a, r).wait()

    def block(p, r_d, n, first):
        dil = DILATED_PATTERNS[p][1]
        m = ATT_RES // dil
        w = B // m

        def slab(j, nblk):
            start = nblk * w
            return r_d + dil * j, pl.ds(start if isinstance(start, int) else pl.multiple_of(start, w), w)

        def gather(src, nblk):
            parts = []
            for j in range(m):
                r, rows = slab(j, nblk)
                parts.append(src[r, rows, :])
            return parts[0] if m == 1 else jnp.concatenate(parts, axis=0)

        qb = gather(qd_ref, n) * scale
        if first:
            kb, vb = gather(kd_ref, n), gather(vd_ref, n)
            bias = lambda e: b1_ref[e * len(DILATED_PATTERNS) + p]
        else:
            kb = jnp.concatenate([gather(kd_ref, n - 1), gather(kd_ref, n)], axis=0)
            vb = jnp.concatenate([gather(vd_ref, n - 1), gather(vd_ref, n)], axis=0)
            bias = lambda e: b2_ref[e * len(DILATED_PATTERNS) + p]
        q2 = jnp.concatenate([jnp.where(even, qb, 0.0), jnp.where(even, 0.0, qb)], axis=0).astype(BF16)
        s = _dot_nt(q2, kb.astype(BF16)) + jnp.concatenate([bias(0), bias(1)], axis=0)
        mx = jnp.max(s, axis=-1, keepdims=True)
        pr = jnp.exp(s - mx)
        l = jnp.sum(pr, axis=-1, keepdims=True)
        o2 = _dot(pr.astype(BF16), vb.astype(BF16)) / l
        lse2 = jnp.broadcast_to(mx + jnp.log(l), (2 * B, LANES))
        o = jnp.where(even, o2[:B], o2[B:])
        lse = jnp.where(even, lse2[:B], lse2[B:])
        for j in range(m):
            r, rows = slab(j, n)
            op_ref[p, r, rows, :] = o[j * w:(j + 1) * w]
            lp_ref[p, r, rows, :] = lse[j * w:(j + 1) * w]

    for p, (_, dil) in enumerate(DILATED_PATTERNS):
        nb = seq // (dil * B)
        u_first = _unroll(dil)

        def first(i, carry, p=p, u_first=u_first):
            for u in range(u_first):
                block(p, i * u_first + u, 0, True)
            return carry
        lax.fori_loop(0, dil // u_first, first, 0)
        n_later = dil * (nb - 1)
        if n_later:
            u_later = _unroll(n_later)

            def later(i, carry, p=p, nb=nb, u_later=u_later):
                for u in range(u_later):
                    j = i * u_later + u
                    block(p, j // (nb - 1), j % (nb - 1) + 1, False)
                return carry
            lax.fori_loop(0, n_later // u_later, later, 0)

    def out_copy(r):
        return pltpu.make_async_copy(stage_ref.at[r], o_ref.at[:, r, :], out_sem.at[r])
    for r in range(ATT_RES):
        l0, l1, l2 = lp_ref[0, r], lp_ref[1, r], lp_ref[2, r]
        m = jnp.maximum(jnp.maximum(l0, l1), l2)
        w0, w1, w2 = jnp.exp(l0 - m), jnp.exp(l1 - m), jnp.exp(l2 - m)
        num = w0 * op_ref[0, r] + w1 * op_ref[1, r] + w2 * op_ref[2, r]
        stage_ref[r] = num / (w0 + w1 + w2)
        out_copy(r).start()
    for r in range(ATT_RES):
        out_copy(r).wait()


def _attn_prompt(q, k, v, slopes, batch, seq):
    n_hp = ATT_HEADS // 2
    n_pat = len(DILATED_PATTERNS)
    assert len(DILATED_PATTERNS) == 3 and all(ATT_RES % d == 0 and seq % (d * ATT_BLOCK) == 0 and w // d == ATT_BLOCK
                                              for w, d in DILATED_PATTERNS)
    per = seq // ATT_RES
    by_res = lambda a: a.reshape(batch * per, ATT_RES, ATT_WIDTH)
    blk = pl.BlockSpec((per, ATT_RES, LANES), lambda b, h: (b, 0, h))
    t1, t2 = _attn_tables()
    full = lambda a: pl.BlockSpec(a.shape, lambda b, h: (0,) * a.ndim)
    res = pltpu.VMEM((ATT_RES, per, LANES), F32)
    out = pl.pallas_call(
        functools.partial(_attn_body, seq=seq),
        grid=(batch, n_hp),
        in_specs=[pl.BlockSpec(memory_space=pltpu.SMEM), full(t1), full(t2), blk, blk, blk],
        out_specs=blk,
        out_shape=jax.ShapeDtypeStruct((batch * per, ATT_RES, ATT_WIDTH), F32),
        scratch_shapes=[res, res, res,
                        pltpu.VMEM((n_pat, ATT_RES, per, LANES), F32), pltpu.VMEM((n_pat, ATT_RES, per, LANES), F32),
                        res,
                        pltpu.VMEM((2 * n_pat, ATT_BLOCK, ATT_BLOCK), F32),
                        pltpu.VMEM((2 * n_pat, ATT_BLOCK, 2 * ATT_BLOCK), F32),
                        pltpu.SemaphoreType.DMA((3, ATT_RES)), pltpu.SemaphoreType.DMA((ATT_RES,))],
        compiler_params=_cparams(("parallel", "parallel")),
        name="attn_prompt",
    )(slopes, t1, t2, by_res(q), by_res(k), by_res(v))
    return out.reshape(batch * seq, ATT_WIDTH)


DEC_ROWS = SUBLANES


def _decode_tables(t_new, m_cache):
    slopes = 2.0 ** (-8.0 * np.arange(1, ATT_HEADS + 1) / ATT_HEADS)

    def mult(dist):
        return sum(1 for window, dil in DILATED_PATTERNS if 0 <= dist <= window and dist % dil == 0)

    bias_c = np.zeros((ATT_HEADS, DEC_ROWS, m_cache), np.float32)
    mult_c = np.ones((DEC_ROWS, m_cache), np.float32)
    bias_n = np.full((ATT_HEADS, DEC_ROWS, LANES), -np.inf, np.float32)
    mult_n = np.zeros((DEC_ROWS, LANES), np.float32)
    for t in range(t_new):
        dist = m_cache + t - np.arange(m_cache)
        mu = np.array([mult(d) for d in dist], np.float32)
        mult_c[t] = mu
        bias_c[:, t, :] = np.where(mu > 0, -slopes[:, None] * dist[None, :], -np.inf)
        for t2 in range(t_new):
            if mult(t - t2):
                bias_n[:, t, t2] = -slopes * (t - t2)
                mult_n[t, t2] = mult(t - t2)
    bias_n[:, t_new:, 0] = 0.0
    mult_n[t_new:, 0] = 1.0
    return jnp.asarray(bias_c), jnp.asarray(mult_c), jnp.asarray(bias_n), jnp.asarray(mult_n)


def _attn_dec_body(q_ref, knt_ref, vnt_ref, ck_ref, cv_ref, bc_ref, mc_ref, bn_ref, mn_ref, o_ref):
    scale = ATT_HEAD_DIM ** -0.5
    for h in range(ATT_HEADS):
        qh = (q_ref[0, h] * scale).astype(BF16)
        sc = _dot(qh, ck_ref[0, h].astype(BF16)) + bc_ref[h]
        sn = _dot(qh, knt_ref[0, h].astype(BF16)) + bn_ref[h]
        m = jnp.maximum(jnp.max(sc, axis=-1, keepdims=True), jnp.max(sn, axis=-1, keepdims=True))
        pc = jnp.exp(sc - m) * mc_ref[...]
        pn = jnp.exp(sn - m) * mn_ref[...]
        inv_l = 1.0 / (jnp.sum(pc, axis=-1, keepdims=True) + jnp.sum(pn, axis=-1, keepdims=True))
        o_ref[0, h] = (_dot_nt(cv_ref[0, h].astype(BF16), (pc * inv_l).astype(BF16))
                       + _dot_nt(vnt_ref[0, h].astype(BF16), (pn * inv_l).astype(BF16)))


def _attn_decode(q4, knt, vnt, cache_kt, cache_vt, t_new):
    b = q4.shape[0]
    m_cache = cache_kt.shape[3]
    assert t_new <= DEC_ROWS
    tables = _decode_tables(t_new, m_cache)
    per_b = lambda a: pl.BlockSpec((1,) + a.shape[1:], lambda i: (i, 0, 0, 0))
    full = lambda a: pl.BlockSpec(a.shape, lambda i: (0,) * a.ndim)
    out_shape = jax.ShapeDtypeStruct((b, ATT_HEADS, ATT_HEAD_DIM, DEC_ROWS), F32)
    return pl.pallas_call(
        _attn_dec_body,
        grid=(b,),
        in_specs=[per_b(a) for a in (q4, knt, vnt, cache_kt, cache_vt)] + [full(a) for a in tables],
        out_specs=per_b(out_shape),
        out_shape=out_shape,
        compiler_params=_cparams(("parallel",)),
        name="attn_decode",
    )(q4, knt, vnt, cache_kt, cache_vt, *tables)


def _out_proj_body(x_ref, y_ref, o_ref, an_ref, wy_ref, wo_ref, nf_ref, rw_ref, rb_ref, cnt0_ref,
                   h_ref, xn_ref, idx_ref, gate_ref, rank_ref, cnt_ref, carry_ref, *, tm):
    i = pl.program_id(0)

    @pl.when(i == 0)
    def _start():
        carry_ref[...] = cnt0_ref[...]

    on = _rms(o_ref[...], an_ref[...]).astype(BF16)
    mixed = _dot(y_ref[...].astype(BF16), wy_ref[...]) + _dot(on, wo_ref[...])
    h = x_ref[...] + mixed
    h_ref[...] = h
    xn = _rms(h, nf_ref[...])
    xn_ref[...] = xn
    logits = _dot_nt(rw_ref[...], xn.astype(BF16)) + rb_ref[...]

    e_iota = lax.broadcasted_iota(jnp.int32, (N_EXPERTS, tm), 0)
    vals, idxs, sels = [], [], []
    cur = logits
    for _ in range(TOP_K):
        mx = jnp.max(cur, axis=0, keepdims=True)
        ix = jnp.min(jnp.where(cur == mx, e_iota, N_EXPERTS), axis=0, keepdims=True)
        sel = e_iota == ix
        cur = jnp.where(sel, NEG_INF, cur)
        vals.append(mx)
        idxs.append(ix)
        sels.append(sel)
    ex = [jnp.exp(v - vals[0]) for v in vals]
    den = functools.reduce(lambda a, b: a + b, ex)

    sel_any = functools.reduce(jnp.logical_or, sels)
    sel_f = jnp.where(sel_any, 1.0, 0.0)
    r = lax.broadcasted_iota(jnp.int32, (tm, tm), 0)
    c = lax.broadcasted_iota(jnp.int32, (tm, tm), 1)
    upper = jnp.where(r <= c, 1.0, 0.0).astype(BF16)
    cum = _dot(sel_f.astype(BF16), upper)
    before = carry_ref[:, 0:1] + cum - sel_f
    for k in range(TOP_K):
        idx_ref[k:k + 1, :] = idxs[k]
        gate_ref[k:k + 1, :] = ex[k] / den
        rank_ref[k:k + 1, :] = jnp.sum(jnp.where(sels[k], before, 0.0), axis=0, keepdims=True).astype(jnp.int32)
    idx_ref[TOP_K:SUBLANES, :] = jnp.zeros((SUBLANES - TOP_K, tm), jnp.int32)
    gate_ref[TOP_K:SUBLANES, :] = jnp.zeros((SUBLANES - TOP_K, tm), F32)
    rank_ref[TOP_K:SUBLANES, :] = jnp.zeros((SUBLANES - TOP_K, tm), jnp.int32)
    carry_ref[...] = carry_ref[...] + jnp.max(cum, axis=1, keepdims=True)
    cnt_ref[...] = carry_ref[...]


def _out_proj(x2d, y2d, o2d, consts, cnt0, tm):
    n = x2d.shape[0]
    tm = min(tm, n)
    row = lambda w: pl.BlockSpec((tm, w), lambda i: (i, 0))
    colb = pl.BlockSpec((SUBLANES, tm), lambda i: (0, i))
    full = lambda a: pl.BlockSpec(a.shape, lambda i: (0,) * a.ndim)
    return pl.pallas_call(
        functools.partial(_out_proj_body, tm=tm),
        grid=(n // tm,),
        in_specs=[row(D_MODEL), row(SSD_INNER), row(ATT_WIDTH)] + [full(a) for a in consts] + [full(cnt0)],
        out_specs=[row(D_MODEL), row(D_MODEL), colb, colb, colb, full(cnt0)],
        out_shape=[jax.ShapeDtypeStruct((n, D_MODEL), F32), jax.ShapeDtypeStruct((n, D_MODEL), F32),
                   jax.ShapeDtypeStruct((SUBLANES, n), jnp.int32), jax.ShapeDtypeStruct((SUBLANES, n), F32),
                   jax.ShapeDtypeStruct((SUBLANES, n), jnp.int32), jax.ShapeDtypeStruct(cnt0.shape, F32)],
        scratch_shapes=[pltpu.VMEM(cnt0.shape, F32)],
        compiler_params=_cparams(("arbitrary",)),
        name="out_proj",
    )(x2d, y2d, o2d, *consts, cnt0)


ISSUE_UNROLL = 8


def _dispatch_body(slot_ref, from_ref, n_ref, tail_ref, xa_ref, xb_ref, xs_ref, zero_ref, stage_ref, sem, row_sems,
                   *, tm, tiles_a, tiles_b, n_blocks):
    i = pl.program_id(0)

    @pl.when(i == 0)
    def _pad_fill():
        zero_ref[...] = jnp.zeros_like(zero_ref)
        one_row = lambda slot: pltpu.make_async_copy(zero_ref.at[pl.ds(0, 1)], xs_ref.at[pl.ds(slot, 1)], sem)
        block = lambda j: pltpu.make_async_copy(zero_ref, xs_ref.at[pl.ds(j * MOE_ROWS, MOE_ROWS)], sem)

        def per_expert(e, carry):
            def start(r, c):
                one_row(from_ref[e] + r).start()
                return c

            def wait(r, c):
                one_row(0).wait()
                return c
            lax.fori_loop(0, n_ref[e], start, 0)
            lax.fori_loop(0, n_ref[e], wait, 0)
            return carry
        lax.fori_loop(0, N_EXPERTS, per_expert, 0)

        def tail(j, carry):
            block(j).start()
            block(j).wait()
            return carry
        lax.fori_loop(tail_ref[0], n_blocks, tail, 0)

    def drain(parity):
        for _ in range(TOP_K):
            pltpu.make_async_copy(stage_ref.at[parity], xs_ref.at[pl.ds(0, tm)], row_sems.at[parity]).wait()

    for parity in range(2):
        @pl.when(i % 2 == parity)
        def _step(parity=parity):
            @pl.when(i < tiles_a)
            def _first_group():
                stage_ref[parity] = xa_ref[...]

            @pl.when(i >= tiles_a)
            def _second_group():
                stage_ref[parity] = xb_ref[...]

            def issue(j, carry):
                t0 = pl.multiple_of(j * ISSUE_UNROLL, ISSUE_UNROLL)
                for u in range(ISSUE_UNROLL):
                    for k in range(TOP_K):
                        slot = slot_ref[t0 * TOP_K + (u * TOP_K + k)]
                        pltpu.make_async_copy(stage_ref.at[parity, pl.ds(t0 + u, 1)],
                                              xs_ref.at[pl.ds(slot, 1)], row_sems.at[parity]).start()
                return carry
            lax.fori_loop(0, tm // ISSUE_UNROLL, issue, 0)

            @pl.when(i > 0)
            def _previous():
                drain(1 - parity)

            @pl.when(i == tiles_a + tiles_b - 1)
            def _last():
                drain(parity)


def _dispatch(slot, pad_from, pad_n, tail_block, xa, xb, n_blocks, tm):
    tiles_a, tiles_b = xa.shape[0] // tm, xb.shape[0] // tm
    assert tiles_a * tm == xa.shape[0] and tiles_b * tm == xb.shape[0]
    smem = pl.BlockSpec(memory_space=pltpu.SMEM)
    hbm = pl.BlockSpec(memory_space=pl.ANY)
    return pl.pallas_call(
        functools.partial(_dispatch_body, tm=tm, tiles_a=tiles_a, tiles_b=tiles_b, n_blocks=n_blocks),
        grid=(tiles_a + tiles_b,),
        in_specs=[pl.BlockSpec((tm * TOP_K,), lambda i: (i,), memory_space=pltpu.SMEM), smem, smem, smem,
                  pl.BlockSpec((tm, D_MODEL), lambda i: (jnp.minimum(i, tiles_a - 1), 0)),
                  pl.BlockSpec((tm, D_MODEL), lambda i: (jnp.maximum(i - tiles_a, 0), 0))],
        out_specs=hbm,
        out_shape=jax.ShapeDtypeStruct((n_blocks * MOE_ROWS, D_MODEL), F32),
        scratch_shapes=[pltpu.VMEM((MOE_ROWS, D_MODEL), F32), pltpu.VMEM((2, tm, D_MODEL), F32),
                        pltpu.SemaphoreType.DMA(()), pltpu.SemaphoreType.DMA((2,))],
        compiler_params=_cparams(("arbitrary",)),
        name="dispatch",
    )(slot, pad_from, pad_n, tail_block, xa, xb)


MOE_COLS = 2 * LANES


def _glu_perm():
    src = np.concatenate([np.arange(0, MOE_COLS, 2), np.arange(1, MOE_COLS, 2)])
    perm = np.zeros((MOE_COLS, MOE_COLS), np.float32)
    perm[src, np.arange(MOE_COLS)] = 1.0
    return jnp.asarray(perm, dtype=BF16)


def _experts_body(be_ref, bv_ref, nx_ref, xs_ref, w1_ref, p_ref, b1_ref, w2_ref, b2_ref, ys_ref,
                  w1f_ref, w1s_ref, w2s_ref, g_ref, sem):
    j = pl.program_id(0)
    valid = bv_ref[j]
    e = be_ref[j]

    @pl.when(jnp.logical_and(valid > 0, jnp.logical_or(j == 0, e != be_ref[jnp.maximum(j - 1, 0)])))
    def _new_expert():
        slot, nxt = nx_ref[1, e], nx_ref[0, e]
        fetch = lambda ex, s: pltpu.make_async_copy(w1_ref.at[ex], w1f_ref.at[s], sem.at[s])

        @pl.when(j == 0)
        def _first_fetch():
            fetch(e, slot).start()
        fetch(e, slot).wait()

        @pl.when(nxt >= 0)
        def _next_fetch():
            fetch(nxt, 1 - slot).start()
        for c in range(w1s_ref.shape[1] // MOE_COLS):
            cols = slice(c * MOE_COLS, (c + 1) * MOE_COLS)
            w1s_ref[:, cols] = _dot(w1f_ref[slot, :, cols].astype(BF16), p_ref[...]).astype(BF16)
        w2s_ref[...] = w2_ref[0].astype(BF16)

    @pl.when(valid > 0)
    def _compute():
        x = xs_ref[...].astype(BF16)
        for c in range(g_ref.shape[1] // LANES):
            cols = slice(c * MOE_COLS, (c + 1) * MOE_COLS)
            hc = _dot(x, w1s_ref[:, cols]) + b1_ref[0, :, cols]
            glu = jnp.minimum(hc[:, :LANES], SWIGLU_LIMIT)
            lin = jnp.clip(hc[:, LANES:], -SWIGLU_LIMIT, SWIGLU_LIMIT)
            g_ref[:, c * LANES:(c + 1) * LANES] = (glu * jax.nn.sigmoid(SWIGLU_ALPHA * glu) * (lin + 1.0)).astype(BF16)
        ys_ref[...] = _dot(g_ref[...], w2s_ref[...]) + b2_ref[0]

    @pl.when(valid <= 0)
    def _empty():
        ys_ref[...] = jnp.zeros_like(ys_ref)


def _experts(block_expert, block_valid, counts, xs, w1, b1p, w2, b2):
    n_blocks = xs.shape[0] // MOE_ROWS
    perm = _glu_perm()
    active = counts > 0
    ids = jnp.arange(N_EXPERTS, dtype=jnp.int32)
    later = jnp.where(active[None, :] & (ids[None, :] > ids[:, None]), ids[None, :], N_EXPERTS)
    nxt = jnp.min(later, axis=1)
    nx = jnp.stack([jnp.where(nxt < N_EXPERTS, nxt, -1), (jnp.cumsum(active) - 1) % 2]).astype(jnp.int32)
    wspec = lambda a: pl.BlockSpec((1,) + a.shape[1:], lambda j, be, bv, nx: (be[j], 0, 0))
    grid_spec = pltpu.PrefetchScalarGridSpec(
        num_scalar_prefetch=3,
        grid=(n_blocks,),
        in_specs=[pl.BlockSpec((MOE_ROWS, D_MODEL), lambda j, be, bv, nx: (j, 0)),
                  pl.BlockSpec(memory_space=pl.ANY), pl.BlockSpec(perm.shape, lambda j, be, bv, nx: (0, 0)),
                  wspec(b1p), wspec(w2), wspec(b2)],
        out_specs=pl.BlockSpec((MOE_ROWS, D_MODEL), lambda j, be, bv, nx: (j, 0)),
        scratch_shapes=[pltpu.VMEM((2,) + w1.shape[1:], F32), pltpu.VMEM(w1.shape[1:], BF16),
                        pltpu.VMEM(w2.shape[1:], BF16), pltpu.VMEM((MOE_ROWS, w2.shape[1]), BF16),
                        pltpu.SemaphoreType.DMA((2,))],
    )
    return pl.pallas_call(
        _experts_body,
        grid_spec=grid_spec,
        out_shape=jax.ShapeDtypeStruct(xs.shape, F32),
        compiler_params=_cparams(("arbitrary",)),
        name="experts",
    )(block_expert, block_valid, nx, xs, w1, perm, b1p, w2, b2)


COMBINE_TM = 256
RUN_UNIT = 16
RUN_BITS = tuple(1 << b for b in reversed(range((COMBINE_TM // RUN_UNIT).bit_length())))
BUF_ROWS = -(-(TOP_K * COMBINE_TM + N_EXPERTS * (RUN_UNIT - 1 + SUBLANES - 1)) // LANES) * LANES


def _combine_meta(idx_all, slot_all, pad_start):
    n_tok = idx_all.shape[1]
    tiles = n_tok // COMBINE_TM
    experts = jnp.arange(N_EXPERTS, dtype=jnp.int32)
    hit = idx_all[:TOP_K, :, None] == experts
    cnt = jnp.sum(hit.reshape(TOP_K, tiles, COMBINE_TM, N_EXPERTS), axis=(0, 2), dtype=jnp.int32)
    run_start = pad_start[None, :] + jnp.cumsum(cnt, axis=0) - cnt
    lead = run_start % SUBLANES
    units = jnp.where(cnt > 0, (cnt + lead + RUN_UNIT - 1) // RUN_UNIT, 0)
    first_row = RUN_UNIT * (jnp.cumsum(units, axis=1) - units)
    meta = jnp.concatenate([run_start - lead, units, first_row, jnp.zeros_like(cnt)], axis=1).astype(jnp.int32)
    shift = jnp.repeat(first_row + lead - run_start, COMBINE_TM, axis=0)
    col = slot_all[:TOP_K] + jnp.sum(jnp.where(hit, shift[None], 0), axis=2, dtype=jnp.int32)
    return meta.reshape(tiles, 1, 4 * N_EXPERTS), col.T


def _combine_body(meta_ref, next_ref, col_ref, gate_ref, h_ref, nf_ref, ys_ref, y_ref, buf_ref, sem, *, n):
    i = pl.program_id(0)

    def runs(meta, half, start):
        for e in range(N_EXPERTS):
            src0, units, dst0 = meta[0, 0, e], meta[0, 0, N_EXPERTS + e], meta[0, 0, 2 * N_EXPERTS + e]

            def pieces(bits, done):
                for b in bits:
                    rows = b * RUN_UNIT

                    @pl.when((units & b) != 0)
                    def _piece(done=done, rows=rows):
                        dst = pl.multiple_of(dst0 + done, RUN_UNIT)
                        cp = pltpu.make_async_copy(ys_ref.at[pl.ds(pl.multiple_of(src0 + done, SUBLANES), rows)],
                                                   buf_ref.at[half, pl.ds(dst, rows)], sem.at[half])
                        if start:
                            cp.start()
                        else:
                            cp.wait()
                    done = done + (units & b) * RUN_UNIT
                return done
            done = pieces(RUN_BITS[-2:], jnp.int32(0))

            @pl.when(units >= RUN_BITS[-3])
            def _long_run(done=done):
                pieces(RUN_BITS[:-2], done)

    def per_half(half):
        @pl.when(i + 1 < n)
        def _fetch_next():
            runs(next_ref, 1 - half, True)
        runs(meta_ref, half, False)
        rows = lax.broadcasted_iota(jnp.int32, (COMBINE_TM, BUF_ROWS), 1)
        pick = jnp.zeros((COMBINE_TM, BUF_ROWS), F32)
        for k in range(TOP_K):
            pick = jnp.where(rows == col_ref[:, k:k + 1], gate_ref[:, k:k + 1], pick)
        acc = h_ref[...] + _dot(pick.astype(BF16), buf_ref[half].astype(BF16))
        y_ref[...] = _rms(acc, nf_ref[...])

    @pl.when(i == 0)
    def _first():
        buf_ref[...] = jnp.zeros_like(buf_ref)
        runs(meta_ref, 0, True)

    @pl.when(i % 2 == 0)
    def _even():
        per_half(0)

    @pl.when(i % 2 == 1)
    def _odd():
        per_half(1)


def _combine(meta, col_rows, gates_rows, h, norm_final, ys, first_tile):
    n = h.shape[0]
    tm = COMBINE_TM
    assert n % tm == 0
    last = first_tile + n // tm - 1
    smem_tile = lambda f: pl.BlockSpec((1, 1, meta.shape[2]), lambda i: (f(i), 0, 0), memory_space=pltpu.SMEM)
    return pl.pallas_call(
        functools.partial(_combine_body, n=n // tm),
        grid=(n // tm,),
        in_specs=[smem_tile(lambda i: i + first_tile), smem_tile(lambda i: jnp.minimum(i + first_tile + 1, last)),
                  pl.BlockSpec((tm, TOP_K), lambda i: (i + first_tile, 0)),
                  pl.BlockSpec((tm, TOP_K), lambda i: (i, 0)),
                  pl.BlockSpec((tm, D_MODEL), lambda i: (i, 0)),
                  pl.BlockSpec((1, D_MODEL), lambda i: (0, 0)),
                  pl.BlockSpec(memory_space=pl.ANY)],
        out_specs=pl.BlockSpec((tm, D_MODEL), lambda i: (i, 0)),
        out_shape=jax.ShapeDtypeStruct((n, D_MODEL), F32),
        scratch_shapes=[pltpu.VMEM((2, BUF_ROWS, D_MODEL), F32), pltpu.SemaphoreType.DMA((2,))],
        compiler_params=_cparams(("arbitrary",)),
        name="combine",
    )(meta, meta, col_rows, gates_rows, h, norm_final, ys)


def _expansion(width):
    h = np.arange(LANES)[:, None]
    c = np.arange(SSD_HEADS * width)[None, :] // width
    return jnp.asarray((h == c).astype(np.float32), dtype=BF16)


def _pad_lanes(v):
    return jnp.pad(v.astype(F32), (0, LANES - v.shape[0]))[None, :]


def kernel(x_prompt, x_sample, state_conv, state_ssm, cache_win_k, cache_win_v, norm_mix, w_in, conv_w, conv_b,
           dt_bias, a_log, d_skip, ssd_norm, att_norm, w_out, norm_ffn, router_w, router_b, w1, b1, w2, b2,
           norm_final):
    depth = w_in.shape[0]
    assert depth == 1
    bp, seq, _ = x_prompt.shape
    bs, t_new, _ = x_sample.shape
    n_p, n_s = bp * seq, bs * t_new
    l = 0

    o_dt = SSD_INNER + CONV_DIM
    o_q = o_dt + SSD_HEADS
    wl = w_in[l]
    w_cat = jnp.concatenate([wl[:, :o_dt], wl[:, o_q:], jnp.pad(wl[:, o_dt:o_q], ((0, 0), (0, DT_PAD - SSD_HEADS)))],
                            axis=1).astype(BF16)
    g_mix = norm_mix[l][None, :]
    ssd_consts = (conv_w[l], conv_b[l][None, :], _pad_lanes(dt_bias[l]), _pad_lanes(a_log[l]),
                  jnp.repeat(d_skip[l], SSD_HEAD_DIM)[None, :], ssd_norm[l][None, :],
                  _expansion(SSD_HEAD_DIM))
    slopes = jnp.exp2(-8.0 * jnp.arange(1, ATT_HEADS + 1, dtype=F32) / ATT_HEADS)
    out_consts = (att_norm[l][None, :], w_out[l][:SSD_INNER].astype(BF16), w_out[l][SSD_INNER:].astype(BF16),
                  norm_ffn[l][None, :], router_w[l].T.astype(BF16), router_b[l][:, None].astype(F32))
    b1p = b1[l].reshape(N_EXPERTS, -1, LANES, 2).transpose(0, 1, 3, 2).reshape(N_EXPERTS, 1, -1)
    b2r = b2[l][:, None, :]

    zx, q, k, v, dt, kt, vt = _in_proj(x_prompt.reshape(n_p, D_MODEL), g_mix, w_cat, 512, seq)
    y_p, conv_p, ssm_p = _ssd(zx.reshape(bp, seq, ZX_WIDTH), dt.reshape(bp, seq, DT_PAD), ssd_consts, None, BF16)
    o_p = _attn_prompt(q, k, v, slopes, bp, seq)
    keep = min(DILATED_PATTERNS[-1][0], seq)
    k_p = kt.reshape(bp, ATT_HEADS, ATT_HEAD_DIM, seq).transpose(0, 3, 1, 2)[:, seq - keep:]
    v_p = vt.reshape(bp, ATT_HEADS, ATT_HEAD_DIM, seq).transpose(0, 3, 1, 2)[:, seq - keep:]

    zx_s, q_s, k_s, v_s, dt_s = _in_proj(x_sample.reshape(n_s, D_MODEL), g_mix, w_cat, 256)
    init = (state_conv[l], state_ssm[l].reshape(bs, SSD_INNER, SSD_STATE))
    y_s, conv_s, ssm_s = _ssd(zx_s.reshape(bs, t_new, ZX_WIDTH), dt_s.reshape(bs, t_new, DT_PAD), ssd_consts, init, F32)

    def head_major(a):
        a = a.reshape(bs, t_new, ATT_HEADS, ATT_HEAD_DIM).transpose(0, 2, 1, 3)
        return jnp.pad(a, ((0, 0), (0, 0), (0, DEC_ROWS - t_new), (0, 0)))

    def head_major_t(a):
        a = a.reshape(bs, t_new, ATT_HEADS, ATT_HEAD_DIM).transpose(0, 2, 3, 1)
        return jnp.pad(a, ((0, 0), (0, 0), (0, 0), (0, LANES - t_new)))
    o_s = _attn_decode(head_major(q_s), head_major_t(k_s), head_major_t(v_s),
                       cache_win_k[l].transpose(0, 2, 3, 1), cache_win_v[l].transpose(0, 2, 3, 1), t_new)
    o_s = o_s[:, :, :, :t_new].transpose(0, 3, 1, 2)

    cnt0 = jnp.zeros((N_EXPERTS, LANES), F32)
    h_p, xn_p, idx_p, gate_p, rank_p, cnt_p = _out_proj(x_prompt.reshape(n_p, D_MODEL), y_p.reshape(n_p, SSD_INNER),
                                                        o_p, out_consts, cnt0, 512)
    h_s, xn_s, idx_s, gate_s, rank_s, cnt_all = _out_proj(x_sample.reshape(n_s, D_MODEL), y_s.reshape(n_s, SSD_INNER),
                                                          o_s.reshape(n_s, ATT_WIDTH), out_consts, cnt_p, 512)

    counts = cnt_all[:, 0].astype(jnp.int32)
    padded = (counts + MOE_ROWS - 1) // MOE_ROWS * MOE_ROWS
    pad_end = jnp.cumsum(padded)
    pad_start = pad_end - padded
    n_blocks = -(-((n_p + n_s) * TOP_K) // MOE_ROWS) + N_EXPERTS
    blk0 = jnp.arange(n_blocks, dtype=jnp.int32) * MOE_ROWS
    owner = blk0[:, None] >= pad_end[None, :]
    block_expert = jnp.minimum(jnp.sum(owner, axis=1), N_EXPERTS - 1).astype(jnp.int32)
    onehot = block_expert[:, None] == jnp.arange(N_EXPERTS, dtype=jnp.int32)[None, :]
    used = jnp.sum(jnp.where(onehot, (blk0[:, None] - pad_start[None, :]), 0), axis=1)
    block_valid = jnp.clip(jnp.sum(jnp.where(onehot, counts[None, :], 0), axis=1) - used, 0, MOE_ROWS).astype(jnp.int32)

    tm = min(256, n_s)
    idx_all = jnp.concatenate([idx_p, idx_s], axis=1)
    rank_all = jnp.concatenate([rank_p, rank_s], axis=1)
    first_slot = functools.reduce(lambda acc, e: jnp.where(idx_all == e, pad_start[e], acc), range(N_EXPERTS),
                                  jnp.zeros_like(idx_all))
    slot_all = first_slot + rank_all
    xs = _dispatch(slot_all[:TOP_K].T.reshape(-1), pad_start + counts, padded - counts, pad_end[N_EXPERTS - 1:] // MOE_ROWS,
                   xn_p, xn_s, n_blocks, tm)
    ys = _experts(block_expert, block_valid, counts, xs, w1[l], b1p, w2[l], b2r)
    nfin = norm_final[None, :]
    meta, col_rows = _combine_meta(idx_all, slot_all, pad_start)
    y_prompt = _combine(meta, col_rows, gate_p[:TOP_K].T, h_p, nfin, ys, 0)
    y_sample = _combine(meta, col_rows, gate_s[:TOP_K].T, h_s, nfin, ys, n_p // COMBINE_TM)

    return (y_prompt.reshape(bp, seq, D_MODEL), y_sample.reshape(bs, t_new, D_MODEL),
            conv_p[None], ssm_p.reshape(1, bp, SSD_HEADS, SSD_HEAD_DIM, SSD_STATE), k_p[None], v_p[None],
            conv_s[None], ssm_s.reshape(1, bs, SSD_HEADS, SSD_HEAD_DIM, SSD_STATE),
            k_s.reshape(1, bs, t_new, ATT_HEADS, ATT_HEAD_DIM), v_s.reshape(1, bs, t_new, ATT_HEADS, ATT_HEAD_DIM))
```

```python
import functools

import jax
import jax.numpy as jnp
import numpy as np
from jax import lax
from jax.experimental import pallas as pl
from jax.experimental.pallas import tpu as pltpu

F32 = jnp.float32
BF16 = jnp.bfloat16

D_MODEL = 1024
SSD_HEADS = 16
SSD_HEAD_DIM = 64
SSD_INNER = SSD_HEADS * SSD_HEAD_DIM
SSD_GROUPS = 2
SSD_STATE = 128
SSD_CONV = 4
SSD_CHUNK = 128
CONV_DIM = SSD_INNER + 2 * SSD_GROUPS * SSD_STATE
ATT_HEADS = 8
ATT_HEAD_DIM = 64
ATT_WIDTH = ATT_HEADS * ATT_HEAD_DIM
DILATED_PATTERNS = ((128, 1), (512, 4), (2048, 16))
ATT_BLOCK = 128
N_EXPERTS = 32
TOP_K = 4
SWIGLU_LIMIT = 7.0
SWIGLU_ALPHA = 1.702
NORM_EPS = 1e-5

LANES = 128
SUBLANES = 8
ZX_WIDTH = SSD_INNER + CONV_DIM
DT_PAD = LANES
CONV_HIST = SUBLANES
MOE_ROWS = 1024
VMEM_LIMIT = 56 * 1024 * 1024

NEG_INF = float("-inf")


def _cparams(sem):
    return pltpu.CompilerParams(dimension_semantics=sem, vmem_limit_bytes=VMEM_LIMIT)


def _rms(x, g):
    return x * lax.rsqrt(jnp.mean(x * x, axis=-1, keepdims=True) + NORM_EPS) * g


def _dot(a, b):
    return jnp.dot(a, b, preferred_element_type=F32)


def _dot_nt(a, b):
    return lax.dot_general(a, b, (((1,), (1,)), ((), ())), preferred_element_type=F32)


def _split3(v):
    hi = v.astype(BF16)
    r1 = v - hi.astype(F32)
    mid = r1.astype(BF16)
    lo = (r1 - mid.astype(F32)).astype(BF16)
    return hi, mid, lo


def _dot3(v, m):
    hi, mid, lo = _split3(v)
    return _dot(hi, m) + _dot(mid, m) + _dot(lo, m)


def _dot3_lhs(m, v):
    hi, mid, lo = _split3(v)
    return _dot(m, hi) + _dot(m, mid) + _dot(m, lo)


def _silu(x):
    half = 0.5 * x
    return half * (1.0 + jnp.tanh(half))


def _in_proj_body(x_ref, g_ref, w_ref, zx_ref, q_ref, k_ref, v_ref, dt_ref, *t_refs):
    hn = _rms(x_ref[...], g_ref[...]).astype(BF16)
    o = 0
    for ref, width in ((zx_ref, ZX_WIDTH), (q_ref, ATT_WIDTH), (k_ref, ATT_WIDTH), (v_ref, ATT_WIDTH),
                       (dt_ref, DT_PAD)):
        ref[...] = _dot(hn, w_ref[:, o:o + width])
        o += width
    for src, dst in zip((k_ref, v_ref), t_refs):
        for j in range(ATT_WIDTH // LANES):
            dst[0, j * LANES:(j + 1) * LANES, :] = src[:, j * LANES:(j + 1) * LANES].T


def _in_proj(x2d, g, w_cat, tm, seq=None):
    n = x2d.shape[0]
    tm = min(tm, n)
    per = 1 if seq is None else seq // tm
    row = lambda w: pl.BlockSpec((tm, w), lambda b, i: (b * per + i, 0))
    full = lambda a: pl.BlockSpec(a.shape, lambda b, i: (0,) * a.ndim)
    widths = (ZX_WIDTH, ATT_WIDTH, ATT_WIDTH, ATT_WIDTH, DT_PAD)
    out_specs = [row(w) for w in widths]
    out_shape = [jax.ShapeDtypeStruct((n, w), F32) for w in widths]
    if seq is not None:
        out_specs += [pl.BlockSpec((1, ATT_WIDTH, tm), lambda b, i: (b, 0, i))] * 2
        out_shape += [jax.ShapeDtypeStruct((n // seq, ATT_WIDTH, seq), F32)] * 2
    return pl.pallas_call(
        _in_proj_body,
        grid=(n // (tm * per), per),
        in_specs=[row(D_MODEL), full(g), full(w_cat)],
        out_specs=out_specs,
        out_shape=out_shape,
        compiler_params=_cparams(("parallel", "parallel")),
        name="in_proj",
    )(x2d, g, w_cat)


def _ssd_body(*refs, l_blk, has_init):
    T = SSD_CHUNK
    if has_init:
        (zx_ref, dt_ref, cw_ref, cb_ref, dtb_ref, alog_ref, dskip_ref, norm_ref, e64_ref,
         cinit_ref, sinit_ref, y_ref, conv_out_ref, ssm_out_ref,
         ext_ref, act_ref, state_ref, cst_ref, ybuf_ref, zpad_ref, dtpad_ref) = refs
    else:
        (zx_ref, dt_ref, cw_ref, cb_ref, dtb_ref, alog_ref, dskip_ref, norm_ref, e64_ref,
         y_ref, conv_out_ref, ssm_out_ref,
         ext_ref, act_ref, state_ref, cst_ref, ybuf_ref) = refs
    c = pl.program_id(1)
    n_tile = SSD_INNER // LANES

    @pl.when(c == 0)
    def _start():
        if has_init:
            ext_ref[0:CONV_HIST, :] = jnp.zeros((CONV_HIST, CONV_DIM), F32)
            ext_ref[CONV_HIST - (SSD_CONV - 1):CONV_HIST, :] = cinit_ref[0]
            for j in range(n_tile):
                state_ref[:, j * LANES:(j + 1) * LANES] = sinit_ref[0, j * LANES:(j + 1) * LANES, :].T
        else:
            ext_ref[0:CONV_HIST, :] = jnp.zeros((CONV_HIST, CONV_DIM), F32)
            state_ref[...] = jnp.zeros_like(state_ref)

    if l_blk == T:
        ext_ref[CONV_HIST:CONV_HIST + T, :] = zx_ref[0, :, SSD_INNER:ZX_WIDTH]
        z_of = lambda sl: zx_ref[0, :, sl]
        dt_raw = dt_ref[0]
    else:
        ext_ref[CONV_HIST:CONV_HIST + T, :] = jnp.zeros((T, CONV_DIM), F32)
        ext_ref[CONV_HIST:CONV_HIST + l_blk, :] = zx_ref[0, :, SSD_INNER:ZX_WIDTH]
        zpad_ref[...] = jnp.zeros_like(zpad_ref)
        zpad_ref[0:l_blk, :] = zx_ref[0, :, 0:SSD_INNER]
        dtpad_ref[...] = jnp.zeros_like(dtpad_ref)
        dtpad_ref[0:l_blk, :] = dt_ref[0]
        z_of = lambda sl: zpad_ref[:, sl]
        dt_raw = dtpad_ref[...]

    cw = CONV_DIM // 3
    for cc in range(3):
        sl = slice(cc * cw, (cc + 1) * cw)
        acc = cb_ref[:, sl]
        for j in range(SSD_CONV):
            o = CONV_HIST - (SSD_CONV - 1) + j
            acc = acc + ext_ref[o:o + T, sl] * cw_ref[j:j + 1, sl]
        act_ref[:, sl] = _silu(acc)

    row = lax.broadcasted_iota(jnp.int32, (T, LANES), 0)
    col = lax.broadcasted_iota(jnp.int32, (T, LANES), 1)
    tri = row >= col
    tri_bf = jnp.where(tri, 1.0, 0.0).astype(BF16)
    even = col < SSD_HEAD_DIM

    xdt = dt_raw + dtb_ref[...]
    dtv = jnp.maximum(xdt, 0.0) + jnp.log1p(jnp.exp(-jnp.abs(xdt)))
    if l_blk < T:
        dtv = jnp.where(row < l_blk, dtv, 0.0)
    d_a = dtv * (-jnp.exp(alog_ref[...]))
    cs = _dot3_lhs(tri_bf, d_a)
    cst_ref[...] = cs.T
    ex = _dot3(jnp.concatenate([dtv, cs], axis=0), e64_ref[...])
    dt_ex, cs_ex = ex[0:T], ex[T:2 * T]
    ecs_ex = jnp.exp(cs_ex)
    dd_ex = dt_ex * jnp.exp(cs_ex[T - 1:T] - cs_ex)

    gw = SSD_INNER // SSD_GROUPS
    heads_per_group = SSD_HEADS // SSD_GROUPS
    for g in range(SSD_GROUPS):
        gsl = slice(g * gw, (g + 1) * gw)
        b_g = act_ref[:, SSD_INNER + g * SSD_STATE:SSD_INNER + (g + 1) * SSD_STATE]
        c_off = SSD_INNER + SSD_GROUPS * SSD_STATE
        c_g = act_ref[:, c_off + g * SSD_STATE:c_off + (g + 1) * SSD_STATE].astype(BF16)
        cb = _dot_nt(c_g, b_g.astype(BF16))
        b_gt = b_g.T.astype(BF16)
        x_g = act_ref[:, gsl]
        x_dt = (x_g * dt_ex[:, gsl]).astype(BF16)
        x_dd = (x_g * dd_ex[:, gsl]).astype(BF16)
        st_old = state_ref[:, gsl]
        y_off = _dot(c_g, st_old.astype(BF16)) * ecs_ex[:, gsl]
        state_ref[:, gsl] = st_old * ecs_ex[T - 1:T, gsl] + _dot(b_gt, x_dd)
        for jp in range(heads_per_group // 2):
            h0 = g * heads_per_group + 2 * jp
            psl = slice(jp * LANES, (jp + 1) * LANES)
            osl = slice(g * gw + jp * LANES, g * gw + (jp + 1) * LANES)
            pair = cs_ex[:, osl]
            swapped = pltpu.roll(pair, SSD_HEAD_DIM, axis=1)
            cols = (jnp.where(even, pair, swapped), jnp.where(even, swapped, pair))
            yd = []
            for col, h in zip(cols, (h0, h0 + 1)):
                seg = col - cst_ref[h:h + 1, :]
                lmat = jnp.exp(jnp.where(tri, seg, NEG_INF))
                yd.append(_dot((cb * lmat).astype(BF16), x_dt[:, psl]))
            y_pair = jnp.where(even, yd[0], yd[1]) + y_off[:, psl]
            ybuf_ref[:, osl] = y_pair + dskip_ref[:, osl] * x_g[:, psl]

    for g in range(SSD_GROUPS):
        gsl = slice(g * gw, (g + 1) * gw)
        yg = ybuf_ref[:, gsl] * _silu(z_of(gsl))
        yn = yg * lax.rsqrt(jnp.mean(yg * yg, axis=-1, keepdims=True) + NORM_EPS) * norm_ref[:, gsl]
        y_ref[0, :, gsl] = yn[0:l_blk].astype(y_ref.dtype)

    @pl.when(c == pl.num_programs(1) - 1)
    def _finish():
        lo = CONV_HIST + l_blk - (SSD_CONV - 1)
        conv_out_ref[0] = ext_ref[lo:lo + SSD_CONV - 1, :]
        for j in range(n_tile):
            ssm_out_ref[0, j * LANES:(j + 1) * LANES, :] = state_ref[:, j * LANES:(j + 1) * LANES].T

    ext_ref[0:CONV_HIST, :] = ext_ref[T:T + CONV_HIST, :]


def _ssd(zx3, dt3, consts, init, y_dtype):
    b, L, _ = zx3.shape
    T = SSD_CHUNK
    l_blk = T if L % T == 0 else L
    n_chunks = L // l_blk
    has_init = init is not None
    full = lambda a: pl.BlockSpec(a.shape, lambda i, c: (0,) * a.ndim)
    in_specs = [pl.BlockSpec((1, l_blk, ZX_WIDTH), lambda i, c: (i, c, 0)),
                pl.BlockSpec((1, l_blk, DT_PAD), lambda i, c: (i, c, 0))] + [full(a) for a in consts]
    args = [zx3, dt3, *consts]
    scratch = [pltpu.VMEM((T + CONV_HIST, CONV_DIM), F32),
               pltpu.VMEM((T, CONV_DIM), F32),
               pltpu.VMEM((SSD_STATE, SSD_INNER), F32),
               pltpu.VMEM((LANES, T), F32),
               pltpu.VMEM((T, SSD_INNER), F32)]
    if has_init:
        in_specs += [pl.BlockSpec((1, SSD_CONV - 1, CONV_DIM), lambda i, c: (i, 0, 0)),
                     pl.BlockSpec((1, SSD_INNER, SSD_STATE), lambda i, c: (i, 0, 0))]
        args += list(init)
        scratch += [pltpu.VMEM((T, SSD_INNER), F32), pltpu.VMEM((T, DT_PAD), F32)]
    return pl.pallas_call(
        functools.partial(_ssd_body, l_blk=l_blk, has_init=has_init),
        grid=(b, n_chunks),
        in_specs=in_specs,
        out_specs=[pl.BlockSpec((1, l_blk, SSD_INNER), lambda i, c: (i, c, 0)),
                   pl.BlockSpec((1, SSD_CONV - 1, CONV_DIM), lambda i, c: (i, 0, 0)),
                   pl.BlockSpec((1, SSD_INNER, SSD_STATE), lambda i, c: (i, 0, 0))],
        out_shape=[jax.ShapeDtypeStruct((b, L, SSD_INNER), y_dtype),
                   jax.ShapeDtypeStruct((b, SSD_CONV - 1, CONV_DIM), F32),
                   jax.ShapeDtypeStruct((b, SSD_INNER, SSD_STATE), F32)],
        scratch_shapes=scratch,
        compiler_params=_cparams(("parallel", "arbitrary")),
        name="ssd_init" if has_init else "ssd",
    )(*args)


ATT_UNROLL = 16


def _unroll(n):
    return max(d for d in range(1, ATT_UNROLL + 1) if n % d == 0)


ATT_RES = DILATED_PATTERNS[-1][1]


def _attn_tables():
    B = ATT_BLOCK
    t1 = np.full((len(DILATED_PATTERNS), B, B), -np.inf, np.float32)
    t2 = np.full((len(DILATED_PATTERNS), B, 2 * B), -np.inf, np.float32)
    for p, (window, dil) in enumerate(DILATED_PATTERNS):
        m = ATT_RES // dil
        w = B // m
        rho = np.arange(B)
        c = m * (rho % w) + rho // w
        ck2 = np.concatenate([c - B, c])
        for tab, ck in ((t1, c), (t2, ck2)):
            delta = c[:, None] - ck[None, :]
            ok = (delta >= 0) & (delta <= window // dil)
            tab[p] = np.where(ok, -(dil * delta).astype(np.float32), -np.inf)
    return jnp.asarray(t1), jnp.asarray(t2)


def _attn_body(slopes_ref, t1_ref, t2_ref, q_ref, k_ref, v_ref, o_ref, qd_ref, kd_ref, vd_ref, op_ref, lp_ref,
               stage_ref, b1_ref, b2_ref, in_sem, out_sem, *, seq):
    B = ATT_BLOCK
    hp = pl.program_id(1)
    even = lax.broadcasted_iota(jnp.int32, (B, LANES), 1) < ATT_HEAD_DIM
    scale = ATT_HEAD_DIM ** -0.5

    def in_copy(a, r):
        src, dst = ((q_ref, qd_ref), (k_ref, kd_ref), (v_ref, vd_ref))[a]
        return pltpu.make_async_copy(src.at[:, r, :], dst.at[r], in_sem.at[a, r])
    for r in range(ATT_RES):
        for a in range(3):
            in_copy(a, r).start()

    for e in range(2):
        slope = slopes_ref[2 * hp + e]
        for p in range(len(DILATED_PATTERNS)):
            b1_ref[e * len(DILATED_PATTERNS) + p] = t1_ref[p] * slope
            b2_ref[e * len(DILATED_PATTERNS) + p] = t2_ref[p] * slope

    for r in range(ATT_RES):
        for a in range(3):
            in_copy(a, r).wait()

    def block(p, r_d, n, first):
        dil = DILATED_PATTERNS[p][1]
        m = ATT_RES // dil
        w = B // m

        def slab(j, nblk):
            start = nblk * w
            return r_d + dil * j, pl.ds(start if isinstance(start, int) else pl.multiple_of(start, w), w)

        def gather(src, nblk):
            parts = []
            for j in range(m):
                r, rows = slab(j, nblk)
                parts.append(src[r, rows, :])
            return parts[0] if m == 1 else jnp.concatenate(parts, axis=0)

        qb = gather(qd_ref, n) * scale
        if first:
            kb, vb = gather(kd_ref, n), gather(vd_ref, n)
            bias = lambda e: b1_ref[e * len(DILATED_PATTERNS) + p]
        else:
            kb = jnp.concatenate([gather(kd_ref, n - 1), gather(kd_ref, n)], axis=0)
            vb = jnp.concatenate([gather(vd_ref, n - 1), gather(vd_ref, n)], axis=0)
            bias = lambda e: b2_ref[e * len(DILATED_PATTERNS) + p]
        q2 = jnp.concatenate([jnp.where(even, qb, 0.0), jnp.where(even, 0.0, qb)], axis=0).astype(BF16)
        s = _dot_nt(q2, kb.astype(BF16)) + jnp.concatenate([bias(0), bias(1)], axis=0)
        mx = jnp.max(s, axis=-1, keepdims=True)
        pr = jnp.exp(s - mx)
        l = jnp.sum(pr, axis=-1, keepdims=True)
        o2 = _dot(pr.astype(BF16), vb.astype(BF16)) / l
        lse2 = jnp.broadcast_to(mx + jnp.log(l), (2 * B, LANES))
        o = jnp.where(even, o2[:B], o2[B:])
        lse = jnp.where(even, lse2[:B], lse2[B:])
        for j in range(m):
            r, rows = slab(j, n)
            op_ref[p, r, rows, :] = o[j * w:(j + 1) * w]
            lp_ref[p, r, rows, :] = lse[j * w:(j + 1) * w]

    for p, (_, dil) in enumerate(DILATED_PATTERNS):
        nb = seq // (dil * B)
        u_first = _unroll(dil)

        def first(i, carry, p=p, u_first=u_first):
            for u in range(u_first):
                block(p, i * u_first + u, 0, True)
            return carry
        lax.fori_loop(0, dil // u_first, first, 0)
        n_later = dil * (nb - 1)
        if n_later:
            u_later = _unroll(n_later)

            def later(i, carry, p=p, nb=nb, u_later=u_later):
                for u in range(u_later):
                    j = i * u_later + u
                    block(p, j // (nb - 1), j % (nb - 1) + 1, False)
                return carry
            lax.fori_loop(0, n_later // u_later, later, 0)

    def out_copy(r):
        return pltpu.make_async_copy(stage_ref.at[r], o_ref.at[:, r, :], out_sem.at[r])
    for r in range(ATT_RES):
        l0, l1, l2 = lp_ref[0, r], lp_ref[1, r], lp_ref[2, r]
        m = jnp.maximum(jnp.maximum(l0, l1), l2)
        w0, w1, w2 = jnp.exp(l0 - m), jnp.exp(l1 - m), jnp.exp(l2 - m)
        num = w0 * op_ref[0, r] + w1 * op_ref[1, r] + w2 * op_ref[2, r]
        stage_ref[r] = num / (w0 + w1 + w2)
        out_copy(r).start()
    for r in range(ATT_RES):
        out_copy(r).wait()


def _attn_prompt(q, k, v, slopes, batch, seq):
    n_hp = ATT_HEADS // 2
    n_pat = len(DILATED_PATTERNS)
    assert len(DILATED_PATTERNS) == 3 and all(ATT_RES % d == 0 and seq % (d * ATT_BLOCK) == 0 and w // d == ATT_BLOCK
                                              for w, d in DILATED_PATTERNS)
    per = seq // ATT_RES
    by_res = lambda a: a.reshape(batch * per, ATT_RES, ATT_WIDTH)
    blk = pl.BlockSpec((per, ATT_RES, LANES), lambda b, h: (b, 0, h))
    t1, t2 = _attn_tables()
    full = lambda a: pl.BlockSpec(a.shape, lambda b, h: (0,) * a.ndim)
    res = pltpu.VMEM((ATT_RES, per, LANES), F32)
    out = pl.pallas_call(
        functools.partial(_attn_body, seq=seq),
        grid=(batch, n_hp),
        in_specs=[pl.BlockSpec(memory_space=pltpu.SMEM), full(t1), full(t2), blk, blk, blk],
        out_specs=blk,
        out_shape=jax.ShapeDtypeStruct((batch * per, ATT_RES, ATT_WIDTH), F32),
        scratch_shapes=[res, res, res,
                        pltpu.VMEM((n_pat, ATT_RES, per, LANES), F32), pltpu.VMEM((n_pat, ATT_RES, per, LANES), F32),
                        res,
                        pltpu.VMEM((2 * n_pat, ATT_BLOCK, ATT_BLOCK), F32),
                        pltpu.VMEM((2 * n_pat, ATT_BLOCK, 2 * ATT_BLOCK), F32),
                        pltpu.SemaphoreType.DMA((3, ATT_RES)), pltpu.SemaphoreType.DMA((ATT_RES,))],
        compiler_params=_cparams(("parallel", "parallel")),
        name="attn_prompt",
    )(slopes, t1, t2, by_res(q), by_res(k), by_res(v))
    return out.reshape(batch * seq, ATT_WIDTH)


DEC_ROWS = SUBLANES


def _decode_tables(t_new, m_cache):
    slopes = 2.0 ** (-8.0 * np.arange(1, ATT_HEADS + 1) / ATT_HEADS)

    def mult(dist):
        return sum(1 for window, dil in DILATED_PATTERNS if 0 <= dist <= window and dist % dil == 0)

    bias_c = np.zeros((ATT_HEADS, DEC_ROWS, m_cache), np.float32)
    mult_c = np.ones((DEC_ROWS, m_cache), np.float32)
    bias_n = np.full((ATT_HEADS, DEC_ROWS, LANES), -np.inf, np.float32)
    mult_n = np.zeros((DEC_ROWS, LANES), np.float32)
    for t in range(t_new):
        dist = m_cache + t - np.arange(m_cache)
        mu = np.array([mult(d) for d in dist], np.float32)
        mult_c[t] = mu
        bias_c[:, t, :] = np.where(mu > 0, -slopes[:, None] * dist[None, :], -np.inf)
        for t2 in range(t_new):
            if mult(t - t2):
                bias_n[:, t, t2] = -slopes * (t - t2)
                mult_n[t, t2] = mult(t - t2)
    bias_n[:, t_new:, 0] = 0.0
    mult_n[t_new:, 0] = 1.0
    return jnp.asarray(bias_c), jnp.asarray(mult_c), jnp.asarray(bias_n), jnp.asarray(mult_n)


def _attn_dec_body(q_ref, knt_ref, vnt_ref, ck_ref, cv_ref, bc_ref, mc_ref, bn_ref, mn_ref, o_ref):
    scale = ATT_HEAD_DIM ** -0.5
    for h in range(ATT_HEADS):
        qh = (q_ref[0, h] * scale).astype(BF16)
        sc = _dot(qh, ck_ref[0, h].astype(BF16)) + bc_ref[h]
        sn = _dot(qh, knt_ref[0, h].astype(BF16)) + bn_ref[h]
        m = jnp.maximum(jnp.max(sc, axis=-1, keepdims=True), jnp.max(sn, axis=-1, keepdims=True))
        pc = jnp.exp(sc - m) * mc_ref[...]
        pn = jnp.exp(sn - m) * mn_ref[...]
        inv_l = 1.0 / (jnp.sum(pc, axis=-1, keepdims=True) + jnp.sum(pn, axis=-1, keepdims=True))
        o_ref[0, h] = (_dot_nt(cv_ref[0, h].astype(BF16), (pc * inv_l).astype(BF16))
                       + _dot_nt(vnt_ref[0, h].astype(BF16), (pn * inv_l).astype(BF16)))


def _attn_decode(q4, knt, vnt, cache_kt, cache_vt, t_new):
    b = q4.shape[0]
    m_cache = cache_kt.shape[3]
    assert t_new <= DEC_ROWS
    tables = _decode_tables(t_new, m_cache)
    per_b = lambda a: pl.BlockSpec((1,) + a.shape[1:], lambda i: (i, 0, 0, 0))
    full = lambda a: pl.BlockSpec(a.shape, lambda i: (0,) * a.ndim)
    out_shape = jax.ShapeDtypeStruct((b, ATT_HEADS, ATT_HEAD_DIM, DEC_ROWS), F32)
    return pl.pallas_call(
        _attn_dec_body,
        grid=(b,),
        in_specs=[per_b(a) for a in (q4, knt, vnt, cache_kt, cache_vt)] + [full(a) for a in tables],
        out_specs=per_b(out_shape),
        out_shape=out_shape,
        compiler_params=_cparams(("parallel",)),
        name="attn_decode",
    )(q4, knt, vnt, cache_kt, cache_vt, *tables)


def _out_proj_body(x_ref, y_ref, o_ref, an_ref, wy_ref, wo_ref, nf_ref, rw_ref, rb_ref, cnt0_ref,
                   h_ref, xn_ref, idx_ref, gate_ref, rank_ref, cnt_ref, carry_ref, *, tm):
    i = pl.program_id(0)

    @pl.when(i == 0)
    def _start():
        carry_ref[...] = cnt0_ref[...]

    on = _rms(o_ref[...], an_ref[...]).astype(BF16)
    mixed = _dot(y_ref[...].astype(BF16), wy_ref[...]) + _dot(on, wo_ref[...])
    h = x_ref[...] + mixed
    h_ref[...] = h
    xn = _rms(h, nf_ref[...])
    xn_ref[...] = xn
    logits = _dot_nt(rw_ref[...], xn.astype(BF16)) + rb_ref[...]

    e_iota = lax.broadcasted_iota(jnp.int32, (N_EXPERTS, tm), 0)
    vals, idxs, sels = [], [], []
    cur = logits
    for _ in range(TOP_K):
        mx = jnp.max(cur, axis=0, keepdims=True)
        ix = jnp.min(jnp.where(cur == mx, e_iota, N_EXPERTS), axis=0, keepdims=True)
        sel = e_iota == ix
        cur = jnp.where(sel, NEG_INF, cur)
        vals.append(mx)
        idxs.append(ix)
        sels.append(sel)
    ex = [jnp.exp(v - vals[0]) for v in vals]
    den = functools.reduce(lambda a, b: a + b, ex)

    sel_any = functools.reduce(jnp.logical_or, sels)
    sel_f = jnp.where(sel_any, 1.0, 0.0)
    r = lax.broadcasted_iota(jnp.int32, (tm, tm), 0)
    c = lax.broadcasted_iota(jnp.int32, (tm, tm), 1)
    upper = jnp.where(r <= c, 1.0, 0.0).astype(BF16)
    cum = _dot(sel_f.astype(BF16), upper)
    before = carry_ref[:, 0:1] + cum - sel_f
    for k in range(TOP_K):
        idx_ref[k:k + 1, :] = idxs[k]
        gate_ref[k:k + 1, :] = ex[k] / den
        rank_ref[k:k + 1, :] = jnp.sum(jnp.where(sels[k], before, 0.0), axis=0, keepdims=True).astype(jnp.int32)
    idx_ref[TOP_K:SUBLANES, :] = jnp.zeros((SUBLANES - TOP_K, tm), jnp.int32)
    gate_ref[TOP_K:SUBLANES, :] = jnp.zeros((SUBLANES - TOP_K, tm), F32)
    rank_ref[TOP_K:SUBLANES, :] = jnp.zeros((SUBLANES - TOP_K, tm), jnp.int32)
    carry_ref[...] = carry_ref[...] + jnp.max(cum, axis=1, keepdims=True)
    cnt_ref[...] = carry_ref[...]


def _out_proj(x2d, y2d, o2d, consts, cnt0, tm):
    n = x2d.shape[0]
    tm = min(tm, n)
    row = lambda w: pl.BlockSpec((tm, w), lambda i: (i, 0))
    colb = pl.BlockSpec((SUBLANES, tm), lambda i: (0, i))
    full = lambda a: pl.BlockSpec(a.shape, lambda i: (0,) * a.ndim)
    return pl.pallas_call(
        functools.partial(_out_proj_body, tm=tm),
        grid=(n // tm,),
        in_specs=[row(D_MODEL), row(SSD_INNER), row(ATT_WIDTH)] + [full(a) for a in consts] + [full(cnt0)],
        out_specs=[row(D_MODEL), row(D_MODEL), colb, colb, colb, full(cnt0)],
        out_shape=[jax.ShapeDtypeStruct((n, D_MODEL), F32), jax.ShapeDtypeStruct((n, D_MODEL), F32),
                   jax.ShapeDtypeStruct((SUBLANES, n), jnp.int32), jax.ShapeDtypeStruct((SUBLANES, n), F32),
                   jax.ShapeDtypeStruct((SUBLANES, n), jnp.int32), jax.ShapeDtypeStruct(cnt0.shape, F32)],
        scratch_shapes=[pltpu.VMEM(cnt0.shape, F32)],
        compiler_params=_cparams(("arbitrary",)),
        name="out_proj",
    )(x2d, y2d, o2d, *consts, cnt0)


ISSUE_UNROLL = 8
PAD_BITS = tuple(1 << b for b in range((MOE_ROWS // SUBLANES).bit_length()))


def _dispatch_body(slot_ref, from_ref, n_ref, tail_ref, xa_ref, xb_ref, xs_ref, zero_ref, stage_ref, sem, row_sems,
                   *, tm, tiles_a, tiles_b, n_blocks):
    i = pl.program_id(0)

    @pl.when(i == 0)
    def _pad_fill():
        zero_ref[...] = jnp.zeros_like(zero_ref)
        one_row = lambda slot: pltpu.make_async_copy(zero_ref.at[pl.ds(0, 1)], xs_ref.at[pl.ds(slot, 1)], sem)
        block = lambda j: pltpu.make_async_copy(zero_ref, xs_ref.at[pl.ds(j * MOE_ROWS, MOE_ROWS)], sem)

        def per_expert(e, carry):
            first, n_pad = from_ref[e], n_ref[e]
            head = jnp.minimum((-first) % SUBLANES, n_pad)

            def start(r, c):
                one_row(first + r).start()
                return c

            def wait(r, c):
                one_row(0).wait()
                return c
            lax.fori_loop(0, head, start, 0)
            lax.fori_loop(0, head, wait, 0)
            units = (n_pad - head) // SUBLANES
            for begin in (True, False):
                done = first + head
                for b in PAD_BITS:
                    @pl.when((units & b) != 0)
                    def _piece(done=done, rows=b * SUBLANES):
                        cp = pltpu.make_async_copy(zero_ref.at[pl.ds(0, rows)],
                                                   xs_ref.at[pl.ds(pl.multiple_of(done, SUBLANES), rows)], sem)
                        if begin:
                            cp.start()
                        else:
                            cp.wait()
                    done = done + (units & b) * SUBLANES
            return carry
        lax.fori_loop(0, N_EXPERTS, per_expert, 0)

        def tail(j, carry):
            block(j).start()
            block(j).wait()
            return carry
        lax.fori_loop(tail_ref[0], n_blocks, tail, 0)

    def drain(parity):
        for _ in range(TOP_K):
            pltpu.make_async_copy(stage_ref.at[parity], xs_ref.at[pl.ds(0, tm)], row_sems.at[parity]).wait()

    for parity in range(2):
        @pl.when(i % 2 == parity)
        def _step(parity=parity):
            @pl.when(i < tiles_a)
            def _first_group():
                stage_ref[parity] = xa_ref[...]

            @pl.when(i >= tiles_a)
            def _second_group():
                stage_ref[parity] = xb_ref[...]

            def issue(j, carry):
                t0 = pl.multiple_of(j * ISSUE_UNROLL, ISSUE_UNROLL)
                for u in range(ISSUE_UNROLL):
                    for k in range(TOP_K):
                        slot = slot_ref[t0 * TOP_K + (u * TOP_K + k)]
                        pltpu.make_async_copy(stage_ref.at[parity, pl.ds(t0 + u, 1)],
                                              xs_ref.at[pl.ds(slot, 1)], row_sems.at[parity]).start()
                return carry
            lax.fori_loop(0, tm // ISSUE_UNROLL, issue, 0)

            @pl.when(i > 0)
            def _previous():
                drain(1 - parity)

            @pl.when(i == tiles_a + tiles_b - 1)
            def _last():
                drain(parity)


def _dispatch(slot, pad_from, pad_n, tail_block, xa, xb, n_blocks, tm):
    tiles_a, tiles_b = xa.shape[0] // tm, xb.shape[0] // tm
    assert tiles_a * tm == xa.shape[0] and tiles_b * tm == xb.shape[0]
    smem = pl.BlockSpec(memory_space=pltpu.SMEM)
    hbm = pl.BlockSpec(memory_space=pl.ANY)
    return pl.pallas_call(
        functools.partial(_dispatch_body, tm=tm, tiles_a=tiles_a, tiles_b=tiles_b, n_blocks=n_blocks),
        grid=(tiles_a + tiles_b,),
        in_specs=[pl.BlockSpec((tm * TOP_K,), lambda i: (i,), memory_space=pltpu.SMEM), smem, smem, smem,
                  pl.BlockSpec((tm, D_MODEL), lambda i: (jnp.minimum(i, tiles_a - 1), 0)),
                  pl.BlockSpec((tm, D_MODEL), lambda i: (jnp.maximum(i - tiles_a, 0), 0))],
        out_specs=hbm,
        out_shape=jax.ShapeDtypeStruct((n_blocks * MOE_ROWS, D_MODEL), F32),
        scratch_shapes=[pltpu.VMEM((MOE_ROWS, D_MODEL), F32), pltpu.VMEM((2, tm, D_MODEL), F32),
                        pltpu.SemaphoreType.DMA(()), pltpu.SemaphoreType.DMA((2,))],
        compiler_params=_cparams(("arbitrary",)),
        name="dispatch",
    )(slot, pad_from, pad_n, tail_block, xa, xb)


MOE_COLS = 2 * LANES


def _glu_perm():
    src = np.concatenate([np.arange(0, MOE_COLS, 2), np.arange(1, MOE_COLS, 2)])
    perm = np.zeros((MOE_COLS, MOE_COLS), np.float32)
    perm[src, np.arange(MOE_COLS)] = 1.0
    return jnp.asarray(perm, dtype=BF16)


def _experts_body(be_ref, bv_ref, nx_ref, xs_ref, w1_ref, p_ref, b1_ref, w2_ref, b2_ref, ys_ref,
                  w1f_ref, w1s_ref, w2s_ref, g_ref, sem):
    j = pl.program_id(0)
    valid = bv_ref[j]
    e = be_ref[j]

    @pl.when(jnp.logical_and(valid > 0, jnp.logical_or(j == 0, e != be_ref[jnp.maximum(j - 1, 0)])))
    def _new_expert():
        slot, nxt = nx_ref[1, e], nx_ref[0, e]
        fetch = lambda ex, s: pltpu.make_async_copy(w1_ref.at[ex], w1f_ref.at[s], sem.at[s])

        @pl.when(j == 0)
        def _first_fetch():
            fetch(e, slot).start()
        fetch(e, slot).wait()

        @pl.when(nxt >= 0)
        def _next_fetch():
            fetch(nxt, 1 - slot).start()
        for c in range(w1s_ref.shape[1] // MOE_COLS):
            cols = slice(c * MOE_COLS, (c + 1) * MOE_COLS)
            w1s_ref[:, cols] = _dot(w1f_ref[slot, :, cols].astype(BF16), p_ref[...]).astype(BF16)
        w2s_ref[...] = w2_ref[0].astype(BF16)

    @pl.when(valid > 0)
    def _compute():
        x = xs_ref[...].astype(BF16)
        for c in range(g_ref.shape[1] // LANES):
            cols = slice(c * MOE_COLS, (c + 1) * MOE_COLS)
            hc = _dot(x, w1s_ref[:, cols]) + b1_ref[0, :, cols]
            glu = jnp.minimum(hc[:, :LANES], SWIGLU_LIMIT)
            lin = jnp.clip(hc[:, LANES:], -SWIGLU_LIMIT, SWIGLU_LIMIT)
            g_ref[:, c * LANES:(c + 1) * LANES] = (glu * jax.nn.sigmoid(SWIGLU_ALPHA * glu) * (lin + 1.0)).astype(BF16)
        ys_ref[...] = _dot(g_ref[...], w2s_ref[...]) + b2_ref[0]

    @pl.when(valid <= 0)
    def _empty():
        ys_ref[...] = jnp.zeros_like(ys_ref)


def _experts(block_expert, block_valid, counts, xs, w1, b1p, w2, b2):
    n_blocks = xs.shape[0] // MOE_ROWS
    perm = _glu_perm()
    active = counts > 0
    ids = jnp.arange(N_EXPERTS, dtype=jnp.int32)
    later = jnp.where(active[None, :] & (ids[None, :] > ids[:, None]), ids[None, :], N_EXPERTS)
    nxt = jnp.min(later, axis=1)
    nx = jnp.stack([jnp.where(nxt < N_EXPERTS, nxt, -1), (jnp.cumsum(active) - 1) % 2]).astype(jnp.int32)
    wspec = lambda a: pl.BlockSpec((1,) + a.shape[1:], lambda j, be, bv, nx: (be[j], 0, 0))
    grid_spec = pltpu.PrefetchScalarGridSpec(
        num_scalar_prefetch=3,
        grid=(n_blocks,),
        in_specs=[pl.BlockSpec((MOE_ROWS, D_MODEL), lambda j, be, bv, nx: (j, 0)),
                  pl.BlockSpec(memory_space=pl.ANY), pl.BlockSpec(perm.shape, lambda j, be, bv, nx: (0, 0)),
                  wspec(b1p), wspec(w2), wspec(b2)],
        out_specs=pl.BlockSpec((MOE_ROWS, D_MODEL), lambda j, be, bv, nx: (j, 0)),
        scratch_shapes=[pltpu.VMEM((2,) + w1.shape[1:], F32), pltpu.VMEM(w1.shape[1:], BF16),
                        pltpu.VMEM(w2.shape[1:], BF16), pltpu.VMEM((MOE_ROWS, w2.shape[1]), BF16),
                        pltpu.SemaphoreType.DMA((2,))],
    )
    return pl.pallas_call(
        _experts_body,
        grid_spec=grid_spec,
        out_shape=jax.ShapeDtypeStruct(xs.shape, F32),
        compiler_params=_cparams(("arbitrary",)),
        name="experts",
    )(block_expert, block_valid, nx, xs, w1, perm, b1p, w2, b2)


COMBINE_TM = 256
RUN_UNIT = 16
RUN_BITS = tuple(1 << b for b in reversed(range((COMBINE_TM // RUN_UNIT).bit_length())))
BUF_ROWS = -(-(TOP_K * COMBINE_TM + N_EXPERTS * (RUN_UNIT - 1 + SUBLANES - 1)) // LANES) * LANES


def _combine_meta(idx_all, slot_all, pad_start):
    n_tok = idx_all.shape[1]
    tiles = n_tok // COMBINE_TM
    experts = jnp.arange(N_EXPERTS, dtype=jnp.int32)
    hit = idx_all[:TOP_K, :, None] == experts
    cnt = jnp.sum(hit.reshape(TOP_K, tiles, COMBINE_TM, N_EXPERTS), axis=(0, 2), dtype=jnp.int32)
    run_start = pad_start[None, :] + jnp.cumsum(cnt, axis=0) - cnt
    lead = run_start % SUBLANES
    units = jnp.where(cnt > 0, (cnt + lead + RUN_UNIT - 1) // RUN_UNIT, 0)
    first_row = RUN_UNIT * (jnp.cumsum(units, axis=1) - units)
    meta = jnp.concatenate([run_start - lead, units, first_row, jnp.zeros_like(cnt)], axis=1).astype(jnp.int32)
    shift = jnp.repeat(first_row + lead - run_start, COMBINE_TM, axis=0)
    col = slot_all[:TOP_K] + jnp.sum(jnp.where(hit, shift[None], 0), axis=2, dtype=jnp.int32)
    return meta.reshape(tiles, 1, 4 * N_EXPERTS), col.T


def _combine_body(meta_ref, next_ref, col_ref, gate_ref, h_ref, nf_ref, ys_ref, y_ref, buf_ref, sem, *, n):
    i = pl.program_id(0)

    def runs(meta, half, start):
        for e in range(N_EXPERTS):
            src0, units, dst0 = meta[0, 0, e], meta[0, 0, N_EXPERTS + e], meta[0, 0, 2 * N_EXPERTS + e]

            def pieces(bits, done):
                for b in bits:
                    rows = b * RUN_UNIT

                    @pl.when((units & b) != 0)
                    def _piece(done=done, rows=rows):
                        dst = pl.multiple_of(dst0 + done, RUN_UNIT)
                        cp = pltpu.make_async_copy(ys_ref.at[pl.ds(pl.multiple_of(src0 + done, SUBLANES), rows)],
                                                   buf_ref.at[half, pl.ds(dst, rows)], sem.at[half])
                        if start:
                            cp.start()
                        else:
                            cp.wait()
                    done = done + (units & b) * RUN_UNIT
                return done
            done = pieces(RUN_BITS[-2:], jnp.int32(0))

            @pl.when(units >= RUN_BITS[-3])
            def _long_run(done=done):
                pieces(RUN_BITS[:-2], done)

    def per_half(half):
        @pl.when(i + 1 < n)
        def _fetch_next():
            runs(next_ref, 1 - half, True)
        runs(meta_ref, half, False)
        rows = lax.broadcasted_iota(jnp.int32, (COMBINE_TM, BUF_ROWS), 1)
        pick = jnp.zeros((COMBINE_TM, BUF_ROWS), F32)
        for k in range(TOP_K):
            pick = jnp.where(rows == col_ref[:, k:k + 1], gate_ref[:, k:k + 1], pick)
        acc = h_ref[...] + _dot(pick.astype(BF16), buf_ref[half].astype(BF16))
        y_ref[...] = _rms(acc, nf_ref[...])

    @pl.when(i == 0)
    def _first():
        buf_ref[...] = jnp.zeros_like(buf_ref)
        runs(meta_ref, 0, True)

    @pl.when(i % 2 == 0)
    def _even():
        per_half(0)

    @pl.when(i % 2 == 1)
    def _odd():
        per_half(1)


def _combine(meta, col_rows, gates_rows, h, norm_final, ys, first_tile):
    n = h.shape[0]
    tm = COMBINE_TM
    assert n % tm == 0
    last = first_tile + n // tm - 1
    smem_tile = lambda f: pl.BlockSpec((1, 1, meta.shape[2]), lambda i: (f(i), 0, 0), memory_space=pltpu.SMEM)
    return pl.pallas_call(
        functools.partial(_combine_body, n=n // tm),
        grid=(n // tm,),
        in_specs=[smem_tile(lambda i: i + first_tile), smem_tile(lambda i: jnp.minimum(i + first_tile + 1, last)),
                  pl.BlockSpec((tm, TOP_K), lambda i: (i + first_tile, 0)),
                  pl.BlockSpec((tm, TOP_K), lambda i: (i, 0)),
                  pl.BlockSpec((tm, D_MODEL), lambda i: (i, 0)),
                  pl.BlockSpec((1, D_MODEL), lambda i: (0, 0)),
                  pl.BlockSpec(memory_space=pl.ANY)],
        out_specs=pl.BlockSpec((tm, D_MODEL), lambda i: (i, 0)),
        out_shape=jax.ShapeDtypeStruct((n, D_MODEL), F32),
        scratch_shapes=[pltpu.VMEM((2, BUF_ROWS, D_MODEL), F32), pltpu.SemaphoreType.DMA((2,))],
        compiler_params=_cparams(("arbitrary",)),
        name="combine",
    )(meta, meta, col_rows, gates_rows, h, norm_final, ys)


def _expansion(width):
    h = np.arange(LANES)[:, None]
    c = np.arange(SSD_HEADS * width)[None, :] // width
    return jnp.asarray((h == c).astype(np.float32), dtype=BF16)


def _pad_lanes(v):
    return jnp.pad(v.astype(F32), (0, LANES - v.shape[0]))[None, :]


def kernel(x_prompt, x_sample, state_conv, state_ssm, cache_win_k, cache_win_v, norm_mix, w_in, conv_w, conv_b,
           dt_bias, a_log, d_skip, ssd_norm, att_norm, w_out, norm_ffn, router_w, router_b, w1, b1, w2, b2,
           norm_final):
    depth = w_in.shape[0]
    assert depth == 1
    bp, seq, _ = x_prompt.shape
    bs, t_new, _ = x_sample.shape
    n_p, n_s = bp * seq, bs * t_new
    l = 0

    o_dt = SSD_INNER + CONV_DIM
    o_q = o_dt + SSD_HEADS
    wl = w_in[l]
    w_cat = jnp.concatenate([wl[:, :o_dt], wl[:, o_q:], jnp.pad(wl[:, o_dt:o_q], ((0, 0), (0, DT_PAD - SSD_HEADS)))],
                            axis=1).astype(BF16)
    g_mix = norm_mix[l][None, :]
    ssd_consts = (conv_w[l], conv_b[l][None, :], _pad_lanes(dt_bias[l]), _pad_lanes(a_log[l]),
                  jnp.repeat(d_skip[l], SSD_HEAD_DIM)[None, :], ssd_norm[l][None, :],
                  _expansion(SSD_HEAD_DIM))
    slopes = jnp.exp2(-8.0 * jnp.arange(1, ATT_HEADS + 1, dtype=F32) / ATT_HEADS)
    out_consts = (att_norm[l][None, :], w_out[l][:SSD_INNER].astype(BF16), w_out[l][SSD_INNER:].astype(BF16),
                  norm_ffn[l][None, :], router_w[l].T.astype(BF16), router_b[l][:, None].astype(F32))
    b1p = b1[l].reshape(N_EXPERTS, -1, LANES, 2).transpose(0, 1, 3, 2).reshape(N_EXPERTS, 1, -1)
    b2r = b2[l][:, None, :]

    zx, q, k, v, dt, kt, vt = _in_proj(x_prompt.reshape(n_p, D_MODEL), g_mix, w_cat, 512, seq)
    y_p, conv_p, ssm_p = _ssd(zx.reshape(bp, seq, ZX_WIDTH), dt.reshape(bp, seq, DT_PAD), ssd_consts, None, BF16)
    o_p = _attn_prompt(q, k, v, slopes, bp, seq)
    keep = min(DILATED_PATTERNS[-1][0], seq)
    k_p = kt.reshape(bp, ATT_HEADS, ATT_HEAD_DIM, seq).transpose(0, 3, 1, 2)[:, seq - keep:]
    v_p = vt.reshape(bp, ATT_HEADS, ATT_HEAD_DIM, seq).transpose(0, 3, 1, 2)[:, seq - keep:]

    zx_s, q_s, k_s, v_s, dt_s = _in_proj(x_sample.reshape(n_s, D_MODEL), g_mix, w_cat, 256)
    init = (state_conv[l], state_ssm[l].reshape(bs, SSD_INNER, SSD_STATE))
    y_s, conv_s, ssm_s = _ssd(zx_s.reshape(bs, t_new, ZX_WIDTH), dt_s.reshape(bs, t_new, DT_PAD), ssd_consts, init, F32)

    def head_major(a):
        a = a.reshape(bs, t_new, ATT_HEADS, ATT_HEAD_DIM).transpose(0, 2, 1, 3)
        return jnp.pad(a, ((0, 0), (0, 0), (0, DEC_ROWS - t_new), (0, 0)))

    def head_major_t(a):
        a = a.reshape(bs, t_new, ATT_HEADS, ATT_HEAD_DIM).transpose(0, 2, 3, 1)
        return jnp.pad(a, ((0, 0), (0, 0), (0, 0), (0, LANES - t_new)))
    o_s = _attn_decode(head_major(q_s), head_major_t(k_s), head_major_t(v_s),
                       cache_win_k[l].transpose(0, 2, 3, 1), cache_win_v[l].transpose(0, 2, 3, 1), t_new)
    o_s = o_s[:, :, :, :t_new].transpose(0, 3, 1, 2)

    cnt0 = jnp.zeros((N_EXPERTS, LANES), F32)
    h_p, xn_p, idx_p, gate_p, rank_p, cnt_p = _out_proj(x_prompt.reshape(n_p, D_MODEL), y_p.reshape(n_p, SSD_INNER),
                                                        o_p, out_consts, cnt0, 512)
    h_s, xn_s, idx_s, gate_s, rank_s, cnt_all = _out_proj(x_sample.reshape(n_s, D_MODEL), y_s.reshape(n_s, SSD_INNER),
                                                          o_s.reshape(n_s, ATT_WIDTH), out_consts, cnt_p, 512)

    counts = cnt_all[:, 0].astype(jnp.int32)
    padded = (counts + MOE_ROWS - 1) // MOE_ROWS * MOE_ROWS
    pad_end = jnp.cumsum(padded)
    pad_start = pad_end - padded
    n_blocks = -(-((n_p + n_s) * TOP_K) // MOE_ROWS) + N_EXPERTS
    blk0 = jnp.arange(n_blocks, dtype=jnp.int32) * MOE_ROWS
    owner = blk0[:, None] >= pad_end[None, :]
    block_expert = jnp.minimum(jnp.sum(owner, axis=1), N_EXPERTS - 1).astype(jnp.int32)
    onehot = block_expert[:, None] == jnp.arange(N_EXPERTS, dtype=jnp.int32)[None, :]
    used = jnp.sum(jnp.where(onehot, (blk0[:, None] - pad_start[None, :]), 0), axis=1)
    block_valid = jnp.clip(jnp.sum(jnp.where(onehot, counts[None, :], 0), axis=1) - used, 0, MOE_ROWS).astype(jnp.int32)

    tm = min(256, n_s)
    idx_all = jnp.concatenate([idx_p, idx_s], axis=1)
    rank_all = jnp.concatenate([rank_p, rank_s], axis=1)
    first_slot = functools.reduce(lambda acc, e: jnp.where(idx_all == e, pad_start[e], acc), range(N_EXPERTS),
                                  jnp.zeros_like(idx_all))
    slot_all = first_slot + rank_all
    xs = _dispatch(slot_all[:TOP_K].T.reshape(-1), pad_start + counts, padded - counts, pad_end[N_EXPERTS - 1:] // MOE_ROWS,
                   xn_p, xn_s, n_blocks, tm)
    ys = _experts(block_expert, block_valid, counts, xs, w1[l], b1p, w2[l], b2r)
    nfin = norm_final[None, :]
    meta, col_rows = _combine_meta(idx_all, slot_all, pad_start)
    y_prompt = _combine(meta, col_rows, gate_p[:TOP_K].T, h_p, nfin, ys, 0)
    y_sample = _combine(meta, col_rows, gate_s[:TOP_K].T, h_s, nfin, ys, n_p // COMBINE_TM)

    return (y_prompt.reshape(bp, seq, D_MODEL), y_sample.reshape(bs, t_new, D_MODEL),
            conv_p[None], ssm_p.reshape(1, bp, SSD_HEADS, SSD_HEAD_DIM, SSD_STATE), k_p[None], v_p[None],
            conv_s[None], ssm_s.reshape(1, bs, SSD_HEADS, SSD_HEAD_DIM, SSD_STATE),
            k_s.reshape(1, bs, t_new, ATT_HEADS, ATT_HEAD_DIM), v_s.reshape(1, bs, t_new, ATT_HEADS, ATT_HEAD_DIM))
```

```python
import functools

import jax
import jax.numpy as jnp
import numpy as np
from jax import lax
from jax.experimental import pallas as pl
from jax.experimental.pallas import tpu as pltpu

F32 = jnp.float32
BF16 = jnp.bfloat16

D_MODEL = 1024
SSD_HEADS = 16
SSD_HEAD_DIM = 64
SSD_INNER = SSD_HEADS * SSD_HEAD_DIM
SSD_GROUPS = 2
SSD_STATE = 128
SSD_CONV = 4
SSD_CHUNK = 128
CONV_DIM = SSD_INNER + 2 * SSD_GROUPS * SSD_STATE
ATT_HEADS = 8
ATT_HEAD_DIM = 64
ATT_WIDTH = ATT_HEADS * ATT_HEAD_DIM
DILATED_PATTERNS = ((128, 1), (512, 4), (2048, 16))
ATT_BLOCK = 128
N_EXPERTS = 32
TOP_K = 4
SWIGLU_LIMIT = 7.0
SWIGLU_ALPHA = 1.702
NORM_EPS = 1e-5

LANES = 128
SUBLANES = 8
ZX_WIDTH = SSD_INNER + CONV_DIM
DT_PAD = LANES
CONV_HIST = SUBLANES
MOE_ROWS = 1024
VMEM_LIMIT = 56 * 1024 * 1024

NEG_INF = float("-inf")


def _cparams(sem):
    return pltpu.CompilerParams(dimension_semantics=sem, vmem_limit_bytes=VMEM_LIMIT)


def _rms(x, g):
    return x * lax.rsqrt(jnp.mean(x * x, axis=-1, keepdims=True) + NORM_EPS) * g


def _dot(a, b):
    return jnp.dot(a, b, preferred_element_type=F32)


def _dot_nt(a, b):
    return lax.dot_general(a, b, (((1,), (1,)), ((), ())), preferred_element_type=F32)


def _split3(v):
    hi = v.astype(BF16)
    r1 = v - hi.astype(F32)
    mid = r1.astype(BF16)
    lo = (r1 - mid.astype(F32)).astype(BF16)
    return hi, mid, lo


def _dot3(v, m):
    hi, mid, lo = _split3(v)
    return _dot(hi, m) + _dot(mid, m) + _dot(lo, m)


def _dot3_lhs(m, v):
    hi, mid, lo = _split3(v)
    return _dot(m, hi) + _dot(m, mid) + _dot(m, lo)


def _silu(x):
    half = 0.5 * x
    return half * (1.0 + jnp.tanh(half))


def _in_proj_body(x_ref, g_ref, w_ref, zx_ref, q_ref, k_ref, v_ref, dt_ref, *t_refs):
    hn = _rms(x_ref[...], g_ref[...]).astype(BF16)
    o = 0
    for ref, width in ((zx_ref, ZX_WIDTH), (q_ref, ATT_WIDTH), (k_ref, ATT_WIDTH), (v_ref, ATT_WIDTH),
                       (dt_ref, DT_PAD)):
        ref[...] = _dot(hn, w_ref[:, o:o + width])
        o += width
    for src, dst in zip((k_ref, v_ref), t_refs):
        for j in range(ATT_WIDTH // LANES):
            dst[0, j * LANES:(j + 1) * LANES, :] = src[:, j * LANES:(j + 1) * LANES].T


def _in_proj(x2d, g, w_cat, tm, seq=None):
    n = x2d.shape[0]
    tm = min(tm, n)
    per = 1 if seq is None else seq // tm
    row = lambda w: pl.BlockSpec((tm, w), lambda b, i: (b * per + i, 0))
    full = lambda a: pl.BlockSpec(a.shape, lambda b, i: (0,) * a.ndim)
    widths = (ZX_WIDTH, ATT_WIDTH, ATT_WIDTH, ATT_WIDTH, DT_PAD)
    out_specs = [row(w) for w in widths]
    out_shape = [jax.ShapeDtypeStruct((n, w), F32) for w in widths]
    if seq is not None:
        out_specs += [pl.BlockSpec((1, ATT_WIDTH, tm), lambda b, i: (b, 0, i))] * 2
        out_shape += [jax.ShapeDtypeStruct((n // seq, ATT_WIDTH, seq), F32)] * 2
    return pl.pallas_call(
        _in_proj_body,
        grid=(n // (tm * per), per),
        in_specs=[row(D_MODEL), full(g), full(w_cat)],
        out_specs=out_specs,
        out_shape=out_shape,
        compiler_params=_cparams(("parallel", "parallel")),
        name="in_proj",
    )(x2d, g, w_cat)


def _ssd_body(*refs, l_blk, has_init):
    T = SSD_CHUNK
    if has_init:
        (zx_ref, dt_ref, cw_ref, cb_ref, dtb_ref, alog_ref, dskip_ref, norm_ref, e64_ref,
         cinit_ref, sinit_ref, y_ref, conv_out_ref, ssm_out_ref,
         ext_ref, act_ref, state_ref, cst_ref, ybuf_ref, zpad_ref, dtpad_ref) = refs
    else:
        (zx_ref, dt_ref, cw_ref, cb_ref, dtb_ref, alog_ref, dskip_ref, norm_ref, e64_ref,
         y_ref, conv_out_ref, ssm_out_ref,
         ext_ref, act_ref, state_ref, cst_ref, ybuf_ref) = refs
    c = pl.program_id(1)
    n_tile = SSD_INNER // LANES

    @pl.when(c == 0)
    def _start():
        if has_init:
            ext_ref[0:CONV_HIST, :] = jnp.zeros((CONV_HIST, CONV_DIM), F32)
            ext_ref[CONV_HIST - (SSD_CONV - 1):CONV_HIST, :] = cinit_ref[0]
            for j in range(n_tile):
                state_ref[:, j * LANES:(j + 1) * LANES] = sinit_ref[0, j * LANES:(j + 1) * LANES, :].T
        else:
            ext_ref[0:CONV_HIST, :] = jnp.zeros((CONV_HIST, CONV_DIM), F32)
            state_ref[...] = jnp.zeros_like(state_ref)

    if l_blk == T:
        ext_ref[CONV_HIST:CONV_HIST + T, :] = zx_ref[0, :, SSD_INNER:ZX_WIDTH]
        z_of = lambda sl: zx_ref[0, :, sl]
        dt_raw = dt_ref[0]
    else:
        ext_ref[CONV_HIST:CONV_HIST + T, :] = jnp.zeros((T, CONV_DIM), F32)
        ext_ref[CONV_HIST:CONV_HIST + l_blk, :] = zx_ref[0, :, SSD_INNER:ZX_WIDTH]
        zpad_ref[...] = jnp.zeros_like(zpad_ref)
        zpad_ref[0:l_blk, :] = zx_ref[0, :, 0:SSD_INNER]
        dtpad_ref[...] = jnp.zeros_like(dtpad_ref)
        dtpad_ref[0:l_blk, :] = dt_ref[0]
        z_of = lambda sl: zpad_ref[:, sl]
        dt_raw = dtpad_ref[...]

    cw = CONV_DIM // 3
    for cc in range(3):
        sl = slice(cc * cw, (cc + 1) * cw)
        acc = cb_ref[:, sl]
        for j in range(SSD_CONV):
            o = CONV_HIST - (SSD_CONV - 1) + j
            acc = acc + ext_ref[o:o + T, sl] * cw_ref[j:j + 1, sl]
        act_ref[:, sl] = _silu(acc)

    row = lax.broadcasted_iota(jnp.int32, (T, LANES), 0)
    col = lax.broadcasted_iota(jnp.int32, (T, LANES), 1)
    tri = row >= col
    tri_bf = jnp.where(tri, 1.0, 0.0).astype(BF16)
    even = col < SSD_HEAD_DIM

    xdt = dt_raw + dtb_ref[...]
    dtv = jnp.maximum(xdt, 0.0) + jnp.log1p(jnp.exp(-jnp.abs(xdt)))
    if l_blk < T:
        dtv = jnp.where(row < l_blk, dtv, 0.0)
    d_a = dtv * (-jnp.exp(alog_ref[...]))
    cs = _dot3_lhs(tri_bf, d_a)
    cst_ref[...] = cs.T
    ex = _dot3(jnp.concatenate([dtv, cs], axis=0), e64_ref[...])
    dt_ex, cs_ex = ex[0:T], ex[T:2 * T]
    ecs_ex = jnp.exp(cs_ex)
    dd_ex = dt_ex * jnp.exp(cs_ex[T - 1:T] - cs_ex)

    gw = SSD_INNER // SSD_GROUPS
    heads_per_group = SSD_HEADS // SSD_GROUPS
    for g in range(SSD_GROUPS):
        gsl = slice(g * gw, (g + 1) * gw)
        b_g = act_ref[:, SSD_INNER + g * SSD_STATE:SSD_INNER + (g + 1) * SSD_STATE]
        c_off = SSD_INNER + SSD_GROUPS * SSD_STATE
        c_g = act_ref[:, c_off + g * SSD_STATE:c_off + (g + 1) * SSD_STATE].astype(BF16)
        cb = _dot_nt(c_g, b_g.astype(BF16))
        b_gt = b_g.T.astype(BF16)
        x_g = act_ref[:, gsl]
        x_dt = (x_g * dt_ex[:, gsl]).astype(BF16)
        x_dd = (x_g * dd_ex[:, gsl]).astype(BF16)
        st_old = state_ref[:, gsl]
        y_off = _dot(c_g, st_old.astype(BF16)) * ecs_ex[:, gsl]
        state_ref[:, gsl] = st_old * ecs_ex[T - 1:T, gsl] + _dot(b_gt, x_dd)
        for jp in range(heads_per_group // 2):
            h0 = g * heads_per_group + 2 * jp
            psl = slice(jp * LANES, (jp + 1) * LANES)
            osl = slice(g * gw + jp * LANES, g * gw + (jp + 1) * LANES)
            pair = cs_ex[:, osl]
            swapped = pltpu.roll(pair, SSD_HEAD_DIM, axis=1)
            cols = (jnp.where(even, pair, swapped), jnp.where(even, swapped, pair))
            yd = []
            for col, h in zip(cols, (h0, h0 + 1)):
                seg = col - cst_ref[h:h + 1, :]
                lmat = jnp.exp(jnp.where(tri, seg, NEG_INF))
                yd.append(_dot((cb * lmat).astype(BF16), x_dt[:, psl]))
            y_pair = jnp.where(even, yd[0], yd[1]) + y_off[:, psl]
            ybuf_ref[:, osl] = y_pair + dskip_ref[:, osl] * x_g[:, psl]

    for g in range(SSD_GROUPS):
        gsl = slice(g * gw, (g + 1) * gw)
        yg = ybuf_ref[:, gsl] * _silu(z_of(gsl))
        yn = yg * lax.rsqrt(jnp.mean(yg * yg, axis=-1, keepdims=True) + NORM_EPS) * norm_ref[:, gsl]
        y_ref[0, :, gsl] = yn[0:l_blk].astype(y_ref.dtype)

    @pl.when(c == pl.num_programs(1) - 1)
    def _finish():
        lo = CONV_HIST + l_blk - (SSD_CONV - 1)
        conv_out_ref[0] = ext_ref[lo:lo + SSD_CONV - 1, :]
        for j in range(n_tile):
            ssm_out_ref[0, j * LANES:(j + 1) * LANES, :] = state_ref[:, j * LANES:(j + 1) * LANES].T

    ext_ref[0:CONV_HIST, :] = ext_ref[T:T + CONV_HIST, :]


def _ssd(zx3, dt3, consts, init, y_dtype):
    b, L, _ = zx3.shape
    T = SSD_CHUNK
    l_blk = T if L % T == 0 else L
    n_chunks = L // l_blk
    has_init = init is not None
    full = lambda a: pl.BlockSpec(a.shape, lambda i, c: (0,) * a.ndim)
    in_specs = [pl.BlockSpec((1, l_blk, ZX_WIDTH), lambda i, c: (i, c, 0)),
                pl.BlockSpec((1, l_blk, DT_PAD), lambda i, c: (i, c, 0))] + [full(a) for a in consts]
    args = [zx3, dt3, *consts]
    scratch = [pltpu.VMEM((T + CONV_HIST, CONV_DIM), F32),
               pltpu.VMEM((T, CONV_DIM), F32),
               pltpu.VMEM((SSD_STATE, SSD_INNER), F32),
               pltpu.VMEM((LANES, T), F32),
               pltpu.VMEM((T, SSD_INNER), F32)]
    if has_init:
        in_specs += [pl.BlockSpec((1, SSD_CONV - 1, CONV_DIM), lambda i, c: (i, 0, 0)),
                     pl.BlockSpec((1, SSD_INNER, SSD_STATE), lambda i, c: (i, 0, 0))]
        args += list(init)
        scratch += [pltpu.VMEM((T, SSD_INNER), F32), pltpu.VMEM((T, DT_PAD), F32)]
    return pl.pallas_call(
        functools.partial(_ssd_body, l_blk=l_blk, has_init=has_init),
        grid=(b, n_chunks),
        in_specs=in_specs,
        out_specs=[pl.BlockSpec((1, l_blk, SSD_INNER), lambda i, c: (i, c, 0)),
                   pl.BlockSpec((1, SSD_CONV - 1, CONV_DIM), lambda i, c: (i, 0, 0)),
                   pl.BlockSpec((1, SSD_INNER, SSD_STATE), lambda i, c: (i, 0, 0))],
        out_shape=[jax.ShapeDtypeStruct((b, L, SSD_INNER), y_dtype),
                   jax.ShapeDtypeStruct((b, SSD_CONV - 1, CONV_DIM), F32),
                   jax.ShapeDtypeStruct((b, SSD_INNER, SSD_STATE), F32)],
        scratch_shapes=scratch,
        compiler_params=_cparams(("parallel", "arbitrary")),
        name="ssd_init" if has_init else "ssd",
    )(*args)


ATT_UNROLL = 16


def _unroll(n):
    return max(d for d in range(1, ATT_UNROLL + 1) if n % d == 0)


ATT_RES = DILATED_PATTERNS[-1][1]


def _attn_tables():
    B = ATT_BLOCK
    t1 = np.full((len(DILATED_PATTERNS), B, B), -np.inf, np.float32)
    t2 = np.full((len(DILATED_PATTERNS), B, 2 * B), -np.inf, np.float32)
    for p, (window, dil) in enumerate(DILATED_PATTERNS):
        m = ATT_RES // dil
        w = B // m
        rho = np.arange(B)
        c = m * (rho % w) + rho // w
        ck2 = np.concatenate([c - B, c])
        for tab, ck in ((t1, c), (t2, ck2)):
            delta = c[:, None] - ck[None, :]
            ok = (delta >= 0) & (delta <= window // dil)
            tab[p] = np.where(ok, -(dil * delta).astype(np.float32), -np.inf)
    return jnp.asarray(t1), jnp.asarray(t2)


def _attn_body(slopes_ref, t1_ref, t2_ref, q_ref, k_ref, v_ref, o_ref, qd_ref, kd_ref, vd_ref, op_ref, lp_ref,
               stage_ref, b1_ref, b2_ref, in_sem, out_sem, *, seq):
    B = ATT_BLOCK
    hp = pl.program_id(1)
    even = lax.broadcasted_iota(jnp.int32, (B, LANES), 1) < ATT_HEAD_DIM
    scale = ATT_HEAD_DIM ** -0.5

    def in_copy(a, r):
        src, dst = ((q_ref, qd_ref), (k_ref, kd_ref), (v_ref, vd_ref))[a]
        return pltpu.make_async_copy(src.at[:, r, :], dst.at[r], in_sem.at[a, r])
    for r in range(ATT_RES):
        for a in range(3):
            in_copy(a, r).start()

    for e in range(2):
        slope = slopes_ref[2 * hp + e]
        for p in range(len(DILATED_PATTERNS)):
            b1_ref[e * len(DILATED_PATTERNS) + p] = t1_ref[p] * slope
            b2_ref[e * len(DILATED_PATTERNS) + p] = t2_ref[p] * slope

    for r in range(ATT_RES):
        for a in range(3):
            in_copy(a, r).wait()

    def block(p, r_d, n, first):
        dil = DILATED_PATTERNS[p][1]
        m = ATT_RES // dil
        w = B // m

        def slab(j, nblk):
            start = nblk * w
            return r_d + dil * j, pl.ds(start if isinstance(start, int) else pl.multiple_of(start, w), w)

        def gather(src, nblk):
            parts = []
            for j in range(m):
                r, rows = slab(j, nblk)
                parts.append(src[r, rows, :])
            return parts[0] if m == 1 else jnp.concatenate(parts, axis=0)

        qb = gather(qd_ref, n) * scale
        if first:
            kb, vb = gather(kd_ref, n), gather(vd_ref, n)
            bias = lambda e: b1_ref[e * len(DILATED_PATTERNS) + p]
        else:
            kb = jnp.concatenate([gather(kd_ref, n - 1), gather(kd_ref, n)], axis=0)
            vb = jnp.concatenate([gather(vd_ref, n - 1), gather(vd_ref, n)], axis=0)
            bias = lambda e: b2_ref[e * len(DILATED_PATTERNS) + p]
        q2 = jnp.concatenate([jnp.where(even, qb, 0.0), jnp.where(even, 0.0, qb)], axis=0).astype(BF16)
        s = _dot_nt(q2, kb.astype(BF16)) + jnp.concatenate([bias(0), bias(1)], axis=0)
        mx = jnp.max(s, axis=-1, keepdims=True)
        pr = jnp.exp(s - mx)
        l = jnp.sum(pr, axis=-1, keepdims=True)
        o2 = _dot(pr.astype(BF16), vb.astype(BF16)) / l
        lse2 = jnp.broadcast_to(mx + jnp.log(l), (2 * B, LANES))
        o = jnp.where(even, o2[:B], o2[B:])
        lse = jnp.where(even, lse2[:B], lse2[B:])
        for j in range(m):
            r, rows = slab(j, n)
            op_ref[p, r, rows, :] = o[j * w:(j + 1) * w]
            lp_ref[p, r, rows, :] = lse[j * w:(j + 1) * w]

    for p, (_, dil) in enumerate(DILATED_PATTERNS):
        nb = seq // (dil * B)
        u_first = _unroll(dil)

        def first(i, carry, p=p, u_first=u_first):
            for u in range(u_first):
                block(p, i * u_first + u, 0, True)
            return carry
        lax.fori_loop(0, dil // u_first, first, 0)
        n_later = dil * (nb - 1)
        if n_later:
            u_later = _unroll(n_later)

            def later(i, carry, p=p, nb=nb, u_later=u_later):
                for u in range(u_later):
                    j = i * u_later + u
                    block(p, j // (nb - 1), j % (nb - 1) + 1, False)
                return carry
            lax.fori_loop(0, n_later // u_later, later, 0)

    def out_copy(r):
        return pltpu.make_async_copy(stage_ref.at[r], o_ref.at[:, r, :], out_sem.at[r])
    for r in range(ATT_RES):
        l0, l1, l2 = lp_ref[0, r], lp_ref[1, r], lp_ref[2, r]
        m = jnp.maximum(jnp.maximum(l0, l1), l2)
        w0, w1, w2 = jnp.exp(l0 - m), jnp.exp(l1 - m), jnp.exp(l2 - m)
        num = w0 * op_ref[0, r] + w1 * op_ref[1, r] + w2 * op_ref[2, r]
        stage_ref[r] = num / (w0 + w1 + w2)
        out_copy(r).start()
    for r in range(ATT_RES):
        out_copy(r).wait()


def _attn_prompt(q, k, v, slopes, batch, seq):
    n_hp = ATT_HEADS // 2
    n_pat = len(DILATED_PATTERNS)
    assert len(DILATED_PATTERNS) == 3 and all(ATT_RES % d == 0 and seq % (d * ATT_BLOCK) == 0 and w // d == ATT_BLOCK
                                              for w, d in DILATED_PATTERNS)
    per = seq // ATT_RES
    by_res = lambda a: a.reshape(batch * per, ATT_RES, ATT_WIDTH)
    blk = pl.BlockSpec((per, ATT_RES, LANES), lambda b, h: (b, 0, h))
    t1, t2 = _attn_tables()
    full = lambda a: pl.BlockSpec(a.shape, lambda b, h: (0,) * a.ndim)
    res = pltpu.VMEM((ATT_RES, per, LANES), F32)
    out = pl.pallas_call(
        functools.partial(_attn_body, seq=seq),
        grid=(batch, n_hp),
        in_specs=[pl.BlockSpec(memory_space=pltpu.SMEM), full(t1), full(t2), blk, blk, blk],
        out_specs=blk,
        out_shape=jax.ShapeDtypeStruct((batch * per, ATT_RES, ATT_WIDTH), F32),
        scratch_shapes=[res, res, res,
                        pltpu.VMEM((n_pat, ATT_RES, per, LANES), F32), pltpu.VMEM((n_pat, ATT_RES, per, LANES), F32),
                        res,
                        pltpu.VMEM((2 * n_pat, ATT_BLOCK, ATT_BLOCK), F32),
                        pltpu.VMEM((2 * n_pat, ATT_BLOCK, 2 * ATT_BLOCK), F32),
                        pltpu.SemaphoreType.DMA((3, ATT_RES)), pltpu.SemaphoreType.DMA((ATT_RES,))],
        compiler_params=_cparams(("parallel", "parallel")),
        name="attn_prompt",
    )(slopes, t1, t2, by_res(q), by_res(k), by_res(v))
    return out.reshape(batch * seq, ATT_WIDTH)


DEC_ROWS = SUBLANES


def _decode_tables(t_new, m_cache):
    slopes = 2.0 ** (-8.0 * np.arange(1, ATT_HEADS + 1) / ATT_HEADS)

    def mult(dist):
        return sum(1 for window, dil in DILATED_PATTERNS if 0 <= dist <= window and dist % dil == 0)

    bias_c = np.zeros((ATT_HEADS, DEC_ROWS, m_cache), np.float32)
    mult_c = np.ones((DEC_ROWS, m_cache), np.float32)
    bias_n = np.full((ATT_HEADS, DEC_ROWS, LANES), -np.inf, np.float32)
    mult_n = np.zeros((DEC_ROWS, LANES), np.float32)
    for t in range(t_new):
        dist = m_cache + t - np.arange(m_cache)
        mu = np.array([mult(d) for d in dist], np.float32)
        mult_c[t] = mu
        bias_c[:, t, :] = np.where(mu > 0, -slopes[:, None] * dist[None, :], -np.inf)
        for t2 in range(t_new):
            if mult(t - t2):
                bias_n[:, t, t2] = -slopes * (t - t2)
                mult_n[t, t2] = mult(t - t2)
    bias_n[:, t_new:, 0] = 0.0
    mult_n[t_new:, 0] = 1.0
    return jnp.asarray(bias_c), jnp.asarray(mult_c), jnp.asarray(bias_n), jnp.asarray(mult_n)


def _attn_dec_body(q_ref, knt_ref, vnt_ref, ck_ref, cv_ref, bc_ref, mc_ref, bn_ref, mn_ref, o_ref):
    scale = ATT_HEAD_DIM ** -0.5
    for h in range(ATT_HEADS):
        qh = (q_ref[0, h] * scale).astype(BF16)
        sc = _dot(qh, ck_ref[0, h].astype(BF16)) + bc_ref[h]
        sn = _dot(qh, knt_ref[0, h].astype(BF16)) + bn_ref[h]
        m = jnp.maximum(jnp.max(sc, axis=-1, keepdims=True), jnp.max(sn, axis=-1, keepdims=True))
        pc = jnp.exp(sc - m) * mc_ref[...]
        pn = jnp.exp(sn - m) * mn_ref[...]
        inv_l = 1.0 / (jnp.sum(pc, axis=-1, keepdims=True) + jnp.sum(pn, axis=-1, keepdims=True))
        o_ref[0, h] = (_dot_nt(cv_ref[0, h].astype(BF16), (pc * inv_l).astype(BF16))
                       + _dot_nt(vnt_ref[0, h].astype(BF16), (pn * inv_l).astype(BF16)))


def _attn_decode(q4, knt, vnt, cache_kt, cache_vt, t_new):
    b = q4.shape[0]
    m_cache = cache_kt.shape[3]
    assert t_new <= DEC_ROWS
    tables = _decode_tables(t_new, m_cache)
    per_b = lambda a: pl.BlockSpec((1,) + a.shape[1:], lambda i: (i, 0, 0, 0))
    full = lambda a: pl.BlockSpec(a.shape, lambda i: (0,) * a.ndim)
    out_shape = jax.ShapeDtypeStruct((b, ATT_HEADS, ATT_HEAD_DIM, DEC_ROWS), F32)
    return pl.pallas_call(
        _attn_dec_body,
        grid=(b,),
        in_specs=[per_b(a) for a in (q4, knt, vnt, cache_kt, cache_vt)] + [full(a) for a in tables],
        out_specs=per_b(out_shape),
        out_shape=out_shape,
        compiler_params=_cparams(("parallel",)),
        name="attn_decode",
    )(q4, knt, vnt, cache_kt, cache_vt, *tables)


def _out_proj_body(x_ref, y_ref, o_ref, an_ref, wy_ref, wo_ref, nf_ref, rw_ref, rb_ref, cnt0_ref,
                   h_ref, xn_ref, idx_ref, gate_ref, rank_ref, cnt_ref, carry_ref, *, tm):
    i = pl.program_id(0)

    @pl.when(i == 0)
    def _start():
        carry_ref[...] = cnt0_ref[...]

    on = _rms(o_ref[...], an_ref[...]).astype(BF16)
    mixed = _dot(y_ref[...].astype(BF16), wy_ref[...]) + _dot(on, wo_ref[...])
    h = x_ref[...] + mixed
    h_ref[...] = h
    xn = _rms(h, nf_ref[...])
    xn_ref[...] = xn
    logits = _dot_nt(rw_ref[...], xn.astype(BF16)) + rb_ref[...]

    e_iota = lax.broadcasted_iota(jnp.int32, (N_EXPERTS, tm), 0)
    vals, idxs, sels = [], [], []
    cur = logits
    for _ in range(TOP_K):
        mx = jnp.max(cur, axis=0, keepdims=True)
        ix = jnp.min(jnp.where(cur == mx, e_iota, N_EXPERTS), axis=0, keepdims=True)
        sel = e_iota == ix
        cur = jnp.where(sel, NEG_INF, cur)
        vals.append(mx)
        idxs.append(ix)
        sels.append(sel)
    ex = [jnp.exp(v - vals[0]) for v in vals]
    den = functools.reduce(lambda a, b: a + b, ex)

    sel_any = functools.reduce(jnp.logical_or, sels)
    sel_f = jnp.where(sel_any, 1.0, 0.0)
    r = lax.broadcasted_iota(jnp.int32, (tm, tm), 0)
    c = lax.broadcasted_iota(jnp.int32, (tm, tm), 1)
    upper = jnp.where(r <= c, 1.0, 0.0).astype(BF16)
    cum = _dot(sel_f.astype(BF16), upper)
    before = carry_ref[:, 0:1] + cum - sel_f
    for k in range(TOP_K):
        idx_ref[k:k + 1, :] = idxs[k]
        gate_ref[k:k + 1, :] = ex[k] / den
        rank_ref[k:k + 1, :] = jnp.sum(jnp.where(sels[k], before, 0.0), axis=0, keepdims=True).astype(jnp.int32)
    idx_ref[TOP_K:SUBLANES, :] = jnp.zeros((SUBLANES - TOP_K, tm), jnp.int32)
    gate_ref[TOP_K:SUBLANES, :] = jnp.zeros((SUBLANES - TOP_K, tm), F32)
    rank_ref[TOP_K:SUBLANES, :] = jnp.zeros((SUBLANES - TOP_K, tm), jnp.int32)
    carry_ref[...] = carry_ref[...] + jnp.max(cum, axis=1, keepdims=True)
    cnt_ref[...] = carry_ref[...]


def _out_proj(x2d, y2d, o2d, consts, cnt0, tm):
    n = x2d.shape[0]
    tm = min(tm, n)
    row = lambda w: pl.BlockSpec((tm, w), lambda i: (i, 0))
    colb = pl.BlockSpec((SUBLANES, tm), lambda i: (0, i))
    full = lambda a: pl.BlockSpec(a.shape, lambda i: (0,) * a.ndim)
    return pl.pallas_call(
        functools.partial(_out_proj_body, tm=tm),
        grid=(n // tm,),
        in_specs=[row(D_MODEL), row(SSD_INNER), row(ATT_WIDTH)] + [full(a) for a in consts] + [full(cnt0)],
        out_specs=[row(D_MODEL), row(D_MODEL), colb, colb, colb, full(cnt0)],
        out_shape=[jax.ShapeDtypeStruct((n, D_MODEL), F32), jax.ShapeDtypeStruct((n, D_MODEL), F32),
                   jax.ShapeDtypeStruct((SUBLANES, n), jnp.int32), jax.ShapeDtypeStruct((SUBLANES, n), F32),
                   jax.ShapeDtypeStruct((SUBLANES, n), jnp.int32), jax.ShapeDtypeStruct(cnt0.shape, F32)],
        scratch_shapes=[pltpu.VMEM(cnt0.shape, F32)],
        compiler_params=_cparams(("arbitrary",)),
        name="out_proj",
    )(x2d, y2d, o2d, *consts, cnt0)


ISSUE_UNROLL = 8
PAD_BITS = tuple(1 << b for b in range((MOE_ROWS // SUBLANES).bit_length()))


def _dispatch_body(slot_ref, from_ref, n_ref, tail_ref, xa_ref, xb_ref, xs_ref, zero_ref, stage_ref, sem, row_sems,
                   *, tm, tiles_a, tiles_b, n_blocks):
    i = pl.program_id(0)

    @pl.when(i == 0)
    def _pad_fill():
        zero_ref[...] = jnp.zeros_like(zero_ref)
        one_row = lambda slot: pltpu.make_async_copy(zero_ref.at[pl.ds(0, 1)], xs_ref.at[pl.ds(slot, 1)], sem)
        block = lambda j: pltpu.make_async_copy(zero_ref, xs_ref.at[pl.ds(j * MOE_ROWS, MOE_ROWS)], sem)

        def per_expert(e, carry):
            first, n_pad = from_ref[e], n_ref[e]
            head = jnp.minimum((-first) % SUBLANES, n_pad)

            def start(r, c):
                one_row(first + r).start()
                return c

            def wait(r, c):
                one_row(0).wait()
                return c
            lax.fori_loop(0, head, start, 0)
            lax.fori_loop(0, head, wait, 0)
            units = (n_pad - head) // SUBLANES
            for begin in (True, False):
                done = first + head
                for b in PAD_BITS:
                    @pl.when((units & b) != 0)
                    def _piece(done=done, rows=b * SUBLANES):
                        cp = pltpu.make_async_copy(zero_ref.at[pl.ds(0, rows)],
                                                   xs_ref.at[pl.ds(pl.multiple_of(done, SUBLANES), rows)], sem)
                        if begin:
                            cp.start()
                        else:
                            cp.wait()
                    done = done + (units & b) * SUBLANES
            return carry
        lax.fori_loop(0, N_EXPERTS, per_expert, 0)

        def tail(j, carry):
            block(j).start()
            block(j).wait()
            return carry
        lax.fori_loop(tail_ref[0], n_blocks, tail, 0)

    def drain(parity):
        for _ in range(TOP_K):
            pltpu.make_async_copy(stage_ref.at[parity], xs_ref.at[pl.ds(0, tm)], row_sems.at[parity]).wait()

    for parity in range(2):
        @pl.when(i % 2 == parity)
        def _step(parity=parity):
            @pl.when(i < tiles_a)
            def _first_group():
                stage_ref[parity] = xa_ref[...]

            @pl.when(i >= tiles_a)
            def _second_group():
                stage_ref[parity] = xb_ref[...]

            def issue(j, carry):
                t0 = pl.multiple_of(j * ISSUE_UNROLL, ISSUE_UNROLL)
                for u in range(ISSUE_UNROLL):
                    for k in range(TOP_K):
                        slot = slot_ref[t0 * TOP_K + (u * TOP_K + k)]
                        pltpu.make_async_copy(stage_ref.at[parity, pl.ds(t0 + u, 1)],
                                              xs_ref.at[pl.ds(slot, 1)], row_sems.at[parity]).start(priority=k % 2)
                return carry
            lax.fori_loop(0, tm // ISSUE_UNROLL, issue, 0)

            @pl.when(i > 0)
            def _previous():
                drain(1 - parity)

            @pl.when(i == tiles_a + tiles_b - 1)
            def _last():
                drain(parity)


def _dispatch(slot, pad_from, pad_n, tail_block, xa, xb, n_blocks, tm):
    tiles_a, tiles_b = xa.shape[0] // tm, xb.shape[0] // tm
    assert tiles_a * tm == xa.shape[0] and tiles_b * tm == xb.shape[0]
    smem = pl.BlockSpec(memory_space=pltpu.SMEM)
    hbm = pl.BlockSpec(memory_space=pl.ANY)
    return pl.pallas_call(
        functools.partial(_dispatch_body, tm=tm, tiles_a=tiles_a, tiles_b=tiles_b, n_blocks=n_blocks),
        grid=(tiles_a + tiles_b,),
        in_specs=[pl.BlockSpec((tm * TOP_K,), lambda i: (i,), memory_space=pltpu.SMEM), smem, smem, smem,
                  pl.BlockSpec((tm, D_MODEL), lambda i: (jnp.minimum(i, tiles_a - 1), 0)),
                  pl.BlockSpec((tm, D_MODEL), lambda i: (jnp.maximum(i - tiles_a, 0), 0))],
        out_specs=hbm,
        out_shape=jax.ShapeDtypeStruct((n_blocks * MOE_ROWS, D_MODEL), F32),
        scratch_shapes=[pltpu.VMEM((MOE_ROWS, D_MODEL), F32), pltpu.VMEM((2, tm, D_MODEL), F32),
                        pltpu.SemaphoreType.DMA(()), pltpu.SemaphoreType.DMA((2,))],
        compiler_params=_cparams(("arbitrary",)),
        name="dispatch",
    )(slot, pad_from, pad_n, tail_block, xa, xb)


MOE_COLS = 2 * LANES


def _glu_perm():
    src = np.concatenate([np.arange(0, MOE_COLS, 2), np.arange(1, MOE_COLS, 2)])
    perm = np.zeros((MOE_COLS, MOE_COLS), np.float32)
    perm[src, np.arange(MOE_COLS)] = 1.0
    return jnp.asarray(perm, dtype=BF16)


def _experts_body(be_ref, bv_ref, nx_ref, xs_ref, w1_ref, p_ref, b1_ref, w2_ref, b2_ref, ys_ref,
                  w1f_ref, w1s_ref, w2s_ref, g_ref, sem):
    j = pl.program_id(0)
    valid = bv_ref[j]
    e = be_ref[j]

    @pl.when(jnp.logical_and(valid > 0, jnp.logical_or(j == 0, e != be_ref[jnp.maximum(j - 1, 0)])))
    def _new_expert():
        slot, nxt = nx_ref[1, e], nx_ref[0, e]
        fetch = lambda ex, s: pltpu.make_async_copy(w1_ref.at[ex], w1f_ref.at[s], sem.at[s])

        @pl.when(j == 0)
        def _first_fetch():
            fetch(e, slot).start()
        fetch(e, slot).wait()

        @pl.when(nxt >= 0)
        def _next_fetch():
            fetch(nxt, 1 - slot).start()
        for c in range(w1s_ref.shape[1] // MOE_COLS):
            cols = slice(c * MOE_COLS, (c + 1) * MOE_COLS)
            w1s_ref[:, cols] = _dot(w1f_ref[slot, :, cols].astype(BF16), p_ref[...]).astype(BF16)
        w2s_ref[...] = w2_ref[0].astype(BF16)

    @pl.when(valid > 0)
    def _compute():
        x = xs_ref[...].astype(BF16)
        for c in range(g_ref.shape[1] // LANES):
            cols = slice(c * MOE_COLS, (c + 1) * MOE_COLS)
            hc = _dot(x, w1s_ref[:, cols]) + b1_ref[0, :, cols]
            glu = jnp.minimum(hc[:, :LANES], SWIGLU_LIMIT)
            lin = jnp.clip(hc[:, LANES:], -SWIGLU_LIMIT, SWIGLU_LIMIT)
            g_ref[:, c * LANES:(c + 1) * LANES] = (glu * jax.nn.sigmoid(SWIGLU_ALPHA * glu) * (lin + 1.0)).astype(BF16)
        ys_ref[...] = _dot(g_ref[...], w2s_ref[...]) + b2_ref[0]

    @pl.when(valid <= 0)
    def _empty():
        ys_ref[...] = jnp.zeros_like(ys_ref)


def _experts(block_expert, block_valid, counts, xs, w1, b1p, w2, b2):
    n_blocks = xs.shape[0] // MOE_ROWS
    perm = _glu_perm()
    active = counts > 0
    ids = jnp.arange(N_EXPERTS, dtype=jnp.int32)
    later = jnp.where(active[None, :] & (ids[None, :] > ids[:, None]), ids[None, :], N_EXPERTS)
    nxt = jnp.min(later, axis=1)
    nx = jnp.stack([jnp.where(nxt < N_EXPERTS, nxt, -1), (jnp.cumsum(active) - 1) % 2]).astype(jnp.int32)
    wspec = lambda a: pl.BlockSpec((1,) + a.shape[1:], lambda j, be, bv, nx: (be[j], 0, 0))
    grid_spec = pltpu.PrefetchScalarGridSpec(
        num_scalar_prefetch=3,
        grid=(n_blocks,),
        in_specs=[pl.BlockSpec((MOE_ROWS, D_MODEL), lambda j, be, bv, nx: (j, 0)),
                  pl.BlockSpec(memory_space=pl.ANY), pl.BlockSpec(perm.shape, lambda j, be, bv, nx: (0, 0)),
                  wspec(b1p), wspec(w2), wspec(b2)],
        out_specs=pl.BlockSpec((MOE_ROWS, D_MODEL), lambda j, be, bv, nx: (j, 0)),
        scratch_shapes=[pltpu.VMEM((2,) + w1.shape[1:], F32), pltpu.VMEM(w1.shape[1:], BF16),
                        pltpu.VMEM(w2.shape[1:], BF16), pltpu.VMEM((MOE_ROWS, w2.shape[1]), BF16),
                        pltpu.SemaphoreType.DMA((2,))],
    )
    return pl.pallas_call(
        _experts_body,
        grid_spec=grid_spec,
        out_shape=jax.ShapeDtypeStruct(xs.shape, F32),
        compiler_params=_cparams(("arbitrary",)),
        name="experts",
    )(block_expert, block_valid, nx, xs, w1, perm, b1p, w2, b2)


COMBINE_TM = 256
RUN_UNIT = 16
RUN_BITS = tuple(1 << b for b in reversed(range((COMBINE_TM // RUN_UNIT).bit_length())))
BUF_ROWS = -(-(TOP_K * COMBINE_TM + N_EXPERTS * (RUN_UNIT - 1 + SUBLANES - 1)) // LANES) * LANES


def _combine_meta(idx_all, slot_all, pad_start):
    n_tok = idx_all.shape[1]
    tiles = n_tok // COMBINE_TM
    experts = jnp.arange(N_EXPERTS, dtype=jnp.int32)
    hit = idx_all[:TOP_K, :, None] == experts
    cnt = jnp.sum(hit.reshape(TOP_K, tiles, COMBINE_TM, N_EXPERTS), axis=(0, 2), dtype=jnp.int32)
    run_start = pad_start[None, :] + jnp.cumsum(cnt, axis=0) - cnt
    lead = run_start % SUBLANES
    units = jnp.where(cnt > 0, (cnt + lead + RUN_UNIT - 1) // RUN_UNIT, 0)
    first_row = RUN_UNIT * (jnp.cumsum(units, axis=1) - units)
    meta = jnp.concatenate([run_start - lead, units, first_row, jnp.zeros_like(cnt)], axis=1).astype(jnp.int32)
    shift = jnp.repeat(first_row + lead - run_start, COMBINE_TM, axis=0)
    col = slot_all[:TOP_K] + jnp.sum(jnp.where(hit, shift[None], 0), axis=2, dtype=jnp.int32)
    return meta.reshape(tiles, 1, 4 * N_EXPERTS), col.T


def _combine_body(meta_ref, next_ref, col_ref, gate_ref, h_ref, nf_ref, ys_ref, y_ref, buf_ref, sem, *, n):
    i = pl.program_id(0)

    def runs(meta, half, start):
        for e in range(N_EXPERTS):
            src0, units, dst0 = meta[0, 0, e], meta[0, 0, N_EXPERTS + e], meta[0, 0, 2 * N_EXPERTS + e]

            def pieces(bits, done):
                for b in bits:
                    rows = b * RUN_UNIT

                    @pl.when((units & b) != 0)
                    def _piece(done=done, rows=rows):
                        dst = pl.multiple_of(dst0 + done, RUN_UNIT)
                        cp = pltpu.make_async_copy(ys_ref.at[pl.ds(pl.multiple_of(src0 + done, SUBLANES), rows)],
                                                   buf_ref.at[half, pl.ds(dst, rows)], sem.at[half])
                        if start:
                            cp.start()
                        else:
                            cp.wait()
                    done = done + (units & b) * RUN_UNIT
                return done
            done = pieces(RUN_BITS[-2:], jnp.int32(0))

            @pl.when(units >= RUN_BITS[-3])
            def _long_run(done=done):
                pieces(RUN_BITS[:-2], done)

    def per_half(half):
        @pl.when(i + 1 < n)
        def _fetch_next():
            runs(next_ref, 1 - half, True)
        runs(meta_ref, half, False)
        rows = lax.broadcasted_iota(jnp.int32, (COMBINE_TM, BUF_ROWS), 1)
        pick = jnp.zeros((COMBINE_TM, BUF_ROWS), F32)
        for k in range(TOP_K):
            pick = jnp.where(rows == col_ref[:, k:k + 1], gate_ref[:, k:k + 1], pick)
        acc = h_ref[...] + _dot(pick.astype(BF16), buf_ref[half].astype(BF16))
        y_ref[...] = _rms(acc, nf_ref[...])

    @pl.when(i == 0)
    def _first():
        buf_ref[...] = jnp.zeros_like(buf_ref)
        runs(meta_ref, 0, True)

    @pl.when(i % 2 == 0)
    def _even():
        per_half(0)

    @pl.when(i % 2 == 1)
    def _odd():
        per_half(1)


def _combine(meta, col_rows, gates_rows, h, norm_final, ys, first_tile):
    n = h.shape[0]
    tm = COMBINE_TM
    assert n % tm == 0
    last = first_tile + n // tm - 1
    smem_tile = lambda f: pl.BlockSpec((1, 1, meta.shape[2]), lambda i: (f(i), 0, 0), memory_space=pltpu.SMEM)
    return pl.pallas_call(
        functools.partial(_combine_body, n=n // tm),
        grid=(n // tm,),
        in_specs=[smem_tile(lambda i: i + first_tile), smem_tile(lambda i: jnp.minimum(i + first_tile + 1, last)),
                  pl.BlockSpec((tm, TOP_K), lambda i: (i + first_tile, 0)),
                  pl.BlockSpec((tm, TOP_K), lambda i: (i, 0)),
                  pl.BlockSpec((tm, D_MODEL), lambda i: (i, 0)),
                  pl.BlockSpec((1, D_MODEL), lambda i: (0, 0)),
                  pl.BlockSpec(memory_space=pl.ANY)],
        out_specs=pl.BlockSpec((tm, D_MODEL), lambda i: (i, 0)),
        out_shape=jax.ShapeDtypeStruct((n, D_MODEL), F32),
        scratch_shapes=[pltpu.VMEM((2, BUF_ROWS, D_MODEL), F32), pltpu.SemaphoreType.DMA((2,))],
        compiler_params=_cparams(("arbitrary",)),
        name="combine",
    )(meta, meta, col_rows, gates_rows, h, norm_final, ys)


def _expansion(width):
    h = np.arange(LANES)[:, None]
    c = np.arange(SSD_HEADS * width)[None, :] // width
    return jnp.asarray((h == c).astype(np.float32), dtype=BF16)


def _pad_lanes(v):
    return jnp.pad(v.astype(F32), (0, LANES - v.shape[0]))[None, :]


def kernel(x_prompt, x_sample, state_conv, state_ssm, cache_win_k, cache_win_v, norm_mix, w_in, conv_w, conv_b,
           dt_bias, a_log, d_skip, ssd_norm, att_norm, w_out, norm_ffn, router_w, router_b, w1, b1, w2, b2,
           norm_final):
    depth = w_in.shape[0]
    assert depth == 1
    bp, seq, _ = x_prompt.shape
    bs, t_new, _ = x_sample.shape
    n_p, n_s = bp * seq, bs * t_new
    l = 0

    o_dt = SSD_INNER + CONV_DIM
    o_q = o_dt + SSD_HEADS
    wl = w_in[l]
    w_cat = jnp.concatenate([wl[:, :o_dt], wl[:, o_q:], jnp.pad(wl[:, o_dt:o_q], ((0, 0), (0, DT_PAD - SSD_HEADS)))],
                            axis=1).astype(BF16)
    g_mix = norm_mix[l][None, :]
    ssd_consts = (conv_w[l], conv_b[l][None, :], _pad_lanes(dt_bias[l]), _pad_lanes(a_log[l]),
                  jnp.repeat(d_skip[l], SSD_HEAD_DIM)[None, :], ssd_norm[l][None, :],
                  _expansion(SSD_HEAD_DIM))
    slopes = jnp.exp2(-8.0 * jnp.arange(1, ATT_HEADS + 1, dtype=F32) / ATT_HEADS)
    out_consts = (att_norm[l][None, :], w_out[l][:SSD_INNER].astype(BF16), w_out[l][SSD_INNER:].astype(BF16),
                  norm_ffn[l][None, :], router_w[l].T.astype(BF16), router_b[l][:, None].astype(F32))
    b1p = b1[l].reshape(N_EXPERTS, -1, LANES, 2).transpose(0, 1, 3, 2).reshape(N_EXPERTS, 1, -1)
    b2r = b2[l][:, None, :]

    zx, q, k, v, dt, kt, vt = _in_proj(x_prompt.reshape(n_p, D_MODEL), g_mix, w_cat, 512, seq)
    y_p, conv_p, ssm_p = _ssd(zx.reshape(bp, seq, ZX_WIDTH), dt.reshape(bp, seq, DT_PAD), ssd_consts, None, BF16)
    o_p = _attn_prompt(q, k, v, slopes, bp, seq)
    keep = min(DILATED_PATTERNS[-1][0], seq)
    k_p = kt.reshape(bp, ATT_HEADS, ATT_HEAD_DIM, seq).transpose(0, 3, 1, 2)[:, seq - keep:]
    v_p = vt.reshape(bp, ATT_HEADS, ATT_HEAD_DIM, seq).transpose(0, 3, 1, 2)[:, seq - keep:]

    zx_s, q_s, k_s, v_s, dt_s = _in_proj(x_sample.reshape(n_s, D_MODEL), g_mix, w_cat, 256)
    init = (state_conv[l], state_ssm[l].reshape(bs, SSD_INNER, SSD_STATE))
    y_s, conv_s, ssm_s = _ssd(zx_s.reshape(bs, t_new, ZX_WIDTH), dt_s.reshape(bs, t_new, DT_PAD), ssd_consts, init, F32)

    def head_major(a):
        a = a.reshape(bs, t_new, ATT_HEADS, ATT_HEAD_DIM).transpose(0, 2, 1, 3)
        return jnp.pad(a, ((0, 0), (0, 0), (0, DEC_ROWS - t_new), (0, 0)))

    def head_major_t(a):
        a = a.reshape(bs, t_new, ATT_HEADS, ATT_HEAD_DIM).transpose(0, 2, 3, 1)
        return jnp.pad(a, ((0, 0), (0, 0), (0, 0), (0, LANES - t_new)))
    o_s = _attn_decode(head_major(q_s), head_major_t(k_s), head_major_t(v_s),
                       cache_win_k[l].transpose(0, 2, 3, 1), cache_win_v[l].transpose(0, 2, 3, 1), t_new)
    o_s = o_s[:, :, :, :t_new].transpose(0, 3, 1, 2)

    cnt0 = jnp.zeros((N_EXPERTS, LANES), F32)
    h_p, xn_p, idx_p, gate_p, rank_p, cnt_p = _out_proj(x_prompt.reshape(n_p, D_MODEL), y_p.reshape(n_p, SSD_INNER),
                                                        o_p, out_consts, cnt0, 512)
    h_s, xn_s, idx_s, gate_s, rank_s, cnt_all = _out_proj(x_sample.reshape(n_s, D_MODEL), y_s.reshape(n_s, SSD_INNER),
                                                          o_s.reshape(n_s, ATT_WIDTH), out_consts, cnt_p, 512)

    counts = cnt_all[:, 0].astype(jnp.int32)
    padded = (counts + MOE_ROWS - 1) // MOE_ROWS * MOE_ROWS
    pad_end = jnp.cumsum(padded)
    pad_start = pad_end - padded
    n_blocks = -(-((n_p + n_s) * TOP_K) // MOE_ROWS) + N_EXPERTS
    blk0 = jnp.arange(n_blocks, dtype=jnp.int32) * MOE_ROWS
    owner = blk0[:, None] >= pad_end[None, :]
    block_expert = jnp.minimum(jnp.sum(owner, axis=1), N_EXPERTS - 1).astype(jnp.int32)
    onehot = block_expert[:, None] == jnp.arange(N_EXPERTS, dtype=jnp.int32)[None, :]
    used = jnp.sum(jnp.where(onehot, (blk0[:, None] - pad_start[None, :]), 0), axis=1)
    block_valid = jnp.clip(jnp.sum(jnp.where(onehot, counts[None, :], 0), axis=1) - used, 0, MOE_ROWS).astype(jnp.int32)

    tm = min(256, n_s)
    idx_all = jnp.concatenate([idx_p, idx_s], axis=1)
    rank_all = jnp.concatenate([rank_p, rank_s], axis=1)
    first_slot = functools.reduce(lambda acc, e: jnp.where(idx_all == e, pad_start[e], acc), range(N_EXPERTS),
                                  jnp.zeros_like(idx_all))
    slot_all = first_slot + rank_all
    xs = _dispatch(slot_all[:TOP_K].T.reshape(-1), pad_start + counts, padded - counts, pad_end[N_EXPERTS - 1:] // MOE_ROWS,
                   xn_p, xn_s, n_blocks, tm)
    ys = _experts(block_expert, block_valid, counts, xs, w1[l], b1p, w2[l], b2r)
    nfin = norm_final[None, :]
    meta, col_rows = _combine_meta(idx_all, slot_all, pad_start)
    y_prompt = _combine(meta, col_rows, gate_p[:TOP_K].T, h_p, nfin, ys, 0)
    y_sample = _combine(meta, col_rows, gate_s[:TOP_K].T, h_s, nfin, ys, n_p // COMBINE_TM)

    return (y_prompt.reshape(bp, seq, D_MODEL), y_sample.reshape(bs, t_new, D_MODEL),
            conv_p[None], ssm_p.reshape(1, bp, SSD_HEADS, SSD_HEAD_DIM, SSD_STATE), k_p[None], v_p[None],
            conv_s[None], ssm_s.reshape(1, bs, SSD_HEADS, SSD_HEAD_DIM, SSD_STATE),
            k_s.reshape(1, bs, t_new, ATT_HEADS, ATT_HEAD_DIM), v_s.reshape(1, bs, t_new, ATT_HEADS, ATT_HEAD_DIM))
```
